```python
import math
import jax, jax.numpy as jnp
from jax import lax
import numpy as np

D_MODEL = 2048
BATCH = 2
SEQ = 4096
DEPTH = 2

GRID_W = 64
CTX_LEN = 256
D_FF = 5504
N_MOD = 9
CHUNK = 128
A_WIDTH = 1024
A_GROUPS = 8
A_GROUP_DIM = A_WIDTH // A_GROUPS
B_WIDTH = 1024
B_HEAD_DIM = 64
B_HEADS = B_WIDTH // B_HEAD_DIM
DECAY_LORA = 96
ICL_LORA = 96
GATE_LORA = 256
C_HEADS = 8
C_HEAD_DIM = 64
C_WIDTH = C_HEADS * 2 * C_HEAD_DIM
BRANCH_W = 1024
N_BRANCH = 3
PROJ_SIZES = (A_WIDTH, A_WIDTH, 3 * B_WIDTH, GATE_LORA, 2 * DECAY_LORA, 2 * ICL_LORA, 3 * C_WIDTH, N_BRANCH * D_MODEL)
PROJ_W = sum(PROJ_SIZES)
ROPE_BASE = 10000.0
ROPE_AXIS_DIM = C_HEAD_DIM // 2
Q_BLOCK = 128
NORM_EPS = 1e-6
LN_X_EPS = 64e-5

kernel_name = 'hybrid_prefix_gmlp_rwkv7_diffattn_block'


def rms_norm(x, g, eps=NORM_EPS):
    x32 = x.astype(jnp.float32)
    y = x32 * lax.rsqrt(jnp.mean(x32 * x32, axis=-1, keepdims=True) + eps)
    return (y * g.astype(jnp.float32)).astype(x.dtype)


def modulate(h, shift, scale):
    return h * (1 + scale) + shift


def swiglu(h, w_in, w_out):
    gate, up = jnp.split(h @ w_in, 2, axis=-1)
    return (jax.nn.silu(gate) * up) @ w_out


def split_proj(p):
    out, start = [], 0
    for n in PROJ_SIZES:
        out.append(p[..., start:start + n])
        start += n
    return out


def conv3_centered(x, w):
    xp = jnp.pad(x, ((0, 0), (1, 1), (0, 0)))
    return xp[:, :-2] * w[0] + xp[:, 1:-1] * w[1] + xp[:, 2:] * w[2]


def rope_tables(n_tokens, dtype):
    rows = n_tokens // GRID_W
    row = jnp.repeat(jnp.arange(rows), GRID_W).astype(jnp.float32)
    col = jnp.tile(jnp.arange(GRID_W), rows).astype(jnp.float32)
    inv = 1.0 / (ROPE_BASE ** (jnp.arange(0, ROPE_AXIS_DIM, 2, dtype=jnp.float32) / ROPE_AXIS_DIM))
    ar, ac = row[:, None] * inv, col[:, None] * inv
    return tuple(t[:, None, None, :].astype(dtype) for t in (jnp.cos(ar), jnp.sin(ar), jnp.cos(ac), jnp.sin(ac)))


def _rotate(x, cos, sin):
    x1, x2 = jnp.split(x, 2, axis=-1)
    return jnp.concatenate([x1 * cos - x2 * sin, x2 * cos + x1 * sin], axis=-1)


def rope_2d(x, tabs):
    cr, sr, cc, sc = tabs
    return jnp.concatenate([_rotate(x[..., :ROPE_AXIS_DIM], cr, sr), _rotate(x[..., ROPE_AXIS_DIM:], cc, sc)], axis=-1)


def gmlp_branch(pu, pv, v_norm, ws, bs):
    bn, t, _ = pu.shape
    u = jax.nn.gelu(pu)
    v = rms_norm(jax.nn.gelu(pv).reshape(bn, t, A_GROUPS, A_GROUP_DIM), v_norm.reshape(A_GROUPS, A_GROUP_DIM))
    v = v.reshape(bn, t // CHUNK, CHUNK, A_GROUPS, A_GROUP_DIM)
    sv = jnp.einsum('gpq,bnqgc->bnpgc', ws, v) + bs.T[:, :, None]
    return u * sv.reshape(bn, t, A_WIDTH)


def rwkv_prep(p_rkv, p_w, p_a, conv_w, w0, w_up, a0, a_up, k_k, k_a):
    r, k, v = jnp.split(conv3_centered(p_rkv, conv_w), 3, axis=-1)
    bn, t, _ = r.shape
    f32 = jnp.float32
    heads = lambda z: z.astype(f32).reshape(z.shape[:-1] + (B_HEADS, B_HEAD_DIM))
    w_lo = w0 + jnp.einsum('btdr,drc->btdc', jnp.tanh(p_w.reshape(bn, t, 2, DECAY_LORA)), w_up)
    decay = jnp.exp(-jnp.exp(-jax.nn.softplus(-w_lo.astype(f32)) - 0.5))
    a = jax.nn.sigmoid((a0 + jnp.einsum('btdr,drc->btdc', p_a.reshape(bn, t, 2, ICL_LORA), a_up)).astype(f32))
    kk = heads(k * k_k)
    kk = kk * lax.rsqrt(jnp.sum(kk * kk, axis=-1, keepdims=True) + 1e-12)
    k_dir = k.astype(f32)[:, :, None, :] * (1 + (a - 1) * k_a.astype(f32))
    return (heads(r), heads(k_dir), heads(v), kk, heads(decay), heads(a))


def wkv7_scan(s0, decay, k, v, kk, a, r, reverse):
    seqs = (decay, k, v, kk, kk * a) + ((r,) if r is not None else ())
    xs = tuple(jnp.moveaxis(z, 1, 0) for z in seqs)

    def step(s, inp):
        w_t, k_t, v_t, kk_t, b_t = inp[:5]
        sa = jnp.einsum('bhvk,bhk->bhv', s, kk_t)
        s = s * w_t[:, :, None, :] - sa[..., None] * b_t[:, :, None, :] + v_t[..., None] * k_t[:, :, None, :]
        if r is None:
            return s, None
        return s, jnp.einsum('bhvk,bhk->bhv', s, inp[5])

    s, ys = lax.scan(step, s0, xs, reverse=reverse)
    return s, (None if r is None else jnp.moveaxis(ys, 0, 1))


def rwkv_bidirectional(prep_l, prep_c, ctx_out):
    r_l, kd_l, v_l, kk_l, dec_l, a_l = prep_l
    r_c, kd_c, v_c, kk_c, dec_c, a_c = prep_c
    s0 = jnp.zeros((r_l.shape[0], B_HEADS, B_HEAD_DIM, B_HEAD_DIM), jnp.float32)
    o_l, o_c = None, None
    for d in (0, 1):
        rev = d == 1
        s_c, yc_d = wkv7_scan(s0, dec_c[:, :, d], kd_c[:, :, d], v_c, kk_c, a_c[:, :, d], r_c if ctx_out else None, rev)
        _, yl_d = wkv7_scan(s_c, dec_l[:, :, d], kd_l[:, :, d], v_l, kk_l, a_l[:, :, d], r_l, rev)
        o_l = yl_d if o_l is None else o_l + yl_d
        if ctx_out:
            o_c = yc_d if o_c is None else o_c + yc_d
    return o_l, o_c


def rwkv_out(y, prep, p_g, g_up, ln_g, ln_b, r_k, out_dtype):
    r, k_dir, v = prep[0], prep[1], prep[2]
    mu = jnp.mean(y, axis=-1, keepdims=True)
    var = jnp.mean(jnp.square(y - mu), axis=-1, keepdims=True)
    y = (y - mu) * lax.rsqrt(var + LN_X_EPS) * ln_g.reshape(B_HEADS, B_HEAD_DIM) + ln_b.reshape(B_HEADS, B_HEAD_DIM)
    k_bar = 0.5 * (k_dir[:, :, 0] + k_dir[:, :, 1])
    y = y + jnp.sum(r * k_bar * r_k, axis=-1, keepdims=True) * v
    g = jax.nn.sigmoid(p_g) @ g_up
    bn, t = y.shape[:2]
    return y.reshape(bn, t, B_WIDTH).astype(out_dtype) * g


def diff_heads(p):
    q, k, v = jnp.split(p, 3, axis=-1)
    bn, t, _ = p.shape
    return (q.reshape(bn, t, C_HEADS, 2, C_HEAD_DIM), k.reshape(bn, t, C_HEADS, 2, C_HEAD_DIM),
            v.reshape(bn, t, C_HEADS, 2 * C_HEAD_DIM))


def diff_attn(q, k, v, lam):
    s = jnp.einsum('bqhjd,bkhjd->bhjqk', q, k).astype(jnp.float32) * (C_HEAD_DIM ** -0.5)
    p = jax.nn.softmax(s, axis=-1)
    attn = p[:, :, 0] - lam * p[:, :, 1]
    return jnp.einsum('bhqk,bkhe->bqhe', attn.astype(v.dtype), v)


def diff_post(o, subln, lam_init):
    return (rms_norm(o, subln) * (1.0 - lam_init)).reshape(o.shape[0], o.shape[1], C_WIDTH)


def gated_merge(ys, p_gate, b_gate, w_branch, w_out):
    gates = jnp.split(p_gate, N_BRANCH, axis=-1)
    z = jax.nn.sigmoid(gates[0] + b_gate[0]) * (ys[0] @ w_branch[0])
    for i in range(1, N_BRANCH):
        z = z + jax.nn.sigmoid(gates[i] + b_gate[i]) * (ys[i] @ w_branch[i])
    return z @ w_out


def token_mixer(h, hc, lp, lam_init, tabs, ctx_out):
    pl = split_proj(h @ lp['w_in'])
    pc = split_proj(hc @ lp['w_in'])
    ya = gmlp_branch(pl[0], pl[1], lp['gm_v_norm'], lp['gm_ws'], lp['gm_bs'])
    rw_args = (lp['rw_conv'], lp['rw_w0'], lp['rw_w_up'], lp['rw_a0'], lp['rw_a_up'], lp['rw_k_k'], lp['rw_k_a'])
    prep_l = rwkv_prep(pl[2], pl[4], pl[5], *rw_args)
    prep_c = rwkv_prep(pc[2], pc[4], pc[5], *rw_args)
    o_l, o_c = rwkv_bidirectional(prep_l, prep_c, ctx_out)
    out_args = (lp['rw_g_up'], lp['rw_ln_g'], lp['rw_ln_b'], lp['rw_r_k'])
    yb = rwkv_out(o_l, prep_l, pl[3], *out_args, h.dtype)
    lam_v = lp['da_lam'].astype(jnp.float32)
    lam = jnp.exp(jnp.sum(lam_v[0] * lam_v[1])) - jnp.exp(jnp.sum(lam_v[2] * lam_v[3])) + lam_init
    ql, kl, vl = diff_heads(pl[6])
    ql = rope_2d(rms_norm(ql, lp['da_q_norm']), tabs)
    kl = rope_2d(rms_norm(kl, lp['da_k_norm']), tabs)
    qc_raw, kc_raw, vc = diff_heads(pc[6])
    kc = rms_norm(kc_raw, lp['da_k_norm'])
    k_all = jnp.concatenate([kl, kc], axis=1)
    v_all = jnp.concatenate([vl, vc], axis=1)
    bn, t = h.shape[0], h.shape[1]
    nb = t // Q_BLOCK
    qb = jnp.moveaxis(ql.reshape(bn, nb, Q_BLOCK, C_HEADS, 2, C_HEAD_DIM), 1, 0)
    ob = lax.map(lambda qq: diff_attn(qq, k_all, v_all, lam), qb)
    o = jnp.moveaxis(ob, 0, 1).reshape(bn, t, C_HEADS, 2 * C_HEAD_DIM)
    yc = diff_post(o, lp['da_subln'], lam_init)
    y = gated_merge([ya, yb, yc], pl[7], lp['b_gate'], lp['w_branch'], lp['w_out'])
    if not ctx_out:
        return y, None
    ya_c = gmlp_branch(pc[0], pc[1], lp['gm_v_norm'], lp['gm_ws'], lp['gm_bs'])
    yb_c = rwkv_out(o_c, prep_c, pc[3], *out_args, hc.dtype)
    qc = rms_norm(qc_raw, lp['da_q_norm'])
    yc_c = diff_post(diff_attn(qc, kc, vc, lam), lp['da_subln'], lam_init)
    y_ctx = gated_merge([ya_c, yb_c, yc_c], pc[7], lp['b_gate'], lp['w_branch'], lp['w_out'])
    return y, y_ctx


def layer_forward(x, xc, mod, modc, lp, lam_init, tabs, last):
    ada = lambda m, j: m[..., j:j + 1, :]

    def half_ffn(z, m, i, w):
        hn = modulate(rms_norm(z, lp['norm_g'][i]), ada(m, 3 * i), ada(m, 3 * i + 1))
        return z + 0.5 * ada(m, 3 * i + 2) * swiglu(hn, lp['ffn_w_in'][w], lp['ffn_w_out'][w])

    x = half_ffn(x, mod, 0, 0)
    xc = half_ffn(xc, modc, 0, 0)
    h = modulate(rms_norm(x, lp['norm_g'][1]), ada(mod, 3), ada(mod, 4))
    hc = modulate(rms_norm(xc, lp['norm_g'][1]), ada(modc, 3), ada(modc, 4))
    y, yc = token_mixer(h, hc, lp, lam_init, tabs, ctx_out=not last)
    x = x + ada(mod, 5) * y
    x = half_ffn(x, mod, 2, 1)
    if not last:
        xc = xc + ada(modc, 5) * yc
        xc = half_ffn(xc, modc, 2, 1)
    return x, xc


def setup_inputs(seed: int = 0) -> dict:
    key = jax.random.key(seed)
    ks = jax.random.split(key, 32)
    nrm = lambda k, shape, s: jax.random.normal(k, shape, jnp.float32) * s
    L = DEPTH
    return {
        'x': nrm(ks[0], (BATCH, SEQ, D_MODEL), 1.0),
        'c': nrm(ks[1], (BATCH, D_MODEL), 1.0),
        'ctx': nrm(ks[2], (BATCH, CTX_LEN, D_MODEL), 1.0),
        'c_ctx': nrm(ks[3], (D_MODEL,), 1.0),
        'w_ada': nrm(ks[4], (L, D_MODEL, N_MOD * D_MODEL), 0.5 * D_MODEL ** -0.5),
        'b_ada': nrm(ks[5], (L, N_MOD * D_MODEL), 0.02),
        'norm_g': 1.0 + nrm(ks[6], (L, 3, D_MODEL), 0.02),
        'ffn_w_in': nrm(ks[7], (L, 2, D_MODEL, 2 * D_FF), D_MODEL ** -0.5),
        'ffn_w_out': nrm(ks[8], (L, 2, D_FF, D_MODEL), D_FF ** -0.5),
        'w_in': nrm(ks[9], (L, D_MODEL, PROJ_W), D_MODEL ** -0.5),
        'gm_v_norm': 1.0 + nrm(ks[10], (L, A_WIDTH), 0.02),
        'gm_ws': nrm(ks[11], (L, A_GROUPS, CHUNK, CHUNK), CHUNK ** -0.5),
        'gm_bs': 1.0 + nrm(ks[12], (L, A_GROUPS, CHUNK), 0.02),
        'rw_conv': nrm(ks[13], (L, 3, 3 * B_WIDTH), 0.3).at[:, 1].add(1.0),
        'rw_w0': nrm(ks[14], (L, 2, B_WIDTH), 0.5),
        'rw_w_up': nrm(ks[15], (L, 2, DECAY_LORA, B_WIDTH), DECAY_LORA ** -0.5),
        'rw_a0': nrm(ks[16], (L, 2, B_WIDTH), 0.5),
        'rw_a_up': nrm(ks[17], (L, 2, ICL_LORA, B_WIDTH), ICL_LORA ** -0.5),
        'rw_g_up': nrm(ks[18], (L, GATE_LORA, B_WIDTH), GATE_LORA ** -0.5),
        'rw_k_k': 1.0 + nrm(ks[19], (L, B_WIDTH), 0.1),
        'rw_k_a': 1.0 + nrm(ks[20], (L, B_WIDTH), 0.1),
        'rw_r_k': nrm(ks[21], (L, B_HEADS, B_HEAD_DIM), 0.1),
        'rw_ln_g': 1.0 + nrm(ks[22], (L, B_WIDTH), 0.02),
        'rw_ln_b': nrm(ks[23], (L, B_WIDTH), 0.02),
        'da_q_norm': 1.0 + nrm(ks[24], (L, C_HEAD_DIM), 0.02),
        'da_k_norm': 1.0 + nrm(ks[25], (L, C_HEAD_DIM), 0.02),
        'da_lam': nrm(ks[26], (L, 4, C_HEAD_DIM), 0.1),
        'da_subln': 1.0 + nrm(ks[27], (L, 2 * C_HEAD_DIM), 0.02),
        'w_branch': nrm(ks[28], (L, N_BRANCH, BRANCH_W, D_MODEL), BRANCH_W ** -0.5),
        'b_gate': nrm(ks[29], (L, N_BRANCH, D_MODEL), 0.02),
        'w_out': nrm(ks[30], (L, D_MODEL, D_MODEL), D_MODEL ** -0.5),
    }


def reference(x, c, ctx, c_ctx, w_ada, b_ada, norm_g, ffn_w_in, ffn_w_out, w_in, gm_v_norm, gm_ws, gm_bs,
              rw_conv, rw_w0, rw_w_up, rw_a0, rw_a_up, rw_g_up, rw_k_k, rw_k_a, rw_r_k, rw_ln_g, rw_ln_b,
              da_q_norm, da_k_norm, da_lam, da_subln, w_branch, b_gate, w_out):
    tabs = rope_tables(x.shape[1], x.dtype)
    xc = ctx
    for l in range(DEPTH):
        lp = dict(norm_g=norm_g[l], ffn_w_in=ffn_w_in[l], ffn_w_out=ffn_w_out[l], w_in=w_in[l],
                  gm_v_norm=gm_v_norm[l], gm_ws=gm_ws[l], gm_bs=gm_bs[l], rw_conv=rw_conv[l],
                  rw_w0=rw_w0[l], rw_w_up=rw_w_up[l], rw_a0=rw_a0[l], rw_a_up=rw_a_up[l], rw_g_up=rw_g_up[l],
                  rw_k_k=rw_k_k[l], rw_k_a=rw_k_a[l], rw_r_k=rw_r_k[l], rw_ln_g=rw_ln_g[l], rw_ln_b=rw_ln_b[l],
                  da_q_norm=da_q_norm[l], da_k_norm=da_k_norm[l], da_lam=da_lam[l], da_subln=da_subln[l],
                  w_branch=w_branch[l], b_gate=b_gate[l], w_out=w_out[l])
        mod = (jax.nn.silu(c) @ w_ada[l] + b_ada[l]).reshape(c.shape[0], N_MOD, D_MODEL)
        modc = (jax.nn.silu(c_ctx) @ w_ada[l] + b_ada[l]).reshape(N_MOD, D_MODEL)
        lam_init = 0.8 - 0.6 * math.exp(-0.3 * l)
        x, xc = layer_forward(x, xc, mod, modc, lp, lam_init, tabs, last=(l == DEPTH - 1))
    return x
```

```python
import functools
import math

import jax
import jax.numpy as jnp
from jax import lax
from jax.experimental import pallas as pl
from jax.experimental.pallas import tpu as pltpu

F32 = jnp.float32
BF16 = jnp.bfloat16
HIGHEST = lax.Precision.HIGHEST

N_MOD = 9
CHUNK = 128
A_WIDTH = 1024
A_GROUPS = 8
B_WIDTH = 1024
B_HEAD_DIM = 64
B_HEADS = 16
DECAY_LORA = 96
ICL_LORA = 96
GATE_LORA = 256
C_HEADS = 8
C_HEAD_DIM = 64
C_WIDTH = 1024
N_BRANCH = 3
GRID_W = 64
ROPE_BASE = 10000.0
ROPE_AXIS_DIM = 32
NORM_EPS = 1e-6
LN_X_EPS = 64e-5

LANES = 128
SUBLANES = 8
VMEM_LIMIT = 56 * 1024 * 1024

TM = 512
TF = 512
LORA_PAD = 128
SCAN_TS = 64


def _params(sem):
    return pltpu.CompilerParams(dimension_semantics=sem, vmem_limit_bytes=VMEM_LIMIT)


def _rms(x, eps):
    return x * lax.rsqrt(jnp.mean(x * x, axis=-1, keepdims=True) + eps)


def _split3(s):
    hi = s.astype(BF16)
    r = s - hi.astype(F32)
    mid = r.astype(BF16)
    lo = (r - mid.astype(F32)).astype(BF16)
    return hi, mid, lo


def _gsum(s, g_bf16):
    out = None
    for piece in _split3(s):
        d = jnp.dot(piece, g_bf16, preferred_element_type=F32)
        out = d if out is None else out + d
    return out


def _gelu_tanh(x):
    cdf = 0.5 * (1.0 + jnp.tanh(math.sqrt(2.0 / math.pi) * (x + 0.044715 * (x * x * x))))
    return x * cdf


def _ada_kernel(c_ref, w_ref, b_ref, o_ref):
    s = c_ref[...]
    s = s * jax.nn.sigmoid(s)
    o_ref[...] = jnp.dot(s, w_ref[...], precision=HIGHEST, preferred_element_type=F32) + b_ref[...]


def _ada(cvec, w_ada, b_ada3, l):
    d = cvec.shape[1]
    n = w_ada.shape[2]
    tn = 1024
    return pl.pallas_call(
        _ada_kernel,
        grid=(n // tn,),
        in_specs=[
            pl.BlockSpec((SUBLANES, d), lambda j: (0, 0)),
            pl.BlockSpec((None, d, tn), lambda j: (l, 0, j)),
            pl.BlockSpec((None, 1, tn), lambda j: (l, 0, j)),
        ],
        out_specs=pl.BlockSpec((SUBLANES, tn), lambda j: (0, j)),
        out_shape=jax.ShapeDtypeStruct((SUBLANES, n), F32),
        compiler_params=_params(("arbitrary",)),
        name="ada",
    )(cvec, w_ada, b_ada3)


def _ffn_kernel(x_ref, mod_ref, g_ref, win_ref, wout_ref, *rest, tf, nf, mi, emit_h):
    if emit_h:
        o_ref, h_ref, hn_sc, acc_sc = rest
    else:
        o_ref, hn_sc, acc_sc = rest
    f = pl.program_id(1)

    @pl.when(f == 0)
    def _():
        xn = _rms(x_ref[...], NORM_EPS) * g_ref[mi:mi + 1, :]
        hn = xn * (1.0 + mod_ref[3 * mi + 1:3 * mi + 2, :]) + mod_ref[3 * mi:3 * mi + 1, :]
        hn_sc[...] = hn.astype(BF16)
        acc_sc[...] = jnp.zeros_like(acc_sc)

    gu = jnp.dot(hn_sc[...], win_ref[...], preferred_element_type=F32)
    g = gu[:, :tf]
    u = gu[:, tf:]
    act = (g * jax.nn.sigmoid(g) * u).astype(BF16)
    acc_sc[...] += jnp.dot(act, wout_ref[...], preferred_element_type=F32)

    @pl.when(f == nf - 1)
    def _():
        out = x_ref[...] + 0.5 * mod_ref[3 * mi + 2:3 * mi + 3, :] * acc_sc[...]
        o_ref[...] = out
        if emit_h:
            hn = _rms(out, NORM_EPS) * g_ref[1:2, :]
            h_ref[...] = (hn * (1.0 + mod_ref[4:5, :]) + mod_ref[3:4, :]).astype(BF16)


def _ffn(xs, mod, norm_g, w_gu, w_out, mod_idx, *, n_tiles, mi, emit_h):
    m, d = xs.shape
    nf = w_out.shape[0] // TF
    out_rows = n_tiles * TM
    out_shape = [jax.ShapeDtypeStruct((out_rows, d), F32)]
    out_specs = [pl.BlockSpec((TM, d), lambda i, f: (i, 0))]
    if emit_h:
        out_shape.append(jax.ShapeDtypeStruct((out_rows, d), BF16))
        out_specs.append(pl.BlockSpec((TM, d), lambda i, f: (i, 0)))
    res = pl.pallas_call(
        functools.partial(_ffn_kernel, tf=TF, nf=nf, mi=mi, emit_h=emit_h),
        grid=(n_tiles, nf),
        in_specs=[
            pl.BlockSpec((TM, d), lambda i, f: (i, 0)),
            pl.BlockSpec((None, N_MOD, d), lambda i, f: (mod_idx(i), 0, 0)),
            pl.BlockSpec((3, d), lambda i, f: (0, 0)),
            pl.BlockSpec((d, 2 * TF), lambda i, f: (0, f)),
            pl.BlockSpec((TF, d), lambda i, f: (f, 0)),
        ],
        out_specs=out_specs,
        out_shape=out_shape,
        scratch_shapes=[pltpu.VMEM((TM, d), BF16), pltpu.VMEM((TM, d), F32)],
        compiler_params=_params(("parallel", "arbitrary")),
        name="ffn",
    )(xs, mod, norm_g, w_gu, w_out)
    return res if emit_h else res[0]


def _mm_kernel(a_ref, w_ref, o_ref):
    o_ref[...] = jnp.dot(a_ref[...], w_ref[...], preferred_element_type=F32).astype(o_ref.dtype)


def _mm(a, w, tn, out_dtype):
    m, k = a.shape
    n = w.shape[1]
    return pl.pallas_call(
        _mm_kernel,
        grid=(m // TM, n // tn),
        in_specs=[pl.BlockSpec((TM, k), lambda i, j: (i, 0)),
                  pl.BlockSpec((k, tn), lambda i, j: (0, j))],
        out_specs=pl.BlockSpec((TM, tn), lambda i, j: (i, j)),
        out_shape=jax.ShapeDtypeStruct((m, n), out_dtype),
        compiler_params=_params(("parallel", "arbitrary")),
        name="proj",
    )(a, w)


def _gmlp_kernel(h_ref, w_ref, vn_ref, ws_ref, bsb_ref, o_ref, p_sc, *, tm):
    p_sc[...] = jnp.dot(h_ref[...], w_ref[...], preferred_element_type=F32)
    gd = A_WIDTH // A_GROUPS
    for g in range(A_GROUPS):
        wsg = ws_ref[g].astype(BF16)
        bias = bsb_ref[g]
        gain = vn_ref[:, g * gd:(g + 1) * gd]
        for n in range(tm // CHUNK):
            rows = slice(n * CHUNK, (n + 1) * CHUNK)
            u = _gelu_tanh(p_sc[rows, g * gd:(g + 1) * gd])
            v = _gelu_tanh(p_sc[rows, A_WIDTH + g * gd:A_WIDTH + (g + 1) * gd])
            v = _rms(v, NORM_EPS) * gain
            sv = jnp.dot(wsg, v.astype(BF16), preferred_element_type=F32) + bias
            o_ref[rows, g * gd:(g + 1) * gd] = (u * sv).astype(o_ref.dtype)


def _gmlp(h, w_uv, vn, ws, bsb, n_tiles):
    m, d = h.shape
    return pl.pallas_call(
        functools.partial(_gmlp_kernel, tm=TM),
        grid=(n_tiles,),
        in_specs=[
            pl.BlockSpec((TM, d), lambda i: (i, 0)),
            pl.BlockSpec((d, 2 * A_WIDTH), lambda i: (0, 0)),
            pl.BlockSpec((1, A_WIDTH), lambda i: (0, 0)),
            pl.BlockSpec((A_GROUPS, CHUNK, CHUNK), lambda i: (0, 0, 0)),
            pl.BlockSpec((A_GROUPS, CHUNK, A_WIDTH // A_GROUPS), lambda i: (0, 0, 0)),
        ],
        out_specs=pl.BlockSpec((TM, A_WIDTH), lambda i: (i, 0)),
        out_shape=jax.ShapeDtypeStruct((n_tiles * TM, A_WIDTH), BF16),
        scratch_shapes=[pltpu.VMEM((TM, 2 * A_WIDTH), F32)],
        compiler_params=_params(("parallel",)),
        name="gmlp",
    )(h, w_uv, vn, ws, bsb)


def _qkv_kernel(h_ref, w_ref, gain_ref, cos_ref, sin_ref, g64_ref, o_ref, p_sc):
    j = pl.program_id(1)
    p_sc[...] = jnp.dot(h_ref[...], w_ref[...], preferred_element_type=F32)

    @pl.when(j == 2)
    def _():
        o_ref[...] = p_sc[...].astype(BF16)

    @pl.when(j < 2)
    def _():
        lane = lax.broadcasted_iota(jnp.int32, (1, LANES), 1)
        first = (lane % ROPE_AXIS_DIM) < (ROPE_AXIS_DIM // 2)
        cos = cos_ref[...]
        sin = sin_ref[...]
        gain = gain_ref[...]
        g64 = g64_ref[...]
        for hs in range(C_HEADS):
            x = p_sc[:, hs * LANES:(hs + 1) * LANES]
            ms = _gsum(x * x, g64) * (1.0 / C_HEAD_DIM)
            y = x * lax.rsqrt(ms + NORM_EPS) * gain
            half = ROPE_AXIS_DIM // 2
            rot = jnp.where(first, pltpu.roll(y, LANES - half, 1), pltpu.roll(y, half, 1))
            o_ref[:, hs * LANES:(hs + 1) * LANES] = (y * cos + rot * sin).astype(BF16)


def _qkv(h, w_qkv, gains, cos_t, sin_t, g64):
    m, d = h.shape
    return pl.pallas_call(
        _qkv_kernel,
        grid=(m // TM, 3),
        in_specs=[
            pl.BlockSpec((TM, d), lambda i, j: (i, 0)),
            pl.BlockSpec((d, C_WIDTH), lambda i, j: (0, j)),
            pl.BlockSpec((None, 1, LANES), lambda i, j: (j, 0, 0)),
            pl.BlockSpec((TM, LANES), lambda i, j: (i, 0)),
            pl.BlockSpec((TM, LANES), lambda i, j: (i, 0)),
            pl.BlockSpec((LANES, LANES), lambda i, j: (0, 0)),
        ],
        out_specs=pl.BlockSpec((None, TM, C_WIDTH), lambda i, j: (j, i, 0)),
        out_shape=jax.ShapeDtypeStruct((3, m, C_WIDTH), BF16),
        scratch_shapes=[pltpu.VMEM((TM, C_WIDTH), F32)],
        compiler_params=_params(("parallel", "arbitrary")),
        name="qkv",
    )(h, w_qkv, gains, cos_t, sin_t, g64)


def _attn_kernel(lam_ref, sub_ref, q_ref, *refs, nseg, lam_init):
    kv = refs[:2 * nseg]
    o_ref = refs[2 * nseg]
    lv = lam_ref[...]
    lam = (jnp.exp(jnp.sum(lv[0:1] * lv[1:2], axis=-1, keepdims=True))
           - jnp.exp(jnp.sum(lv[2:3] * lv[3:4], axis=-1, keepdims=True)) + lam_init)
    q = q_ref[...]
    lane = lax.broadcasted_iota(jnp.int32, (1, LANES), 1)
    zero = jnp.zeros_like(q)
    outs = []
    for j in range(2):
        in_map = (lane < C_HEAD_DIM) if j == 0 else (lane >= C_HEAD_DIM)
        qj = jnp.where(in_map, q, zero)
        ss = [lax.dot_general(qj, kv[2 * s][...], (((1,), (1,)), ((), ())),
                              preferred_element_type=F32) for s in range(nseg)]
        mx = None
        for s in ss:
            ms = jnp.max(s, axis=-1, keepdims=True)
            mx = ms if mx is None else jnp.maximum(mx, ms)
        den = None
        acc = None
        for si, s in enumerate(ss):
            e = jnp.exp(s - mx)
            ds = jnp.sum(e, axis=-1, keepdims=True)
            den = ds if den is None else den + ds
            pv = jnp.dot(e.astype(BF16), kv[2 * si + 1][...], preferred_element_type=F32)
            acc = pv if acc is None else acc + pv
        outs.append(acc / den)
    o = outs[0] - lam * outs[1]
    o = _rms(o, NORM_EPS) * sub_ref[...]
    o_ref[...] = (o * (1.0 - lam_init)).astype(o_ref.dtype)


def _attn(qkv, lam, subln, yc_prev, *, b, t, c, lam_init, ctx_queries):
    m = qkv.shape[1]
    if ctx_queries:
        tq, nq = c, 1
        q_blk = lambda bi, qi: (b * t) // c + bi
        segs = [(c, lambda bi: (b * t) // c + bi)]
    else:
        tq = 256
        nq = t // tq
        q_blk = lambda bi, qi: bi * nq + qi
        segs = [(t, lambda bi: bi), (c, lambda bi: (b * t) // c + bi)]
    in_specs = [
        pl.BlockSpec((4, C_HEAD_DIM), lambda bi, h, qi: (0, 0)),
        pl.BlockSpec((1, LANES), lambda bi, h, qi: (0, 0)),
        pl.BlockSpec((None, tq, LANES), lambda bi, h, qi: (0, q_blk(bi, qi), h)),
    ]
    args = [lam, subln, qkv]
    for ls, blk in segs:
        in_specs.append(pl.BlockSpec((None, ls, LANES), lambda bi, h, qi, blk=blk: (1, blk(bi), h)))
        in_specs.append(pl.BlockSpec((None, ls, LANES), lambda bi, h, qi, blk=blk: (2, blk(bi), h)))
        args += [qkv, qkv]
    aliases = {}
    if yc_prev is not None:
        in_specs.append(pl.BlockSpec(memory_space=pl.ANY))
        args.append(yc_prev)
        aliases = {len(args) - 1: 0}

    def body(*refs):
        n_in = 3 + 2 * len(segs)
        _attn_kernel(*refs[:n_in], refs[-1], nseg=len(segs), lam_init=lam_init)

    return pl.pallas_call(
        body,
        grid=(b, C_HEADS, nq),
        in_specs=in_specs,
        out_specs=pl.BlockSpec((tq, LANES), lambda bi, h, qi: (q_blk(bi, qi), h)),
        out_shape=jax.ShapeDtypeStruct((m, C_WIDTH), BF16),
        input_output_aliases=aliases,
        compiler_params=_params(("parallel", "parallel", "arbitrary")),
        name="diff_attn_ctx" if ctx_queries else "diff_attn",
    )(*args)


def _rwkv_prep_kernel(p_ref, prev_ref, next_ref, lo_ref, conv_ref, w0_ref, wup_ref, a0_ref, aup_ref,
                      kk_ref, ka_ref, g64_ref, r_o, v_o, kk_o, w_o, kd_o, b_o, *, tm, starts, ends):
    i = pl.program_id(0)
    is_start = functools.reduce(jnp.logical_or, [i == s for s in starts])
    is_end = functools.reduce(jnp.logical_or, [i == s for s in ends])
    row = lax.broadcasted_iota(jnp.int32, (tm, 1), 0)
    g64 = g64_ref[...]

    def conv(c0):
        cs = slice(c0, c0 + LANES)
        x = p_ref[:, cs]
        pm = jnp.where(is_start, 0.0, prev_ref[SUBLANES - 1:SUBLANES, cs])
        nx = jnp.where(is_end, 0.0, next_ref[0:1, cs])
        xm1 = jnp.where(row == 0, pm, pltpu.roll(x, 1, 0))
        xp1 = jnp.where(row == tm - 1, nx, pltpu.roll(x, tm - 1, 0))
        return xm1 * conv_ref[0:1, cs] + x * conv_ref[1:2, cs] + xp1 * conv_ref[2:3, cs]

    tw = [jnp.tanh(lo_ref[:, GATE_LORA + d * LORA_PAD:GATE_LORA + (d + 1) * LORA_PAD]) for d in range(2)]
    a_off = GATE_LORA + 2 * LORA_PAD
    pa = [lo_ref[:, a_off + d * LORA_PAD:a_off + (d + 1) * LORA_PAD] for d in range(2)]
    for s in range(B_WIDTH // LANES):
        cs = slice(s * LANES, (s + 1) * LANES)
        r_o[:, cs] = conv(s * LANES)
        k = conv(B_WIDTH + s * LANES)
        v_o[:, cs] = conv(2 * B_WIDTH + s * LANES)
        kkr = k * kk_ref[:, cs]
        kk = kkr * lax.rsqrt(_gsum(kkr * kkr, g64) + 1e-12)
        kk_o[:, cs] = kk
        for d in range(2):
            w_lo = w0_ref[d:d + 1, cs] + jnp.dot(tw[d], wup_ref[d, :, cs], precision=HIGHEST,
                                                   preferred_element_type=F32)
            w_o[d, :, cs] = jnp.exp(-jax.nn.sigmoid(w_lo) * math.exp(-0.5))
            a = jax.nn.sigmoid(a0_ref[d:d + 1, cs] + jnp.dot(pa[d], aup_ref[d, :, cs], precision=HIGHEST,
                                                             preferred_element_type=F32))
            kd_o[d, :, cs] = k * (1.0 + (a - 1.0) * ka_ref[:, cs])
            b_o[d, :, cs] = kk * a


def _rwkv_prep(p_rkv, p_lo, conv_w, w0, wup, a0, aup, k_k, k_a, g64, *, b, t, c):
    m = p_rkv.shape[0]
    tm = 256
    nt = m // tm
    lat = t // tm
    starts = [bi * lat for bi in range(b)] + [b * lat + bi * (c // tm) for bi in range(b)]
    ends = [(bi + 1) * lat - 1 for bi in range(b)] + [b * lat + (bi + 1) * (c // tm) - 1 for bi in range(b)]
    hb = tm // SUBLANES
    row1 = lambda i: (i, 0)
    full2 = lambda i: (0, 0)
    full3 = lambda i: (0, 0, 0)
    o1 = jax.ShapeDtypeStruct((m, B_WIDTH), F32)
    o2 = jax.ShapeDtypeStruct((2, m, B_WIDTH), F32)
    return pl.pallas_call(
        functools.partial(_rwkv_prep_kernel, tm=tm, starts=starts, ends=ends),
        grid=(nt,),
        in_specs=[
            pl.BlockSpec((tm, 3 * B_WIDTH), row1),
            pl.BlockSpec((SUBLANES, 3 * B_WIDTH), lambda i: (jnp.maximum(i * hb - 1, 0), 0)),
            pl.BlockSpec((SUBLANES, 3 * B_WIDTH), lambda i: (jnp.minimum((i + 1) * hb, m // SUBLANES - 1), 0)),
            pl.BlockSpec((tm, p_lo.shape[1]), row1),
            pl.BlockSpec((3, 3 * B_WIDTH), full2),
            pl.BlockSpec((2, B_WIDTH), full2),
            pl.BlockSpec((2, LORA_PAD, B_WIDTH), full3),
            pl.BlockSpec((2, B_WIDTH), full2),
            pl.BlockSpec((2, LORA_PAD, B_WIDTH), full3),
            pl.BlockSpec((1, B_WIDTH), full2),
            pl.BlockSpec((1, B_WIDTH), full2),
            pl.BlockSpec((LANES, LANES), full2),
        ],
        out_specs=[pl.BlockSpec((tm, B_WIDTH), row1)] * 3
        + [pl.BlockSpec((2, tm, B_WIDTH), lambda i: (0, i, 0))] * 3,
        out_shape=[o1, o1, o1, o2, o2, o2],
        compiler_params=_params(("parallel",)),
        name="rwkv_prep",
    )(p_rkv, p_rkv, p_rkv, p_lo, conv_w, w0, wup, a0, aup, k_k, k_a, g64)


def _scan_kernel(w_ref, kd_ref, b_ref, kk_ref, r_ref, v_ref, y_ref, s_sc, *, ts):
    nkl = s_sc.shape[0]
    nvb = s_sc.shape[1] // SUBLANES
    half = LANES // 2

    @pl.when(pl.program_id(0) == 0)
    def _():
        s_sc[...] = jnp.zeros_like(s_sc)

    def row(ref, t, kl):
        return ref[t, kl:kl + 1, :]

    def step(t, carry):
        sa = [None] * nvb
        for kl in range(nkl):
            kkrow = row(kk_ref, t, kl)
            for vb in range(nvb):
                term = s_sc[kl, vb * SUBLANES:(vb + 1) * SUBLANES, :] * kkrow
                sa[vb] = term if sa[vb] is None else sa[vb] + term
        sa = [x + pltpu.roll(x, half, 1) for x in sa]
        vv = [v_ref[t, vb * SUBLANES:(vb + 1) * SUBLANES, :] for vb in range(nvb)]
        ys = [None] * nvb
        for kl in range(nkl):
            wrow = row(w_ref, t, kl)
            brow = row(b_ref, t, kl)
            kdrow = row(kd_ref, t, kl)
            rrow = row(r_ref, t, kl)
            for vb in range(nvb):
                sl = slice(vb * SUBLANES, (vb + 1) * SUBLANES)
                sn = s_sc[kl, sl, :] * wrow - sa[vb] * brow + vv[vb] * kdrow
                s_sc[kl, sl, :] = sn
                term = sn * rrow
                ys[vb] = term if ys[vb] is None else ys[vb] + term
        for vb in range(nvb):
            y = ys[vb]
            y_ref[t, vb * SUBLANES:(vb + 1) * SUBLANES, :] = y + pltpu.roll(y, half, 1)
        return carry

    lax.fori_loop(0, ts, step, 0)


def _scan(w_c, kd_c, b_c, kk_c, r_c, v_c):
    s, nkl, _ = w_c.shape
    nv = v_c.shape[1]
    ts = SCAN_TS
    kspec = pl.BlockSpec((ts, nkl, LANES), lambda i: (i, 0, 0))
    vspec = pl.BlockSpec((ts, nv, LANES), lambda i: (i, 0, 0))
    return pl.pallas_call(
        functools.partial(_scan_kernel, ts=ts),
        grid=(s // ts,),
        in_specs=[kspec] * 5 + [vspec],
        out_specs=vspec,
        out_shape=jax.ShapeDtypeStruct((s, nv, LANES), F32),
        scratch_shapes=[pltpu.VMEM((nkl, nv, LANES), F32)],
        compiler_params=_params(("arbitrary",)),
        name="wkv7_scan",
    )(w_c, kd_c, b_c, kk_c, r_c, v_c)


def _rwkv_out_kernel(yf_ref, yb_ref, r_ref, kd_ref, v_ref, lo_ref, lng_ref, lnb_ref, rk_ref, gup_ref,
                     g64_ref, o_ref):
    g64 = g64_ref[...]
    gate = jnp.dot(jax.nn.sigmoid(lo_ref[:, :GATE_LORA]).astype(BF16), gup_ref[...],
                   preferred_element_type=F32)
    inv = 1.0 / B_HEAD_DIM
    for s in range(B_WIDTH // LANES):
        cs = slice(s * LANES, (s + 1) * LANES)
        y = yf_ref[:, cs] + yb_ref[:, cs]
        mu = _gsum(y, g64) * inv
        dlt = y - mu
        var = _gsum(dlt * dlt, g64) * inv
        yn = dlt * lax.rsqrt(var + LN_X_EPS) * lng_ref[:, cs] + lnb_ref[:, cs]
        k_bar = 0.5 * (kd_ref[0, :, cs] + kd_ref[1, :, cs])
        bonus = _gsum(r_ref[:, cs] * k_bar * rk_ref[:, cs], g64)
        yn = yn + bonus * v_ref[:, cs]
        o_ref[:, cs] = (yn * gate[:, cs]).astype(o_ref.dtype)


def _rwkv_out(yf, yb, r, kd, v, p_lo, ln_g, ln_b, r_k, g_up, g64, rows):
    tm = 256
    row1 = lambda i: (i, 0)
    full2 = lambda i: (0, 0)
    return pl.pallas_call(
        _rwkv_out_kernel,
        grid=(rows // tm,),
        in_specs=[
            pl.BlockSpec((tm, B_WIDTH), row1),
            pl.BlockSpec((tm, B_WIDTH), row1),
            pl.BlockSpec((tm, B_WIDTH), row1),
            pl.BlockSpec((2, tm, B_WIDTH), lambda i: (0, i, 0)),
            pl.BlockSpec((tm, B_WIDTH), row1),
            pl.BlockSpec((tm, p_lo.shape[1]), row1),
            pl.BlockSpec((1, B_WIDTH), full2),
            pl.BlockSpec((1, B_WIDTH), full2),
            pl.BlockSpec((1, B_WIDTH), full2),
            pl.BlockSpec((GATE_LORA, B_WIDTH), full2),
            pl.BlockSpec((LANES, LANES), full2),
        ],
        out_specs=pl.BlockSpec((tm, B_WIDTH), row1),
        out_shape=jax.ShapeDtypeStruct((rows, B_WIDTH), BF16),
        compiler_params=_params(("parallel",)),
        name="rwkv_out",
    )(yf, yb, r, kd, v, p_lo, ln_g, ln_b, r_k, g_up, g64)


def _merge_kernel(x_ref, h_ref, ya_ref, yb_ref, yc_ref, mod_ref, wg_ref, bg_ref, wb_ref, wo_ref,
                  o_ref, acc_sc, *, nz):
    j = pl.program_id(1)

    @pl.when(j == 0)
    def _():
        acc_sc[...] = jnp.zeros_like(acc_sc)

    h = h_ref[...]
    z = None
    for i, y_ref in enumerate((ya_ref, yb_ref, yc_ref)):
        gate = jax.nn.sigmoid(jnp.dot(h, wg_ref[i], preferred_element_type=F32) + bg_ref[i])
        term = gate * jnp.dot(y_ref[...], wb_ref[i], preferred_element_type=F32)
        z = term if z is None else z + term
    acc_sc[...] += jnp.dot(z.astype(BF16), wo_ref[...], preferred_element_type=F32)

    @pl.when(j == nz - 1)
    def _():
        o_ref[...] = x_ref[...] + mod_ref[5:6, :] * acc_sc[...]


def _merge(xs, h, ya, yb, yc, mod, wg, bg, wb, wo, mod_idx, n_tiles):
    d = xs.shape[1]
    tz = 256
    nz = d // tz
    rowt = lambda i, j: (i, 0)
    return pl.pallas_call(
        functools.partial(_merge_kernel, nz=nz),
        grid=(n_tiles, nz),
        in_specs=[
            pl.BlockSpec((TM, d), rowt),
            pl.BlockSpec((TM, d), rowt),
            pl.BlockSpec((TM, A_WIDTH), rowt),
            pl.BlockSpec((TM, B_WIDTH), rowt),
            pl.BlockSpec((TM, C_WIDTH), rowt),
            pl.BlockSpec((None, N_MOD, d), lambda i, j: (mod_idx(i), 0, 0)),
            pl.BlockSpec((N_BRANCH, d, tz), lambda i, j: (0, 0, j)),
            pl.BlockSpec((N_BRANCH, 1, tz), lambda i, j: (0, 0, j)),
            pl.BlockSpec((N_BRANCH, A_WIDTH, tz), lambda i, j: (0, 0, j)),
            pl.BlockSpec((tz, d), lambda i, j: (j, 0)),
        ],
        out_specs=pl.BlockSpec((TM, d), rowt),
        out_shape=jax.ShapeDtypeStruct((n_tiles * TM, d), F32),
        scratch_shapes=[pltpu.VMEM((TM, d), F32)],
        compiler_params=_params(("parallel", "arbitrary")),
        name="merge",
    )(xs, h, ya, yb, yc, mod, wg, bg, wb, wo)


def _chain_seqs(x_f, x_b, b, t, c):
    w = x_f.shape[-1]
    lat_f = x_f[:b * t].reshape(b, t, w)
    ctx_f = x_f[b * t:].reshape(b, c, w)
    lat_b = x_b[:b * t].reshape(b, t, w)
    ctx_b = x_b[b * t:].reshape(b, c, w)
    fwd = jnp.concatenate([ctx_f, lat_f], axis=1)
    bwd = jnp.concatenate([ctx_b[:, ::-1], lat_b[:, ::-1]], axis=1)
    return jnp.stack([fwd, bwd])


def _to_chain_k(x_f, x_b, b, t, c):
    s = t + c
    a = _chain_seqs(x_f, x_b, b, t, c).reshape(2, b, s, B_HEADS, 2, B_HEAD_DIM // 2)
    a = jnp.transpose(a, (2, 5, 4, 0, 1, 3))
    return a.reshape(s, B_HEAD_DIM // 2, 2 * 2 * b * B_HEADS)


def _to_chain_v(x, b, t, c):
    s = t + c
    a = _chain_seqs(x, x, b, t, c).reshape(2, b, s, B_HEADS, B_HEAD_DIM)
    a = jnp.transpose(a, (2, 4, 0, 1, 3)).reshape(s, B_HEAD_DIM, 2 * b * B_HEADS)
    return jnp.concatenate([a, a], axis=-1)


def _from_chain(y, b, t, c):
    s = t + c
    nch = 2 * b * B_HEADS
    a = y[:, :, :nch].reshape(s, B_HEAD_DIM, 2, b, B_HEADS)
    a = jnp.transpose(a, (2, 3, 0, 4, 1)).reshape(2, b, s, B_WIDTH)
    yf = jnp.concatenate([a[0, :, c:].reshape(b * t, B_WIDTH), a[0, :, :c].reshape(b * c, B_WIDTH)])
    yb = jnp.concatenate([a[1, :, c:][:, ::-1].reshape(b * t, B_WIDTH),
                          a[1, :, :c][:, ::-1].reshape(b * c, B_WIDTH)])
    return yf, yb


def _rope_tables(b, t, c):
    rows = t // GRID_W
    rowp = jnp.repeat(jnp.arange(rows), GRID_W).astype(F32)
    colp = jnp.tile(jnp.arange(GRID_W), rows).astype(F32)
    inv = 1.0 / (ROPE_BASE ** (jnp.arange(0, ROPE_AXIS_DIM, 2, dtype=F32) / ROPE_AXIS_DIM))
    ar, ac = rowp[:, None] * inv, colp[:, None] * inv
    cr, sr, cc, sc = jnp.cos(ar), jnp.sin(ar), jnp.cos(ac), jnp.sin(ac)
    cos64 = jnp.concatenate([cr, cr, cc, cc], axis=-1)
    sin64 = jnp.concatenate([-sr, sr, -sc, sc], axis=-1)
    cos_t = jnp.tile(cos64, (b, 2))
    sin_t = jnp.tile(sin64, (b, 2))
    cos_t = jnp.concatenate([cos_t, jnp.ones((b * c, LANES), F32)])
    sin_t = jnp.concatenate([sin_t, jnp.zeros((b * c, LANES), F32)])
    return cos_t, sin_t


def _pad_lora_cols(w, n):
    d = w.shape[0]
    w = w.reshape(d, n, -1)
    w = jnp.pad(w, ((0, 0), (0, 0), (0, LORA_PAD - w.shape[-1])))
    return w.reshape(d, n * LORA_PAD)


def kernel(x, c, ctx, c_ctx, w_ada, b_ada, norm_g, ffn_w_in, ffn_w_out, w_in, gm_v_norm, gm_ws, gm_bs,
           rw_conv, rw_w0, rw_w_up, rw_a0, rw_a_up, rw_g_up, rw_k_k, rw_k_a, rw_r_k, rw_ln_g, rw_ln_b,
           da_q_norm, da_k_norm, da_lam, da_subln, w_branch, b_gate, w_out):
    b, t, d = x.shape
    cl = ctx.shape[1]
    depth = w_ada.shape[0]
    d_ff = ffn_w_out.shape[2]
    assert b * cl == TM and t % TM == 0 and cl % 256 == 0
    n_lat = (b * t) // TM
    n_all = n_lat + 1
    tiles_per_batch = t // TM
    mod_idx = lambda i: jnp.where(i < n_lat, i // tiles_per_batch, b)

    xs = jnp.concatenate([x.reshape(b * t, d), ctx.reshape(b * cl, d)])
    cvec = jnp.zeros((SUBLANES, d), F32).at[:b].set(c).at[b].set(c_ctx)
    b_ada3 = b_ada.reshape(depth, 1, N_MOD * d)
    cos_t, sin_t = _rope_tables(b, t, cl)
    lane = jnp.arange(LANES)
    g64 = (lane[:, None] // C_HEAD_DIM == lane[None, :] // C_HEAD_DIM).astype(BF16)
    ffp = -(-d_ff // TF) * TF
    nf = ffp // TF

    o = 0
    offs = []
    for n in (A_WIDTH, A_WIDTH, 3 * B_WIDTH, GATE_LORA, 2 * DECAY_LORA, 2 * ICL_LORA, 3 * C_WIDTH, N_BRANCH * d):
        offs.append((o, o + n))
        o += n

    for l in range(depth):
        last = l == depth - 1
        lam_init = 0.8 - 0.6 * math.exp(-0.3 * l)
        mod = _ada(cvec, w_ada, b_ada3, l)[:b + 1].reshape(b + 1, N_MOD, d)

        def ffn_weights(w):
            wi = ffn_w_in[l, w]
            gate = jnp.pad(wi[:, :d_ff], ((0, 0), (0, ffp - d_ff))).reshape(d, nf, 1, TF)
            up = jnp.pad(wi[:, d_ff:], ((0, 0), (0, ffp - d_ff))).reshape(d, nf, 1, TF)
            w_gu = jnp.concatenate([gate, up], axis=2).reshape(d, 2 * ffp).astype(BF16)
            wo = jnp.pad(ffn_w_out[l, w], ((0, ffp - d_ff), (0, 0))).astype(BF16)
            return w_gu, wo

        w_gu0, w_o0 = ffn_weights(0)
        xs, h = _ffn(xs, mod, norm_g[l], w_gu0, w_o0, mod_idx, n_tiles=n_all, mi=0, emit_h=True)

        wl = w_in[l]
        sl = lambda i: wl[:, offs[i][0]:offs[i][1]]
        n_mix = n_lat if last else n_all
        rows_mix = n_mix * TM

        w_uv = jnp.concatenate([sl(0), sl(1)], axis=1).astype(BF16)
        bsb = jnp.broadcast_to(gm_bs[l][:, :, None], (A_GROUPS, CHUNK, A_WIDTH // A_GROUPS))
        ya = _gmlp(h, w_uv, gm_v_norm[l].reshape(1, A_WIDTH), gm_ws[l], bsb, n_mix)

        w_lo = jnp.concatenate([sl(3), _pad_lora_cols(sl(4), 2), _pad_lora_cols(sl(5), 2)], axis=1).astype(BF16)
        p_rkv = _mm(h, sl(2).astype(BF16), 1024, F32)
        p_lo = _mm(h, w_lo, w_lo.shape[1], F32)
        pad_up = lambda w: jnp.pad(w, ((0, 0), (0, LORA_PAD - w.shape[1]), (0, 0)))
        r_t, v_t, kk_t, w_t, kd_t, b_t = _rwkv_prep(
            p_rkv, p_lo, rw_conv[l], rw_w0[l], pad_up(rw_w_up[l]), rw_a0[l], pad_up(rw_a_up[l]),
            rw_k_k[l].reshape(1, B_WIDTH), rw_k_a[l].reshape(1, B_WIDTH), g64, b=b, t=t, c=cl)
        y_c = _scan(_to_chain_k(w_t[0], w_t[1], b, t, cl), _to_chain_k(kd_t[0], kd_t[1], b, t, cl),
                    _to_chain_k(b_t[0], b_t[1], b, t, cl), _to_chain_k(kk_t, kk_t, b, t, cl),
                    _to_chain_k(r_t, r_t, b, t, cl), _to_chain_v(v_t, b, t, cl))
        yf, ybk = _from_chain(y_c, b, t, cl)
        yb = _rwkv_out(yf, ybk, r_t, kd_t, v_t, p_lo, rw_ln_g[l].reshape(1, B_WIDTH),
                       rw_ln_b[l].reshape(1, B_WIDTH), rw_r_k[l].reshape(1, B_WIDTH),
                       rw_g_up[l].astype(BF16), g64, rows_mix)

        gains = jnp.stack([jnp.tile(da_q_norm[l], 2) * (C_HEAD_DIM ** -0.5), jnp.tile(da_k_norm[l], 2),
                           jnp.ones((LANES,), F32)]).reshape(3, 1, LANES)
        qkv = _qkv(h, sl(6).astype(BF16), gains, cos_t, sin_t, g64)
        sub = da_subln[l].reshape(1, LANES)
        yc = _attn(qkv, da_lam[l], sub, None, b=b, t=t, c=cl, lam_init=lam_init, ctx_queries=False)
        if not last:
            yc = _attn(qkv, da_lam[l], sub, yc, b=b, t=t, c=cl, lam_init=lam_init, ctx_queries=True)

        wg = jnp.transpose(sl(7).reshape(d, N_BRANCH, d), (1, 0, 2)).astype(BF16)
        xs = _merge(xs, h, ya, yb, yc, mod, wg, b_gate[l].reshape(N_BRANCH, 1, d),
                    w_branch[l].astype(BF16), w_out[l].astype(BF16), mod_idx, n_mix)

        w_gu1, w_o1 = ffn_weights(1)
        xs = _ffn(xs, mod, norm_g[l], w_gu1, w_o1, mod_idx, n_tiles=n_mix, mi=2, emit_h=False)

    return xs[:b * t].reshape(b, t, d)
```

```python
import functools
import math

import jax
import jax.numpy as jnp
from jax import lax
from jax.experimental import pallas as pl
from jax.experimental.pallas import tpu as pltpu

F32 = jnp.float32
BF16 = jnp.bfloat16
HIGHEST = lax.Precision.HIGHEST

N_MOD = 9
CHUNK = 128
A_WIDTH = 1024
A_GROUPS = 8
B_WIDTH = 1024
B_HEAD_DIM = 64
B_HEADS = 16
DECAY_LORA = 96
ICL_LORA = 96
GATE_LORA = 256
C_HEADS = 8
C_HEAD_DIM = 64
C_WIDTH = 1024
N_BRANCH = 3
GRID_W = 64
ROPE_BASE = 10000.0
ROPE_AXIS_DIM = 32
NORM_EPS = 1e-6
LN_X_EPS = 64e-5

LANES = 128
SUBLANES = 8
VMEM_LIMIT = 56 * 1024 * 1024

TM = 512
TF = 512
LORA_PAD = 128
SCAN_TS = 64


def _params(sem):
    return pltpu.CompilerParams(dimension_semantics=sem, vmem_limit_bytes=VMEM_LIMIT)


def _rms(x, eps):
    return x * lax.rsqrt(jnp.mean(x * x, axis=-1, keepdims=True) + eps)


def _split3(s):
    hi = s.astype(BF16)
    r = s - hi.astype(F32)
    mid = r.astype(BF16)
    lo = (r - mid.astype(F32)).astype(BF16)
    return hi, mid, lo


def _gsum(s, g_bf16):
    out = None
    for piece in _split3(s):
        d = jnp.dot(piece, g_bf16, preferred_element_type=F32)
        out = d if out is None else out + d
    return out


def _gelu_tanh(x):
    cdf = 0.5 * (1.0 + jnp.tanh(math.sqrt(2.0 / math.pi) * (x + 0.044715 * (x * x * x))))
    return x * cdf


def _ada_kernel(c_ref, w_ref, b_ref, o_ref):
    s = c_ref[...]
    s = s * jax.nn.sigmoid(s)
    o_ref[...] = jnp.dot(s, w_ref[...], precision=HIGHEST, preferred_element_type=F32) + b_ref[...]


def _ada(cvec, w_ada, b_ada3, l):
    d = cvec.shape[1]
    n = w_ada.shape[2]
    tn = 1024
    return pl.pallas_call(
        _ada_kernel,
        grid=(n // tn,),
        in_specs=[
            pl.BlockSpec((SUBLANES, d), lambda j: (0, 0)),
            pl.BlockSpec((None, d, tn), lambda j: (l, 0, j)),
            pl.BlockSpec((None, 1, tn), lambda j: (l, 0, j)),
        ],
        out_specs=pl.BlockSpec((SUBLANES, tn), lambda j: (0, j)),
        out_shape=jax.ShapeDtypeStruct((SUBLANES, n), F32),
        compiler_params=_params(("arbitrary",)),
        name="ada",
    )(cvec, w_ada, b_ada3)


def _ffn_kernel(x_ref, mod_ref, g_ref, win_ref, wout_ref, *rest, tf, nf, mi, emit_h):
    if emit_h:
        o_ref, h_ref, hn_sc, acc_sc = rest
    else:
        o_ref, hn_sc, acc_sc = rest
    f = pl.program_id(1)

    @pl.when(f == 0)
    def _():
        xn = _rms(x_ref[...], NORM_EPS) * g_ref[mi:mi + 1, :]
        hn = xn * (1.0 + mod_ref[3 * mi + 1:3 * mi + 2, :]) + mod_ref[3 * mi:3 * mi + 1, :]
        hn_sc[...] = hn.astype(BF16)
        acc_sc[...] = jnp.zeros_like(acc_sc)

    hn = hn_sc[...]
    g = jnp.dot(hn, win_ref[0], preferred_element_type=F32)
    u = jnp.dot(hn, win_ref[1], preferred_element_type=F32)
    act = (g * jax.nn.sigmoid(g) * u).astype(BF16)
    acc_sc[...] += jnp.dot(act, wout_ref[...], preferred_element_type=F32)

    @pl.when(f == nf - 1)
    def _():
        out = x_ref[...] + 0.5 * mod_ref[3 * mi + 2:3 * mi + 3, :] * acc_sc[...]
        o_ref[...] = out
        if emit_h:
            hn = _rms(out, NORM_EPS) * g_ref[1:2, :]
            h_ref[...] = (hn * (1.0 + mod_ref[4:5, :]) + mod_ref[3:4, :]).astype(BF16)


def _ffn(xs, mod, norm_g, w_gu, w_out, mod_idx, *, n_tiles, mi, emit_h):
    m, d = xs.shape
    nf = w_out.shape[0] // TF
    out_rows = n_tiles * TM
    out_shape = [jax.ShapeDtypeStruct((out_rows, d), F32)]
    out_specs = [pl.BlockSpec((TM, d), lambda i, f: (i, 0))]
    if emit_h:
        out_shape.append(jax.ShapeDtypeStruct((out_rows, d), BF16))
        out_specs.append(pl.BlockSpec((TM, d), lambda i, f: (i, 0)))
    res = pl.pallas_call(
        functools.partial(_ffn_kernel, tf=TF, nf=nf, mi=mi, emit_h=emit_h),
        grid=(n_tiles, nf),
        in_specs=[
            pl.BlockSpec((TM, d), lambda i, f: (i, 0)),
            pl.BlockSpec((None, N_MOD, d), lambda i, f: (mod_idx(i), 0, 0)),
            pl.BlockSpec((3, d), lambda i, f: (0, 0)),
            pl.BlockSpec((2, d, TF), lambda i, f: (0, 0, f)),
            pl.BlockSpec((TF, d), lambda i, f: (f, 0)),
        ],
        out_specs=out_specs,
        out_shape=out_shape,
        scratch_shapes=[pltpu.VMEM((TM, d), BF16), pltpu.VMEM((TM, d), F32)],
        compiler_params=_params(("parallel", "arbitrary")),
        name="ffn",
    )(xs, mod, norm_g, w_gu, w_out)
    return res if emit_h else res[0]


def _gmlp_kernel(h_ref, w_ref, vn_ref, ws_ref, bsb_ref, o_ref, p_sc, *, tm):
    p_sc[...] = jnp.dot(h_ref[...], w_ref[...], preferred_element_type=F32)
    gd = A_WIDTH // A_GROUPS
    for g in range(A_GROUPS):
        wsg = ws_ref[g].astype(BF16)
        bias = bsb_ref[g]
        gain = vn_ref[:, g * gd:(g + 1) * gd]
        for n in range(tm // CHUNK):
            rows = slice(n * CHUNK, (n + 1) * CHUNK)
            u = _gelu_tanh(p_sc[rows, g * gd:(g + 1) * gd])
            v = _gelu_tanh(p_sc[rows, A_WIDTH + g * gd:A_WIDTH + (g + 1) * gd])
            v = _rms(v, NORM_EPS) * gain
            sv = jnp.dot(wsg, v.astype(BF16), preferred_element_type=F32) + bias
            o_ref[rows, g * gd:(g + 1) * gd] = (u * sv).astype(o_ref.dtype)


def _gmlp(h, w_uv, vn, ws, bsb, n_tiles):
    m, d = h.shape
    return pl.pallas_call(
        functools.partial(_gmlp_kernel, tm=TM),
        grid=(n_tiles,),
        in_specs=[
            pl.BlockSpec((TM, d), lambda i: (i, 0)),
            pl.BlockSpec((d, 2 * A_WIDTH), lambda i: (0, 0)),
            pl.BlockSpec((1, A_WIDTH), lambda i: (0, 0)),
            pl.BlockSpec((A_GROUPS, CHUNK, CHUNK), lambda i: (0, 0, 0)),
            pl.BlockSpec((A_GROUPS, CHUNK, A_WIDTH // A_GROUPS), lambda i: (0, 0, 0)),
        ],
        out_specs=pl.BlockSpec((TM, A_WIDTH), lambda i: (i, 0)),
        out_shape=jax.ShapeDtypeStruct((n_tiles * TM, A_WIDTH), BF16),
        scratch_shapes=[pltpu.VMEM((TM, 2 * A_WIDTH), F32)],
        compiler_params=_params(("parallel",)),
        name="gmlp",
    )(h, w_uv, vn, ws, bsb)


def _qkv_kernel(h_ref, w_ref, gain_ref, cos_ref, sin_ref, g64_ref, o_ref, p_sc):
    j = pl.program_id(1)
    p_sc[...] = jnp.dot(h_ref[...], w_ref[...], preferred_element_type=F32)

    @pl.when(j == 2)
    def _():
        o_ref[...] = p_sc[...].astype(BF16)

    @pl.when(j < 2)
    def _():
        lane = lax.broadcasted_iota(jnp.int32, (1, LANES), 1)
        first = (lane % ROPE_AXIS_DIM) < (ROPE_AXIS_DIM // 2)
        cos = cos_ref[...]
        sin = sin_ref[...]
        gain = gain_ref[...]
        g64 = g64_ref[...]
        for hs in range(C_HEADS):
            x = p_sc[:, hs * LANES:(hs + 1) * LANES]
            ms = _gsum(x * x, g64) * (1.0 / C_HEAD_DIM)
            y = x * lax.rsqrt(ms + NORM_EPS) * gain
            half = ROPE_AXIS_DIM // 2
            rot = jnp.where(first, pltpu.roll(y, LANES - half, 1), pltpu.roll(y, half, 1))
            o_ref[:, hs * LANES:(hs + 1) * LANES] = (y * cos + rot * sin).astype(BF16)


def _qkv(h, w_qkv, gains, cos_t, sin_t, g64):
    m, d = h.shape
    return pl.pallas_call(
        _qkv_kernel,
        grid=(m // TM, 3),
        in_specs=[
            pl.BlockSpec((TM, d), lambda i, j: (i, 0)),
            pl.BlockSpec((d, C_WIDTH), lambda i, j: (0, j)),
            pl.BlockSpec((None, 1, LANES), lambda i, j: (j, 0, 0)),
            pl.BlockSpec((TM, LANES), lambda i, j: (i, 0)),
            pl.BlockSpec((TM, LANES), lambda i, j: (i, 0)),
            pl.BlockSpec((LANES, LANES), lambda i, j: (0, 0)),
        ],
        out_specs=pl.BlockSpec((None, TM, C_WIDTH), lambda i, j: (j, i, 0)),
        out_shape=jax.ShapeDtypeStruct((3, m, C_WIDTH), BF16),
        scratch_shapes=[pltpu.VMEM((TM, C_WIDTH), F32)],
        compiler_params=_params(("parallel", "arbitrary")),
        name="qkv",
    )(h, w_qkv, gains, cos_t, sin_t, g64)


def _attn_kernel(lam_ref, sub_ref, q_ref, *refs, nseg, lam_init):
    kv = refs[:2 * nseg]
    o_ref = refs[2 * nseg]
    lv = lam_ref[...]
    lam = (jnp.exp(jnp.sum(lv[0:1] * lv[1:2], axis=-1, keepdims=True))
           - jnp.exp(jnp.sum(lv[2:3] * lv[3:4], axis=-1, keepdims=True)) + lam_init)
    q = q_ref[...]
    lane = lax.broadcasted_iota(jnp.int32, (1, LANES), 1)
    zero = jnp.zeros_like(q)
    outs = []
    for j in range(2):
        in_map = (lane < C_HEAD_DIM) if j == 0 else (lane >= C_HEAD_DIM)
        qj = jnp.where(in_map, q, zero)
        ss = [lax.dot_general(qj, kv[2 * s][...], (((1,), (1,)), ((), ())),
                              preferred_element_type=F32) for s in range(nseg)]
        mx = None
        for s in ss:
            ms = jnp.max(s, axis=-1, keepdims=True)
            mx = ms if mx is None else jnp.maximum(mx, ms)
        den = None
        acc = None
        for si, s in enumerate(ss):
            e = jnp.exp(s - mx)
            ds = jnp.sum(e, axis=-1, keepdims=True)
            den = ds if den is None else den + ds
            pv = jnp.dot(e.astype(BF16), kv[2 * si + 1][...], preferred_element_type=F32)
            acc = pv if acc is None else acc + pv
        outs.append(acc / den)
    o = outs[0] - lam * outs[1]
    o = _rms(o, NORM_EPS) * sub_ref[...]
    o_ref[...] = (o * (1.0 - lam_init)).astype(o_ref.dtype)


def _attn(qkv, lam, subln, yc_prev, *, b, t, c, lam_init, ctx_queries):
    m = qkv.shape[1]
    if ctx_queries:
        tq, nq = c, 1
        q_blk = lambda bi, qi: (b * t) // c + bi
        segs = [(c, lambda bi: (b * t) // c + bi)]
    else:
        tq = 256
        nq = t // tq
        q_blk = lambda bi, qi: bi * nq + qi
        segs = [(t, lambda bi: bi), (c, lambda bi: (b * t) // c + bi)]
    in_specs = [
        pl.BlockSpec((4, C_HEAD_DIM), lambda bi, h, qi: (0, 0)),
        pl.BlockSpec((1, LANES), lambda bi, h, qi: (0, 0)),
        pl.BlockSpec((None, tq, LANES), lambda bi, h, qi: (0, q_blk(bi, qi), h)),
    ]
    args = [lam, subln, qkv]
    for ls, blk in segs:
        in_specs.append(pl.BlockSpec((None, ls, LANES), lambda bi, h, qi, blk=blk: (1, blk(bi), h)))
        in_specs.append(pl.BlockSpec((None, ls, LANES), lambda bi, h, qi, blk=blk: (2, blk(bi), h)))
        args += [qkv, qkv]
    aliases = {}
    if yc_prev is not None:
        in_specs.append(pl.BlockSpec(memory_space=pl.ANY))
        args.append(yc_prev)
        aliases = {len(args) - 1: 0}

    def body(*refs):
        n_in = 3 + 2 * len(segs)
        _attn_kernel(*refs[:n_in], refs[-1], nseg=len(segs), lam_init=lam_init)

    return pl.pallas_call(
        body,
        grid=(b, C_HEADS, nq),
        in_specs=in_specs,
        out_specs=pl.BlockSpec((tq, LANES), lambda bi, h, qi: (q_blk(bi, qi), h)),
        out_shape=jax.ShapeDtypeStruct((m, C_WIDTH), BF16),
        input_output_aliases=aliases,
        compiler_params=_params(("parallel", "parallel", "arbitrary")),
        name="diff_attn_ctx" if ctx_queries else "diff_attn",
    )(*args)


KQ = 4
NKL = B_HEAD_DIM // KQ
MERGED = 2 * B_WIDTH
BH = 2 * B_HEADS
LORA_MERGED = 2 * GATE_LORA + 8 * LORA_PAD
PROJ_MERGED = 3 * MERGED + LORA_MERGED


def _rkv_proj_kernel(h0_ref, h1_ref, w_ref, o_ref):
    o_ref[...] = (jnp.dot(h0_ref[...], w_ref[0], preferred_element_type=F32)
                  + jnp.dot(h1_ref[...], w_ref[1], preferred_element_type=F32))


def _rkv_proj(h, wz, *, t, c):
    d = h.shape[1]
    tm = 256
    tn = 1536
    s = t + c
    nlat = t // tm
    hblk = lambda bi, i: jnp.where(i < nlat, bi * nlat + i, (2 * t + bi * c) // tm + (i - nlat))
    return pl.pallas_call(
        _rkv_proj_kernel,
        grid=(PROJ_MERGED // tn, s // tm),
        in_specs=[
            pl.BlockSpec((tm, d), lambda j, i: (hblk(0, i), 0)),
            pl.BlockSpec((tm, d), lambda j, i: (hblk(1, i), 0)),
            pl.BlockSpec((2, d, tn), lambda j, i: (0, 0, j)),
        ],
        out_specs=pl.BlockSpec((tm, tn), lambda j, i: (i, j)),
        out_shape=jax.ShapeDtypeStruct((s, PROJ_MERGED), F32),
        compiler_params=_params(("parallel", "arbitrary")),
        name="rkv_proj",
    )(h, h, wz)


def _rwkv_prep_kernel(p_ref, prev_ref, next_ref, lo_ref, conv_ref, w0_ref, wup_ref, a0_ref, aup_ref,
                      kkg_ref, ka_ref, g32_ref, rep_ref,
                      r_o, kk_o, w_o, kd_o, b_o, vrep_o, v_o, k_sc, *, tm, starts, ends):
    i = pl.program_id(0)
    is_start = functools.reduce(jnp.logical_or, [i == s for s in starts])
    is_end = functools.reduce(jnp.logical_or, [i == s for s in ends])
    row = lax.broadcasted_iota(jnp.int32, (tm, 1), 0)

    def conv(c0):
        cs = slice(c0, c0 + LANES)
        x = p_ref[:, cs]
        pm = jnp.where(is_start, 0.0, prev_ref[SUBLANES - 1:SUBLANES, cs])
        nx = jnp.where(is_end, 0.0, next_ref[0:1, cs])
        xm1 = jnp.where(row == 0, pm, pltpu.roll(x, 1, 0))
        xp1 = jnp.where(row == tm - 1, nx, pltpu.roll(x, tm - 1, 0))
        return xm1 * conv_ref[0:1, cs] + x * conv_ref[1:2, cs] + xp1 * conv_ref[2:3, cs]

    ss = None
    for m in range(NKL):
        cs = slice(m * LANES, (m + 1) * LANES)
        r_o[:, m, :] = conv(m * LANES)
        k = conv(MERGED + m * LANES)
        k_sc[:, cs] = k
        kkr = k * kkg_ref[:, cs]
        ss = kkr * kkr if ss is None else ss + kkr * kkr
    rs = lax.rsqrt(_gsum(ss, g32_ref[...]) + 1e-12)

    w_off = 2 * GATE_LORA
    a_off = w_off + 4 * LORA_PAD
    tw = [jnp.tanh(lo_ref[:, w_off + 2 * d * LORA_PAD:w_off + 2 * (d + 1) * LORA_PAD]) for d in range(2)]
    pa = [lo_ref[:, a_off + 2 * d * LORA_PAD:a_off + 2 * (d + 1) * LORA_PAD] for d in range(2)]
    for m in range(NKL):
        cs = slice(m * LANES, (m + 1) * LANES)
        k = k_sc[:, cs]
        kk = k * kkg_ref[:, cs] * rs
        kk_o[:, m, :] = kk
        for d in range(2):
            w_lo = w0_ref[d:d + 1, cs] + jnp.dot(tw[d], wup_ref[d, :, cs], precision=HIGHEST,
                                                   preferred_element_type=F32)
            w_o[d, :, m, :] = jnp.exp(-jax.nn.sigmoid(w_lo) * math.exp(-0.5))
            a = jax.nn.sigmoid(a0_ref[d:d + 1, cs] + jnp.dot(pa[d], aup_ref[d, :, cs], precision=HIGHEST,
                                                             preferred_element_type=F32))
            kd_o[d, :, m, :] = k * (1.0 + (a - 1.0) * ka_ref[:, cs])
            b_o[d, :, m, :] = kk * a

    per_slab = LANES // BH
    for j in range(MERGED // LANES):
        vs = conv(2 * MERGED + j * LANES)
        v_o[:, j * LANES:(j + 1) * LANES] = vs
        for vi in range(per_slab):
            vrep_o[:, per_slab * j + vi, :] = _gsum(vs, rep_ref[vi])


def _rwkv_prep(p, conv_m, w0_m, wup_m, a0_m, aup_m, kk_m, ka_m, g32, rep, *, t, c):
    s = t + c
    tm = 128
    nt = s // tm
    starts = [0, t // tm]
    ends = [t // tm - 1, nt - 1]
    hb = tm // SUBLANES
    rkv_w = 3 * MERGED
    full2 = lambda i: (0, 0)
    full3 = lambda i: (0, 0, 0)
    k3 = jax.ShapeDtypeStruct((s, NKL, LANES), F32)
    k4 = jax.ShapeDtypeStruct((2, s, NKL, LANES), F32)
    k3s = pl.BlockSpec((tm, NKL, LANES), lambda i: (i, 0, 0))
    k4s = pl.BlockSpec((2, tm, NKL, LANES), lambda i: (0, i, 0, 0))
    return pl.pallas_call(
        functools.partial(_rwkv_prep_kernel, tm=tm, starts=starts, ends=ends),
        grid=(nt,),
        in_specs=[
            pl.BlockSpec((tm, rkv_w), lambda i: (i, 0)),
            pl.BlockSpec((SUBLANES, rkv_w), lambda i: (jnp.maximum(i * hb - 1, 0), 0)),
            pl.BlockSpec((SUBLANES, rkv_w), lambda i: (jnp.minimum((i + 1) * hb, s // SUBLANES - 1), 0)),
            pl.BlockSpec((tm, LORA_MERGED), lambda i: (i, rkv_w // LORA_MERGED)),
            pl.BlockSpec((3, rkv_w), full2),
            pl.BlockSpec((2, MERGED), full2),
            pl.BlockSpec((2, 2 * LORA_PAD, MERGED), full3),
            pl.BlockSpec((2, MERGED), full2),
            pl.BlockSpec((2, 2 * LORA_PAD, MERGED), full3),
            pl.BlockSpec((1, MERGED), full2),
            pl.BlockSpec((1, MERGED), full2),
            pl.BlockSpec((LANES, LANES), full2),
            pl.BlockSpec((LANES // BH, LANES, LANES), full3),
        ],
        out_specs=[k3s, k3s, k4s, k4s, k4s,
                   pl.BlockSpec((tm, B_HEAD_DIM, LANES), lambda i: (i, 0, 0)),
                   pl.BlockSpec((tm, MERGED), lambda i: (i, 0))],
        out_shape=[k3, k3, k4, k4, k4, jax.ShapeDtypeStruct((s, B_HEAD_DIM, LANES), F32),
                   jax.ShapeDtypeStruct((s, MERGED), F32)],
        scratch_shapes=[pltpu.VMEM((tm, MERGED), F32)],
        compiler_params=_params(("parallel",)),
        name="rwkv_prep",
    )(p, p, p, p, conv_m, w0_m, wup_m, a0_m, aup_m, kk_m, ka_m, g32, rep)


def _scan_kernel(*refs, ts):
    ins, (yf_ref, yb_ref, s_sc) = refs[:12], refs[12:]
    streams = [ins[:6] + (yf_ref,), ins[6:] + (yb_ref,)]
    nkl = s_sc.shape[1]
    nvb = s_sc.shape[2] // SUBLANES

    @pl.when(pl.program_id(0) == 0)
    def _():
        s_sc[...] = jnp.zeros_like(s_sc)

    def vsl(vb):
        return slice(vb * SUBLANES, (vb + 1) * SUBLANES)

    def step(j, carry):
        ts_of = (j, ts - 1 - j)
        sas = []
        for d, (w_ref, kd_ref, b_ref, kk_ref, r_ref, v_ref, y_ref) in enumerate(streams):
            t = ts_of[d]
            sa = [None] * nvb
            for kl in range(nkl):
                kkrow = kk_ref[t, kl:kl + 1, :]
                for vb in range(nvb):
                    term = s_sc[d, kl, vsl(vb), :] * kkrow
                    sa[vb] = term if sa[vb] is None else sa[vb] + term
            sas.append(sa)
        for d in range(2):
            red = []
            for x in sas[d]:
                x = x + pltpu.roll(x, BH, 1)
                red.append(x + pltpu.roll(x, 2 * BH, 1))
            sas[d] = red
        for d, (w_ref, kd_ref, b_ref, kk_ref, r_ref, v_ref, y_ref) in enumerate(streams):
            t = ts_of[d]
            sa = sas[d]
            ys = [None] * nvb
            for kl in range(nkl):
                wrow = w_ref[t, kl:kl + 1, :]
                brow = b_ref[t, kl:kl + 1, :]
                kdrow = kd_ref[t, kl:kl + 1, :]
                rrow = r_ref[t, kl:kl + 1, :]
                for vb in range(nvb):
                    sn = s_sc[d, kl, vsl(vb), :] * wrow - sa[vb] * brow + v_ref[t, vsl(vb), :] * kdrow
                    s_sc[d, kl, vsl(vb), :] = sn
                    term = sn * rrow
                    ys[vb] = term if ys[vb] is None else ys[vb] + term
            for vb in range(nvb):
                y_ref[t, vsl(vb), :] = ys[vb]
        return carry

    lax.fori_loop(0, ts, step, 0)


def _scan(w4, kd4, b4, kk3, r3, vrep, *, t, c):
    s = t + c
    ts = SCAN_TS
    nctx = c // ts
    nlat = t // ts
    fblk = lambda g: jnp.where(g < nctx, nlat + g, g - nctx)
    bblk = lambda g: jnp.where(g < nctx, nlat + (nctx - 1 - g), nlat - 1 - (g - nctx))
    in_specs = []
    args = []
    for d, blk in enumerate((fblk, bblk)):
        for a in (w4, kd4, b4):
            in_specs.append(pl.BlockSpec((None, ts, NKL, LANES), lambda g, d=d, blk=blk: (d, blk(g), 0, 0)))
            args.append(a)
        for a in (kk3, r3):
            in_specs.append(pl.BlockSpec((ts, NKL, LANES), lambda g, blk=blk: (blk(g), 0, 0)))
            args.append(a)
        in_specs.append(pl.BlockSpec((ts, B_HEAD_DIM, LANES), lambda g, blk=blk: (blk(g), 0, 0)))
        args.append(vrep)
    yshape = jax.ShapeDtypeStruct((s, B_HEAD_DIM, LANES), F32)
    return pl.pallas_call(
        functools.partial(_scan_kernel, ts=ts),
        grid=(s // ts,),
        in_specs=in_specs,
        out_specs=[pl.BlockSpec((ts, B_HEAD_DIM, LANES), lambda g: (fblk(g), 0, 0)),
                   pl.BlockSpec((ts, B_HEAD_DIM, LANES), lambda g: (bblk(g), 0, 0))],
        out_shape=[yshape, yshape],
        scratch_shapes=[pltpu.VMEM((2, NKL, B_HEAD_DIM, LANES), F32)],
        compiler_params=_params(("arbitrary",)),
        name="wkv7_scan",
    )(*args)


def _rwkv_out_kernel(yf_ref, yb_ref, r_ref, kd_ref, v_ref, pg_ref, lng_ref, lnb_ref, rk_ref, gup_ref,
                     selk_ref, selv_ref, selb_ref, g16_ref, o_ref):
    gate = jnp.dot(jax.nn.sigmoid(pg_ref[...]).astype(BF16), gup_ref[...], preferred_element_type=F32)
    acc = None
    for m in range(NKL):
        k_bar = 0.5 * (kd_ref[0, :, m, :] + kd_ref[1, :, m, :])
        term = r_ref[:, m, :] * k_bar * rk_ref[:, m * LANES:(m + 1) * LANES]
        acc = term if acc is None else acc + term
    bonus = _gsum(acc, selb_ref[...])

    selk = selk_ref[...]
    nslab = B_WIDTH // LANES
    ys = []
    for j in range(nslab):
        ycat = jnp.concatenate([yf_ref[:, SUBLANES * j + vi, :] + yb_ref[:, SUBLANES * j + vi, :]
                                for vi in range(SUBLANES)], axis=1)
        ys.append(_gsum(ycat, selk))
    inv = 1.0 / B_HEAD_DIM
    g16 = g16_ref[...]
    mu = _gsum(functools.reduce(lambda a, b_: a + b_, ys), g16) * inv
    ds = [y - mu for y in ys]
    var = _gsum(functools.reduce(lambda a, b_: a + b_, [x * x for x in ds]), g16) * inv
    rstd = lax.rsqrt(var + LN_X_EPS)
    selv = selv_ref[...]
    for j in range(nslab):
        cs = slice(j * LANES, (j + 1) * LANES)
        yn = ds[j] * rstd * lng_ref[:, cs] + lnb_ref[:, cs]
        vj = _gsum(v_ref[:, 2 * j * LANES:2 * (j + 1) * LANES], selv)
        o_ref[:, cs] = ((yn + bonus * vj) * gate[:, cs]).astype(o_ref.dtype)


def _rwkv_out(yf, yb, r3, kd4, v2d, p, lng, lnb, rk_m, gup, selk, selv, selb, g16, *, t, c, ctx_rows):
    tm = 128
    nlat = t // tm
    ns = (t + c) // tm if ctx_rows else nlat
    rows = 2 * (t + c) if ctx_rows else 2 * t
    oblk = lambda i, bi: jnp.where(i < nlat, bi * nlat + i, (2 * t + bi * c) // tm + (i - nlat))
    y3s = pl.BlockSpec((tm, B_HEAD_DIM, LANES), lambda i, bi: (i, 0, 0))
    full2 = lambda i, bi: (0, 0)
    return pl.pallas_call(
        _rwkv_out_kernel,
        grid=(ns, 2),
        in_specs=[
            y3s, y3s,
            pl.BlockSpec((tm, NKL, LANES), lambda i, bi: (i, 0, 0)),
            pl.BlockSpec((2, tm, NKL, LANES), lambda i, bi: (0, i, 0, 0)),
            pl.BlockSpec((tm, MERGED), lambda i, bi: (i, 0)),
            pl.BlockSpec((tm, GATE_LORA), lambda i, bi: (i, 3 * MERGED // GATE_LORA + bi)),
            pl.BlockSpec((1, B_WIDTH), full2),
            pl.BlockSpec((1, B_WIDTH), full2),
            pl.BlockSpec((1, MERGED), full2),
            pl.BlockSpec((GATE_LORA, B_WIDTH), full2),
            pl.BlockSpec((None, SUBLANES * LANES, LANES), lambda i, bi: (bi, 0, 0)),
            pl.BlockSpec((None, 2 * LANES, LANES), lambda i, bi: (bi, 0, 0)),
            pl.BlockSpec((None, LANES, LANES), lambda i, bi: (bi, 0, 0)),
            pl.BlockSpec((LANES, LANES), full2),
        ],
        out_specs=pl.BlockSpec((tm, B_WIDTH), lambda i, bi: (oblk(i, bi), 0)),
        out_shape=jax.ShapeDtypeStruct((rows, B_WIDTH), BF16),
        compiler_params=_params(("parallel", "arbitrary")),
        name="rwkv_out",
    )(yf, yb, r3, kd4, v2d, p, lng, lnb, rk_m, gup, selk, selv, selb, g16)


def _kmerge_cols(w):
    r = w.shape[0]
    wt = w.reshape(r, B_HEADS, NKL, KQ).transpose(0, 2, 3, 1)
    z = jnp.zeros_like(wt)
    return jnp.stack([jnp.stack([wt, z], axis=3), jnp.stack([z, wt], axis=3)]).reshape(2, r, MERGED)


def _vmerge_cols(w):
    r = w.shape[0]
    wt = w.reshape(r, B_HEADS, B_HEAD_DIM).transpose(0, 2, 1)
    z = jnp.zeros_like(wt)
    return jnp.stack([jnp.stack([wt, z], axis=2), jnp.stack([z, wt], axis=2)]).reshape(2, r, MERGED)


def _bmerge_cols(w, width):
    wp = jnp.pad(w, ((0, 0), (0, width - w.shape[1])))
    z = jnp.zeros_like(wp)
    return jnp.stack([jnp.concatenate([wp, z], axis=1), jnp.concatenate([z, wp], axis=1)])


def _both(m):
    return m[0] + m[1]


def _vperm(w):
    r = w.shape[0]
    return w.reshape(r, B_HEADS, B_HEAD_DIM).transpose(0, 2, 1).reshape(r, B_WIDTH)


def _selectors():
    ri = jnp.arange(SUBLANES * LANES)[:, None]
    ci = jnp.arange(LANES)[None, :]
    bsel = jnp.arange(2)[:, None, None]
    c_vi, c_h = ci // B_HEADS, ci % B_HEADS
    selk = ((ri // LANES == c_vi) & ((ri % BH) // B_HEADS == bsel) & (ri % B_HEADS == c_h)).astype(BF16)
    r2 = jnp.arange(2 * LANES)[:, None]
    selv = ((r2 // BH == c_vi) & ((r2 % BH) // B_HEADS == bsel) & (r2 % B_HEADS == c_h)).astype(BF16)
    r1 = jnp.arange(LANES)[:, None]
    selb = (((r1 % BH) // B_HEADS == bsel) & (r1 % B_HEADS == c_h)).astype(BF16)
    g16 = (r1 % B_HEADS == c_h).astype(BF16)
    g32 = (r1 % BH == ci % BH).astype(BF16)
    rep = jnp.stack([((r1 // BH == vi) & (r1 % BH == ci % BH)) for vi in range(LANES // BH)]).astype(BF16)
    return selk, selv, selb, g16, g32, rep


def _merge_kernel(x_ref, h_ref, ya_ref, yb_ref, yc_ref, mod_ref, wg_ref, bg_ref, wb_ref, wo_ref,
                  o_ref, acc_sc, *, nz):
    j = pl.program_id(1)

    @pl.when(j == 0)
    def _():
        acc_sc[...] = jnp.zeros_like(acc_sc)

    h = h_ref[...]
    z = None
    for i, y_ref in enumerate((ya_ref, yb_ref, yc_ref)):
        gate = jax.nn.sigmoid(jnp.dot(h, wg_ref[i], preferred_element_type=F32) + bg_ref[i])
        term = gate * jnp.dot(y_ref[...], wb_ref[i], preferred_element_type=F32)
        z = term if z is None else z + term
    acc_sc[...] += jnp.dot(z.astype(BF16), wo_ref[...], preferred_element_type=F32)

    @pl.when(j == nz - 1)
    def _():
        o_ref[...] = x_ref[...] + mod_ref[5:6, :] * acc_sc[...]


def _merge(xs, h, ya, yb, yc, mod, wg, bg, wb, wo, mod_idx, n_tiles):
    d = xs.shape[1]
    tz = 256
    nz = d // tz
    rowt = lambda i, j: (i, 0)
    return pl.pallas_call(
        functools.partial(_merge_kernel, nz=nz),
        grid=(n_tiles, nz),
        in_specs=[
            pl.BlockSpec((TM, d), rowt),
            pl.BlockSpec((TM, d), rowt),
            pl.BlockSpec((TM, A_WIDTH), rowt),
            pl.BlockSpec((TM, B_WIDTH), rowt),
            pl.BlockSpec((TM, C_WIDTH), rowt),
            pl.BlockSpec((None, N_MOD, d), lambda i, j: (mod_idx(i), 0, 0)),
            pl.BlockSpec((N_BRANCH, d, tz), lambda i, j: (0, 0, j)),
            pl.BlockSpec((N_BRANCH, 1, tz), lambda i, j: (0, 0, j)),
            pl.BlockSpec((N_BRANCH, A_WIDTH, tz), lambda i, j: (0, 0, j)),
            pl.BlockSpec((tz, d), lambda i, j: (j, 0)),
        ],
        out_specs=pl.BlockSpec((TM, d), rowt),
        out_shape=jax.ShapeDtypeStruct((n_tiles * TM, d), F32),
        scratch_shapes=[pltpu.VMEM((TM, d), F32)],
        compiler_params=_params(("parallel", "arbitrary")),
        name="merge",
    )(xs, h, ya, yb, yc, mod, wg, bg, wb, wo)


def _rope_tables(b, t, c):
    rows = t // GRID_W
    rowp = jnp.repeat(jnp.arange(rows), GRID_W).astype(F32)
    colp = jnp.tile(jnp.arange(GRID_W), rows).astype(F32)
    inv = 1.0 / (ROPE_BASE ** (jnp.arange(0, ROPE_AXIS_DIM, 2, dtype=F32) / ROPE_AXIS_DIM))
    ar, ac = rowp[:, None] * inv, colp[:, None] * inv
    cr, sr, cc, sc = jnp.cos(ar), jnp.sin(ar), jnp.cos(ac), jnp.sin(ac)
    cos64 = jnp.concatenate([cr, cr, cc, cc], axis=-1)
    sin64 = jnp.concatenate([-sr, sr, -sc, sc], axis=-1)
    cos_t = jnp.tile(cos64, (b, 2))
    sin_t = jnp.tile(sin64, (b, 2))
    cos_t = jnp.concatenate([cos_t, jnp.ones((b * c, LANES), F32)])
    sin_t = jnp.concatenate([sin_t, jnp.zeros((b * c, LANES), F32)])
    return cos_t, sin_t


def kernel(x, c, ctx, c_ctx, w_ada, b_ada, norm_g, ffn_w_in, ffn_w_out, w_in, gm_v_norm, gm_ws, gm_bs,
           rw_conv, rw_w0, rw_w_up, rw_a0, rw_a_up, rw_g_up, rw_k_k, rw_k_a, rw_r_k, rw_ln_g, rw_ln_b,
           da_q_norm, da_k_norm, da_lam, da_subln, w_branch, b_gate, w_out):
    b, t, d = x.shape
    cl = ctx.shape[1]
    depth = w_ada.shape[0]
    d_ff = ffn_w_out.shape[2]
    assert b == 2 and b * cl == TM and t % TM == 0 and cl % 256 == 0
    n_lat = (b * t) // TM
    n_all = n_lat + 1
    tiles_per_batch = t // TM
    mod_idx = lambda i: jnp.where(i < n_lat, i // tiles_per_batch, b)

    xs = jnp.concatenate([x.reshape(b * t, d), ctx.reshape(b * cl, d)])
    cvec = jnp.zeros((SUBLANES, d), F32).at[:b].set(c).at[b].set(c_ctx)
    b_ada3 = b_ada.reshape(depth, 1, N_MOD * d)
    cos_t, sin_t = _rope_tables(b, t, cl)
    lane = jnp.arange(LANES)
    g64 = (lane[:, None] // C_HEAD_DIM == lane[None, :] // C_HEAD_DIM).astype(BF16)
    selk, selv, selb, g16, g32, rep = _selectors()
    ffp = -(-d_ff // TF) * TF

    o = 0
    offs = []
    for n in (A_WIDTH, A_WIDTH, 3 * B_WIDTH, GATE_LORA, 2 * DECAY_LORA, 2 * ICL_LORA, 3 * C_WIDTH, N_BRANCH * d):
        offs.append((o, o + n))
        o += n

    for l in range(depth):
        last = l == depth - 1
        lam_init = 0.8 - 0.6 * math.exp(-0.3 * l)
        mod = _ada(cvec, w_ada, b_ada3, l)[:b + 1].reshape(b + 1, N_MOD, d)

        def ffn_weights(w):
            wi = jnp.transpose(ffn_w_in[l, w].reshape(d, 2, d_ff), (1, 0, 2))
            w_gu = jnp.pad(wi, ((0, 0), (0, 0), (0, ffp - d_ff))).astype(BF16)
            wo = jnp.pad(ffn_w_out[l, w], ((0, ffp - d_ff), (0, 0))).astype(BF16)
            return w_gu, wo

        w_gu0, w_o0 = ffn_weights(0)
        xs, h = _ffn(xs, mod, norm_g[l], w_gu0, w_o0, mod_idx, n_tiles=n_all, mi=0, emit_h=True)

        wl = w_in[l]
        sl = lambda i: wl[:, offs[i][0]:offs[i][1]]
        n_mix = n_lat if last else n_all

        w_uv = jnp.concatenate([sl(0), sl(1)], axis=1).astype(BF16)
        bsb = jnp.broadcast_to(gm_bs[l][:, :, None], (A_GROUPS, CHUNK, A_WIDTH // A_GROUPS))
        ya = _gmlp(h, w_uv, gm_v_norm[l].reshape(1, A_WIDTH), gm_ws[l], bsb, n_mix)

        w_rkv, w_dec, w_icl = sl(2), sl(4), sl(5)
        wz = jnp.concatenate(
            [_kmerge_cols(w_rkv[:, :B_WIDTH]), _kmerge_cols(w_rkv[:, B_WIDTH:2 * B_WIDTH]),
             _vmerge_cols(w_rkv[:, 2 * B_WIDTH:]), _bmerge_cols(sl(3), GATE_LORA),
             _bmerge_cols(w_dec[:, :DECAY_LORA], LORA_PAD), _bmerge_cols(w_dec[:, DECAY_LORA:], LORA_PAD),
             _bmerge_cols(w_icl[:, :ICL_LORA], LORA_PAD), _bmerge_cols(w_icl[:, ICL_LORA:], LORA_PAD)],
            axis=2).astype(BF16)
        p = _rkv_proj(h, wz, t=t, c=cl)
        conv = rw_conv[l]
        conv_m = jnp.concatenate([_both(_kmerge_cols(conv[:, :B_WIDTH])),
                                  _both(_kmerge_cols(conv[:, B_WIDTH:2 * B_WIDTH])),
                                  _both(_vmerge_cols(conv[:, 2 * B_WIDTH:]))], axis=1)
        up_m = lambda w: jnp.stack([
            _kmerge_cols(jnp.pad(w[dd], ((0, LORA_PAD - w.shape[1]), (0, 0)))).reshape(2 * LORA_PAD, MERGED)
            for dd in range(2)])
        r3, kk3, w4, kd4, b4, vrep, v2d = _rwkv_prep(
            p, conv_m, _both(_kmerge_cols(rw_w0[l])), up_m(rw_w_up[l]), _both(_kmerge_cols(rw_a0[l])),
            up_m(rw_a_up[l]), _both(_kmerge_cols(rw_k_k[l].reshape(1, B_WIDTH))),
            _both(_kmerge_cols(rw_k_a[l].reshape(1, B_WIDTH))), g32, rep, t=t, c=cl)
        yf, ybk = _scan(w4, kd4, b4, kk3, r3, vrep, t=t, c=cl)
        yb = _rwkv_out(yf, ybk, r3, kd4, v2d, p, _vperm(rw_ln_g[l].reshape(1, B_WIDTH)),
                       _vperm(rw_ln_b[l].reshape(1, B_WIDTH)),
                       _both(_kmerge_cols(rw_r_k[l].reshape(1, B_WIDTH))),
                       _vperm(rw_g_up[l]).astype(BF16), selk, selv, selb, g16, t=t, c=cl, ctx_rows=not last)

        gains = jnp.stack([jnp.tile(da_q_norm[l], 2) * (C_HEAD_DIM ** -0.5), jnp.tile(da_k_norm[l], 2),
                           jnp.ones((LANES,), F32)]).reshape(3, 1, LANES)
        qkv = _qkv(h, sl(6).astype(BF16), gains, cos_t, sin_t, g64)
        sub = da_subln[l].reshape(1, LANES)
        yc = _attn(qkv, da_lam[l], sub, None, b=b, t=t, c=cl, lam_init=lam_init, ctx_queries=False)
        if not last:
            yc = _attn(qkv, da_lam[l], sub, yc, b=b, t=t, c=cl, lam_init=lam_init, ctx_queries=True)

        wg = jnp.transpose(sl(7).reshape(d, N_BRANCH, d), (1, 0, 2)).astype(BF16)
        wb = w_branch[l]
        wb_b = wb[1].reshape(B_HEADS, B_HEAD_DIM, d).transpose(1, 0, 2).reshape(B_WIDTH, d)
        wbr = jnp.stack([wb[0], wb_b, wb[2]]).astype(BF16)
        xs = _merge(xs, h, ya, yb, yc, mod, wg, b_gate[l].reshape(N_BRANCH, 1, d),
                    wbr, w_out[l].astype(BF16), mod_idx, n_mix)

        w_gu1, w_o1 = ffn_weights(1)
        xs = _ffn(xs, mod, norm_g[l], w_gu1, w_o1, mod_idx, n_tiles=n_mix, mi=2, emit_h=False)

    return xs[:b * t].reshape(b, t, d)
```

```python
import functools
import math

import jax
import jax.numpy as jnp
from jax import lax
from jax.experimental import pallas as pl
from jax.experimental.pallas import tpu as pltpu

F32 = jnp.float32
BF16 = jnp.bfloat16
HIGHEST = lax.Precision.HIGHEST

N_MOD = 9
CHUNK = 128
A_WIDTH = 1024
A_GROUPS = 8
B_WIDTH = 1024
B_HEAD_DIM = 64
B_HEADS = 16
DECAY_LORA = 96
ICL_LORA = 96
GATE_LORA = 256
C_HEADS = 8
C_HEAD_DIM = 64
C_WIDTH = 1024
N_BRANCH = 3
GRID_W = 64
ROPE_BASE = 10000.0
ROPE_AXIS_DIM = 32
NORM_EPS = 1e-6
LN_X_EPS = 64e-5

LANES = 128
SUBLANES = 8
VMEM_LIMIT = 56 * 1024 * 1024

TM = 512
TF = 512
LORA_PAD = 128
SCAN_TS = 64
ATT_TQ = 512
ATT_KC = 512


def _params(sem):
    return pltpu.CompilerParams(dimension_semantics=sem, vmem_limit_bytes=VMEM_LIMIT)


def _rms(x, eps):
    return x * lax.rsqrt(jnp.mean(x * x, axis=-1, keepdims=True) + eps)


def _split3(s):
    hi = s.astype(BF16)
    r = s - hi.astype(F32)
    mid = r.astype(BF16)
    lo = (r - mid.astype(F32)).astype(BF16)
    return hi, mid, lo


def _gsum(s, g_bf16):
    out = None
    for piece in _split3(s):
        d = jnp.dot(piece, g_bf16, preferred_element_type=F32)
        out = d if out is None else out + d
    return out


def _gelu_tanh(x):
    cdf = 0.5 * (1.0 + jnp.tanh(math.sqrt(2.0 / math.pi) * (x + 0.044715 * (x * x * x))))
    return x * cdf


def _ada_kernel(c_ref, w_ref, b_ref, o_ref):
    s = c_ref[...]
    s = s * jax.nn.sigmoid(s)
    o_ref[...] = jnp.dot(s, w_ref[...], precision=HIGHEST, preferred_element_type=F32) + b_ref[...]


def _ada(cvec, w_ada, b_ada3, l):
    d = cvec.shape[1]
    n = w_ada.shape[2]
    tn = 1024
    return pl.pallas_call(
        _ada_kernel,
        grid=(n // tn,),
        in_specs=[
            pl.BlockSpec((SUBLANES, d), lambda j: (0, 0)),
            pl.BlockSpec((None, d, tn), lambda j: (l, 0, j)),
            pl.BlockSpec((None, 1, tn), lambda j: (l, 0, j)),
        ],
        out_specs=pl.BlockSpec((SUBLANES, tn), lambda j: (0, j)),
        out_shape=jax.ShapeDtypeStruct((SUBLANES, n), F32),
        compiler_params=_params(("arbitrary",)),
        name="ada",
    )(cvec, w_ada, b_ada3)


def _ffn_kernel(x_ref, mod_ref, g_ref, win_ref, wout_ref, *rest, tf, nf, mi, emit_h):
    if emit_h:
        o_ref, h_ref, hn_sc, acc_sc = rest
    else:
        o_ref, hn_sc, acc_sc = rest
    f = pl.program_id(1)

    @pl.when(f == 0)
    def _():
        xn = _rms(x_ref[...], NORM_EPS) * g_ref[mi:mi + 1, :]
        hn = xn * (1.0 + mod_ref[3 * mi + 1:3 * mi + 2, :]) + mod_ref[3 * mi:3 * mi + 1, :]
        hn_sc[...] = hn.astype(BF16)
        acc_sc[...] = jnp.zeros_like(acc_sc)

    hn = hn_sc[...]
    g = jnp.dot(hn, win_ref[0], preferred_element_type=F32)
    u = jnp.dot(hn, win_ref[1], preferred_element_type=F32)
    act = (g * jax.nn.sigmoid(g) * u).astype(BF16)
    acc_sc[...] += jnp.dot(act, wout_ref[...], preferred_element_type=F32)

    @pl.when(f == nf - 1)
    def _():
        out = x_ref[...] + 0.5 * mod_ref[3 * mi + 2:3 * mi + 3, :] * acc_sc[...]
        o_ref[...] = out
        if emit_h:
            hn = _rms(out, NORM_EPS) * g_ref[1:2, :]
            h_ref[...] = (hn * (1.0 + mod_ref[4:5, :]) + mod_ref[3:4, :]).astype(BF16)


def _ffn(xs, mod, norm_g, w_gu, w_out, mod_idx, *, n_tiles, mi, emit_h):
    m, d = xs.shape
    nf = w_out.shape[0] // TF
    out_rows = n_tiles * TM
    out_shape = [jax.ShapeDtypeStruct((out_rows, d), F32)]
    out_specs = [pl.BlockSpec((TM, d), lambda i, f: (i, 0))]
    if emit_h:
        out_shape.append(jax.ShapeDtypeStruct((out_rows, d), BF16))
        out_specs.append(pl.BlockSpec((TM, d), lambda i, f: (i, 0)))
    res = pl.pallas_call(
        functools.partial(_ffn_kernel, tf=TF, nf=nf, mi=mi, emit_h=emit_h),
        grid=(n_tiles, nf),
        in_specs=[
            pl.BlockSpec((TM, d), lambda i, f: (i, 0)),
            pl.BlockSpec((None, N_MOD, d), lambda i, f: (mod_idx(i), 0, 0)),
            pl.BlockSpec((3, d), lambda i, f: (0, 0)),
            pl.BlockSpec((2, d, TF), lambda i, f: (0, 0, f)),
            pl.BlockSpec((TF, d), lambda i, f: (f, 0)),
        ],
        out_specs=out_specs,
        out_shape=out_shape,
        scratch_shapes=[pltpu.VMEM((TM, d), BF16), pltpu.VMEM((TM, d), F32)],
        compiler_params=_params(("parallel", "arbitrary")),
        name="ffn",
    )(xs, mod, norm_g, w_gu, w_out)
    return res if emit_h else res[0]


def _gmlp_kernel(h_ref, w_ref, vn_ref, ws_ref, bsb_ref, o_ref, p_sc, *, tm):
    p_sc[...] = jnp.dot(h_ref[...], w_ref[...], preferred_element_type=F32)
    gd = A_WIDTH // A_GROUPS
    for g in range(A_GROUPS):
        wsg = ws_ref[g].astype(BF16)
        bias = bsb_ref[g]
        gain = vn_ref[:, g * gd:(g + 1) * gd]
        for n in range(tm // CHUNK):
            rows = slice(n * CHUNK, (n + 1) * CHUNK)
            u = _gelu_tanh(p_sc[rows, g * gd:(g + 1) * gd])
            v = _gelu_tanh(p_sc[rows, A_WIDTH + g * gd:A_WIDTH + (g + 1) * gd])
            v = _rms(v, NORM_EPS) * gain
            sv = jnp.dot(wsg, v.astype(BF16), preferred_element_type=F32) + bias
            o_ref[rows, g * gd:(g + 1) * gd] = (u * sv).astype(o_ref.dtype)


def _gmlp(h, w_uv, vn, ws, bsb, n_tiles):
    m, d = h.shape
    return pl.pallas_call(
        functools.partial(_gmlp_kernel, tm=TM),
        grid=(n_tiles,),
        in_specs=[
            pl.BlockSpec((TM, d), lambda i: (i, 0)),
            pl.BlockSpec((d, 2 * A_WIDTH), lambda i: (0, 0)),
            pl.BlockSpec((1, A_WIDTH), lambda i: (0, 0)),
            pl.BlockSpec((A_GROUPS, CHUNK, CHUNK), lambda i: (0, 0, 0)),
            pl.BlockSpec((A_GROUPS, CHUNK, A_WIDTH // A_GROUPS), lambda i: (0, 0, 0)),
        ],
        out_specs=pl.BlockSpec((TM, A_WIDTH), lambda i: (i, 0)),
        out_shape=jax.ShapeDtypeStruct((n_tiles * TM, A_WIDTH), BF16),
        scratch_shapes=[pltpu.VMEM((TM, 2 * A_WIDTH), F32)],
        compiler_params=_params(("parallel",)),
        name="gmlp",
    )(h, w_uv, vn, ws, bsb)


def _qkv_kernel(h_ref, w_ref, gain_ref, cos_ref, sin_ref, g64_ref, o_ref, vx_ref, p_sc):
    j = pl.program_id(1)
    p_sc[...] = jnp.dot(h_ref[...], w_ref[...], preferred_element_type=F32)

    @pl.when(j == 2)
    def _():
        ones = jnp.ones((p_sc.shape[0], LANES), BF16)
        for hs in range(C_HEADS):
            vx_ref[:, 2 * hs * LANES:(2 * hs + 1) * LANES] = p_sc[:, hs * LANES:(hs + 1) * LANES].astype(BF16)
            vx_ref[:, (2 * hs + 1) * LANES:(2 * hs + 2) * LANES] = ones

    @pl.when(j < 2)
    def _():
        lane = lax.broadcasted_iota(jnp.int32, (1, LANES), 1)
        first = (lane % ROPE_AXIS_DIM) < (ROPE_AXIS_DIM // 2)
        cos = cos_ref[...]
        sin = sin_ref[...]
        gain = gain_ref[...]
        g64 = g64_ref[...]
        for hs in range(C_HEADS):
            x = p_sc[:, hs * LANES:(hs + 1) * LANES]
            ms = _gsum(x * x, g64) * (1.0 / C_HEAD_DIM)
            y = x * lax.rsqrt(ms + NORM_EPS) * gain
            half = ROPE_AXIS_DIM // 2
            rot = jnp.where(first, pltpu.roll(y, LANES - half, 1), pltpu.roll(y, half, 1))
            o_ref[:, hs * LANES:(hs + 1) * LANES] = (y * cos + rot * sin).astype(BF16)


def _qkv(h, w_qkv, gains, cos_t, sin_t, g64):
    m, d = h.shape
    return pl.pallas_call(
        _qkv_kernel,
        grid=(m // TM, 3),
        in_specs=[
            pl.BlockSpec((TM, d), lambda i, j: (i, 0)),
            pl.BlockSpec((d, C_WIDTH), lambda i, j: (0, j)),
            pl.BlockSpec((None, 1, LANES), lambda i, j: (j, 0, 0)),
            pl.BlockSpec((TM, LANES), lambda i, j: (i, 0)),
            pl.BlockSpec((TM, LANES), lambda i, j: (i, 0)),
            pl.BlockSpec((LANES, LANES), lambda i, j: (0, 0)),
        ],
        out_specs=[pl.BlockSpec((None, TM, C_WIDTH), lambda i, j: (jnp.minimum(j, 1), i, 0)),
                   pl.BlockSpec((TM, 2 * C_WIDTH), lambda i, j: (i, 0))],
        out_shape=[jax.ShapeDtypeStruct((2, m, C_WIDTH), BF16), jax.ShapeDtypeStruct((m, 2 * C_WIDTH), BF16)],
        scratch_shapes=[pltpu.VMEM((TM, C_WIDTH), F32)],
        compiler_params=_params(("parallel", "arbitrary")),
        name="qkv",
    )(h, w_qkv, gains, cos_t, sin_t, g64)


def _attn_kernel(lam_ref, sub_ref, q_ref, *refs, nseg, lam_init):
    kv = refs[:2 * nseg]
    o_ref, s_sc = refs[2 * nseg:]
    lv = lam_ref[...]
    lam = (jnp.exp(jnp.sum(lv[0:1] * lv[1:2], axis=-1, keepdims=True))
           - jnp.exp(jnp.sum(lv[2:3] * lv[3:4], axis=-1, keepdims=True)) + lam_init)
    q = q_ref[...]
    lane = lax.broadcasted_iota(jnp.int32, (1, LANES), 1)
    zero = jnp.zeros_like(q)
    outs = []
    for j in range(2):
        in_map = (lane < C_HEAD_DIM) if j == 0 else (lane >= C_HEAD_DIM)
        qj = jnp.where(in_map, q, zero)
        chunks = []
        off = 0
        for si in range(nseg):
            ls = kv[2 * si].shape[0]
            for c0 in range(0, ls, ATT_KC):
                chunks.append((si, c0, min(ATT_KC, ls - c0), off + c0))
            off += ls
        mlane = None
        for si, c0, kc, col in chunks:
            s = lax.dot_general(qj, kv[2 * si][c0:c0 + kc, :], (((1,), (1,)), ((), ())),
                                preferred_element_type=F32)
            s_sc[:, col:col + kc] = s
            for l0 in range(0, kc, LANES):
                sl = s[:, l0:l0 + LANES]
                mlane = sl if mlane is None else jnp.maximum(mlane, sl)
        mx = jnp.max(mlane, axis=-1, keepdims=True)
        acc = None
        for si, c0, kc, col in chunks:
            e = jnp.exp2((s_sc[:, col:col + kc] - mx).astype(BF16))
            pv = jnp.dot(e, kv[2 * si + 1][c0:c0 + kc, :], preferred_element_type=F32)
            acc = pv if acc is None else acc + pv
        outs.append(acc[:, :LANES] / acc[:, LANES:])
    o = outs[0] - lam * outs[1]
    o = _rms(o, NORM_EPS) * sub_ref[...]
    o_ref[...] = (o * (1.0 - lam_init)).astype(o_ref.dtype)


def _attn(qk, vx, lam, subln, yc_prev, *, b, t, c, lam_init, ctx_queries):
    m = qk.shape[1]
    if ctx_queries:
        tq, nq = c, 1
        q_blk = lambda bi, qi: (b * t) // c + bi
        segs = [(c, lambda bi: (b * t) // c + bi)]
    else:
        tq = ATT_TQ
        nq = t // tq
        q_blk = lambda bi, qi: bi * nq + qi
        segs = [(t, lambda bi: bi), (c, lambda bi: (b * t) // c + bi)]
    in_specs = [
        pl.BlockSpec((4, C_HEAD_DIM), lambda bi, h, qi: (0, 0)),
        pl.BlockSpec((1, LANES), lambda bi, h, qi: (0, 0)),
        pl.BlockSpec((None, tq, LANES), lambda bi, h, qi: (0, q_blk(bi, qi), h)),
    ]
    args = [lam, subln, qk]
    for ls, blk in segs:
        in_specs.append(pl.BlockSpec((None, ls, LANES), lambda bi, h, qi, blk=blk: (1, blk(bi), h)))
        in_specs.append(pl.BlockSpec((ls, 2 * LANES), lambda bi, h, qi, blk=blk: (blk(bi), h)))
        args += [qk, vx]
    aliases = {}
    if yc_prev is not None:
        in_specs.append(pl.BlockSpec(memory_space=pl.ANY))
        args.append(yc_prev)
        aliases = {len(args) - 1: 0}

    def body(*refs):
        n_in = 3 + 2 * len(segs)
        _attn_kernel(*refs[:n_in], *refs[-2:], nseg=len(segs), lam_init=lam_init)

    return pl.pallas_call(
        body,
        grid=(b, C_HEADS, nq),
        in_specs=in_specs,
        out_specs=pl.BlockSpec((tq, LANES), lambda bi, h, qi: (q_blk(bi, qi), h)),
        out_shape=jax.ShapeDtypeStruct((m, C_WIDTH), BF16),
        scratch_shapes=[pltpu.VMEM((tq, sum(ls for ls, _ in segs)), F32)],
        input_output_aliases=aliases,
        compiler_params=_params(("parallel", "parallel", "arbitrary")),
        name="diff_attn_ctx" if ctx_queries else "diff_attn",
    )(*args)


KQ = 4
NKL = B_HEAD_DIM // KQ
MERGED = 2 * B_WIDTH
BH = 2 * B_HEADS
LORA_MERGED = 2 * GATE_LORA + 8 * LORA_PAD
PROJ_MERGED = 3 * MERGED + LORA_MERGED


def _rkv_proj_kernel(h0_ref, h1_ref, w_ref, o_ref):
    o_ref[...] = (jnp.dot(h0_ref[...], w_ref[0], preferred_element_type=F32)
                  + jnp.dot(h1_ref[...], w_ref[1], preferred_element_type=F32))


def _rkv_proj(h, wz, *, t, c):
    d = h.shape[1]
    tm = 256
    tn = 1536
    s = t + c
    nlat = t // tm
    hblk = lambda bi, i: jnp.where(i < nlat, bi * nlat + i, (2 * t + bi * c) // tm + (i - nlat))
    return pl.pallas_call(
        _rkv_proj_kernel,
        grid=(PROJ_MERGED // tn, s // tm),
        in_specs=[
            pl.BlockSpec((tm, d), lambda j, i: (hblk(0, i), 0)),
            pl.BlockSpec((tm, d), lambda j, i: (hblk(1, i), 0)),
            pl.BlockSpec((2, d, tn), lambda j, i: (0, 0, j)),
        ],
        out_specs=pl.BlockSpec((tm, tn), lambda j, i: (i, j)),
        out_shape=jax.ShapeDtypeStruct((s, PROJ_MERGED), F32),
        compiler_params=_params(("parallel", "arbitrary")),
        name="rkv_proj",
    )(h, h, wz)


def _rwkv_prep_kernel(p_ref, prev_ref, next_ref, lo_ref, conv_ref, w0_ref, wup_ref, a0_ref, aup_ref,
                      kkg_ref, ka_ref, rk_ref, g32_ref, rep_ref,
                      r_o, kk_o, w_o, kd_o, b_o, vrep_o, v_o, bonus_o, k_sc, r_sc, *, tm, starts, ends):
    i = pl.program_id(0)
    is_start = functools.reduce(jnp.logical_or, [i == s for s in starts])
    is_end = functools.reduce(jnp.logical_or, [i == s for s in ends])
    row = lax.broadcasted_iota(jnp.int32, (tm, 1), 0)

    def conv(c0):
        cs = slice(c0, c0 + LANES)
        x = p_ref[:, cs]
        pm = jnp.where(is_start, 0.0, prev_ref[SUBLANES - 1:SUBLANES, cs])
        nx = jnp.where(is_end, 0.0, next_ref[0:1, cs])
        xm1 = jnp.where(row == 0, pm, pltpu.roll(x, 1, 0))
        xp1 = jnp.where(row == tm - 1, nx, pltpu.roll(x, tm - 1, 0))
        return xm1 * conv_ref[0:1, cs] + x * conv_ref[1:2, cs] + xp1 * conv_ref[2:3, cs]

    ss = None
    for m in range(NKL):
        cs = slice(m * LANES, (m + 1) * LANES)
        r = conv(m * LANES)
        r_o[:, m, :] = r
        r_sc[:, cs] = r
        k = conv(MERGED + m * LANES)
        k_sc[:, cs] = k
        kkr = k * kkg_ref[:, cs]
        ss = kkr * kkr if ss is None else ss + kkr * kkr
    rs = lax.rsqrt(_gsum(ss, g32_ref[...]) + 1e-12)

    w_off = 2 * GATE_LORA
    a_off = w_off + 4 * LORA_PAD
    tw = [jnp.tanh(lo_ref[:, w_off + 2 * d * LORA_PAD:w_off + 2 * (d + 1) * LORA_PAD]) for d in range(2)]
    pa = [lo_ref[:, a_off + 2 * d * LORA_PAD:a_off + 2 * (d + 1) * LORA_PAD] for d in range(2)]
    bonus = None
    for m in range(NKL):
        cs = slice(m * LANES, (m + 1) * LANES)
        k = k_sc[:, cs]
        kk = k * kkg_ref[:, cs] * rs
        kk_o[:, m, :] = kk
        kd_sum = None
        for d in range(2):
            w_lo = w0_ref[d:d + 1, cs] + jnp.dot(tw[d], wup_ref[d, :, cs], precision=HIGHEST,
                                                   preferred_element_type=F32)
            w_o[d, :, m, :] = jnp.exp(-jax.nn.sigmoid(w_lo) * math.exp(-0.5))
            a = jax.nn.sigmoid(a0_ref[d:d + 1, cs] + jnp.dot(pa[d], aup_ref[d, :, cs], precision=HIGHEST,
                                                             preferred_element_type=F32))
            kd = k * (1.0 + (a - 1.0) * ka_ref[:, cs])
            kd_o[d, :, m, :] = kd
            kd_sum = kd if kd_sum is None else kd_sum + kd
            b_o[d, :, m, :] = kk * a
        term = r_sc[:, cs] * (0.5 * kd_sum) * rk_ref[:, cs]
        bonus = term if bonus is None else bonus + term
    bonus_o[...] = bonus

    per_slab = LANES // BH
    for j in range(MERGED // LANES):
        vs = conv(2 * MERGED + j * LANES)
        v_o[:, j * LANES:(j + 1) * LANES] = vs
        for vi in range(per_slab):
            vrep_o[:, per_slab * j + vi, :] = _gsum(vs, rep_ref[vi])


def _rwkv_prep(p, conv_m, w0_m, wup_m, a0_m, aup_m, kk_m, ka_m, rk_m, g32, rep, *, t, c):
    s = t + c
    tm = 128
    nt = s // tm
    starts = [0, t // tm]
    ends = [t // tm - 1, nt - 1]
    hb = tm // SUBLANES
    rkv_w = 3 * MERGED
    full2 = lambda i: (0, 0)
    full3 = lambda i: (0, 0, 0)
    k3 = jax.ShapeDtypeStruct((s, NKL, LANES), F32)
    k4 = jax.ShapeDtypeStruct((2, s, NKL, LANES), F32)
    k3s = pl.BlockSpec((tm, NKL, LANES), lambda i: (i, 0, 0))
    k4s = pl.BlockSpec((2, tm, NKL, LANES), lambda i: (0, i, 0, 0))
    return pl.pallas_call(
        functools.partial(_rwkv_prep_kernel, tm=tm, starts=starts, ends=ends),
        grid=(nt,),
        in_specs=[
            pl.BlockSpec((tm, rkv_w), lambda i: (i, 0)),
            pl.BlockSpec((SUBLANES, rkv_w), lambda i: (jnp.maximum(i * hb - 1, 0), 0)),
            pl.BlockSpec((SUBLANES, rkv_w), lambda i: (jnp.minimum((i + 1) * hb, s // SUBLANES - 1), 0)),
            pl.BlockSpec((tm, LORA_MERGED), lambda i: (i, rkv_w // LORA_MERGED)),
            pl.BlockSpec((3, rkv_w), full2),
            pl.BlockSpec((2, MERGED), full2),
            pl.BlockSpec((2, 2 * LORA_PAD, MERGED), full3),
            pl.BlockSpec((2, MERGED), full2),
            pl.BlockSpec((2, 2 * LORA_PAD, MERGED), full3),
            pl.BlockSpec((1, MERGED), full2),
            pl.BlockSpec((1, MERGED), full2),
            pl.BlockSpec((1, MERGED), full2),
            pl.BlockSpec((LANES, LANES), full2),
            pl.BlockSpec((LANES // BH, LANES, LANES), full3),
        ],
        out_specs=[k3s, k3s, k4s, k4s, k4s,
                   pl.BlockSpec((tm, B_HEAD_DIM, LANES), lambda i: (i, 0, 0)),
                   pl.BlockSpec((tm, MERGED), lambda i: (i, 0)),
                   pl.BlockSpec((tm, LANES), lambda i: (i, 0))],
        out_shape=[k3, k3, k4, k4, k4, jax.ShapeDtypeStruct((s, B_HEAD_DIM, LANES), F32),
                   jax.ShapeDtypeStruct((s, MERGED), F32), jax.ShapeDtypeStruct((s, LANES), F32)],
        scratch_shapes=[pltpu.VMEM((tm, MERGED), F32), pltpu.VMEM((tm, MERGED), F32)],
        compiler_params=_params(("parallel",)),
        name="rwkv_prep",
    )(p, p, p, p, conv_m, w0_m, wup_m, a0_m, aup_m, kk_m, ka_m, rk_m, g32, rep)


def _scan_kernel(*refs, ts):
    ins, (yf_ref, yb_ref, s_sc, y_sc) = refs[:12], refs[12:]
    streams = [ins[:6] + (yf_ref,), ins[6:] + (yb_ref,)]
    nkl = s_sc.shape[1]
    nvb = s_sc.shape[2] // SUBLANES

    @pl.when(pl.program_id(0) == 0)
    def _():
        s_sc[...] = jnp.zeros_like(s_sc)
        y_sc[...] = jnp.zeros_like(y_sc)

    def vsl(vb):
        return slice(vb * SUBLANES, (vb + 1) * SUBLANES)

    diag = (lax.broadcasted_iota(jnp.int32, (SUBLANES, LANES), 1) // BH
            == lax.broadcasted_iota(jnp.int32, (SUBLANES, LANES), 0) % KQ)

    def emit_y(d, y_ref, row):
        for vb in range(nvb):
            y = y_sc[d, vsl(vb), :]
            y = y + pltpu.roll(y, BH, 1)
            y = y + pltpu.roll(y, 2 * BH, 1)
            y = jnp.where(diag, y, 0.0)
            y = y + pltpu.roll(y, 1, 0)
            y = y + pltpu.roll(y, 2, 0)
            for half in range(SUBLANES // KQ):
                srow = half * KQ + KQ - 1
                y_ref[row, 2 * vb + half:2 * vb + half + 1, :] = y[srow:srow + 1, :]

    def step(j, carry):
        ts_of = (j, ts - 1 - j)
        prev = (jnp.maximum(j - 1, 0), jnp.minimum(ts - j, ts - 1))
        for d in range(2):
            emit_y(d, streams[d][-1], prev[d])
        sas = []
        for d, (w_ref, kd_ref, b_ref, kk_ref, r_ref, v_ref, y_ref) in enumerate(streams):
            t = ts_of[d]
            sa = [None] * nvb
            for kl in range(nkl):
                kkrow = kk_ref[t, kl:kl + 1, :]
                for vb in range(nvb):
                    term = s_sc[d, kl, vsl(vb), :] * kkrow
                    sa[vb] = term if sa[vb] is None else sa[vb] + term
            sas.append(sa)
        for d in range(2):
            red = []
            for x in sas[d]:
                x = x + pltpu.roll(x, BH, 1)
                red.append(x + pltpu.roll(x, 2 * BH, 1))
            sas[d] = red
        for d, (w_ref, kd_ref, b_ref, kk_ref, r_ref, v_ref, y_ref) in enumerate(streams):
            t = ts_of[d]
            sa = sas[d]
            ys = [None] * nvb
            for kl in range(nkl):
                wrow = w_ref[t, kl:kl + 1, :]
                brow = b_ref[t, kl:kl + 1, :]
                kdrow = kd_ref[t, kl:kl + 1, :]
                rrow = r_ref[t, kl:kl + 1, :]
                for vb in range(nvb):
                    sn = s_sc[d, kl, vsl(vb), :] * wrow - sa[vb] * brow + v_ref[t, vsl(vb), :] * kdrow
                    s_sc[d, kl, vsl(vb), :] = sn
                    term = sn * rrow
                    ys[vb] = term if ys[vb] is None else ys[vb] + term
            for vb in range(nvb):
                y_sc[d, vsl(vb), :] = ys[vb]
        return carry

    lax.fori_loop(0, ts, step, 0)
    emit_y(0, yf_ref, ts - 1)
    emit_y(1, yb_ref, 0)


def _scan(w4, kd4, b4, kk3, r3, vrep, *, t, c):
    s = t + c
    ts = SCAN_TS
    nctx = c // ts
    nlat = t // ts
    fblk = lambda g: jnp.where(g < nctx, nlat + g, g - nctx)
    bblk = lambda g: jnp.where(g < nctx, nlat + (nctx - 1 - g), nlat - 1 - (g - nctx))
    in_specs = []
    args = []
    for d, blk in enumerate((fblk, bblk)):
        for a in (w4, kd4, b4):
            in_specs.append(pl.BlockSpec((None, ts, NKL, LANES), lambda g, d=d, blk=blk: (d, blk(g), 0, 0)))
            args.append(a)
        for a in (kk3, r3):
            in_specs.append(pl.BlockSpec((ts, NKL, LANES), lambda g, blk=blk: (blk(g), 0, 0)))
            args.append(a)
        in_specs.append(pl.BlockSpec((ts, B_HEAD_DIM, LANES), lambda g, blk=blk: (blk(g), 0, 0)))
        args.append(vrep)
    nslab = MERGED // LANES
    yshape = jax.ShapeDtypeStruct((s, nslab, LANES), F32)
    return pl.pallas_call(
        functools.partial(_scan_kernel, ts=ts),
        grid=(s // ts,),
        in_specs=in_specs,
        out_specs=[pl.BlockSpec((ts, nslab, LANES), lambda g: (fblk(g), 0, 0)),
                   pl.BlockSpec((ts, nslab, LANES), lambda g: (bblk(g), 0, 0))],
        out_shape=[yshape, yshape],
        scratch_shapes=[pltpu.VMEM((2, NKL, B_HEAD_DIM, LANES), F32),
                        pltpu.VMEM((2, B_HEAD_DIM, LANES), F32)],
        compiler_params=_params(("arbitrary",)),
        name="wkv7_scan",
    )(*args)


def _rwkv_out_kernel(yf_ref, yb_ref, v_ref, bonus_ref, pg_ref, lng_ref, lnb_ref, gup_ref,
                     selv_ref, g32_ref, o_ref):
    g32 = g32_ref[...]
    gate = jnp.dot(jax.nn.sigmoid(pg_ref[...]).astype(BF16), gup_ref[...], preferred_element_type=F32)
    bonus = _gsum(bonus_ref[...], g32)
    nslab = MERGED // LANES
    ys = [yf_ref[:, j, :] + yb_ref[:, j, :] for j in range(nslab)]
    inv = 1.0 / B_HEAD_DIM
    mu = _gsum(functools.reduce(lambda a, b_: a + b_, ys), g32) * inv
    ds = [y - mu for y in ys]
    var = _gsum(functools.reduce(lambda a, b_: a + b_, [x * x for x in ds]), g32) * inv
    rstd = lax.rsqrt(var + LN_X_EPS)
    outs = []
    for j in range(nslab):
        cs = slice(j * LANES, (j + 1) * LANES)
        yn = ds[j] * rstd * lng_ref[:, cs] + lnb_ref[:, cs]
        outs.append(((yn + bonus * v_ref[:, cs]) * gate[:, cs]).astype(BF16))
    for j in range(nslab // 2):
        pair = jnp.concatenate([outs[2 * j], outs[2 * j + 1]], axis=1)
        for bi in range(2):
            o_ref[bi, :, j * LANES:(j + 1) * LANES] = jnp.dot(
                pair, selv_ref[bi], preferred_element_type=F32).astype(o_ref.dtype)


def _rwkv_out(yf, yb, v2d, bonus, p, lng_m, lnb_m, gup_m, selv, g32, *, rows):
    tm = 256
    nslab = MERGED // LANES
    y3s = pl.BlockSpec((tm, nslab, LANES), lambda i: (i, 0, 0))
    full2 = lambda i: (0, 0)
    return pl.pallas_call(
        _rwkv_out_kernel,
        grid=(rows // tm,),
        in_specs=[
            y3s, y3s,
            pl.BlockSpec((tm, MERGED), lambda i: (i, 0)),
            pl.BlockSpec((tm, LANES), lambda i: (i, 0)),
            pl.BlockSpec((tm, 2 * GATE_LORA), lambda i: (i, 3 * MERGED // (2 * GATE_LORA))),
            pl.BlockSpec((1, MERGED), full2),
            pl.BlockSpec((1, MERGED), full2),
            pl.BlockSpec((2 * GATE_LORA, MERGED), full2),
            pl.BlockSpec((2, 2 * LANES, LANES), lambda i: (0, 0, 0)),
            pl.BlockSpec((LANES, LANES), full2),
        ],
        out_specs=pl.BlockSpec((2, tm, B_WIDTH), lambda i: (0, i, 0)),
        out_shape=jax.ShapeDtypeStruct((2, rows, B_WIDTH), BF16),
        compiler_params=_params(("parallel",)),
        name="rwkv_out",
    )(yf, yb, v2d, bonus, p, lng_m, lnb_m, gup_m, selv, g32)


def _kmerge_cols(w):
    r = w.shape[0]
    wt = w.reshape(r, B_HEADS, NKL, KQ).transpose(0, 2, 3, 1)
    z = jnp.zeros_like(wt)
    return jnp.stack([jnp.stack([wt, z], axis=3), jnp.stack([z, wt], axis=3)]).reshape(2, r, MERGED)


def _vmerge_cols(w):
    r = w.shape[0]
    wt = w.reshape(r, B_HEADS, B_HEAD_DIM).transpose(0, 2, 1)
    z = jnp.zeros_like(wt)
    return jnp.stack([jnp.stack([wt, z], axis=2), jnp.stack([z, wt], axis=2)]).reshape(2, r, MERGED)


def _bmerge_cols(w, width):
    wp = jnp.pad(w, ((0, 0), (0, width - w.shape[1])))
    z = jnp.zeros_like(wp)
    return jnp.stack([jnp.concatenate([wp, z], axis=1), jnp.concatenate([z, wp], axis=1)])


def _both(m):
    return m[0] + m[1]


def _selectors():
    ci = jnp.arange(LANES)[None, :]
    bsel = jnp.arange(2)[:, None, None]
    c_vi, c_h = ci // B_HEADS, ci % B_HEADS
    r2 = jnp.arange(2 * LANES)[:, None]
    selv = ((r2 // BH == c_vi) & ((r2 % BH) // B_HEADS == bsel) & (r2 % B_HEADS == c_h)).astype(BF16)
    r1 = jnp.arange(LANES)[:, None]
    g32 = (r1 % BH == ci % BH).astype(BF16)
    rep = jnp.stack([((r1 // BH == vi) & (r1 % BH == ci % BH)) for vi in range(LANES // BH)]).astype(BF16)
    return selv, g32, rep


def _merge_kernel(x_ref, h_ref, ya_ref, yb_ref, yc_ref, mod_ref, wg_ref, bg_ref, wb_ref, wo_ref,
                  o_ref, acc_sc, *, nz):
    j = pl.program_id(1)

    @pl.when(j == 0)
    def _():
        acc_sc[...] = jnp.zeros_like(acc_sc)

    h = h_ref[...]
    z = None
    for i, y_ref in enumerate((ya_ref, yb_ref, yc_ref)):
        gate = jax.nn.sigmoid(jnp.dot(h, wg_ref[i], preferred_element_type=F32) + bg_ref[i])
        term = gate * jnp.dot(y_ref[...], wb_ref[i], preferred_element_type=F32)
        z = term if z is None else z + term
    acc_sc[...] += jnp.dot(z.astype(BF16), wo_ref[...], preferred_element_type=F32)

    @pl.when(j == nz - 1)
    def _():
        o_ref[...] = x_ref[...] + mod_ref[5:6, :] * acc_sc[...]


def _merge(xs, h, ya, yb, yc, mod, wg, bg, wb, wo, mod_idx, n_tiles):
    d = xs.shape[1]
    tz = 256
    nz = d // tz
    rowt = lambda i, j: (i, 0)
    return pl.pallas_call(
        functools.partial(_merge_kernel, nz=nz),
        grid=(n_tiles, nz),
        in_specs=[
            pl.BlockSpec((TM, d), rowt),
            pl.BlockSpec((TM, d), rowt),
            pl.BlockSpec((TM, A_WIDTH), rowt),
            pl.BlockSpec((TM, B_WIDTH), rowt),
            pl.BlockSpec((TM, C_WIDTH), rowt),
            pl.BlockSpec((None, N_MOD, d), lambda i, j: (mod_idx(i), 0, 0)),
            pl.BlockSpec((N_BRANCH, d, tz), lambda i, j: (0, 0, j)),
            pl.BlockSpec((N_BRANCH, 1, tz), lambda i, j: (0, 0, j)),
            pl.BlockSpec((N_BRANCH, A_WIDTH, tz), lambda i, j: (0, 0, j)),
            pl.BlockSpec((tz, d), lambda i, j: (j, 0)),
        ],
        out_specs=pl.BlockSpec((TM, d), rowt),
        out_shape=jax.ShapeDtypeStruct((n_tiles * TM, d), F32),
        scratch_shapes=[pltpu.VMEM((TM, d), F32)],
        compiler_params=_params(("parallel", "arbitrary")),
        name="merge",
    )(xs, h, ya, yb, yc, mod, wg, bg, wb, wo)


def _rope_tables(b, t, c):
    rows = t // GRID_W
    rowp = jnp.repeat(jnp.arange(rows), GRID_W).astype(F32)
    colp = jnp.tile(jnp.arange(GRID_W), rows).astype(F32)
    inv = 1.0 / (ROPE_BASE ** (jnp.arange(0, ROPE_AXIS_DIM, 2, dtype=F32) / ROPE_AXIS_DIM))
    ar, ac = rowp[:, None] * inv, colp[:, None] * inv
    cr, sr, cc, sc = jnp.cos(ar), jnp.sin(ar), jnp.cos(ac), jnp.sin(ac)
    cos64 = jnp.concatenate([cr, cr, cc, cc], axis=-1)
    sin64 = jnp.concatenate([-sr, sr, -sc, sc], axis=-1)
    cos_t = jnp.tile(cos64, (b, 2))
    sin_t = jnp.tile(sin64, (b, 2))
    cos_t = jnp.concatenate([cos_t, jnp.ones((b * c, LANES), F32)])
    sin_t = jnp.concatenate([sin_t, jnp.zeros((b * c, LANES), F32)])
    return cos_t, sin_t


def kernel(x, c, ctx, c_ctx, w_ada, b_ada, norm_g, ffn_w_in, ffn_w_out, w_in, gm_v_norm, gm_ws, gm_bs,
           rw_conv, rw_w0, rw_w_up, rw_a0, rw_a_up, rw_g_up, rw_k_k, rw_k_a, rw_r_k, rw_ln_g, rw_ln_b,
           da_q_norm, da_k_norm, da_lam, da_subln, w_branch, b_gate, w_out):
    b, t, d = x.shape
    cl = ctx.shape[1]
    depth = w_ada.shape[0]
    d_ff = ffn_w_out.shape[2]
    assert b == 2 and b * cl == TM and t % TM == 0 and cl % 256 == 0
    n_lat = (b * t) // TM
    n_all = n_lat + 1
    tiles_per_batch = t // TM
    mod_idx = lambda i: jnp.where(i < n_lat, i // tiles_per_batch, b)

    xs = jnp.concatenate([x.reshape(b * t, d), ctx.reshape(b * cl, d)])
    cvec = jnp.zeros((SUBLANES, d), F32).at[:b].set(c).at[b].set(c_ctx)
    b_ada3 = b_ada.reshape(depth, 1, N_MOD * d)
    cos_t, sin_t = _rope_tables(b, t, cl)
    lane = jnp.arange(LANES)
    g64 = (lane[:, None] // C_HEAD_DIM == lane[None, :] // C_HEAD_DIM).astype(BF16)
    selv, g32, rep = _selectors()
    ffp = -(-d_ff // TF) * TF

    o = 0
    offs = []
    for n in (A_WIDTH, A_WIDTH, 3 * B_WIDTH, GATE_LORA, 2 * DECAY_LORA, 2 * ICL_LORA, 3 * C_WIDTH, N_BRANCH * d):
        offs.append((o, o + n))
        o += n

    for l in range(depth):
        last = l == depth - 1
        lam_init = 0.8 - 0.6 * math.exp(-0.3 * l)
        mod = _ada(cvec, w_ada, b_ada3, l)[:b + 1].reshape(b + 1, N_MOD, d)

        def ffn_weights(w):
            wi = jnp.transpose(ffn_w_in[l, w].reshape(d, 2, d_ff), (1, 0, 2))
            w_gu = jnp.pad(wi, ((0, 0), (0, 0), (0, ffp - d_ff))).astype(BF16)
            wo = jnp.pad(ffn_w_out[l, w], ((0, ffp - d_ff), (0, 0))).astype(BF16)
            return w_gu, wo

        w_gu0, w_o0 = ffn_weights(0)
        xs, h = _ffn(xs, mod, norm_g[l], w_gu0, w_o0, mod_idx, n_tiles=n_all, mi=0, emit_h=True)

        wl = w_in[l]
        sl = lambda i: wl[:, offs[i][0]:offs[i][1]]
        n_mix = n_lat if last else n_all

        w_uv = jnp.concatenate([sl(0), sl(1)], axis=1).astype(BF16)
        bsb = jnp.broadcast_to(gm_bs[l][:, :, None], (A_GROUPS, CHUNK, A_WIDTH // A_GROUPS))
        ya = _gmlp(h, w_uv, gm_v_norm[l].reshape(1, A_WIDTH), gm_ws[l], bsb, n_mix)

        w_rkv, w_dec, w_icl = sl(2), sl(4), sl(5)
        wz = jnp.concatenate(
            [_kmerge_cols(w_rkv[:, :B_WIDTH]), _kmerge_cols(w_rkv[:, B_WIDTH:2 * B_WIDTH]),
             _vmerge_cols(w_rkv[:, 2 * B_WIDTH:]), _bmerge_cols(sl(3), GATE_LORA),
             _bmerge_cols(w_dec[:, :DECAY_LORA], LORA_PAD), _bmerge_cols(w_dec[:, DECAY_LORA:], LORA_PAD),
             _bmerge_cols(w_icl[:, :ICL_LORA], LORA_PAD), _bmerge_cols(w_icl[:, ICL_LORA:], LORA_PAD)],
            axis=2).astype(BF16)
        p = _rkv_proj(h, wz, t=t, c=cl)
        conv = rw_conv[l]
        conv_m = jnp.concatenate([_both(_kmerge_cols(conv[:, :B_WIDTH])),
                                  _both(_kmerge_cols(conv[:, B_WIDTH:2 * B_WIDTH])),
                                  _both(_vmerge_cols(conv[:, 2 * B_WIDTH:]))], axis=1)
        up_m = lambda w: jnp.stack([
            _kmerge_cols(jnp.pad(w[dd], ((0, LORA_PAD - w.shape[1]), (0, 0)))).reshape(2 * LORA_PAD, MERGED)
            for dd in range(2)])
        r3, kk3, w4, kd4, b4, vrep, v2d, bonus = _rwkv_prep(
            p, conv_m, _both(_kmerge_cols(rw_w0[l])), up_m(rw_w_up[l]), _both(_kmerge_cols(rw_a0[l])),
            up_m(rw_a_up[l]), _both(_kmerge_cols(rw_k_k[l].reshape(1, B_WIDTH))),
            _both(_kmerge_cols(rw_k_a[l].reshape(1, B_WIDTH))),
            _both(_kmerge_cols(rw_r_k[l].reshape(1, B_WIDTH))), g32, rep, t=t, c=cl)
        yf, ybk = _scan(w4, kd4, b4, kk3, r3, vrep, t=t, c=cl)
        yb2 = _rwkv_out(yf, ybk, v2d, bonus, p, _both(_vmerge_cols(rw_ln_g[l].reshape(1, B_WIDTH))),
                        _both(_vmerge_cols(rw_ln_b[l].reshape(1, B_WIDTH))),
                        _vmerge_cols(rw_g_up[l]).reshape(2 * GATE_LORA, MERGED).astype(BF16),
                        selv, g32, rows=t if last else t + cl)
        if last:
            yb = yb2.reshape(b * t, B_WIDTH)
        else:
            yb = jnp.concatenate([yb2[0, :t], yb2[1, :t], yb2[0, t:], yb2[1, t:]])

        gains = jnp.stack([jnp.tile(da_q_norm[l], 2) * (C_HEAD_DIM ** -0.5 * math.log2(math.e)),
                           jnp.tile(da_k_norm[l], 2), jnp.ones((LANES,), F32)]).reshape(3, 1, LANES)
        qk, vx = _qkv(h, sl(6).astype(BF16), gains, cos_t, sin_t, g64)
        sub = da_subln[l].reshape(1, LANES)
        yc = _attn(qk, vx, da_lam[l], sub, None, b=b, t=t, c=cl, lam_init=lam_init, ctx_queries=False)
        if not last:
            yc = _attn(qk, vx, da_lam[l], sub, yc, b=b, t=t, c=cl, lam_init=lam_init, ctx_queries=True)

        wg = jnp.transpose(sl(7).reshape(d, N_BRANCH, d), (1, 0, 2)).astype(BF16)
        wb = w_branch[l]
        wb_b = wb[1].reshape(B_HEADS, B_HEAD_DIM, d).transpose(1, 0, 2).reshape(B_WIDTH, d)
        wbr = jnp.stack([wb[0], wb_b, wb[2]]).astype(BF16)
        xs = _merge(xs, h, ya, yb, yc, mod, wg, b_gate[l].reshape(N_BRANCH, 1, d),
                    wbr, w_out[l].astype(BF16), mod_idx, n_mix)

        w_gu1, w_o1 = ffn_weights(1)
        xs = _ffn(xs, mod, norm_g[l], w_gu1, w_o1, mod_idx, n_tiles=n_mix, mi=2, emit_h=False)

    return xs[:b * t].reshape(b, t, d)
```

```python
import functools
import math

import jax
import jax.numpy as jnp
from jax import lax
from jax.experimental import pallas as pl
from jax.experimental.pallas import tpu as pltpu

F32 = jnp.float32
BF16 = jnp.bfloat16
HIGHEST = lax.Precision.HIGHEST

N_MOD = 9
CHUNK = 128
A_WIDTH = 1024
A_GROUPS = 8
B_WIDTH = 1024
B_HEAD_DIM = 64
B_HEADS = 16
DECAY_LORA = 96
ICL_LORA = 96
GATE_LORA = 256
C_HEADS = 8
C_HEAD_DIM = 64
C_WIDTH = 1024
N_BRANCH = 3
GRID_W = 64
ROPE_BASE = 10000.0
ROPE_AXIS_DIM = 32
NORM_EPS = 1e-6
LN_X_EPS = 64e-5

LANES = 128
SUBLANES = 8
VMEM_LIMIT = 56 * 1024 * 1024

TM = 512
TF = 512
LORA_PAD = 128
SCAN_TS = 64
ATT_TQ = 512
ATT_KC = 512


def _params(sem):
    return pltpu.CompilerParams(dimension_semantics=sem, vmem_limit_bytes=VMEM_LIMIT)


def _rms(x, eps):
    return x * lax.rsqrt(jnp.mean(x * x, axis=-1, keepdims=True) + eps)


def _split3(s):
    hi = s.astype(BF16)
    r = s - hi.astype(F32)
    mid = r.astype(BF16)
    lo = (r - mid.astype(F32)).astype(BF16)
    return hi, mid, lo


def _gsum(s, g_bf16):
    out = None
    for piece in _split3(s):
        d = jnp.dot(piece, g_bf16, preferred_element_type=F32)
        out = d if out is None else out + d
    return out


def _gelu_tanh(x):
    cdf = 0.5 * (1.0 + jnp.tanh(math.sqrt(2.0 / math.pi) * (x + 0.044715 * (x * x * x))))
    return x * cdf


def _ada_kernel(c_ref, w_ref, b_ref, o_ref):
    s = c_ref[...]
    s = s * jax.nn.sigmoid(s)
    o_ref[...] = jnp.dot(s, w_ref[...], precision=HIGHEST, preferred_element_type=F32) + b_ref[...]


def _ada(cvec, w_ada, b_ada3, l):
    d = cvec.shape[1]
    n = w_ada.shape[2]
    tn = 1024
    return pl.pallas_call(
        _ada_kernel,
        grid=(n // tn,),
        in_specs=[
            pl.BlockSpec((SUBLANES, d), lambda j: (0, 0)),
            pl.BlockSpec((None, d, tn), lambda j: (l, 0, j)),
            pl.BlockSpec((None, 1, tn), lambda j: (l, 0, j)),
        ],
        out_specs=pl.BlockSpec((SUBLANES, tn), lambda j: (0, j)),
        out_shape=jax.ShapeDtypeStruct((SUBLANES, n), F32),
        compiler_params=_params(("arbitrary",)),
        name="ada",
    )(cvec, w_ada, b_ada3)


def _ffn_kernel(x_ref, mod_ref, g_ref, win_ref, wout_ref, *rest, tf, nf, last_valid, mi, emit_h):
    if emit_h:
        o_ref, h_ref, hn_sc, acc_sc = rest
    else:
        o_ref, hn_sc, acc_sc = rest
    f = pl.program_id(1)

    @pl.when(f == 0)
    def _():
        xn = _rms(x_ref[...], NORM_EPS) * g_ref[mi:mi + 1, :]
        hn = xn * (1.0 + mod_ref[3 * mi + 1:3 * mi + 2, :]) + mod_ref[3 * mi:3 * mi + 1, :]
        hn_sc[...] = hn.astype(BF16)
        acc_sc[...] = jnp.zeros_like(acc_sc)

    def hidden_tile(valid):
        hn = hn_sc[...]
        g = jnp.dot(hn, win_ref[0], preferred_element_type=F32)
        u = jnp.dot(hn, win_ref[1], preferred_element_type=F32)
        act = g * jax.nn.sigmoid(g) * u
        wout = wout_ref[...]
        if valid < tf:
            act = jnp.where(lax.broadcasted_iota(jnp.int32, (1, tf), 1) < valid, act, 0.0)
            wout = jnp.where(lax.broadcasted_iota(jnp.int32, (tf, 1), 0) < valid, wout, jnp.zeros_like(wout))
        acc_sc[...] += jnp.dot(act.astype(BF16), wout, preferred_element_type=F32)

    if last_valid == tf:
        hidden_tile(tf)
    else:
        pl.when(f < nf - 1)(lambda: hidden_tile(tf))
        pl.when(f == nf - 1)(lambda: hidden_tile(last_valid))

    @pl.when(f == nf - 1)
    def _():
        out = x_ref[...] + 0.5 * mod_ref[3 * mi + 2:3 * mi + 3, :] * acc_sc[...]
        o_ref[...] = out
        if emit_h:
            hn = _rms(out, NORM_EPS) * g_ref[1:2, :]
            h_ref[...] = (hn * (1.0 + mod_ref[4:5, :]) + mod_ref[3:4, :]).astype(BF16)


def _ffn(xs, mod, norm_g, w_gu, w_out, l, w, mod_idx, *, n_tiles, mi, emit_h):
    m, d = xs.shape
    d_ff = w_out.shape[2]
    nf = pl.cdiv(d_ff, TF)
    out_rows = n_tiles * TM
    out_shape = [jax.ShapeDtypeStruct((out_rows, d), F32)]
    out_specs = [pl.BlockSpec((TM, d), lambda i, f: (i, 0))]
    if emit_h:
        out_shape.append(jax.ShapeDtypeStruct((out_rows, d), BF16))
        out_specs.append(pl.BlockSpec((TM, d), lambda i, f: (i, 0)))
    res = pl.pallas_call(
        functools.partial(_ffn_kernel, tf=TF, nf=nf, last_valid=d_ff - (nf - 1) * TF, mi=mi, emit_h=emit_h),
        grid=(n_tiles, nf),
        in_specs=[
            pl.BlockSpec((TM, d), lambda i, f: (i, 0)),
            pl.BlockSpec((None, N_MOD, d), lambda i, f: (mod_idx(i), 0, 0)),
            pl.BlockSpec((3, d), lambda i, f: (0, 0)),
            pl.BlockSpec((None, None, 2, d, TF), lambda i, f: (l, w, 0, 0, f)),
            pl.BlockSpec((None, None, TF, d), lambda i, f: (l, w, f, 0)),
        ],
        out_specs=out_specs,
        out_shape=out_shape,
        scratch_shapes=[pltpu.VMEM((TM, d), BF16), pltpu.VMEM((TM, d), F32)],
        compiler_params=_params(("parallel", "arbitrary")),
        name="ffn",
    )(xs, mod, norm_g, w_gu, w_out)
    return res if emit_h else res[0]


def _gmlp_kernel(h_ref, w_ref, vn_ref, ws_ref, bsb_ref, o_ref, p_sc, *, tm):
    p_sc[...] = jnp.dot(h_ref[...], w_ref[...], preferred_element_type=F32)
    gd = A_WIDTH // A_GROUPS
    for g in range(A_GROUPS):
        wsg = ws_ref[g].astype(BF16)
        bias = bsb_ref[g]
        gain = vn_ref[:, g * gd:(g + 1) * gd]
        for n in range(tm // CHUNK):
            rows = slice(n * CHUNK, (n + 1) * CHUNK)
            u = _gelu_tanh(p_sc[rows, g * gd:(g + 1) * gd])
            v = _gelu_tanh(p_sc[rows, A_WIDTH + g * gd:A_WIDTH + (g + 1) * gd])
            v = _rms(v, NORM_EPS) * gain
            sv = jnp.dot(wsg, v.astype(BF16), preferred_element_type=F32) + bias
            o_ref[rows, g * gd:(g + 1) * gd] = (u * sv).astype(o_ref.dtype)


def _gmlp(h, w_uv, vn, ws, bsb, n_tiles):
    m, d = h.shape
    return pl.pallas_call(
        functools.partial(_gmlp_kernel, tm=TM),
        grid=(n_tiles,),
        in_specs=[
            pl.BlockSpec((TM, d), lambda i: (i, 0)),
            pl.BlockSpec((d, 2 * A_WIDTH), lambda i: (0, 0)),
            pl.BlockSpec((1, A_WIDTH), lambda i: (0, 0)),
            pl.BlockSpec((A_GROUPS, CHUNK, CHUNK), lambda i: (0, 0, 0)),
            pl.BlockSpec((A_GROUPS, CHUNK, A_WIDTH // A_GROUPS), lambda i: (0, 0, 0)),
        ],
        out_specs=pl.BlockSpec((TM, A_WIDTH), lambda i: (i, 0)),
        out_shape=jax.ShapeDtypeStruct((n_tiles * TM, A_WIDTH), BF16),
        scratch_shapes=[pltpu.VMEM((TM, 2 * A_WIDTH), F32)],
        compiler_params=_params(("parallel",)),
        name="gmlp",
    )(h, w_uv, vn, ws, bsb)


def _qkv_kernel(h_ref, w_ref, gain_ref, cos_ref, sin_ref, g64_ref, o_ref, vx_ref, p_sc):
    j = pl.program_id(1)
    p_sc[...] = jnp.dot(h_ref[...], w_ref[...], preferred_element_type=F32)

    @pl.when(j == 2)
    def _():
        ones = jnp.ones((p_sc.shape[0], LANES), BF16)
        for hs in range(C_HEADS):
            vx_ref[:, 2 * hs * LANES:(2 * hs + 1) * LANES] = p_sc[:, hs * LANES:(hs + 1) * LANES].astype(BF16)
            vx_ref[:, (2 * hs + 1) * LANES:(2 * hs + 2) * LANES] = ones

    @pl.when(j < 2)
    def _():
        lane = lax.broadcasted_iota(jnp.int32, (1, LANES), 1)
        first = (lane % ROPE_AXIS_DIM) < (ROPE_AXIS_DIM // 2)
        cos = cos_ref[...]
        sin = sin_ref[...]
        gain = gain_ref[...]
        g64 = g64_ref[...]
        for hs in range(C_HEADS):
            x = p_sc[:, hs * LANES:(hs + 1) * LANES]
            ms = _gsum(x * x, g64) * (1.0 / C_HEAD_DIM)
            y = x * lax.rsqrt(ms + NORM_EPS) * gain
            half = ROPE_AXIS_DIM // 2
            rot = jnp.where(first, pltpu.roll(y, LANES - half, 1), pltpu.roll(y, half, 1))
            o_ref[:, hs * LANES:(hs + 1) * LANES] = (y * cos + rot * sin).astype(BF16)


def _qkv(h, w_qkv, gains, cos_t, sin_t, g64):
    m, d = h.shape
    return pl.pallas_call(
        _qkv_kernel,
        grid=(m // TM, 3),
        in_specs=[
            pl.BlockSpec((TM, d), lambda i, j: (i, 0)),
            pl.BlockSpec((d, C_WIDTH), lambda i, j: (0, j)),
            pl.BlockSpec((None, 1, LANES), lambda i, j: (j, 0, 0)),
            pl.BlockSpec((TM, LANES), lambda i, j: (i, 0)),
            pl.BlockSpec((TM, LANES), lambda i, j: (i, 0)),
            pl.BlockSpec((LANES, LANES), lambda i, j: (0, 0)),
        ],
        out_specs=[pl.BlockSpec((None, TM, C_WIDTH), lambda i, j: (jnp.minimum(j, 1), i, 0)),
                   pl.BlockSpec((TM, 2 * C_WIDTH), lambda i, j: (i, 0))],
        out_shape=[jax.ShapeDtypeStruct((2, m, C_WIDTH), BF16), jax.ShapeDtypeStruct((m, 2 * C_WIDTH), BF16)],
        scratch_shapes=[pltpu.VMEM((TM, C_WIDTH), F32)],
        compiler_params=_params(("parallel", "arbitrary")),
        name="qkv",
    )(h, w_qkv, gains, cos_t, sin_t, g64)


def _attn_kernel(lam_ref, sub_ref, q_ref, *refs, nseg, lam_init):
    kv = refs[:2 * nseg]
    o_ref = refs[2 * nseg]
    lv = lam_ref[...]
    lam = (jnp.exp(jnp.sum(lv[0:1] * lv[1:2], axis=-1, keepdims=True))
           - jnp.exp(jnp.sum(lv[2:3] * lv[3:4], axis=-1, keepdims=True)) + lam_init)
    q = q_ref[...]
    lane = lax.broadcasted_iota(jnp.int32, (1, LANES), 1)
    zero = jnp.zeros_like(q)
    outs = []
    for j in range(2):
        in_map = (lane < C_HEAD_DIM) if j == 0 else (lane >= C_HEAD_DIM)
        qj = jnp.where(in_map, q, zero)
        m = None
        acc = None
        for si in range(nseg):
            k_ref, v_ref = kv[2 * si], kv[2 * si + 1]
            ls = k_ref.shape[0]
            for c0 in range(0, ls, ATT_KC):
                kc = min(ATT_KC, ls - c0)
                s = lax.dot_general(qj, k_ref[c0:c0 + kc, :], (((1,), (1,)), ((), ())),
                                    preferred_element_type=F32)
                cm = jnp.max(s, axis=-1, keepdims=True)
                m_new = cm if m is None else jnp.maximum(m, cm)
                e = jnp.exp2((s - m_new).astype(BF16))
                pv = jnp.dot(e, v_ref[c0:c0 + kc, :], preferred_element_type=F32)
                acc = pv if acc is None else acc * jnp.exp2(m - m_new) + pv
                m = m_new
        outs.append(acc[:, :LANES] / acc[:, LANES:])
    o = outs[0] - lam * outs[1]
    o = _rms(o, NORM_EPS) * sub_ref[...]
    o_ref[...] = (o * (1.0 - lam_init)).astype(o_ref.dtype)


def _attn(qk, vx, lam, subln, yc_prev, *, b, t, c, lam_init, ctx_queries):
    m = qk.shape[1]
    if ctx_queries:
        tq, nq = c, 1
        q_blk = lambda bi, qi: (b * t) // c + bi
        segs = [(c, lambda bi: (b * t) // c + bi)]
    else:
        tq = ATT_TQ
        nq = t // tq
        q_blk = lambda bi, qi: bi * nq + qi
        segs = [(t, lambda bi: bi), (c, lambda bi: (b * t) // c + bi)]
    in_specs = [
        pl.BlockSpec((4, C_HEAD_DIM), lambda bi, h, qi: (0, 0)),
        pl.BlockSpec((1, LANES), lambda bi, h, qi: (0, 0)),
        pl.BlockSpec((None, tq, LANES), lambda bi, h, qi: (0, q_blk(bi, qi), h)),
    ]
    args = [lam, subln, qk]
    for ls, blk in segs:
        in_specs.append(pl.BlockSpec((None, ls, LANES), lambda bi, h, qi, blk=blk: (1, blk(bi), h)))
        in_specs.append(pl.BlockSpec((ls, 2 * LANES), lambda bi, h, qi, blk=blk: (blk(bi), h)))
        args += [qk, vx]
    aliases = {}
    if yc_prev is not None:
        in_specs.append(pl.BlockSpec(memory_space=pl.ANY))
        args.append(yc_prev)
        aliases = {len(args) - 1: 0}

    def body(*refs):
        n_in = 3 + 2 * len(segs)
        _attn_kernel(*refs[:n_in], refs[-1], nseg=len(segs), lam_init=lam_init)

    return pl.pallas_call(
        body,
        grid=(b, C_HEADS, nq),
        in_specs=in_specs,
        out_specs=pl.BlockSpec((tq, LANES), lambda bi, h, qi: (q_blk(bi, qi), h)),
        out_shape=jax.ShapeDtypeStruct((m, C_WIDTH), BF16),
        input_output_aliases=aliases,
        compiler_params=_params(("parallel", "parallel", "arbitrary")),
        name="diff_attn_ctx" if ctx_queries else "diff_attn",
    )(*args)


KQ = 4
NKL = B_HEAD_DIM // KQ
MERGED = 2 * B_WIDTH
BH = 2 * B_HEADS
LORA_MERGED = 2 * GATE_LORA + 8 * LORA_PAD
PROJ_MERGED = 3 * MERGED + LORA_MERGED


def _rkv_proj_kernel(h0_ref, h1_ref, w_ref, o_ref):
    o_ref[...] = (jnp.dot(h0_ref[...], w_ref[0], preferred_element_type=F32)
                  + jnp.dot(h1_ref[...], w_ref[1], preferred_element_type=F32))


def _rkv_proj(h, wz, *, t, c):
    d = h.shape[1]
    tm = 256
    tn = 1536
    s = t + c
    nlat = t // tm
    hblk = lambda bi, i: jnp.where(i < nlat, bi * nlat + i, (2 * t + bi * c) // tm + (i - nlat))
    return pl.pallas_call(
        _rkv_proj_kernel,
        grid=(PROJ_MERGED // tn, s // tm),
        in_specs=[
            pl.BlockSpec((tm, d), lambda j, i: (hblk(0, i), 0)),
            pl.BlockSpec((tm, d), lambda j, i: (hblk(1, i), 0)),
            pl.BlockSpec((2, d, tn), lambda j, i: (0, 0, j)),
        ],
        out_specs=pl.BlockSpec((tm, tn), lambda j, i: (i, j)),
        out_shape=jax.ShapeDtypeStruct((s, PROJ_MERGED), F32),
        compiler_params=_params(("parallel", "arbitrary")),
        name="rkv_proj",
    )(h, h, wz)


def _rwkv_prep_kernel(p_ref, prev_ref, next_ref, lo_ref, conv_ref, w0_ref, wup_ref, a0_ref, aup_ref,
                      kkg_ref, ka_ref, rk_ref, g32_ref, rep_ref,
                      r_o, kk_o, w_o, kd_o, b_o, vrep_o, v_o, bonus_o, k_sc, r_sc, *, tm, starts, ends):
    i = pl.program_id(0)
    is_start = functools.reduce(jnp.logical_or, [i == s for s in starts])
    is_end = functools.reduce(jnp.logical_or, [i == s for s in ends])
    row = lax.broadcasted_iota(jnp.int32, (tm, 1), 0)

    def conv(c0):
        cs = slice(c0, c0 + LANES)
        x = p_ref[:, cs]
        pm = jnp.where(is_start, 0.0, prev_ref[SUBLANES - 1:SUBLANES, cs])
        nx = jnp.where(is_end, 0.0, next_ref[0:1, cs])
        xm1 = jnp.where(row == 0, pm, pltpu.roll(x, 1, 0))
        xp1 = jnp.where(row == tm - 1, nx, pltpu.roll(x, tm - 1, 0))
        return xm1 * conv_ref[0:1, cs] + x * conv_ref[1:2, cs] + xp1 * conv_ref[2:3, cs]

    ss = None
    for m in range(NKL):
        cs = slice(m * LANES, (m + 1) * LANES)
        r = conv(m * LANES)
        r_o[:, m, :] = r
        r_sc[:, cs] = r
        k = conv(MERGED + m * LANES)
        k_sc[:, cs] = k
        kkr = k * kkg_ref[:, cs]
        ss = kkr * kkr if ss is None else ss + kkr * kkr
    rs = lax.rsqrt(_gsum(ss, g32_ref[...]) + 1e-12)

    w_off = 2 * GATE_LORA
    a_off = w_off + 4 * LORA_PAD
    tw = [jnp.tanh(lo_ref[:, w_off + 2 * d * LORA_PAD:w_off + 2 * (d + 1) * LORA_PAD]) for d in range(2)]
    pa = [lo_ref[:, a_off + 2 * d * LORA_PAD:a_off + 2 * (d + 1) * LORA_PAD] for d in range(2)]
    bonus = None
    for m in range(NKL):
        cs = slice(m * LANES, (m + 1) * LANES)
        k = k_sc[:, cs]
        kk = k * kkg_ref[:, cs] * rs
        kk_o[:, m, :] = kk
        kd_sum = None
        for d in range(2):
            w_lo = w0_ref[d:d + 1, cs] + jnp.dot(tw[d], wup_ref[d, :, cs], precision=HIGHEST,
                                                   preferred_element_type=F32)
            w_o[d, :, m, :] = jnp.exp(-jax.nn.sigmoid(w_lo) * math.exp(-0.5))
            a = jax.nn.sigmoid(a0_ref[d:d + 1, cs] + jnp.dot(pa[d], aup_ref[d, :, cs], precision=HIGHEST,
                                                             preferred_element_type=F32))
            kd = k * (1.0 + (a - 1.0) * ka_ref[:, cs])
            kd_o[d, :, m, :] = kd
            kd_sum = kd if kd_sum is None else kd_sum + kd
            b_o[d, :, m, :] = kk * a
        term = r_sc[:, cs] * (0.5 * kd_sum) * rk_ref[:, cs]
        bonus = term if bonus is None else bonus + term
    bonus_o[...] = bonus

    per_slab = LANES // BH
    for j in range(MERGED // LANES):
        vs = conv(2 * MERGED + j * LANES)
        v_o[:, j * LANES:(j + 1) * LANES] = vs
        for vi in range(per_slab):
            vrep_o[:, per_slab * j + vi, :] = _gsum(vs, rep_ref[vi])


def _rwkv_prep(p, conv_m, w0_m, wup_m, a0_m, aup_m, kk_m, ka_m, rk_m, g32, rep, *, t, c):
    s = t + c
    tm = 128
    nt = s // tm
    starts = [0, t // tm]
    ends = [t // tm - 1, nt - 1]
    hb = tm // SUBLANES
    rkv_w = 3 * MERGED
    full2 = lambda i: (0, 0)
    full3 = lambda i: (0, 0, 0)
    k3 = jax.ShapeDtypeStruct((s, NKL, LANES), F32)
    k4 = jax.ShapeDtypeStruct((2, s, NKL, LANES), F32)
    k3s = pl.BlockSpec((tm, NKL, LANES), lambda i: (i, 0, 0))
    k4s = pl.BlockSpec((2, tm, NKL, LANES), lambda i: (0, i, 0, 0))
    return pl.pallas_call(
        functools.partial(_rwkv_prep_kernel, tm=tm, starts=starts, ends=ends),
        grid=(nt,),
        in_specs=[
            pl.BlockSpec((tm, rkv_w), lambda i: (i, 0)),
            pl.BlockSpec((SUBLANES, rkv_w), lambda i: (jnp.maximum(i * hb - 1, 0), 0)),
            pl.BlockSpec((SUBLANES, rkv_w), lambda i: (jnp.minimum((i + 1) * hb, s // SUBLANES - 1), 0)),
            pl.BlockSpec((tm, LORA_MERGED), lambda i: (i, rkv_w // LORA_MERGED)),
            pl.BlockSpec((3, rkv_w), full2),
            pl.BlockSpec((2, MERGED), full2),
            pl.BlockSpec((2, 2 * LORA_PAD, MERGED), full3),
            pl.BlockSpec((2, MERGED), full2),
            pl.BlockSpec((2, 2 * LORA_PAD, MERGED), full3),
            pl.BlockSpec((1, MERGED), full2),
            pl.BlockSpec((1, MERGED), full2),
            pl.BlockSpec((1, MERGED), full2),
            pl.BlockSpec((LANES, LANES), full2),
            pl.BlockSpec((LANES // BH, LANES, LANES), full3),
        ],
        out_specs=[k3s, k3s, k4s, k4s, k4s,
                   pl.BlockSpec((tm, B_HEAD_DIM, LANES), lambda i: (i, 0, 0)),
                   pl.BlockSpec((tm, MERGED), lambda i: (i, 0)),
                   pl.BlockSpec((tm, LANES), lambda i: (i, 0))],
        out_shape=[k3, k3, k4, k4, k4, jax.ShapeDtypeStruct((s, B_HEAD_DIM, LANES), F32),
                   jax.ShapeDtypeStruct((s, MERGED), F32), jax.ShapeDtypeStruct((s, LANES), F32)],
        scratch_shapes=[pltpu.VMEM((tm, MERGED), F32), pltpu.VMEM((tm, MERGED), F32)],
        compiler_params=_params(("parallel",)),
        name="rwkv_prep",
    )(p, p, p, p, conv_m, w0_m, wup_m, a0_m, aup_m, kk_m, ka_m, rk_m, g32, rep)


def _scan_kernel(*refs, ts):
    ins, (yf_ref, yb_ref, s_sc, y_sc, sa_sc) = refs[:12], refs[12:]
    streams = [ins[:6] + (yf_ref,), ins[6:] + (yb_ref,)]
    nkl = s_sc.shape[1]
    nvb = s_sc.shape[2] // SUBLANES

    @pl.when(pl.program_id(0) == 0)
    def _():
        s_sc[...] = jnp.zeros_like(s_sc)
        y_sc[...] = jnp.zeros_like(y_sc)

    def vsl(vb):
        return slice(vb * SUBLANES, (vb + 1) * SUBLANES)

    diag = (lax.broadcasted_iota(jnp.int32, (SUBLANES, LANES), 1) // BH
            == lax.broadcasted_iota(jnp.int32, (SUBLANES, LANES), 0) % KQ)

    def emit_y(d, y_ref, row):
        for vb in range(nvb):
            y = y_sc[d, vsl(vb), :]
            y = y + pltpu.roll(y, BH, 1)
            y = y + pltpu.roll(y, 2 * BH, 1)
            y = jnp.where(diag, y, 0.0)
            y = y + pltpu.roll(y, 1, 0)
            y = y + pltpu.roll(y, 2, 0)
            for half in range(SUBLANES // KQ):
                srow = half * KQ + KQ - 1
                y_ref[row, 2 * vb + half:2 * vb + half + 1, :] = y[srow:srow + 1, :]

    def allreduce(x):
        x = x + pltpu.roll(x, BH, 1)
        return x + pltpu.roll(x, 2 * BH, 1)

    def step(j, carry):
        ts_of = (j, ts - 1 - j)
        prev = (jnp.maximum(j - 1, 0), jnp.minimum(ts - j, ts - 1))
        for d in range(2):
            emit_y(d, streams[d][-1], prev[d])
        nxt = (jnp.minimum(j + 1, ts - 1), jnp.maximum(ts - 2 - j, 0))
        for d, (w_ref, kd_ref, b_ref, kk_ref, r_ref, v_ref, y_ref) in enumerate(streams):
            t = ts_of[d]
            sa = [sa_sc[d, vsl(vb), :] if d == 0 else allreduce(sa_sc[d, vsl(vb), :]) for vb in range(nvb)]
            ys = [None] * nvb
            sa_next = [None] * nvb
            for kl in range(nkl):
                wrow = w_ref[t, kl:kl + 1, :]
                brow = b_ref[t, kl:kl + 1, :]
                kdrow = kd_ref[t, kl:kl + 1, :]
                rrow = r_ref[t, kl:kl + 1, :]
                kknext = kk_ref[nxt[d], kl:kl + 1, :]
                for vb in range(nvb):
                    sn = s_sc[d, kl, vsl(vb), :] * wrow - sa[vb] * brow + v_ref[t, vsl(vb), :] * kdrow
                    s_sc[d, kl, vsl(vb), :] = sn
                    term = sn * rrow
                    ys[vb] = term if ys[vb] is None else ys[vb] + term
                    term = sn * kknext
                    sa_next[vb] = term if sa_next[vb] is None else sa_next[vb] + term
            for vb in range(nvb):
                y_sc[d, vsl(vb), :] = ys[vb]
                sa_sc[d, vsl(vb), :] = allreduce(sa_next[vb]) if d == 0 else sa_next[vb]
        return carry

    for d, first in enumerate((0, ts - 1)):
        kk_ref = streams[d][3]
        for vb in range(nvb):
            acc = None
            for kl in range(nkl):
                term = s_sc[d, kl, vsl(vb), :] * kk_ref[first, kl:kl + 1, :]
                acc = term if acc is None else acc + term
            sa_sc[d, vsl(vb), :] = allreduce(acc) if d == 0 else acc

    lax.fori_loop(0, ts, step, 0)
    emit_y(0, yf_ref, ts - 1)
    emit_y(1, yb_ref, 0)


def _scan(w4, kd4, b4, kk3, r3, vrep, *, t, c):
    s = t + c
    ts = SCAN_TS
    nctx = c // ts
    nlat = t // ts
    fblk = lambda g: jnp.where(g < nctx, nlat + g, g - nctx)
    bblk = lambda g: jnp.where(g < nctx, nlat + (nctx - 1 - g), nlat - 1 - (g - nctx))
    in_specs = []
    args = []
    for d, blk in enumerate((fblk, bblk)):
        for a in (w4, kd4, b4):
            in_specs.append(pl.BlockSpec((None, ts, NKL, LANES), lambda g, d=d, blk=blk: (d, blk(g), 0, 0)))
            args.append(a)
        for a in (kk3, r3):
            in_specs.append(pl.BlockSpec((ts, NKL, LANES), lambda g, blk=blk: (blk(g), 0, 0)))
            args.append(a)
        in_specs.append(pl.BlockSpec((ts, B_HEAD_DIM, LANES), lambda g, blk=blk: (blk(g), 0, 0)))
        args.append(vrep)
    nslab = MERGED // LANES
    yshape = jax.ShapeDtypeStruct((s, nslab, LANES), F32)
    return pl.pallas_call(
        functools.partial(_scan_kernel, ts=ts),
        grid=(s // ts,),
        in_specs=in_specs,
        out_specs=[pl.BlockSpec((ts, nslab, LANES), lambda g: (fblk(g), 0, 0)),
                   pl.BlockSpec((ts, nslab, LANES), lambda g: (bblk(g), 0, 0))],
        out_shape=[yshape, yshape],
        scratch_shapes=[pltpu.VMEM((2, NKL, B_HEAD_DIM, LANES), F32),
                        pltpu.VMEM((2, B_HEAD_DIM, LANES), F32),
                        pltpu.VMEM((2, B_HEAD_DIM, LANES), F32)],
        compiler_params=_params(("arbitrary",)),
        name="wkv7_scan",
    )(*args)


def _rwkv_out_kernel(yf_ref, yb_ref, v_ref, bonus_ref, pg_ref, lng_ref, lnb_ref, gup_ref,
                     selv_ref, g32_ref, o_ref):
    g32 = g32_ref[...]
    gate = jnp.dot(jax.nn.sigmoid(pg_ref[...]).astype(BF16), gup_ref[...], preferred_element_type=F32)
    bonus = _gsum(bonus_ref[...], g32)
    nslab = MERGED // LANES
    ys = [yf_ref[:, j, :] + yb_ref[:, j, :] for j in range(nslab)]
    inv = 1.0 / B_HEAD_DIM
    mu = _gsum(functools.reduce(lambda a, b_: a + b_, ys), g32) * inv
    ds = [y - mu for y in ys]
    var = _gsum(functools.reduce(lambda a, b_: a + b_, [x * x for x in ds]), g32) * inv
    rstd = lax.rsqrt(var + LN_X_EPS)
    outs = []
    for j in range(nslab):
        cs = slice(j * LANES, (j + 1) * LANES)
        yn = ds[j] * rstd * lng_ref[:, cs] + lnb_ref[:, cs]
        outs.append(((yn + bonus * v_ref[:, cs]) * gate[:, cs]).astype(BF16))
    for j in range(nslab // 2):
        pair = jnp.concatenate([outs[2 * j], outs[2 * j + 1]], axis=1)
        for bi in range(2):
            o_ref[bi, :, j * LANES:(j + 1) * LANES] = jnp.dot(
                pair, selv_ref[bi], preferred_element_type=F32).astype(o_ref.dtype)


def _rwkv_out(yf, yb, v2d, bonus, p, lng_m, lnb_m, gup_m, selv, g32, *, rows):
    tm = 256
    nslab = MERGED // LANES
    y3s = pl.BlockSpec((tm, nslab, LANES), lambda i: (i, 0, 0))
    full2 = lambda i: (0, 0)
    return pl.pallas_call(
        _rwkv_out_kernel,
        grid=(rows // tm,),
        in_specs=[
            y3s, y3s,
            pl.BlockSpec((tm, MERGED), lambda i: (i, 0)),
            pl.BlockSpec((tm, LANES), lambda i: (i, 0)),
            pl.BlockSpec((tm, 2 * GATE_LORA), lambda i: (i, 3 * MERGED // (2 * GATE_LORA))),
            pl.BlockSpec((1, MERGED), full2),
            pl.BlockSpec((1, MERGED), full2),
            pl.BlockSpec((2 * GATE_LORA, MERGED), full2),
            pl.BlockSpec((2, 2 * LANES, LANES), lambda i: (0, 0, 0)),
            pl.BlockSpec((LANES, LANES), full2),
        ],
        out_specs=pl.BlockSpec((2, tm, B_WIDTH), lambda i: (0, i, 0)),
        out_shape=jax.ShapeDtypeStruct((2, rows, B_WIDTH), BF16),
        compiler_params=_params(("parallel",)),
        name="rwkv_out",
    )(yf, yb, v2d, bonus, p, lng_m, lnb_m, gup_m, selv, g32)


def _kmerge_cols(w):
    r = w.shape[0]
    wt = w.reshape(r, B_HEADS, NKL, KQ).transpose(0, 2, 3, 1)
    z = jnp.zeros_like(wt)
    return jnp.stack([jnp.stack([wt, z], axis=3), jnp.stack([z, wt], axis=3)]).reshape(2, r, MERGED)


def _vmerge_cols(w):
    r = w.shape[0]
    wt = w.reshape(r, B_HEADS, B_HEAD_DIM).transpose(0, 2, 1)
    z = jnp.zeros_like(wt)
    return jnp.stack([jnp.stack([wt, z], axis=2), jnp.stack([z, wt], axis=2)]).reshape(2, r, MERGED)


def _bmerge_cols(w, width):
    wp = jnp.pad(w, ((0, 0), (0, width - w.shape[1])))
    z = jnp.zeros_like(wp)
    return jnp.stack([jnp.concatenate([wp, z], axis=1), jnp.concatenate([z, wp], axis=1)])


def _both(m):
    return m[0] + m[1]


def _selectors():
    ci = jnp.arange(LANES)[None, :]
    bsel = jnp.arange(2)[:, None, None]
    c_vi, c_h = ci // B_HEADS, ci % B_HEADS
    r2 = jnp.arange(2 * LANES)[:, None]
    selv = ((r2 // BH == c_vi) & ((r2 % BH) // B_HEADS == bsel) & (r2 % B_HEADS == c_h)).astype(BF16)
    r1 = jnp.arange(LANES)[:, None]
    g32 = (r1 % BH == ci % BH).astype(BF16)
    rep = jnp.stack([((r1 // BH == vi) & (r1 % BH == ci % BH)) for vi in range(LANES // BH)]).astype(BF16)
    return selv, g32, rep


def _merge_kernel(x_ref, h_ref, ya_ref, yb_ref, yc_ref, mod_ref, wg_ref, bg_ref, wb_ref, wo_ref,
                  o_ref, acc_sc, *, nz):
    j = pl.program_id(1)

    @pl.when(j == 0)
    def _():
        acc_sc[...] = jnp.zeros_like(acc_sc)

    h = h_ref[...]
    z = None
    for i, y_ref in enumerate((ya_ref, yb_ref, yc_ref)):
        gate = jax.nn.sigmoid(jnp.dot(h, wg_ref[i], preferred_element_type=F32) + bg_ref[i])
        term = gate * jnp.dot(y_ref[...], wb_ref[i], preferred_element_type=F32)
        z = term if z is None else z + term
    acc_sc[...] += jnp.dot(z.astype(BF16), wo_ref[...], preferred_element_type=F32)

    @pl.when(j == nz - 1)
    def _():
        o_ref[...] = x_ref[...] + mod_ref[5:6, :] * acc_sc[...]


def _merge(xs, h, ya, yb, yc, mod, wg, bg, wb, wo, mod_idx, n_tiles):
    d = xs.shape[1]
    tz = 256
    nz = d // tz
    rowt = lambda i, j: (i, 0)
    return pl.pallas_call(
        functools.partial(_merge_kernel, nz=nz),
        grid=(n_tiles, nz),
        in_specs=[
            pl.BlockSpec((TM, d), rowt),
            pl.BlockSpec((TM, d), rowt),
            pl.BlockSpec((TM, A_WIDTH), rowt),
            pl.BlockSpec((TM, B_WIDTH), rowt),
            pl.BlockSpec((TM, C_WIDTH), rowt),
            pl.BlockSpec((None, N_MOD, d), lambda i, j: (mod_idx(i), 0, 0)),
            pl.BlockSpec((N_BRANCH, d, tz), lambda i, j: (0, 0, j)),
            pl.BlockSpec((N_BRANCH, 1, tz), lambda i, j: (0, 0, j)),
            pl.BlockSpec((N_BRANCH, A_WIDTH, tz), lambda i, j: (0, 0, j)),
            pl.BlockSpec((tz, d), lambda i, j: (j, 0)),
        ],
        out_specs=pl.BlockSpec((TM, d), rowt),
        out_shape=jax.ShapeDtypeStruct((n_tiles * TM, d), F32),
        scratch_shapes=[pltpu.VMEM((TM, d), F32)],
        compiler_params=_params(("parallel", "arbitrary")),
        name="merge",
    )(xs, h, ya, yb, yc, mod, wg, bg, wb, wo)


def _rope_tables(b, t, c):
    rows = t // GRID_W
    rowp = jnp.repeat(jnp.arange(rows), GRID_W).astype(F32)
    colp = jnp.tile(jnp.arange(GRID_W), rows).astype(F32)
    inv = 1.0 / (ROPE_BASE ** (jnp.arange(0, ROPE_AXIS_DIM, 2, dtype=F32) / ROPE_AXIS_DIM))
    ar, ac = rowp[:, None] * inv, colp[:, None] * inv
    cr, sr, cc, sc = jnp.cos(ar), jnp.sin(ar), jnp.cos(ac), jnp.sin(ac)
    cos64 = jnp.concatenate([cr, cr, cc, cc], axis=-1)
    sin64 = jnp.concatenate([-sr, sr, -sc, sc], axis=-1)
    cos_t = jnp.tile(cos64, (b, 2))
    sin_t = jnp.tile(sin64, (b, 2))
    cos_t = jnp.concatenate([cos_t, jnp.ones((b * c, LANES), F32)])
    sin_t = jnp.concatenate([sin_t, jnp.zeros((b * c, LANES), F32)])
    return cos_t, sin_t


def kernel(x, c, ctx, c_ctx, w_ada, b_ada, norm_g, ffn_w_in, ffn_w_out, w_in, gm_v_norm, gm_ws, gm_bs,
           rw_conv, rw_w0, rw_w_up, rw_a0, rw_a_up, rw_g_up, rw_k_k, rw_k_a, rw_r_k, rw_ln_g, rw_ln_b,
           da_q_norm, da_k_norm, da_lam, da_subln, w_branch, b_gate, w_out):
    b, t, d = x.shape
    cl = ctx.shape[1]
    depth = w_ada.shape[0]
    d_ff = ffn_w_out.shape[2]
    assert b == 2 and b * cl == TM and t % TM == 0 and cl % 256 == 0
    n_lat = (b * t) // TM
    n_all = n_lat + 1
    tiles_per_batch = t // TM
    mod_idx = lambda i: jnp.where(i < n_lat, i // tiles_per_batch, b)

    xs = jnp.concatenate([x.reshape(b * t, d), ctx.reshape(b * cl, d)])
    cvec = jnp.zeros((SUBLANES, d), F32).at[:b].set(c).at[b].set(c_ctx)
    b_ada3 = b_ada.reshape(depth, 1, N_MOD * d)
    cos_t, sin_t = _rope_tables(b, t, cl)
    lane = jnp.arange(LANES)
    g64 = (lane[:, None] // C_HEAD_DIM == lane[None, :] // C_HEAD_DIM).astype(BF16)
    selv, g32, rep = _selectors()
    w_gu = jnp.transpose(ffn_w_in.reshape(depth, 2, d, 2, d_ff), (0, 1, 3, 2, 4)).astype(BF16)
    w_dn = ffn_w_out.astype(BF16)

    o = 0
    offs = []
    for n in (A_WIDTH, A_WIDTH, 3 * B_WIDTH, GATE_LORA, 2 * DECAY_LORA, 2 * ICL_LORA, 3 * C_WIDTH, N_BRANCH * d):
        offs.append((o, o + n))
        o += n

    for l in range(depth):
        last = l == depth - 1
        lam_init = 0.8 - 0.6 * math.exp(-0.3 * l)
        mod = _ada(cvec, w_ada, b_ada3, l)[:b + 1].reshape(b + 1, N_MOD, d)

        xs, h = _ffn(xs, mod, norm_g[l], w_gu, w_dn, l, 0, mod_idx, n_tiles=n_all, mi=0, emit_h=True)

        wl = w_in[l]
        sl = lambda i: wl[:, offs[i][0]:offs[i][1]]
        n_mix = n_lat if last else n_all

        w_uv = jnp.concatenate([sl(0), sl(1)], axis=1).astype(BF16)
        bsb = jnp.broadcast_to(gm_bs[l][:, :, None], (A_GROUPS, CHUNK, A_WIDTH // A_GROUPS))
        ya = _gmlp(h, w_uv, gm_v_norm[l].reshape(1, A_WIDTH), gm_ws[l], bsb, n_mix)

        w_rkv, w_dec, w_icl = sl(2), sl(4), sl(5)
        wz = jnp.concatenate(
            [_kmerge_cols(w_rkv[:, :B_WIDTH]), _kmerge_cols(w_rkv[:, B_WIDTH:2 * B_WIDTH]),
             _vmerge_cols(w_rkv[:, 2 * B_WIDTH:]), _bmerge_cols(sl(3), GATE_LORA),
             _bmerge_cols(w_dec[:, :DECAY_LORA], LORA_PAD), _bmerge_cols(w_dec[:, DECAY_LORA:], LORA_PAD),
             _bmerge_cols(w_icl[:, :ICL_LORA], LORA_PAD), _bmerge_cols(w_icl[:, ICL_LORA:], LORA_PAD)],
            axis=2).astype(BF16)
        p = _rkv_proj(h, wz, t=t, c=cl)
        conv = rw_conv[l]
        conv_m = jnp.concatenate([_both(_kmerge_cols(conv[:, :B_WIDTH])),
                                  _both(_kmerge_cols(conv[:, B_WIDTH:2 * B_WIDTH])),
                                  _both(_vmerge_cols(conv[:, 2 * B_WIDTH:]))], axis=1)
        up_m = lambda w: jnp.stack([
            _kmerge_cols(jnp.pad(w[dd], ((0, LORA_PAD - w.shape[1]), (0, 0)))).reshape(2 * LORA_PAD, MERGED)
            for dd in range(2)])
        r3, kk3, w4, kd4, b4, vrep, v2d, bonus = _rwkv_prep(
            p, conv_m, _both(_kmerge_cols(rw_w0[l])), up_m(rw_w_up[l]), _both(_kmerge_cols(rw_a0[l])),
            up_m(rw_a_up[l]), _both(_kmerge_cols(rw_k_k[l].reshape(1, B_WIDTH))),
            _both(_kmerge_cols(rw_k_a[l].reshape(1, B_WIDTH))),
            _both(_kmerge_cols(rw_r_k[l].reshape(1, B_WIDTH))), g32, rep, t=t, c=cl)
        yf, ybk = _scan(w4, kd4, b4, kk3, r3, vrep, t=t, c=cl)
        yb2 = _rwkv_out(yf, ybk, v2d, bonus, p, _both(_vmerge_cols(rw_ln_g[l].reshape(1, B_WIDTH))),
                        _both(_vmerge_cols(rw_ln_b[l].reshape(1, B_WIDTH))),
                        _vmerge_cols(rw_g_up[l]).reshape(2 * GATE_LORA, MERGED).astype(BF16),
                        selv, g32, rows=t if last else t + cl)
        if last:
            yb = yb2.reshape(b * t, B_WIDTH)
        else:
            yb = jnp.concatenate([yb2[0, :t], yb2[1, :t], yb2[0, t:], yb2[1, t:]])

        gains = jnp.stack([jnp.tile(da_q_norm[l], 2) * (C_HEAD_DIM ** -0.5 * math.log2(math.e)),
                           jnp.tile(da_k_norm[l], 2), jnp.ones((LANES,), F32)]).reshape(3, 1, LANES)
        qk, vx = _qkv(h, sl(6).astype(BF16), gains, cos_t, sin_t, g64)
        sub = da_subln[l].reshape(1, LANES)
        yc = _attn(qk, vx, da_lam[l], sub, None, b=b, t=t, c=cl, lam_init=lam_init, ctx_queries=False)
        if not last:
            yc = _attn(qk, vx, da_lam[l], sub, yc, b=b, t=t, c=cl, lam_init=lam_init, ctx_queries=True)

        wg = jnp.transpose(sl(7).reshape(d, N_BRANCH, d), (1, 0, 2)).astype(BF16)
        wb = w_branch[l]
        wb_b = wb[1].reshape(B_HEADS, B_HEAD_DIM, d).transpose(1, 0, 2).reshape(B_WIDTH, d)
        wbr = jnp.stack([wb[0], wb_b, wb[2]]).astype(BF16)
        xs = _merge(xs, h, ya, yb, yc, mod, wg, b_gate[l].reshape(N_BRANCH, 1, d),
                    wbr, w_out[l].astype(BF16), mod_idx, n_mix)

        xs = _ffn(xs, mod, norm_g[l], w_gu, w_dn, l, 1, mod_idx, n_tiles=n_mix, mi=2, emit_h=False)

    return xs[:b * t].reshape(b, t, d)
```

```python
import functools
import math

import jax
import jax.numpy as jnp
from jax import lax
from jax.experimental import pallas as pl
from jax.experimental.pallas import tpu as pltpu

F32 = jnp.float32
BF16 = jnp.bfloat16

N_MOD = 9
CHUNK = 128
A_WIDTH = 1024
A_GROUPS = 8
B_WIDTH = 1024
B_HEAD_DIM = 64
B_HEADS = 16
DECAY_LORA = 96
ICL_LORA = 96
GATE_LORA = 256
C_HEADS = 8
C_HEAD_DIM = 64
C_WIDTH = 1024
N_BRANCH = 3
GRID_W = 64
ROPE_BASE = 10000.0
ROPE_AXIS_DIM = 32
NORM_EPS = 1e-6
LN_X_EPS = 64e-5

LANES = 128
SUBLANES = 8
VMEM_LIMIT = 56 * 1024 * 1024

TM = 512
TF = 512
LORA_PAD = 128
SCAN_TS = 64
ATT_TQ = 512
ATT_KC = 512


def _params(sem):
    return pltpu.CompilerParams(dimension_semantics=sem, vmem_limit_bytes=VMEM_LIMIT)


def _rms(x, eps):
    return x * lax.rsqrt(jnp.mean(x * x, axis=-1, keepdims=True) + eps)


def _split3(s):
    hi = s.astype(BF16)
    r = s - hi.astype(F32)
    mid = r.astype(BF16)
    lo = (r - mid.astype(F32)).astype(BF16)
    return hi, mid, lo


def _gsum(s, g_bf16):
    out = None
    for piece in _split3(s):
        d = jnp.dot(piece, g_bf16, preferred_element_type=F32)
        out = d if out is None else out + d
    return out


def _gelu_tanh(x):
    cdf = 0.5 * (1.0 + jnp.tanh(math.sqrt(2.0 / math.pi) * (x + 0.044715 * (x * x * x))))
    return x * cdf


def _ada_kernel(c_ref, w_ref, b_ref, o_ref):
    s = c_ref[...]
    s = s * jax.nn.sigmoid(s)
    w = w_ref[...]
    w_hi = w.astype(BF16)
    w_lo = (w - w_hi.astype(F32)).astype(BF16)
    s3 = _split3(s)
    acc = b_ref[...]
    for piece in s3:
        acc = acc + jnp.dot(piece, w_hi, preferred_element_type=F32)
    for piece in s3[:2]:
        acc = acc + jnp.dot(piece, w_lo, preferred_element_type=F32)
    o_ref[...] = acc


def _ada(cvec, w_ada, b_ada3, l):
    d = cvec.shape[1]
    n = w_ada.shape[2]
    tn = 1024
    return pl.pallas_call(
        _ada_kernel,
        grid=(n // tn,),
        in_specs=[
            pl.BlockSpec((SUBLANES, d), lambda j: (0, 0)),
            pl.BlockSpec((None, d, tn), lambda j: (l, 0, j)),
            pl.BlockSpec((None, 1, tn), lambda j: (l, 0, j)),
        ],
        out_specs=pl.BlockSpec((SUBLANES, tn), lambda j: (0, j)),
        out_shape=jax.ShapeDtypeStruct((SUBLANES, n), F32),
        compiler_params=_params(("arbitrary",)),
        name="ada",
    )(cvec, w_ada, b_ada3)


def _ffn_kernel(x_ref, mod_ref, g_ref, win_ref, wout_ref, *rest, tf, nf, last_valid, mi, emit_h):
    if emit_h:
        o_ref, h_ref, hn_sc, acc_sc = rest
    else:
        o_ref, hn_sc, acc_sc = rest
    f = pl.program_id(1)

    @pl.when(f == 0)
    def _():
        xn = _rms(x_ref[...], NORM_EPS) * g_ref[mi:mi + 1, :]
        hn = xn * (1.0 + mod_ref[3 * mi + 1:3 * mi + 2, :]) + mod_ref[3 * mi:3 * mi + 1, :]
        hn_sc[...] = hn.astype(BF16)
        acc_sc[...] = jnp.zeros_like(acc_sc)

    def hidden_tile(valid):
        hn = hn_sc[...]
        g = jnp.dot(hn, win_ref[0], preferred_element_type=F32)
        u = jnp.dot(hn, win_ref[1], preferred_element_type=F32)
        act = g * jax.nn.sigmoid(g) * u
        wout = wout_ref[...]
        if valid < tf:
            act = jnp.where(lax.broadcasted_iota(jnp.int32, (1, tf), 1) < valid, act, 0.0)
            wout = jnp.where(lax.broadcasted_iota(jnp.int32, (tf, 1), 0) < valid, wout, jnp.zeros_like(wout))
        acc_sc[...] += jnp.dot(act.astype(BF16), wout, preferred_element_type=F32)

    if last_valid == tf:
        hidden_tile(tf)
    else:
        pl.when(f < nf - 1)(lambda: hidden_tile(tf))
        pl.when(f == nf - 1)(lambda: hidden_tile(last_valid))

    @pl.when(f == nf - 1)
    def _():
        out = x_ref[...] + 0.5 * mod_ref[3 * mi + 2:3 * mi + 3, :] * acc_sc[...]
        o_ref[...] = out
        if emit_h:
            hn = _rms(out, NORM_EPS) * g_ref[1:2, :]
            h_ref[...] = (hn * (1.0 + mod_ref[4:5, :]) + mod_ref[3:4, :]).astype(BF16)


def _ffn(xs, mod, norm_g, w_gu, w_out, l, w, mod_idx, *, n_tiles, mi, emit_h):
    m, d = xs.shape
    d_ff = w_out.shape[2]
    nf = pl.cdiv(d_ff, TF)
    out_rows = n_tiles * TM
    out_shape = [jax.ShapeDtypeStruct((out_rows, d), F32)]
    out_specs = [pl.BlockSpec((TM, d), lambda i, f: (i, 0))]
    if emit_h:
        out_shape.append(jax.ShapeDtypeStruct((out_rows, d), BF16))
        out_specs.append(pl.BlockSpec((TM, d), lambda i, f: (i, 0)))
    res = pl.pallas_call(
        functools.partial(_ffn_kernel, tf=TF, nf=nf, last_valid=d_ff - (nf - 1) * TF, mi=mi, emit_h=emit_h),
        grid=(n_tiles, nf),
        in_specs=[
            pl.BlockSpec((TM, d), lambda i, f: (i, 0)),
            pl.BlockSpec((None, N_MOD, d), lambda i, f: (mod_idx(i), 0, 0)),
            pl.BlockSpec((3, d), lambda i, f: (0, 0)),
            pl.BlockSpec((None, None, 2, d, TF), lambda i, f: (l, w, 0, 0, f)),
            pl.BlockSpec((None, None, TF, d), lambda i, f: (l, w, f, 0)),
        ],
        out_specs=out_specs,
        out_shape=out_shape,
        scratch_shapes=[pltpu.VMEM((TM, d), BF16), pltpu.VMEM((TM, d), F32)],
        compiler_params=_params(("parallel", "arbitrary")),
        name="ffn",
    )(xs, mod, norm_g, w_gu, w_out)
    return res if emit_h else res[0]


def _gmlp_kernel(h_ref, w_ref, vn_ref, ws_ref, bsb_ref, o_ref, p_sc, *, tm):
    p_sc[...] = jnp.dot(h_ref[...], w_ref[...], preferred_element_type=F32)
    gd = A_WIDTH // A_GROUPS
    for g in range(A_GROUPS):
        wsg = ws_ref[g].astype(BF16)
        bias = bsb_ref[g]
        gain = vn_ref[:, g * gd:(g + 1) * gd]
        for n in range(tm // CHUNK):
            rows = slice(n * CHUNK, (n + 1) * CHUNK)
            u = _gelu_tanh(p_sc[rows, g * gd:(g + 1) * gd])
            v = _gelu_tanh(p_sc[rows, A_WIDTH + g * gd:A_WIDTH + (g + 1) * gd])
            v = _rms(v, NORM_EPS) * gain
            sv = jnp.dot(wsg, v.astype(BF16), preferred_element_type=F32) + bias
            o_ref[rows, g * gd:(g + 1) * gd] = (u * sv).astype(o_ref.dtype)


def _gmlp(h, w_uv, vn, ws, bsb, n_tiles):
    m, d = h.shape
    return pl.pallas_call(
        functools.partial(_gmlp_kernel, tm=TM),
        grid=(n_tiles,),
        in_specs=[
            pl.BlockSpec((TM, d), lambda i: (i, 0)),
            pl.BlockSpec((d, 2 * A_WIDTH), lambda i: (0, 0)),
            pl.BlockSpec((1, A_WIDTH), lambda i: (0, 0)),
            pl.BlockSpec((A_GROUPS, CHUNK, CHUNK), lambda i: (0, 0, 0)),
            pl.BlockSpec((A_GROUPS, CHUNK, A_WIDTH // A_GROUPS), lambda i: (0, 0, 0)),
        ],
        out_specs=pl.BlockSpec((TM, A_WIDTH), lambda i: (i, 0)),
        out_shape=jax.ShapeDtypeStruct((n_tiles * TM, A_WIDTH), BF16),
        scratch_shapes=[pltpu.VMEM((TM, 2 * A_WIDTH), F32)],
        compiler_params=_params(("parallel",)),
        name="gmlp",
    )(h, w_uv, vn, ws, bsb)


def _qkv_kernel(h_ref, w_ref, gain_ref, cos_ref, sin_ref, g64_ref, o_ref, vx_ref, p_sc):
    j = pl.program_id(1)
    p_sc[...] = jnp.dot(h_ref[...], w_ref[...], preferred_element_type=F32)

    @pl.when(j == 2)
    def _():
        ones = jnp.ones((p_sc.shape[0], LANES), BF16)
        for hs in range(C_HEADS):
            vx_ref[:, 2 * hs * LANES:(2 * hs + 1) * LANES] = p_sc[:, hs * LANES:(hs + 1) * LANES].astype(BF16)
            vx_ref[:, (2 * hs + 1) * LANES:(2 * hs + 2) * LANES] = ones

    @pl.when(j < 2)
    def _():
        lane = lax.broadcasted_iota(jnp.int32, (1, LANES), 1)
        first = (lane % ROPE_AXIS_DIM) < (ROPE_AXIS_DIM // 2)
        cos = cos_ref[...]
        sin = sin_ref[...]
        gain = gain_ref[...]
        g64 = g64_ref[...]
        for hs in range(C_HEADS):
            x = p_sc[:, hs * LANES:(hs + 1) * LANES]
            ms = _gsum(x * x, g64) * (1.0 / C_HEAD_DIM)
            y = x * lax.rsqrt(ms + NORM_EPS) * gain
            half = ROPE_AXIS_DIM // 2
            rot = jnp.where(first, pltpu.roll(y, LANES - half, 1), pltpu.roll(y, half, 1))
            o_ref[:, hs * LANES:(hs + 1) * LANES] = (y * cos + rot * sin).astype(BF16)


def _qkv(h, w_qkv, gains, cos_t, sin_t, g64):
    m, d = h.shape
    return pl.pallas_call(
        _qkv_kernel,
        grid=(m // TM, 3),
        in_specs=[
            pl.BlockSpec((TM, d), lambda i, j: (i, 0)),
            pl.BlockSpec((d, C_WIDTH), lambda i, j: (0, j)),
            pl.BlockSpec((None, 1, LANES), lambda i, j: (j, 0, 0)),
            pl.BlockSpec((TM, LANES), lambda i, j: (i, 0)),
            pl.BlockSpec((TM, LANES), lambda i, j: (i, 0)),
            pl.BlockSpec((LANES, LANES), lambda i, j: (0, 0)),
        ],
        out_specs=[pl.BlockSpec((None, TM, C_WIDTH), lambda i, j: (jnp.minimum(j, 1), i, 0)),
                   pl.BlockSpec((TM, 2 * C_WIDTH), lambda i, j: (i, 0))],
        out_shape=[jax.ShapeDtypeStruct((2, m, C_WIDTH), BF16), jax.ShapeDtypeStruct((m, 2 * C_WIDTH), BF16)],
        scratch_shapes=[pltpu.VMEM((TM, C_WIDTH), F32)],
        compiler_params=_params(("parallel", "arbitrary")),
        name="qkv",
    )(h, w_qkv, gains, cos_t, sin_t, g64)


def _attn_kernel(lam_ref, sub_ref, q_ref, *refs, nseg, lam_init):
    kv = refs[:2 * nseg]
    o_ref = refs[2 * nseg]
    lv = lam_ref[...]
    lam = (jnp.exp(jnp.sum(lv[0:1] * lv[1:2], axis=-1, keepdims=True))
           - jnp.exp(jnp.sum(lv[2:3] * lv[3:4], axis=-1, keepdims=True)) + lam_init)
    q = q_ref[...]
    lane = lax.broadcasted_iota(jnp.int32, (1, LANES), 1)
    zero = jnp.zeros_like(q)
    outs = []
    for j in range(2):
        in_map = (lane < C_HEAD_DIM) if j == 0 else (lane >= C_HEAD_DIM)
        qj = jnp.where(in_map, q, zero)
        m = None
        acc = None
        for si in range(nseg):
            k_ref, v_ref = kv[2 * si], kv[2 * si + 1]
            ls = k_ref.shape[0]
            for c0 in range(0, ls, ATT_KC):
                kc = min(ATT_KC, ls - c0)
                s = lax.dot_general(qj, k_ref[c0:c0 + kc, :], (((1,), (1,)), ((), ())),
                                    preferred_element_type=F32)
                cm = jnp.max(s, axis=-1, keepdims=True)
                m_new = cm if m is None else jnp.maximum(m, cm)
                e = jnp.exp2((s - m_new).astype(BF16))
                pv = jnp.dot(e, v_ref[c0:c0 + kc, :], preferred_element_type=F32)
                acc = pv if acc is None else acc * jnp.exp2(m - m_new) + pv
                m = m_new
        outs.append(acc[:, :LANES] / acc[:, LANES:])
    o = outs[0] - lam * outs[1]
    o = _rms(o, NORM_EPS) * sub_ref[...]
    o_ref[...] = (o * (1.0 - lam_init)).astype(o_ref.dtype)


def _attn(qk, vx, lam, subln, yc_prev, *, b, t, c, lam_init, ctx_queries):
    m = qk.shape[1]
    if ctx_queries:
        tq, nq = c, 1
        q_blk = lambda bi, qi: (b * t) // c + bi
        segs = [(c, lambda bi: (b * t) // c + bi)]
    else:
        tq = ATT_TQ
        nq = t // tq
        q_blk = lambda bi, qi: bi * nq + qi
        segs = [(t, lambda bi: bi), (c, lambda bi: (b * t) // c + bi)]
    in_specs = [
        pl.BlockSpec((4, C_HEAD_DIM), lambda bi, h, qi: (0, 0)),
        pl.BlockSpec((1, LANES), lambda bi, h, qi: (0, 0)),
        pl.BlockSpec((None, tq, LANES), lambda bi, h, qi: (0, q_blk(bi, qi), h)),
    ]
    args = [lam, subln, qk]
    for ls, blk in segs:
        in_specs.append(pl.BlockSpec((None, ls, LANES), lambda bi, h, qi, blk=blk: (1, blk(bi), h)))
        in_specs.append(pl.BlockSpec((ls, 2 * LANES), lambda bi, h, qi, blk=blk: (blk(bi), h)))
        args += [qk, vx]
    aliases = {}
    if yc_prev is not None:
        in_specs.append(pl.BlockSpec(memory_space=pl.ANY))
        args.append(yc_prev)
        aliases = {len(args) - 1: 0}

    def body(*refs):
        n_in = 3 + 2 * len(segs)
        _attn_kernel(*refs[:n_in], refs[-1], nseg=len(segs), lam_init=lam_init)

    return pl.pallas_call(
        body,
        grid=(b, C_HEADS, nq),
        in_specs=in_specs,
        out_specs=pl.BlockSpec((tq, LANES), lambda bi, h, qi: (q_blk(bi, qi), h)),
        out_shape=jax.ShapeDtypeStruct((m, C_WIDTH), BF16),
        input_output_aliases=aliases,
        compiler_params=_params(("parallel", "parallel", "arbitrary")),
        name="diff_attn_ctx" if ctx_queries else "diff_attn",
    )(*args)


KQ = 4
NKL = B_HEAD_DIM // KQ
MERGED = 2 * B_WIDTH
BH = 2 * B_HEADS
LORA_MERGED = 2 * GATE_LORA + 8 * LORA_PAD
PROJ_MERGED = 3 * MERGED + LORA_MERGED


def _rkv_proj_kernel(h0_ref, h1_ref, w_ref, o_ref):
    o_ref[...] = (jnp.dot(h0_ref[...], w_ref[0], preferred_element_type=F32)
                  + jnp.dot(h1_ref[...], w_ref[1], preferred_element_type=F32))


def _rkv_proj(h, wz, *, t, c):
    d = h.shape[1]
    tm = 256
    tn = 1536
    s = t + c
    nlat = t // tm
    hblk = lambda bi, i: jnp.where(i < nlat, bi * nlat + i, (2 * t + bi * c) // tm + (i - nlat))
    return pl.pallas_call(
        _rkv_proj_kernel,
        grid=(PROJ_MERGED // tn, s // tm),
        in_specs=[
            pl.BlockSpec((tm, d), lambda j, i: (hblk(0, i), 0)),
            pl.BlockSpec((tm, d), lambda j, i: (hblk(1, i), 0)),
            pl.BlockSpec((2, d, tn), lambda j, i: (0, 0, j)),
        ],
        out_specs=pl.BlockSpec((tm, tn), lambda j, i: (i, j)),
        out_shape=jax.ShapeDtypeStruct((s, PROJ_MERGED), F32),
        compiler_params=_params(("parallel", "arbitrary")),
        name="rkv_proj",
    )(h, h, wz)


def _rwkv_prep_kernel(p_ref, prev_ref, next_ref, lo_ref, conv_ref, w0_ref, wup_ref, a0_ref, aup_ref,
                      kkg_ref, ka_ref, rk_ref, g32_ref, rep_ref,
                      r_o, kk_o, w_o, kd_o, b_o, vrep_o, v_o, bonus_o, k_sc, r_sc, *, tm, starts, ends):
    i = pl.program_id(0)
    is_start = functools.reduce(jnp.logical_or, [i == s for s in starts])
    is_end = functools.reduce(jnp.logical_or, [i == s for s in ends])
    row = lax.broadcasted_iota(jnp.int32, (tm, 1), 0)

    def conv(c0):
        cs = slice(c0, c0 + LANES)
        x = p_ref[:, cs]
        pm = jnp.where(is_start, 0.0, prev_ref[SUBLANES - 1:SUBLANES, cs])
        nx = jnp.where(is_end, 0.0, next_ref[0:1, cs])
        xm1 = jnp.where(row == 0, pm, pltpu.roll(x, 1, 0))
        xp1 = jnp.where(row == tm - 1, nx, pltpu.roll(x, tm - 1, 0))
        return xm1 * conv_ref[0:1, cs] + x * conv_ref[1:2, cs] + xp1 * conv_ref[2:3, cs]

    ss = None
    for m in range(NKL):
        cs = slice(m * LANES, (m + 1) * LANES)
        r = conv(m * LANES)
        r_o[:, m, :] = r
        r_sc[:, cs] = r
        k = conv(MERGED + m * LANES)
        k_sc[:, cs] = k
        kkr = k * kkg_ref[:, cs]
        ss = kkr * kkr if ss is None else ss + kkr * kkr
    rs = lax.rsqrt(_gsum(ss, g32_ref[...]) + 1e-12)

    w_off = 2 * GATE_LORA
    a_off = w_off + 4 * LORA_PAD
    def split2(x):
        hi = x.astype(BF16)
        return hi, (x - hi.astype(F32)).astype(BF16)

    def lora(x2, up_ref, d, cs):
        return (jnp.dot(x2[0], up_ref[0, d, :, cs], preferred_element_type=F32)
                + jnp.dot(x2[1], up_ref[0, d, :, cs], preferred_element_type=F32)
                + jnp.dot(x2[0], up_ref[1, d, :, cs], preferred_element_type=F32))

    tw = [split2(jnp.tanh(lo_ref[:, w_off + 2 * d * LORA_PAD:w_off + 2 * (d + 1) * LORA_PAD])) for d in range(2)]
    pa = [split2(lo_ref[:, a_off + 2 * d * LORA_PAD:a_off + 2 * (d + 1) * LORA_PAD]) for d in range(2)]
    bonus = None
    for m in range(NKL):
        cs = slice(m * LANES, (m + 1) * LANES)
        k = k_sc[:, cs]
        kk = k * kkg_ref[:, cs] * rs
        kk_o[:, m, :] = kk
        kd_sum = None
        for d in range(2):
            w_lo = w0_ref[d:d + 1, cs] + lora(tw[d], wup_ref, d, cs)
            w_o[d, :, m, :] = jnp.exp(-jax.nn.sigmoid(w_lo) * math.exp(-0.5))
            a = jax.nn.sigmoid(a0_ref[d:d + 1, cs] + lora(pa[d], aup_ref, d, cs))
            kd = k * (1.0 + (a - 1.0) * ka_ref[:, cs])
            kd_o[d, :, m, :] = kd
            kd_sum = kd if kd_sum is None else kd_sum + kd
            b_o[d, :, m, :] = kk * a
        term = r_sc[:, cs] * (0.5 * kd_sum) * rk_ref[:, cs]
        bonus = term if bonus is None else bonus + term
    bonus_o[...] = bonus

    per_slab = LANES // BH
    for j in range(MERGED // LANES):
        vs = conv(2 * MERGED + j * LANES)
        v_o[:, j * LANES:(j + 1) * LANES] = vs
        for vi in range(per_slab):
            vrep_o[:, per_slab * j + vi, :] = _gsum(vs, rep_ref[vi])


def _rwkv_prep(p, conv_m, w0_m, wup_m, a0_m, aup_m, kk_m, ka_m, rk_m, g32, rep, *, t, c):
    s = t + c
    tm = 128
    nt = s // tm
    starts = [0, t // tm]
    ends = [t // tm - 1, nt - 1]
    hb = tm // SUBLANES
    rkv_w = 3 * MERGED
    full2 = lambda i: (0, 0)
    full3 = lambda i: (0, 0, 0)
    k3 = jax.ShapeDtypeStruct((s, NKL, LANES), F32)
    k4 = jax.ShapeDtypeStruct((2, s, NKL, LANES), F32)
    k3s = pl.BlockSpec((tm, NKL, LANES), lambda i: (i, 0, 0))
    k4s = pl.BlockSpec((2, tm, NKL, LANES), lambda i: (0, i, 0, 0))
    return pl.pallas_call(
        functools.partial(_rwkv_prep_kernel, tm=tm, starts=starts, ends=ends),
        grid=(nt,),
        in_specs=[
            pl.BlockSpec((tm, rkv_w), lambda i: (i, 0)),
            pl.BlockSpec((SUBLANES, rkv_w), lambda i: (jnp.maximum(i * hb - 1, 0), 0)),
            pl.BlockSpec((SUBLANES, rkv_w), lambda i: (jnp.minimum((i + 1) * hb, s // SUBLANES - 1), 0)),
            pl.BlockSpec((tm, LORA_MERGED), lambda i: (i, rkv_w // LORA_MERGED)),
            pl.BlockSpec((3, rkv_w), full2),
            pl.BlockSpec((2, MERGED), full2),
            pl.BlockSpec((2, 2, 2 * LORA_PAD, MERGED), lambda i: (0, 0, 0, 0)),
            pl.BlockSpec((2, MERGED), full2),
            pl.BlockSpec((2, 2, 2 * LORA_PAD, MERGED), lambda i: (0, 0, 0, 0)),
            pl.BlockSpec((1, MERGED), full2),
            pl.BlockSpec((1, MERGED), full2),
            pl.BlockSpec((1, MERGED), full2),
            pl.BlockSpec((LANES, LANES), full2),
            pl.BlockSpec((LANES // BH, LANES, LANES), full3),
        ],
        out_specs=[k3s, k3s, k4s, k4s, k4s,
                   pl.BlockSpec((tm, B_HEAD_DIM, LANES), lambda i: (i, 0, 0)),
                   pl.BlockSpec((tm, MERGED), lambda i: (i, 0)),
                   pl.BlockSpec((tm, LANES), lambda i: (i, 0))],
        out_shape=[k3, k3, k4, k4, k4, jax.ShapeDtypeStruct((s, B_HEAD_DIM, LANES), F32),
                   jax.ShapeDtypeStruct((s, MERGED), F32), jax.ShapeDtypeStruct((s, LANES), F32)],
        scratch_shapes=[pltpu.VMEM((tm, MERGED), F32), pltpu.VMEM((tm, MERGED), F32)],
        compiler_params=_params(("parallel",)),
        name="rwkv_prep",
    )(p, p, p, p, conv_m, w0_m, wup_m, a0_m, aup_m, kk_m, ka_m, rk_m, g32, rep)


def _scan_kernel(*refs, ts):
    ins, (yf_ref, yb_ref, s_sc, y_sc, sa_sc) = refs[:12], refs[12:]
    streams = [ins[:6] + (yf_ref,), ins[6:] + (yb_ref,)]
    nkl = s_sc.shape[1]
    nvb = s_sc.shape[2] // SUBLANES

    @pl.when(pl.program_id(0) == 0)
    def _():
        s_sc[...] = jnp.zeros_like(s_sc)
        y_sc[...] = jnp.zeros_like(y_sc)

    def vsl(vb):
        return slice(vb * SUBLANES, (vb + 1) * SUBLANES)

    diag = (lax.broadcasted_iota(jnp.int32, (SUBLANES, LANES), 1) // BH
            == lax.broadcasted_iota(jnp.int32, (SUBLANES, LANES), 0) % KQ)

    def emit_y(d, y_ref, row):
        for vb in range(nvb):
            y = y_sc[d, vsl(vb), :]
            y = y + pltpu.roll(y, BH, 1)
            y = y + pltpu.roll(y, 2 * BH, 1)
            y = jnp.where(diag, y, 0.0)
            y = y + pltpu.roll(y, 1, 0)
            y = y + pltpu.roll(y, 2, 0)
            for half in range(SUBLANES // KQ):
                srow = half * KQ + KQ - 1
                y_ref[row, 2 * vb + half:2 * vb + half + 1, :] = y[srow:srow + 1, :]

    def allreduce(x):
        x = x + pltpu.roll(x, BH, 1)
        return x + pltpu.roll(x, 2 * BH, 1)

    def step(j, carry):
        ts_of = (j, ts - 1 - j)
        prev = (jnp.maximum(j - 1, 0), jnp.minimum(ts - j, ts - 1))
        for d in range(2):
            emit_y(d, streams[d][-1], prev[d])
        nxt = (jnp.minimum(j + 1, ts - 1), jnp.maximum(ts - 2 - j, 0))
        for d, (w_ref, kd_ref, b_ref, kk_ref, r_ref, v_ref, y_ref) in enumerate(streams):
            t = ts_of[d]
            sa = [sa_sc[d, vsl(vb), :] if d == 0 else allreduce(sa_sc[d, vsl(vb), :]) for vb in range(nvb)]
            ys = [None] * nvb
            sa_next = [None] * nvb
            for kl in range(nkl):
                wrow = w_ref[t, kl:kl + 1, :]
                brow = b_ref[t, kl:kl + 1, :]
                kdrow = kd_ref[t, kl:kl + 1, :]
                rrow = r_ref[t, kl:kl + 1, :]
                kknext = kk_ref[nxt[d], kl:kl + 1, :]
                for vb in range(nvb):
                    sn = s_sc[d, kl, vsl(vb), :] * wrow - sa[vb] * brow + v_ref[t, vsl(vb), :] * kdrow
                    s_sc[d, kl, vsl(vb), :] = sn
                    term = sn * rrow
                    ys[vb] = term if ys[vb] is None else ys[vb] + term
                    term = sn * kknext
                    sa_next[vb] = term if sa_next[vb] is None else sa_next[vb] + term
            for vb in range(nvb):
                y_sc[d, vsl(vb), :] = ys[vb]
                sa_sc[d, vsl(vb), :] = allreduce(sa_next[vb]) if d == 0 else sa_next[vb]
        return carry

    for d, first in enumerate((0, ts - 1)):
        kk_ref = streams[d][3]
        for vb in range(nvb):
            acc = None
            for kl in range(nkl):
                term = s_sc[d, kl, vsl(vb), :] * kk_ref[first, kl:kl + 1, :]
                acc = term if acc is None else acc + term
            sa_sc[d, vsl(vb), :] = allreduce(acc) if d == 0 else acc

    lax.fori_loop(0, ts, step, 0)
    emit_y(0, yf_ref, ts - 1)
    emit_y(1, yb_ref, 0)


def _scan(w4, kd4, b4, kk3, r3, vrep, *, t, c):
    s = t + c
    ts = SCAN_TS
    nctx = c // ts
    nlat = t // ts
    fblk = lambda g: jnp.where(g < nctx, nlat + g, g - nctx)
    bblk = lambda g: jnp.where(g < nctx, nlat + (nctx - 1 - g), nlat - 1 - (g - nctx))
    in_specs = []
    args = []
    for d, blk in enumerate((fblk, bblk)):
        for a in (w4, kd4, b4):
            in_specs.append(pl.BlockSpec((None, ts, NKL, LANES), lambda g, d=d, blk=blk: (d, blk(g), 0, 0)))
            args.append(a)
        for a in (kk3, r3):
            in_specs.append(pl.BlockSpec((ts, NKL, LANES), lambda g, blk=blk: (blk(g), 0, 0)))
            args.append(a)
        in_specs.append(pl.BlockSpec((ts, B_HEAD_DIM, LANES), lambda g, blk=blk: (blk(g), 0, 0)))
        args.append(vrep)
    nslab = MERGED // LANES
    yshape = jax.ShapeDtypeStruct((s, nslab, LANES), F32)
    return pl.pallas_call(
        functools.partial(_scan_kernel, ts=ts),
        grid=(s // ts,),
        in_specs=in_specs,
        out_specs=[pl.BlockSpec((ts, nslab, LANES), lambda g: (fblk(g), 0, 0)),
                   pl.BlockSpec((ts, nslab, LANES), lambda g: (bblk(g), 0, 0))],
        out_shape=[yshape, yshape],
        scratch_shapes=[pltpu.VMEM((2, NKL, B_HEAD_DIM, LANES), F32),
                        pltpu.VMEM((2, B_HEAD_DIM, LANES), F32),
                        pltpu.VMEM((2, B_HEAD_DIM, LANES), F32)],
        compiler_params=_params(("arbitrary",)),
        name="wkv7_scan",
    )(*args)


def _rwkv_out_kernel(yf_ref, yb_ref, v_ref, bonus_ref, pg_ref, lng_ref, lnb_ref, gup_ref,
                     selv_ref, g32_ref, o_ref):
    g32 = g32_ref[...]
    gate = jnp.dot(jax.nn.sigmoid(pg_ref[...]).astype(BF16), gup_ref[...], preferred_element_type=F32)
    bonus = _gsum(bonus_ref[...], g32)
    nslab = MERGED // LANES
    ys = [yf_ref[:, j, :] + yb_ref[:, j, :] for j in range(nslab)]
    inv = 1.0 / B_HEAD_DIM
    mu = _gsum(functools.reduce(lambda a, b_: a + b_, ys), g32) * inv
    ds = [y - mu for y in ys]
    var = _gsum(functools.reduce(lambda a, b_: a + b_, [x * x for x in ds]), g32) * inv
    rstd = lax.rsqrt(var + LN_X_EPS)
    outs = []
    for j in range(nslab):
        cs = slice(j * LANES, (j + 1) * LANES)
        yn = ds[j] * rstd * lng_ref[:, cs] + lnb_ref[:, cs]
        outs.append(((yn + bonus * v_ref[:, cs]) * gate[:, cs]).astype(BF16))
    for j in range(nslab // 2):
        pair = jnp.concatenate([outs[2 * j], outs[2 * j + 1]], axis=1)
        for bi in range(2):
            o_ref[bi, :, j * LANES:(j + 1) * LANES] = jnp.dot(
                pair, selv_ref[bi], preferred_element_type=F32).astype(o_ref.dtype)


def _rwkv_out(yf, yb, v2d, bonus, p, lng_m, lnb_m, gup_m, selv, g32, *, rows):
    tm = 256
    nslab = MERGED // LANES
    y3s = pl.BlockSpec((tm, nslab, LANES), lambda i: (i, 0, 0))
    full2 = lambda i: (0, 0)
    return pl.pallas_call(
        _rwkv_out_kernel,
        grid=(rows // tm,),
        in_specs=[
            y3s, y3s,
            pl.BlockSpec((tm, MERGED), lambda i: (i, 0)),
            pl.BlockSpec((tm, LANES), lambda i: (i, 0)),
            pl.BlockSpec((tm, 2 * GATE_LORA), lambda i: (i, 3 * MERGED // (2 * GATE_LORA))),
            pl.BlockSpec((1, MERGED), full2),
            pl.BlockSpec((1, MERGED), full2),
            pl.BlockSpec((2 * GATE_LORA, MERGED), full2),
            pl.BlockSpec((2, 2 * LANES, LANES), lambda i: (0, 0, 0)),
            pl.BlockSpec((LANES, LANES), full2),
        ],
        out_specs=pl.BlockSpec((2, tm, B_WIDTH), lambda i: (0, i, 0)),
        out_shape=jax.ShapeDtypeStruct((2, rows, B_WIDTH), BF16),
        compiler_params=_params(("parallel",)),
        name="rwkv_out",
    )(yf, yb, v2d, bonus, p, lng_m, lnb_m, gup_m, selv, g32)


def _kmerge_cols(w):
    r = w.shape[0]
    wt = w.reshape(r, B_HEADS, NKL, KQ).transpose(0, 2, 3, 1)
    z = jnp.zeros_like(wt)
    return jnp.stack([jnp.stack([wt, z], axis=3), jnp.stack([z, wt], axis=3)]).reshape(2, r, MERGED)


def _vmerge_cols(w):
    r = w.shape[0]
    wt = w.reshape(r, B_HEADS, B_HEAD_DIM).transpose(0, 2, 1)
    z = jnp.zeros_like(wt)
    return jnp.stack([jnp.stack([wt, z], axis=2), jnp.stack([z, wt], axis=2)]).reshape(2, r, MERGED)


def _bmerge_cols(w, width):
    wp = jnp.pad(w, ((0, 0), (0, width - w.shape[1])))
    z = jnp.zeros_like(wp)
    return jnp.stack([jnp.concatenate([wp, z], axis=1), jnp.concatenate([z, wp], axis=1)])


def _both(m):
    return m[0] + m[1]


def _selectors():
    ci = jnp.arange(LANES)[None, :]
    bsel = jnp.arange(2)[:, None, None]
    c_vi, c_h = ci // B_HEADS, ci % B_HEADS
    r2 = jnp.arange(2 * LANES)[:, None]
    selv = ((r2 // BH == c_vi) & ((r2 % BH) // B_HEADS == bsel) & (r2 % B_HEADS == c_h)).astype(BF16)
    r1 = jnp.arange(LANES)[:, None]
    g32 = (r1 % BH == ci % BH).astype(BF16)
    rep = jnp.stack([((r1 // BH == vi) & (r1 % BH == ci % BH)) for vi in range(LANES // BH)]).astype(BF16)
    return selv, g32, rep


def _merge_kernel(x_ref, h_ref, ya_ref, yb_ref, yc_ref, mod_ref, wg_ref, bg_ref, wb_ref, wo_ref,
                  o_ref, *, nz):
    j = pl.program_id(1)
    h = h_ref[...]
    z = None
    for i, y_ref in enumerate((ya_ref, yb_ref, yc_ref)):
        gate = jax.nn.sigmoid(jnp.dot(h, wg_ref[i], preferred_element_type=F32) + bg_ref[i])
        term = gate * jnp.dot(y_ref[...], wb_ref[i], preferred_element_type=F32)
        z = term if z is None else z + term
    part = jnp.dot(z.astype(BF16), wo_ref[...], preferred_element_type=F32)

    @pl.when(j == 0)
    def _():
        o_ref[...] = part

    @pl.when(j > 0)
    def _():
        o_ref[...] += part

    @pl.when(j == nz - 1)
    def _():
        o_ref[...] = x_ref[...] + mod_ref[5:6, :] * o_ref[...]


def _merge(xs, h, ya, yb, yc, mod, wg, bg, wb, wo, mod_idx, n_tiles):
    d = xs.shape[1]
    tz = 512
    nz = d // tz
    rowt = lambda i, j: (i, 0)
    return pl.pallas_call(
        functools.partial(_merge_kernel, nz=nz),
        grid=(n_tiles, nz),
        in_specs=[
            pl.BlockSpec((TM, d), rowt),
            pl.BlockSpec((TM, d), rowt),
            pl.BlockSpec((TM, A_WIDTH), rowt),
            pl.BlockSpec((TM, B_WIDTH), rowt),
            pl.BlockSpec((TM, C_WIDTH), rowt),
            pl.BlockSpec((None, N_MOD, d), lambda i, j: (mod_idx(i), 0, 0)),
            pl.BlockSpec((N_BRANCH, d, tz), lambda i, j: (0, 0, j)),
            pl.BlockSpec((N_BRANCH, 1, tz), lambda i, j: (0, 0, j)),
            pl.BlockSpec((N_BRANCH, A_WIDTH, tz), lambda i, j: (0, 0, j)),
            pl.BlockSpec((tz, d), lambda i, j: (j, 0)),
        ],
        out_specs=pl.BlockSpec((TM, d), rowt),
        out_shape=jax.ShapeDtypeStruct((n_tiles * TM, d), F32),
        compiler_params=_params(("parallel", "arbitrary")),
        name="merge",
    )(xs, h, ya, yb, yc, mod, wg, bg, wb, wo)


def _rope_tables(b, t, c):
    rows = t // GRID_W
    rowp = jnp.repeat(jnp.arange(rows), GRID_W).astype(F32)
    colp = jnp.tile(jnp.arange(GRID_W), rows).astype(F32)
    inv = 1.0 / (ROPE_BASE ** (jnp.arange(0, ROPE_AXIS_DIM, 2, dtype=F32) / ROPE_AXIS_DIM))
    ar, ac = rowp[:, None] * inv, colp[:, None] * inv
    cr, sr, cc, sc = jnp.cos(ar), jnp.sin(ar), jnp.cos(ac), jnp.sin(ac)
    cos64 = jnp.concatenate([cr, cr, cc, cc], axis=-1)
    sin64 = jnp.concatenate([-sr, sr, -sc, sc], axis=-1)
    cos_t = jnp.tile(cos64, (b, 2))
    sin_t = jnp.tile(sin64, (b, 2))
    cos_t = jnp.concatenate([cos_t, jnp.ones((b * c, LANES), F32)])
    sin_t = jnp.concatenate([sin_t, jnp.zeros((b * c, LANES), F32)])
    return cos_t, sin_t


def kernel(x, c, ctx, c_ctx, w_ada, b_ada, norm_g, ffn_w_in, ffn_w_out, w_in, gm_v_norm, gm_ws, gm_bs,
           rw_conv, rw_w0, rw_w_up, rw_a0, rw_a_up, rw_g_up, rw_k_k, rw_k_a, rw_r_k, rw_ln_g, rw_ln_b,
           da_q_norm, da_k_norm, da_lam, da_subln, w_branch, b_gate, w_out):
    b, t, d = x.shape
    cl = ctx.shape[1]
    depth = w_ada.shape[0]
    d_ff = ffn_w_out.shape[2]
    assert b == 2 and b * cl == TM and t % TM == 0 and cl % 256 == 0
    n_lat = (b * t) // TM
    n_all = n_lat + 1
    tiles_per_batch = t // TM
    mod_idx = lambda i: jnp.where(i < n_lat, i // tiles_per_batch, b)

    xs = jnp.concatenate([x.reshape(b * t, d), ctx.reshape(b * cl, d)])
    cvec = jnp.zeros((SUBLANES, d), F32).at[:b].set(c).at[b].set(c_ctx)
    b_ada3 = b_ada.reshape(depth, 1, N_MOD * d)
    cos_t, sin_t = _rope_tables(b, t, cl)
    lane = jnp.arange(LANES)
    g64 = (lane[:, None] // C_HEAD_DIM == lane[None, :] // C_HEAD_DIM).astype(BF16)
    selv, g32, rep = _selectors()
    w_gu = jnp.transpose(ffn_w_in.reshape(depth, 2, d, 2, d_ff), (0, 1, 3, 2, 4)).astype(BF16)
    w_dn = ffn_w_out.astype(BF16)

    o = 0
    offs = []
    for n in (A_WIDTH, A_WIDTH, 3 * B_WIDTH, GATE_LORA, 2 * DECAY_LORA, 2 * ICL_LORA, 3 * C_WIDTH, N_BRANCH * d):
        offs.append((o, o + n))
        o += n

    for l in range(depth):
        last = l == depth - 1
        lam_init = 0.8 - 0.6 * math.exp(-0.3 * l)
        mod = _ada(cvec, w_ada, b_ada3, l)[:b + 1].reshape(b + 1, N_MOD, d)

        xs, h = _ffn(xs, mod, norm_g[l], w_gu, w_dn, l, 0, mod_idx, n_tiles=n_all, mi=0, emit_h=True)

        wl = w_in[l]
        sl = lambda i: wl[:, offs[i][0]:offs[i][1]]
        n_mix = n_lat if last else n_all

        w_uv = jnp.concatenate([sl(0), sl(1)], axis=1).astype(BF16)
        bsb = jnp.broadcast_to(gm_bs[l][:, :, None], (A_GROUPS, CHUNK, A_WIDTH // A_GROUPS))
        ya = _gmlp(h, w_uv, gm_v_norm[l].reshape(1, A_WIDTH), gm_ws[l], bsb, n_mix)

        w_rkv, w_dec, w_icl = sl(2), sl(4), sl(5)
        wz = jnp.concatenate(
            [_kmerge_cols(w_rkv[:, :B_WIDTH]), _kmerge_cols(w_rkv[:, B_WIDTH:2 * B_WIDTH]),
             _vmerge_cols(w_rkv[:, 2 * B_WIDTH:]), _bmerge_cols(sl(3), GATE_LORA),
             _bmerge_cols(w_dec[:, :DECAY_LORA], LORA_PAD), _bmerge_cols(w_dec[:, DECAY_LORA:], LORA_PAD),
             _bmerge_cols(w_icl[:, :ICL_LORA], LORA_PAD), _bmerge_cols(w_icl[:, ICL_LORA:], LORA_PAD)],
            axis=2).astype(BF16)
        p = _rkv_proj(h, wz, t=t, c=cl)
        conv = rw_conv[l]
        conv_m = jnp.concatenate([_both(_kmerge_cols(conv[:, :B_WIDTH])),
                                  _both(_kmerge_cols(conv[:, B_WIDTH:2 * B_WIDTH])),
                                  _both(_vmerge_cols(conv[:, 2 * B_WIDTH:]))], axis=1)
        def up_m(w):
            w32 = jnp.stack([
                _kmerge_cols(jnp.pad(w[dd], ((0, LORA_PAD - w.shape[1]), (0, 0)))).reshape(2 * LORA_PAD, MERGED)
                for dd in range(2)])
            hi = w32.astype(BF16)
            return jnp.stack([hi, (w32 - hi.astype(F32)).astype(BF16)])
        r3, kk3, w4, kd4, b4, vrep, v2d, bonus = _rwkv_prep(
            p, conv_m, _both(_kmerge_cols(rw_w0[l])), up_m(rw_w_up[l]), _both(_kmerge_cols(rw_a0[l])),
            up_m(rw_a_up[l]), _both(_kmerge_cols(rw_k_k[l].reshape(1, B_WIDTH))),
            _both(_kmerge_cols(rw_k_a[l].reshape(1, B_WIDTH))),
            _both(_kmerge_cols(rw_r_k[l].reshape(1, B_WIDTH))), g32, rep, t=t, c=cl)
        yf, ybk = _scan(w4, kd4, b4, kk3, r3, vrep, t=t, c=cl)
        yb2 = _rwkv_out(yf, ybk, v2d, bonus, p, _both(_vmerge_cols(rw_ln_g[l].reshape(1, B_WIDTH))),
                        _both(_vmerge_cols(rw_ln_b[l].reshape(1, B_WIDTH))),
                        _vmerge_cols(rw_g_up[l]).reshape(2 * GATE_LORA, MERGED).astype(BF16),
                        selv, g32, rows=t if last else t + cl)
        if last:
            yb = yb2.reshape(b * t, B_WIDTH)
        else:
            yb = jnp.concatenate([yb2[0, :t], yb2[1, :t], yb2[0, t:], yb2[1, t:]])

        gains = jnp.stack([jnp.tile(da_q_norm[l], 2) * (C_HEAD_DIM ** -0.5 * math.log2(math.e)),
                           jnp.tile(da_k_norm[l], 2), jnp.ones((LANES,), F32)]).reshape(3, 1, LANES)
        qk, vx = _qkv(h, sl(6).astype(BF16), gains, cos_t, sin_t, g64)
        sub = da_subln[l].reshape(1, LANES)
        yc = _attn(qk, vx, da_lam[l], sub, None, b=b, t=t, c=cl, lam_init=lam_init, ctx_queries=False)
        if not last:
            yc = _attn(qk, vx, da_lam[l], sub, yc, b=b, t=t, c=cl, lam_init=lam_init, ctx_queries=True)

        wg = jnp.transpose(sl(7).reshape(d, N_BRANCH, d), (1, 0, 2)).astype(BF16)
        wb = w_branch[l]
        wb_b = wb[1].reshape(B_HEADS, B_HEAD_DIM, d).transpose(1, 0, 2).reshape(B_WIDTH, d)
        wbr = jnp.stack([wb[0], wb_b, wb[2]]).astype(BF16)
        xs = _merge(xs, h, ya, yb, yc, mod, wg, b_gate[l].reshape(N_BRANCH, 1, d),
                    wbr, w_out[l].astype(BF16), mod_idx, n_mix)

        xs = _ffn(xs, mod, norm_g[l], w_gu, w_dn, l, 1, mod_idx, n_tiles=n_mix, mi=2, emit_h=False)

    return xs[:b * t].reshape(b, t, d)
```

```python
import functools
import math

import jax
import jax.numpy as jnp
from jax import lax
from jax.experimental import pallas as pl
from jax.experimental.pallas import tpu as pltpu

F32 = jnp.float32
BF16 = jnp.bfloat16

N_MOD = 9
CHUNK = 128
A_WIDTH = 1024
A_GROUPS = 8
B_WIDTH = 1024
B_HEAD_DIM = 64
B_HEADS = 16
DECAY_LORA = 96
ICL_LORA = 96
GATE_LORA = 256
C_HEADS = 8
C_HEAD_DIM = 64
C_WIDTH = 1024
N_BRANCH = 3
GRID_W = 64
ROPE_BASE = 10000.0
ROPE_AXIS_DIM = 32
NORM_EPS = 1e-6
LN_X_EPS = 64e-5

LANES = 128
SUBLANES = 8
VMEM_LIMIT = 56 * 1024 * 1024

TM = 512
TF = 512
LORA_PAD = 128
SCAN_TS = 64
ATT_TQ = 512
ATT_KC = 512


def _params(sem):
    return pltpu.CompilerParams(dimension_semantics=sem, vmem_limit_bytes=VMEM_LIMIT)


def _rms(x, eps):
    return x * lax.rsqrt(jnp.mean(x * x, axis=-1, keepdims=True) + eps)


def _split3(s):
    hi = s.astype(BF16)
    r = s - hi.astype(F32)
    mid = r.astype(BF16)
    lo = (r - mid.astype(F32)).astype(BF16)
    return hi, mid, lo


def _gsum(s, g_bf16):
    out = None
    for piece in _split3(s):
        d = jnp.dot(piece, g_bf16, preferred_element_type=F32)
        out = d if out is None else out + d
    return out


def _gelu_tanh(x):
    cdf = 0.5 * (1.0 + jnp.tanh(math.sqrt(2.0 / math.pi) * (x + 0.044715 * (x * x * x))))
    return x * cdf


def _ada_kernel(c_ref, w_ref, b_ref, o_ref):
    s = c_ref[...]
    s = s * jax.nn.sigmoid(s)
    w = w_ref[...]
    w_hi = w.astype(BF16)
    w_lo = (w - w_hi.astype(F32)).astype(BF16)
    s3 = _split3(s)
    acc = b_ref[...]
    for piece in s3:
        acc = acc + jnp.dot(piece, w_hi, preferred_element_type=F32)
    for piece in s3[:2]:
        acc = acc + jnp.dot(piece, w_lo, preferred_element_type=F32)
    o_ref[...] = acc


def _ada(cvec, w_ada, b_ada3, l):
    d = cvec.shape[1]
    n = w_ada.shape[2]
    tn = 1024
    return pl.pallas_call(
        _ada_kernel,
        grid=(n // tn,),
        in_specs=[
            pl.BlockSpec((SUBLANES, d), lambda j: (0, 0)),
            pl.BlockSpec((None, d, tn), lambda j: (l, 0, j)),
            pl.BlockSpec((None, 1, tn), lambda j: (l, 0, j)),
        ],
        out_specs=pl.BlockSpec((SUBLANES, tn), lambda j: (0, j)),
        out_shape=jax.ShapeDtypeStruct((SUBLANES, n), F32),
        compiler_params=_params(("arbitrary",)),
        name="ada",
    )(cvec, w_ada, b_ada3)


def _ffn_kernel(x_ref, mod_ref, g_ref, win_ref, wout_ref, *rest, tf, nf, last_valid, mi, emit_h):
    if emit_h:
        o_ref, h_ref, hn_sc, acc_sc = rest
    else:
        o_ref, hn_sc, acc_sc = rest
    f = pl.program_id(1)

    @pl.when(f == 0)
    def _():
        xn = _rms(x_ref[...], NORM_EPS) * g_ref[mi:mi + 1, :]
        hn = xn * (1.0 + mod_ref[3 * mi + 1:3 * mi + 2, :]) + mod_ref[3 * mi:3 * mi + 1, :]
        hn_sc[...] = hn.astype(BF16)
        acc_sc[...] = jnp.zeros_like(acc_sc)

    def hidden_tile(valid):
        hn = hn_sc[...]
        g = jnp.dot(hn, win_ref[0], preferred_element_type=F32)
        u = jnp.dot(hn, win_ref[1], preferred_element_type=F32)
        act = g * jax.nn.sigmoid(g) * u
        wout = wout_ref[...]
        if valid < tf:
            act = jnp.where(lax.broadcasted_iota(jnp.int32, (1, tf), 1) < valid, act, 0.0)
            wout = jnp.where(lax.broadcasted_iota(jnp.int32, (tf, 1), 0) < valid, wout, jnp.zeros_like(wout))
        acc_sc[...] += jnp.dot(act.astype(BF16), wout, preferred_element_type=F32)

    if last_valid == tf:
        hidden_tile(tf)
    else:
        pl.when(f < nf - 1)(lambda: hidden_tile(tf))
        pl.when(f == nf - 1)(lambda: hidden_tile(last_valid))

    @pl.when(f == nf - 1)
    def _():
        out = x_ref[...] + 0.5 * mod_ref[3 * mi + 2:3 * mi + 3, :] * acc_sc[...]
        o_ref[...] = out
        if emit_h:
            hn = _rms(out, NORM_EPS) * g_ref[1:2, :]
            h_ref[...] = (hn * (1.0 + mod_ref[4:5, :]) + mod_ref[3:4, :]).astype(BF16)


def _ffn(xs, mod, norm_g, w_gu, w_out, l, w, mod_idx, *, n_tiles, mi, emit_h):
    m, d = xs.shape
    d_ff = w_out.shape[2]
    nf = pl.cdiv(d_ff, TF)
    out_rows = n_tiles * TM
    out_shape = [jax.ShapeDtypeStruct((out_rows, d), F32)]
    out_specs = [pl.BlockSpec((TM, d), lambda i, f: (i, 0))]
    if emit_h:
        out_shape.append(jax.ShapeDtypeStruct((out_rows, d), BF16))
        out_specs.append(pl.BlockSpec((TM, d), lambda i, f: (i, 0)))
    res = pl.pallas_call(
        functools.partial(_ffn_kernel, tf=TF, nf=nf, last_valid=d_ff - (nf - 1) * TF, mi=mi, emit_h=emit_h),
        grid=(n_tiles, nf),
        in_specs=[
            pl.BlockSpec((TM, d), lambda i, f: (i, 0)),
            pl.BlockSpec((None, N_MOD, d), lambda i, f: (mod_idx(i), 0, 0)),
            pl.BlockSpec((3, d), lambda i, f: (0, 0)),
            pl.BlockSpec((None, None, 2, d, TF), lambda i, f: (l, w, 0, 0, f)),
            pl.BlockSpec((None, None, TF, d), lambda i, f: (l, w, f, 0)),
        ],
        out_specs=out_specs,
        out_shape=out_shape,
        scratch_shapes=[pltpu.VMEM((TM, d), BF16), pltpu.VMEM((TM, d), F32)],
        compiler_params=_params(("parallel", "arbitrary")),
        name="ffn",
    )(xs, mod, norm_g, w_gu, w_out)
    return res if emit_h else res[0]


def _gmlp_kernel(h_ref, w_ref, vn_ref, ws_ref, bsb_ref, o_ref, p_sc, *, tm):
    p_sc[...] = jnp.dot(h_ref[...], w_ref[...], preferred_element_type=F32)
    gd = A_WIDTH // A_GROUPS
    for g in range(A_GROUPS):
        wsg = ws_ref[g].astype(BF16)
        bias = bsb_ref[g]
        gain = vn_ref[:, g * gd:(g + 1) * gd]
        for n in range(tm // CHUNK):
            rows = slice(n * CHUNK, (n + 1) * CHUNK)
            u = _gelu_tanh(p_sc[rows, g * gd:(g + 1) * gd])
            v = _gelu_tanh(p_sc[rows, A_WIDTH + g * gd:A_WIDTH + (g + 1) * gd])
            v = _rms(v, NORM_EPS) * gain
            sv = jnp.dot(wsg, v.astype(BF16), preferred_element_type=F32) + bias
            o_ref[rows, g * gd:(g + 1) * gd] = (u * sv).astype(o_ref.dtype)


def _gmlp(h, w_uv, vn, ws, bsb, n_tiles):
    m, d = h.shape
    return pl.pallas_call(
        functools.partial(_gmlp_kernel, tm=TM),
        grid=(n_tiles,),
        in_specs=[
            pl.BlockSpec((TM, d), lambda i: (i, 0)),
            pl.BlockSpec((d, 2 * A_WIDTH), lambda i: (0, 0)),
            pl.BlockSpec((1, A_WIDTH), lambda i: (0, 0)),
            pl.BlockSpec((A_GROUPS, CHUNK, CHUNK), lambda i: (0, 0, 0)),
            pl.BlockSpec((A_GROUPS, CHUNK, A_WIDTH // A_GROUPS), lambda i: (0, 0, 0)),
        ],
        out_specs=pl.BlockSpec((TM, A_WIDTH), lambda i: (i, 0)),
        out_shape=jax.ShapeDtypeStruct((n_tiles * TM, A_WIDTH), BF16),
        scratch_shapes=[pltpu.VMEM((TM, 2 * A_WIDTH), F32)],
        compiler_params=_params(("parallel",)),
        name="gmlp",
    )(h, w_uv, vn, ws, bsb)


def _qkv_kernel(h_ref, w_ref, gain_ref, cos_ref, sin_ref, g64_ref, o_ref, vx_ref, p_sc):
    j = pl.program_id(1)
    p_sc[...] = jnp.dot(h_ref[...], w_ref[...], preferred_element_type=F32)

    @pl.when(j == 2)
    def _():
        ones = jnp.ones((p_sc.shape[0], LANES), BF16)
        for hs in range(C_HEADS):
            vx_ref[:, 2 * hs * LANES:(2 * hs + 1) * LANES] = p_sc[:, hs * LANES:(hs + 1) * LANES].astype(BF16)
            vx_ref[:, (2 * hs + 1) * LANES:(2 * hs + 2) * LANES] = ones

    @pl.when(j < 2)
    def _():
        lane = lax.broadcasted_iota(jnp.int32, (1, LANES), 1)
        first = (lane % ROPE_AXIS_DIM) < (ROPE_AXIS_DIM // 2)
        cos = cos_ref[...]
        sin = sin_ref[...]
        gain = gain_ref[...]
        g64 = g64_ref[...]
        for hs in range(C_HEADS):
            x = p_sc[:, hs * LANES:(hs + 1) * LANES]
            ms = _gsum(x * x, g64) * (1.0 / C_HEAD_DIM)
            y = x * lax.rsqrt(ms + NORM_EPS) * gain
            half = ROPE_AXIS_DIM // 2
            rot = jnp.where(first, pltpu.roll(y, LANES - half, 1), pltpu.roll(y, half, 1))
            o_ref[:, hs * LANES:(hs + 1) * LANES] = (y * cos + rot * sin).astype(BF16)


def _qkv(h, w_qkv, gains, cos_t, sin_t, g64):
    m, d = h.shape
    return pl.pallas_call(
        _qkv_kernel,
        grid=(m // TM, 3),
        in_specs=[
            pl.BlockSpec((TM, d), lambda i, j: (i, 0)),
            pl.BlockSpec((d, C_WIDTH), lambda i, j: (0, j)),
            pl.BlockSpec((None, 1, LANES), lambda i, j: (j, 0, 0)),
            pl.BlockSpec((TM, LANES), lambda i, j: (i, 0)),
            pl.BlockSpec((TM, LANES), lambda i, j: (i, 0)),
            pl.BlockSpec((LANES, LANES), lambda i, j: (0, 0)),
        ],
        out_specs=[pl.BlockSpec((None, TM, C_WIDTH), lambda i, j: (jnp.minimum(j, 1), i, 0)),
                   pl.BlockSpec((TM, 2 * C_WIDTH), lambda i, j: (i, 0))],
        out_shape=[jax.ShapeDtypeStruct((2, m, C_WIDTH), BF16), jax.ShapeDtypeStruct((m, 2 * C_WIDTH), BF16)],
        scratch_shapes=[pltpu.VMEM((TM, C_WIDTH), F32)],
        compiler_params=_params(("parallel", "arbitrary")),
        name="qkv",
    )(h, w_qkv, gains, cos_t, sin_t, g64)


def _attn_kernel(lam_ref, sub_ref, q_ref, *refs, nseg, lam_init):
    kv = refs[:2 * nseg]
    o_ref = refs[2 * nseg]
    lv = lam_ref[...]
    lam = (jnp.exp(jnp.sum(lv[0:1] * lv[1:2], axis=-1, keepdims=True))
           - jnp.exp(jnp.sum(lv[2:3] * lv[3:4], axis=-1, keepdims=True)) + lam_init)
    q = q_ref[...]
    lane = lax.broadcasted_iota(jnp.int32, (1, LANES), 1)
    zero = jnp.zeros_like(q)
    outs = []
    for j in range(2):
        in_map = (lane < C_HEAD_DIM) if j == 0 else (lane >= C_HEAD_DIM)
        qj = jnp.where(in_map, q, zero)
        m = None
        acc = None
        for si in range(nseg):
            k_ref, v_ref = kv[2 * si], kv[2 * si + 1]
            ls = k_ref.shape[0]
            for c0 in range(0, ls, ATT_KC):
                kc = min(ATT_KC, ls - c0)
                s = lax.dot_general(qj, k_ref[c0:c0 + kc, :], (((1,), (1,)), ((), ())),
                                    preferred_element_type=F32)
                cm = jnp.max(s, axis=-1, keepdims=True)
                m_new = cm if m is None else jnp.maximum(m, cm)
                e = jnp.exp2((s - m_new).astype(BF16))
                pv = jnp.dot(e, v_ref[c0:c0 + kc, :], preferred_element_type=F32)
                acc = pv if acc is None else acc * jnp.exp2(m - m_new) + pv
                m = m_new
        outs.append(acc[:, :LANES] / acc[:, LANES:])
    o = outs[0] - lam * outs[1]
    o = _rms(o, NORM_EPS) * sub_ref[...]
    o_ref[...] = (o * (1.0 - lam_init)).astype(o_ref.dtype)


def _attn(qk, vx, lam, subln, yc_prev, *, b, t, c, lam_init, ctx_queries):
    m = qk.shape[1]
    if ctx_queries:
        tq, nq = c, 1
        q_blk = lambda bi, qi: (b * t) // c + bi
        segs = [(c, lambda bi: (b * t) // c + bi)]
    else:
        tq = ATT_TQ
        nq = t // tq
        q_blk = lambda bi, qi: bi * nq + qi
        segs = [(t, lambda bi: bi), (c, lambda bi: (b * t) // c + bi)]
    in_specs = [
        pl.BlockSpec((4, C_HEAD_DIM), lambda bi, h, qi: (0, 0)),
        pl.BlockSpec((1, LANES), lambda bi, h, qi: (0, 0)),
        pl.BlockSpec((None, tq, LANES), lambda bi, h, qi: (0, q_blk(bi, qi), h)),
    ]
    args = [lam, subln, qk]
    for ls, blk in segs:
        in_specs.append(pl.BlockSpec((None, ls, LANES), lambda bi, h, qi, blk=blk: (1, blk(bi), h)))
        in_specs.append(pl.BlockSpec((ls, 2 * LANES), lambda bi, h, qi, blk=blk: (blk(bi), h)))
        args += [qk, vx]
    aliases = {}
    if yc_prev is not None:
        in_specs.append(pl.BlockSpec(memory_space=pl.ANY))
        args.append(yc_prev)
        aliases = {len(args) - 1: 0}

    def body(*refs):
        n_in = 3 + 2 * len(segs)
        _attn_kernel(*refs[:n_in], refs[-1], nseg=len(segs), lam_init=lam_init)

    return pl.pallas_call(
        body,
        grid=(b, C_HEADS, nq),
        in_specs=in_specs,
        out_specs=pl.BlockSpec((tq, LANES), lambda bi, h, qi: (q_blk(bi, qi), h)),
        out_shape=jax.ShapeDtypeStruct((m, C_WIDTH), BF16),
        input_output_aliases=aliases,
        compiler_params=_params(("parallel", "parallel", "arbitrary")),
        name="diff_attn_ctx" if ctx_queries else "diff_attn",
    )(*args)


KQ = 4
NKL = B_HEAD_DIM // KQ
MERGED = 2 * B_WIDTH
BH = 2 * B_HEADS
LORA_MERGED = 2 * GATE_LORA + 8 * LORA_PAD
PROJ_MERGED = 3 * MERGED + LORA_MERGED


def _rkv_proj_kernel(h0_ref, h1_ref, w_ref, o_ref):
    o_ref[...] = (jnp.dot(h0_ref[...], w_ref[0], preferred_element_type=F32)
                  + jnp.dot(h1_ref[...], w_ref[1], preferred_element_type=F32))


def _rkv_proj(h, wz, *, t, c):
    d = h.shape[1]
    tm = 256
    tn = 1536
    s = t + c
    nlat = t // tm
    hblk = lambda bi, i: jnp.where(i < nlat, bi * nlat + i, (2 * t + bi * c) // tm + (i - nlat))
    return pl.pallas_call(
        _rkv_proj_kernel,
        grid=(PROJ_MERGED // tn, s // tm),
        in_specs=[
            pl.BlockSpec((tm, d), lambda j, i: (hblk(0, i), 0)),
            pl.BlockSpec((tm, d), lambda j, i: (hblk(1, i), 0)),
            pl.BlockSpec((2, d, tn), lambda j, i: (0, 0, j)),
        ],
        out_specs=pl.BlockSpec((tm, tn), lambda j, i: (i, j)),
        out_shape=jax.ShapeDtypeStruct((s, PROJ_MERGED), F32),
        compiler_params=_params(("parallel", "arbitrary")),
        name="rkv_proj",
    )(h, h, wz)


class _SlabWriter:
    def __init__(self, ref, lead=()):
        self.ref, self.lead, self.parts, self.base = ref, lead, [], 0

    def add(self, slab):
        self.parts.append(slab)
        if len(self.parts) == SUBLANES:
            block = jnp.swapaxes(jnp.stack(self.parts, axis=0), 0, 1)
            self.ref[self.lead + (slice(None), slice(self.base, self.base + SUBLANES), slice(None))] = block
            self.parts, self.base = [], self.base + SUBLANES


def _rwkv_prep_kernel(p_ref, prev_ref, next_ref, lo_ref, conv_ref, w0_ref, wup_ref, a0_ref, aup_ref,
                      kkg_ref, ka_ref, rk_ref, g32_ref, rep_ref,
                      r_o, kk_o, w_o, kd_o, b_o, vrep_o, v_o, bonus_o, k_sc, r_sc, *, tm, starts, ends):
    i = pl.program_id(0)
    is_start = functools.reduce(jnp.logical_or, [i == s for s in starts])
    is_end = functools.reduce(jnp.logical_or, [i == s for s in ends])
    row = lax.broadcasted_iota(jnp.int32, (tm, 1), 0)

    def conv(c0):
        cs = slice(c0, c0 + LANES)
        x = p_ref[:, cs]
        pm = jnp.where(is_start, 0.0, prev_ref[SUBLANES - 1:SUBLANES, cs])
        nx = jnp.where(is_end, 0.0, next_ref[0:1, cs])
        xm1 = jnp.where(row == 0, pm, pltpu.roll(x, 1, 0))
        xp1 = jnp.where(row == tm - 1, nx, pltpu.roll(x, tm - 1, 0))
        return xm1 * conv_ref[0:1, cs] + x * conv_ref[1:2, cs] + xp1 * conv_ref[2:3, cs]

    r_out, kk_out, vrep_out = _SlabWriter(r_o), _SlabWriter(kk_o), _SlabWriter(vrep_o)
    w_out = [_SlabWriter(w_o, (d,)) for d in range(2)]
    kd_out = [_SlabWriter(kd_o, (d,)) for d in range(2)]
    b_out = [_SlabWriter(b_o, (d,)) for d in range(2)]

    ss = None
    for m in range(NKL):
        cs = slice(m * LANES, (m + 1) * LANES)
        r = conv(m * LANES)
        r_out.add(r)
        r_sc[:, cs] = r
        k = conv(MERGED + m * LANES)
        k_sc[:, cs] = k
        kkr = k * kkg_ref[:, cs]
        ss = kkr * kkr if ss is None else ss + kkr * kkr
    rs = lax.rsqrt(_gsum(ss, g32_ref[...]) + 1e-12)

    w_off = 2 * GATE_LORA
    a_off = w_off + 4 * LORA_PAD
    def split2(x):
        hi = x.astype(BF16)
        return hi, (x - hi.astype(F32)).astype(BF16)

    def lora(x2, up_ref, d, cs):
        return (jnp.dot(x2[0], up_ref[0, d, :, cs], preferred_element_type=F32)
                + jnp.dot(x2[1], up_ref[0, d, :, cs], preferred_element_type=F32)
                + jnp.dot(x2[0], up_ref[1, d, :, cs], preferred_element_type=F32))

    tw = [split2(jnp.tanh(lo_ref[:, w_off + 2 * d * LORA_PAD:w_off + 2 * (d + 1) * LORA_PAD])) for d in range(2)]
    pa = [split2(lo_ref[:, a_off + 2 * d * LORA_PAD:a_off + 2 * (d + 1) * LORA_PAD]) for d in range(2)]
    bonus = None
    for m in range(NKL):
        cs = slice(m * LANES, (m + 1) * LANES)
        k = k_sc[:, cs]
        kk = k * kkg_ref[:, cs] * rs
        kk_out.add(kk)
        kd_sum = None
        for d in range(2):
            w_lo = w0_ref[d:d + 1, cs] + lora(tw[d], wup_ref, d, cs)
            w_out[d].add(jnp.exp(-jax.nn.sigmoid(w_lo) * math.exp(-0.5)))
            a = jax.nn.sigmoid(a0_ref[d:d + 1, cs] + lora(pa[d], aup_ref, d, cs))
            kd = k * (1.0 + (a - 1.0) * ka_ref[:, cs])
            kd_out[d].add(kd)
            kd_sum = kd if kd_sum is None else kd_sum + kd
            b_out[d].add(kk * a)
        term = r_sc[:, cs] * (0.5 * kd_sum) * rk_ref[:, cs]
        bonus = term if bonus is None else bonus + term
    bonus_o[...] = bonus

    per_slab = LANES // BH
    for j in range(MERGED // LANES):
        vs = conv(2 * MERGED + j * LANES)
        v_o[:, j * LANES:(j + 1) * LANES] = vs
        for vi in range(per_slab):
            vrep_out.add(_gsum(vs, rep_ref[vi]))


def _rwkv_prep(p, conv_m, w0_m, wup_m, a0_m, aup_m, kk_m, ka_m, rk_m, g32, rep, *, t, c):
    s = t + c
    tm = 128
    nt = s // tm
    starts = [0, t // tm]
    ends = [t // tm - 1, nt - 1]
    hb = tm // SUBLANES
    rkv_w = 3 * MERGED
    full2 = lambda i: (0, 0)
    full3 = lambda i: (0, 0, 0)
    k3 = jax.ShapeDtypeStruct((s, NKL, LANES), F32)
    k4 = jax.ShapeDtypeStruct((2, s, NKL, LANES), F32)
    k3s = pl.BlockSpec((tm, NKL, LANES), lambda i: (i, 0, 0))
    k4s = pl.BlockSpec((2, tm, NKL, LANES), lambda i: (0, i, 0, 0))
    return pl.pallas_call(
        functools.partial(_rwkv_prep_kernel, tm=tm, starts=starts, ends=ends),
        grid=(nt,),
        in_specs=[
            pl.BlockSpec((tm, rkv_w), lambda i: (i, 0)),
            pl.BlockSpec((SUBLANES, rkv_w), lambda i: (jnp.maximum(i * hb - 1, 0), 0)),
            pl.BlockSpec((SUBLANES, rkv_w), lambda i: (jnp.minimum((i + 1) * hb, s // SUBLANES - 1), 0)),
            pl.BlockSpec((tm, LORA_MERGED), lambda i: (i, rkv_w // LORA_MERGED)),
            pl.BlockSpec((3, rkv_w), full2),
            pl.BlockSpec((2, MERGED), full2),
            pl.BlockSpec((2, 2, 2 * LORA_PAD, MERGED), lambda i: (0, 0, 0, 0)),
            pl.BlockSpec((2, MERGED), full2),
            pl.BlockSpec((2, 2, 2 * LORA_PAD, MERGED), lambda i: (0, 0, 0, 0)),
            pl.BlockSpec((1, MERGED), full2),
            pl.BlockSpec((1, MERGED), full2),
            pl.BlockSpec((1, MERGED), full2),
            pl.BlockSpec((LANES, LANES), full2),
            pl.BlockSpec((LANES // BH, LANES, LANES), full3),
        ],
        out_specs=[k3s, k3s, k4s, k4s, k4s,
                   pl.BlockSpec((tm, B_HEAD_DIM, LANES), lambda i: (i, 0, 0)),
                   pl.BlockSpec((tm, MERGED), lambda i: (i, 0)),
                   pl.BlockSpec((tm, LANES), lambda i: (i, 0))],
        out_shape=[k3, k3, k4, k4, k4, jax.ShapeDtypeStruct((s, B_HEAD_DIM, LANES), F32),
                   jax.ShapeDtypeStruct((s, MERGED), F32), jax.ShapeDtypeStruct((s, LANES), F32)],
        scratch_shapes=[pltpu.VMEM((tm, MERGED), F32), pltpu.VMEM((tm, MERGED), F32)],
        compiler_params=_params(("parallel",)),
        name="rwkv_prep",
    )(p, p, p, p, conv_m, w0_m, wup_m, a0_m, aup_m, kk_m, ka_m, rk_m, g32, rep)


def _scan_kernel(*refs, ts):
    ins, (yf_ref, yb_ref, s_sc, y_sc, sa_sc) = refs[:12], refs[12:]
    streams = [ins[:6] + (yf_ref,), ins[6:] + (yb_ref,)]
    nkl = s_sc.shape[1]
    nvb = s_sc.shape[2] // SUBLANES

    @pl.when(pl.program_id(0) == 0)
    def _():
        s_sc[...] = jnp.zeros_like(s_sc)
        y_sc[...] = jnp.zeros_like(y_sc)

    def vsl(vb):
        return slice(vb * SUBLANES, (vb + 1) * SUBLANES)

    diag = (lax.broadcasted_iota(jnp.int32, (SUBLANES, LANES), 1) // BH
            == lax.broadcasted_iota(jnp.int32, (SUBLANES, LANES), 0) % KQ)

    def emit_y(d, y_ref, row):
        for vb in range(nvb):
            y = y_sc[d, vsl(vb), :]
            y = y + pltpu.roll(y, BH, 1)
            y = y + pltpu.roll(y, 2 * BH, 1)
            y = jnp.where(diag, y, 0.0)
            y = y + pltpu.roll(y, 1, 0)
            y = y + pltpu.roll(y, 2, 0)
            for half in range(SUBLANES // KQ):
                srow = half * KQ + KQ - 1
                y_ref[row, 2 * vb + half:2 * vb + half + 1, :] = y[srow:srow + 1, :]

    def allreduce(x):
        x = x + pltpu.roll(x, BH, 1)
        return x + pltpu.roll(x, 2 * BH, 1)

    def step(j, carry):
        ts_of = (j, ts - 1 - j)
        prev = (jnp.maximum(j - 1, 0), jnp.minimum(ts - j, ts - 1))
        for d in range(2):
            emit_y(d, streams[d][-1], prev[d])
        nxt = (jnp.minimum(j + 1, ts - 1), jnp.maximum(ts - 2 - j, 0))
        for d, (w_ref, kd_ref, b_ref, kk_ref, r_ref, v_ref, y_ref) in enumerate(streams):
            t = ts_of[d]
            sa = [sa_sc[d, vsl(vb), :] if d == 0 else allreduce(sa_sc[d, vsl(vb), :]) for vb in range(nvb)]
            ys = [None] * nvb
            sa_next = [None] * nvb
            for kl in range(nkl):
                wrow = w_ref[t, kl:kl + 1, :]
                brow = b_ref[t, kl:kl + 1, :]
                kdrow = kd_ref[t, kl:kl + 1, :]
                rrow = r_ref[t, kl:kl + 1, :]
                kknext = kk_ref[nxt[d], kl:kl + 1, :]
                for vb in range(nvb):
                    sn = s_sc[d, kl, vsl(vb), :] * wrow - sa[vb] * brow + v_ref[t, vsl(vb), :] * kdrow
                    s_sc[d, kl, vsl(vb), :] = sn
                    term = sn * rrow
                    ys[vb] = term if ys[vb] is None else ys[vb] + term
                    term = sn * kknext
                    sa_next[vb] = term if sa_next[vb] is None else sa_next[vb] + term
            for vb in range(nvb):
                y_sc[d, vsl(vb), :] = ys[vb]
                sa_sc[d, vsl(vb), :] = allreduce(sa_next[vb]) if d == 0 else sa_next[vb]
        return carry

    for d, first in enumerate((0, ts - 1)):
        kk_ref = streams[d][3]
        for vb in range(nvb):
            acc = None
            for kl in range(nkl):
                term = s_sc[d, kl, vsl(vb), :] * kk_ref[first, kl:kl + 1, :]
                acc = term if acc is None else acc + term
            sa_sc[d, vsl(vb), :] = allreduce(acc) if d == 0 else acc

    lax.fori_loop(0, ts, step, 0)
    emit_y(0, yf_ref, ts - 1)
    emit_y(1, yb_ref, 0)


def _scan(w4, kd4, b4, kk3, r3, vrep, *, t, c):
    s = t + c
    ts = SCAN_TS
    nctx = c // ts
    nlat = t // ts
    fblk = lambda g: jnp.where(g < nctx, nlat + g, g - nctx)
    bblk = lambda g: jnp.where(g < nctx, nlat + (nctx - 1 - g), nlat - 1 - (g - nctx))
    in_specs = []
    args = []
    for d, blk in enumerate((fblk, bblk)):
        for a in (w4, kd4, b4):
            in_specs.append(pl.BlockSpec((None, ts, NKL, LANES), lambda g, d=d, blk=blk: (d, blk(g), 0, 0)))
            args.append(a)
        for a in (kk3, r3):
            in_specs.append(pl.BlockSpec((ts, NKL, LANES), lambda g, blk=blk: (blk(g), 0, 0)))
            args.append(a)
        in_specs.append(pl.BlockSpec((ts, B_HEAD_DIM, LANES), lambda g, blk=blk: (blk(g), 0, 0)))
        args.append(vrep)
    nslab = MERGED // LANES
    yshape = jax.ShapeDtypeStruct((s, nslab, LANES), F32)
    return pl.pallas_call(
        functools.partial(_scan_kernel, ts=ts),
        grid=(s // ts,),
        in_specs=in_specs,
        out_specs=[pl.BlockSpec((ts, nslab, LANES), lambda g: (fblk(g), 0, 0)),
                   pl.BlockSpec((ts, nslab, LANES), lambda g: (bblk(g), 0, 0))],
        out_shape=[yshape, yshape],
        scratch_shapes=[pltpu.VMEM((2, NKL, B_HEAD_DIM, LANES), F32),
                        pltpu.VMEM((2, B_HEAD_DIM, LANES), F32),
                        pltpu.VMEM((2, B_HEAD_DIM, LANES), F32)],
        compiler_params=_params(("arbitrary",)),
        name="wkv7_scan",
    )(*args)


def _rwkv_out_kernel(yf_ref, yb_ref, v_ref, bonus_ref, pg_ref, lng_ref, lnb_ref, gup_ref,
                     selv_ref, g32_ref, o_ref):
    g32 = g32_ref[...]
    gate = jnp.dot(jax.nn.sigmoid(pg_ref[...]).astype(BF16), gup_ref[...], preferred_element_type=F32)
    bonus = _gsum(bonus_ref[...], g32)
    nslab = MERGED // LANES
    ys = []
    for g in range(nslab // SUBLANES):
        rows = slice(g * SUBLANES, (g + 1) * SUBLANES)
        both = jnp.swapaxes(yf_ref[:, rows, :] + yb_ref[:, rows, :], 0, 1)
        ys += [both[j] for j in range(SUBLANES)]
    inv = 1.0 / B_HEAD_DIM
    mu = _gsum(functools.reduce(lambda a, b_: a + b_, ys), g32) * inv
    ds = [y - mu for y in ys]
    var = _gsum(functools.reduce(lambda a, b_: a + b_, [x * x for x in ds]), g32) * inv
    rstd = lax.rsqrt(var + LN_X_EPS)
    outs = []
    for j in range(nslab):
        cs = slice(j * LANES, (j + 1) * LANES)
        yn = ds[j] * rstd * lng_ref[:, cs] + lnb_ref[:, cs]
        outs.append(((yn + bonus * v_ref[:, cs]) * gate[:, cs]).astype(BF16))
    for j in range(nslab // 2):
        pair = jnp.concatenate([outs[2 * j], outs[2 * j + 1]], axis=1)
        for bi in range(2):
            o_ref[bi, :, j * LANES:(j + 1) * LANES] = jnp.dot(
                pair, selv_ref[bi], preferred_element_type=F32).astype(o_ref.dtype)


def _rwkv_out(yf, yb, v2d, bonus, p, lng_m, lnb_m, gup_m, selv, g32, *, rows):
    tm = 256
    nslab = MERGED // LANES
    y3s = pl.BlockSpec((tm, nslab, LANES), lambda i: (i, 0, 0))
    full2 = lambda i: (0, 0)
    return pl.pallas_call(
        _rwkv_out_kernel,
        grid=(rows // tm,),
        in_specs=[
            y3s, y3s,
            pl.BlockSpec((tm, MERGED), lambda i: (i, 0)),
            pl.BlockSpec((tm, LANES), lambda i: (i, 0)),
            pl.BlockSpec((tm, 2 * GATE_LORA), lambda i: (i, 3 * MERGED // (2 * GATE_LORA))),
            pl.BlockSpec((1, MERGED), full2),
            pl.BlockSpec((1, MERGED), full2),
            pl.BlockSpec((2 * GATE_LORA, MERGED), full2),
            pl.BlockSpec((2, 2 * LANES, LANES), lambda i: (0, 0, 0)),
            pl.BlockSpec((LANES, LANES), full2),
        ],
        out_specs=pl.BlockSpec((2, tm, B_WIDTH), lambda i: (0, i, 0)),
        out_shape=jax.ShapeDtypeStruct((2, rows, B_WIDTH), BF16),
        compiler_params=_params(("parallel",)),
        name="rwkv_out",
    )(yf, yb, v2d, bonus, p, lng_m, lnb_m, gup_m, selv, g32)


def _kmerge_cols(w):
    r = w.shape[0]
    wt = w.reshape(r, B_HEADS, NKL, KQ).transpose(0, 2, 3, 1)
    z = jnp.zeros_like(wt)
    return jnp.stack([jnp.stack([wt, z], axis=3), jnp.stack([z, wt], axis=3)]).reshape(2, r, MERGED)


def _vmerge_cols(w):
    r = w.shape[0]
    wt = w.reshape(r, B_HEADS, B_HEAD_DIM).transpose(0, 2, 1)
    z = jnp.zeros_like(wt)
    return jnp.stack([jnp.stack([wt, z], axis=2), jnp.stack([z, wt], axis=2)]).reshape(2, r, MERGED)


def _bmerge_cols(w, width):
    wp = jnp.pad(w, ((0, 0), (0, width - w.shape[1])))
    z = jnp.zeros_like(wp)
    return jnp.stack([jnp.concatenate([wp, z], axis=1), jnp.concatenate([z, wp], axis=1)])


def _both(m):
    return m[0] + m[1]


def _selectors():
    ci = jnp.arange(LANES)[None, :]
    bsel = jnp.arange(2)[:, None, None]
    c_vi, c_h = ci // B_HEADS, ci % B_HEADS
    r2 = jnp.arange(2 * LANES)[:, None]
    selv = ((r2 // BH == c_vi) & ((r2 % BH) // B_HEADS == bsel) & (r2 % B_HEADS == c_h)).astype(BF16)
    r1 = jnp.arange(LANES)[:, None]
    g32 = (r1 % BH == ci % BH).astype(BF16)
    rep = jnp.stack([((r1 // BH == vi) & (r1 % BH == ci % BH)) for vi in range(LANES // BH)]).astype(BF16)
    return selv, g32, rep


def _merge_kernel(x_ref, h_ref, ya_ref, yb_ref, yc_ref, mod_ref, wg_ref, bg_ref, wb_ref, wo_ref,
                  o_ref, *, nz):
    j = pl.program_id(1)
    h = h_ref[...]
    z = None
    for i, y_ref in enumerate((ya_ref, yb_ref, yc_ref)):
        gate = jax.nn.sigmoid(jnp.dot(h, wg_ref[i], preferred_element_type=F32) + bg_ref[i])
        term = gate * jnp.dot(y_ref[...], wb_ref[i], preferred_element_type=F32)
        z = term if z is None else z + term
    part = jnp.dot(z.astype(BF16), wo_ref[...], preferred_element_type=F32)

    @pl.when(j == 0)
    def _():
        o_ref[...] = part

    @pl.when(j > 0)
    def _():
        o_ref[...] += part

    @pl.when(j == nz - 1)
    def _():
        o_ref[...] = x_ref[...] + mod_ref[5:6, :] * o_ref[...]


def _merge(xs, h, ya, yb, yc, mod, wg, bg, wb, wo, mod_idx, n_tiles):
    d = xs.shape[1]
    tz = 512
    nz = d // tz
    rowt = lambda i, j: (i, 0)
    return pl.pallas_call(
        functools.partial(_merge_kernel, nz=nz),
        grid=(n_tiles, nz),
        in_specs=[
            pl.BlockSpec((TM, d), rowt),
            pl.BlockSpec((TM, d), rowt),
            pl.BlockSpec((TM, A_WIDTH), rowt),
            pl.BlockSpec((TM, B_WIDTH), rowt),
            pl.BlockSpec((TM, C_WIDTH), rowt),
            pl.BlockSpec((None, N_MOD, d), lambda i, j: (mod_idx(i), 0, 0)),
            pl.BlockSpec((N_BRANCH, d, tz), lambda i, j: (0, 0, j)),
            pl.BlockSpec((N_BRANCH, 1, tz), lambda i, j: (0, 0, j)),
            pl.BlockSpec((N_BRANCH, A_WIDTH, tz), lambda i, j: (0, 0, j)),
            pl.BlockSpec((tz, d), lambda i, j: (j, 0)),
        ],
        out_specs=pl.BlockSpec((TM, d), rowt),
        out_shape=jax.ShapeDtypeStruct((n_tiles * TM, d), F32),
        compiler_params=_params(("parallel", "arbitrary")),
        name="merge",
    )(xs, h, ya, yb, yc, mod, wg, bg, wb, wo)


def _rope_tables(b, t, c):
    rows = t // GRID_W
    rowp = jnp.repeat(jnp.arange(rows), GRID_W).astype(F32)
    colp = jnp.tile(jnp.arange(GRID_W), rows).astype(F32)
    inv = 1.0 / (ROPE_BASE ** (jnp.arange(0, ROPE_AXIS_DIM, 2, dtype=F32) / ROPE_AXIS_DIM))
    ar, ac = rowp[:, None] * inv, colp[:, None] * inv
    cr, sr, cc, sc = jnp.cos(ar), jnp.sin(ar), jnp.cos(ac), jnp.sin(ac)
    cos64 = jnp.concatenate([cr, cr, cc, cc], axis=-1)
    sin64 = jnp.concatenate([-sr, sr, -sc, sc], axis=-1)
    cos_t = jnp.tile(cos64, (b, 2))
    sin_t = jnp.tile(sin64, (b, 2))
    cos_t = jnp.concatenate([cos_t, jnp.ones((b * c, LANES), F32)])
    sin_t = jnp.concatenate([sin_t, jnp.zeros((b * c, LANES), F32)])
    return cos_t, sin_t


def kernel(x, c, ctx, c_ctx, w_ada, b_ada, norm_g, ffn_w_in, ffn_w_out, w_in, gm_v_norm, gm_ws, gm_bs,
           rw_conv, rw_w0, rw_w_up, rw_a0, rw_a_up, rw_g_up, rw_k_k, rw_k_a, rw_r_k, rw_ln_g, rw_ln_b,
           da_q_norm, da_k_norm, da_lam, da_subln, w_branch, b_gate, w_out):
    b, t, d = x.shape
    cl = ctx.shape[1]
    depth = w_ada.shape[0]
    d_ff = ffn_w_out.shape[2]
    assert b == 2 and b * cl == TM and t % TM == 0 and cl % 256 == 0
    n_lat = (b * t) // TM
    n_all = n_lat + 1
    tiles_per_batch = t // TM
    mod_idx = lambda i: jnp.where(i < n_lat, i // tiles_per_batch, b)

    xs = jnp.concatenate([x.reshape(b * t, d), ctx.reshape(b * cl, d)])
    cvec = jnp.zeros((SUBLANES, d), F32).at[:b].set(c).at[b].set(c_ctx)
    b_ada3 = b_ada.reshape(depth, 1, N_MOD * d)
    cos_t, sin_t = _rope_tables(b, t, cl)
    lane = jnp.arange(LANES)
    g64 = (lane[:, None] // C_HEAD_DIM == lane[None, :] // C_HEAD_DIM).astype(BF16)
    selv, g32, rep = _selectors()
    w_gu = jnp.transpose(ffn_w_in.reshape(depth, 2, d, 2, d_ff), (0, 1, 3, 2, 4)).astype(BF16)
    w_dn = ffn_w_out.astype(BF16)

    o = 0
    offs = []
    for n in (A_WIDTH, A_WIDTH, 3 * B_WIDTH, GATE_LORA, 2 * DECAY_LORA, 2 * ICL_LORA, 3 * C_WIDTH, N_BRANCH * d):
        offs.append((o, o + n))
        o += n

    for l in range(depth):
        last = l == depth - 1
        lam_init = 0.8 - 0.6 * math.exp(-0.3 * l)
        mod = _ada(cvec, w_ada, b_ada3, l)[:b + 1].reshape(b + 1, N_MOD, d)

        xs, h = _ffn(xs, mod, norm_g[l], w_gu, w_dn, l, 0, mod_idx, n_tiles=n_all, mi=0, emit_h=True)

        wl = w_in[l]
        sl = lambda i: wl[:, offs[i][0]:offs[i][1]]
        n_mix = n_lat if last else n_all

        w_uv = jnp.concatenate([sl(0), sl(1)], axis=1).astype(BF16)
        bsb = jnp.broadcast_to(gm_bs[l][:, :, None], (A_GROUPS, CHUNK, A_WIDTH // A_GROUPS))
        ya = _gmlp(h, w_uv, gm_v_norm[l].reshape(1, A_WIDTH), gm_ws[l], bsb, n_mix)

        w_rkv, w_dec, w_icl = sl(2), sl(4), sl(5)
        wz = jnp.concatenate(
            [_kmerge_cols(w_rkv[:, :B_WIDTH]), _kmerge_cols(w_rkv[:, B_WIDTH:2 * B_WIDTH]),
             _vmerge_cols(w_rkv[:, 2 * B_WIDTH:]), _bmerge_cols(sl(3), GATE_LORA),
             _bmerge_cols(w_dec[:, :DECAY_LORA], LORA_PAD), _bmerge_cols(w_dec[:, DECAY_LORA:], LORA_PAD),
             _bmerge_cols(w_icl[:, :ICL_LORA], LORA_PAD), _bmerge_cols(w_icl[:, ICL_LORA:], LORA_PAD)],
            axis=2).astype(BF16)
        p = _rkv_proj(h, wz, t=t, c=cl)
        conv = rw_conv[l]
        conv_m = jnp.concatenate([_both(_kmerge_cols(conv[:, :B_WIDTH])),
                                  _both(_kmerge_cols(conv[:, B_WIDTH:2 * B_WIDTH])),
                                  _both(_vmerge_cols(conv[:, 2 * B_WIDTH:]))], axis=1)
        def up_m(w):
            w32 = jnp.stack([
                _kmerge_cols(jnp.pad(w[dd], ((0, LORA_PAD - w.shape[1]), (0, 0)))).reshape(2 * LORA_PAD, MERGED)
                for dd in range(2)])
            hi = w32.astype(BF16)
            return jnp.stack([hi, (w32 - hi.astype(F32)).astype(BF16)])
        r3, kk3, w4, kd4, b4, vrep, v2d, bonus = _rwkv_prep(
            p, conv_m, _both(_kmerge_cols(rw_w0[l])), up_m(rw_w_up[l]), _both(_kmerge_cols(rw_a0[l])),
            up_m(rw_a_up[l]), _both(_kmerge_cols(rw_k_k[l].reshape(1, B_WIDTH))),
            _both(_kmerge_cols(rw_k_a[l].reshape(1, B_WIDTH))),
            _both(_kmerge_cols(rw_r_k[l].reshape(1, B_WIDTH))), g32, rep, t=t, c=cl)
        yf, ybk = _scan(w4, kd4, b4, kk3, r3, vrep, t=t, c=cl)
        yb2 = _rwkv_out(yf, ybk, v2d, bonus, p, _both(_vmerge_cols(rw_ln_g[l].reshape(1, B_WIDTH))),
                        _both(_vmerge_cols(rw_ln_b[l].reshape(1, B_WIDTH))),
                        _vmerge_cols(rw_g_up[l]).reshape(2 * GATE_LORA, MERGED).astype(BF16),
                        selv, g32, rows=t if last else t + cl)
        if last:
            yb = yb2.reshape(b * t, B_WIDTH)
        else:
            yb = jnp.concatenate([yb2[0, :t], yb2[1, :t], yb2[0, t:], yb2[1, t:]])

        gains = jnp.stack([jnp.tile(da_q_norm[l], 2) * (C_HEAD_DIM ** -0.5 * math.log2(math.e)),
                           jnp.tile(da_k_norm[l], 2), jnp.ones((LANES,), F32)]).reshape(3, 1, LANES)
        qk, vx = _qkv(h, sl(6).astype(BF16), gains, cos_t, sin_t, g64)
        sub = da_subln[l].reshape(1, LANES)
        yc = _attn(qk, vx, da_lam[l], sub, None, b=b, t=t, c=cl, lam_init=lam_init, ctx_queries=False)
        if not last:
            yc = _attn(qk, vx, da_lam[l], sub, yc, b=b, t=t, c=cl, lam_init=lam_init, ctx_queries=True)

        wg = jnp.transpose(sl(7).reshape(d, N_BRANCH, d), (1, 0, 2)).astype(BF16)
        wb = w_branch[l]
        wb_b = wb[1].reshape(B_HEADS, B_HEAD_DIM, d).transpose(1, 0, 2).reshape(B_WIDTH, d)
        wbr = jnp.stack([wb[0], wb_b, wb[2]]).astype(BF16)
        xs = _merge(xs, h, ya, yb, yc, mod, wg, b_gate[l].reshape(N_BRANCH, 1, d),
                    wbr, w_out[l].astype(BF16), mod_idx, n_mix)

        xs = _ffn(xs, mod, norm_g[l], w_gu, w_dn, l, 1, mod_idx, n_tiles=n_mix, mi=2, emit_h=False)

    return xs[:b * t].reshape(b, t, d)
```

```python
import functools
import math

import jax
import jax.numpy as jnp
from jax import lax
from jax.experimental import pallas as pl
from jax.experimental.pallas import tpu as pltpu

F32 = jnp.float32
BF16 = jnp.bfloat16

N_MOD = 9
CHUNK = 128
A_WIDTH = 1024
A_GROUPS = 8
B_WIDTH = 1024
B_HEAD_DIM = 64
B_HEADS = 16
DECAY_LORA = 96
ICL_LORA = 96
GATE_LORA = 256
C_HEADS = 8
C_HEAD_DIM = 64
C_WIDTH = 1024
N_BRANCH = 3
GRID_W = 64
ROPE_BASE = 10000.0
ROPE_AXIS_DIM = 32
NORM_EPS = 1e-6
LN_X_EPS = 64e-5

LANES = 128
SUBLANES = 8
VMEM_LIMIT = 56 * 1024 * 1024

TM = 512
TF = 512
LORA_PAD = 128
SCAN_TS = 64
ATT_TQ = 512
ATT_KC = 512


def _params(sem):
    return pltpu.CompilerParams(dimension_semantics=sem, vmem_limit_bytes=VMEM_LIMIT)


def _rms(x, eps):
    return x * lax.rsqrt(jnp.mean(x * x, axis=-1, keepdims=True) + eps)


def _split3(s):
    hi = s.astype(BF16)
    r = s - hi.astype(F32)
    mid = r.astype(BF16)
    lo = (r - mid.astype(F32)).astype(BF16)
    return hi, mid, lo


def _gsum(s, g_bf16):
    out = None
    for piece in _split3(s):
        d = jnp.dot(piece, g_bf16, preferred_element_type=F32)
        out = d if out is None else out + d
    return out


def _gelu_tanh(x):
    cdf = 0.5 * (1.0 + jnp.tanh(math.sqrt(2.0 / math.pi) * (x + 0.044715 * (x * x * x))))
    return x * cdf


def _ada_kernel(c_ref, w_ref, b_ref, o_ref):
    s = c_ref[...]
    s = s * jax.nn.sigmoid(s)
    w = w_ref[...]
    w_hi = w.astype(BF16)
    w_lo = (w - w_hi.astype(F32)).astype(BF16)
    s3 = _split3(s)
    acc = b_ref[...]
    for piece in s3:
        acc = acc + jnp.dot(piece, w_hi, preferred_element_type=F32)
    for piece in s3[:2]:
        acc = acc + jnp.dot(piece, w_lo, preferred_element_type=F32)
    o_ref[...] = acc


def _ada(cvec, w_ada, b_ada3, l):
    d = cvec.shape[1]
    n = w_ada.shape[2]
    tn = 1024
    return pl.pallas_call(
        _ada_kernel,
        grid=(n // tn,),
        in_specs=[
            pl.BlockSpec((SUBLANES, d), lambda j: (0, 0)),
            pl.BlockSpec((None, d, tn), lambda j: (l, 0, j)),
            pl.BlockSpec((None, 1, tn), lambda j: (l, 0, j)),
        ],
        out_specs=pl.BlockSpec((SUBLANES, tn), lambda j: (0, j)),
        out_shape=jax.ShapeDtypeStruct((SUBLANES, n), F32),
        compiler_params=_params(("arbitrary",)),
        name="ada",
    )(cvec, w_ada, b_ada3)


def _ffn_kernel(x_ref, mod_ref, g_ref, win_ref, wout_ref, *rest, tf, nf, last_valid, mi, emit_h):
    if emit_h:
        o_ref, h_ref, hn_sc, acc_sc = rest
    else:
        o_ref, hn_sc, acc_sc = rest
    f = pl.program_id(1)

    @pl.when(f == 0)
    def _():
        xn = _rms(x_ref[...], NORM_EPS) * g_ref[mi:mi + 1, :]
        hn = xn * (1.0 + mod_ref[3 * mi + 1:3 * mi + 2, :]) + mod_ref[3 * mi:3 * mi + 1, :]
        hn_sc[...] = hn.astype(BF16)
        acc_sc[...] = jnp.zeros_like(acc_sc)

    def hidden_tile(valid):
        hn = hn_sc[...]
        g = jnp.dot(hn, win_ref[0, :, :valid], preferred_element_type=F32)
        u = jnp.dot(hn, win_ref[1, :, :valid], preferred_element_type=F32)
        act = (g * jax.nn.sigmoid(g) * u).astype(BF16)
        acc_sc[...] += jnp.dot(act, wout_ref[:valid, :], preferred_element_type=F32)

    if last_valid == tf:
        hidden_tile(tf)
    else:
        pl.when(f < nf - 1)(lambda: hidden_tile(tf))
        pl.when(f == nf - 1)(lambda: hidden_tile(last_valid))

    @pl.when(f == nf - 1)
    def _():
        out = x_ref[...] + 0.5 * mod_ref[3 * mi + 2:3 * mi + 3, :] * acc_sc[...]
        o_ref[...] = out
        if emit_h:
            hn = _rms(out, NORM_EPS) * g_ref[1:2, :]
            h_ref[...] = (hn * (1.0 + mod_ref[4:5, :]) + mod_ref[3:4, :]).astype(BF16)


def _ffn(xs, mod, norm_g, w_gu, w_out, l, w, mod_idx, *, n_tiles, mi, emit_h):
    m, d = xs.shape
    d_ff = w_out.shape[2]
    assert d_ff % LANES == 0
    nf = pl.cdiv(d_ff, TF)
    out_rows = n_tiles * TM
    out_shape = [jax.ShapeDtypeStruct((out_rows, d), F32)]
    out_specs = [pl.BlockSpec((TM, d), lambda i, f: (i, 0))]
    if emit_h:
        out_shape.append(jax.ShapeDtypeStruct((out_rows, d), BF16))
        out_specs.append(pl.BlockSpec((TM, d), lambda i, f: (i, 0)))
    res = pl.pallas_call(
        functools.partial(_ffn_kernel, tf=TF, nf=nf, last_valid=d_ff - (nf - 1) * TF, mi=mi, emit_h=emit_h),
        grid=(n_tiles, nf),
        in_specs=[
            pl.BlockSpec((TM, d), lambda i, f: (i, 0)),
            pl.BlockSpec((None, N_MOD, d), lambda i, f: (mod_idx(i), 0, 0)),
            pl.BlockSpec((3, d), lambda i, f: (0, 0)),
            pl.BlockSpec((None, None, 2, d, TF), lambda i, f: (l, w, 0, 0, f)),
            pl.BlockSpec((None, None, TF, d), lambda i, f: (l, w, f, 0)),
        ],
        out_specs=out_specs,
        out_shape=out_shape,
        scratch_shapes=[pltpu.VMEM((TM, d), BF16), pltpu.VMEM((TM, d), F32)],
        compiler_params=_params(("parallel", "arbitrary")),
        name="ffn",
    )(xs, mod, norm_g, w_gu, w_out)
    return res if emit_h else res[0]


def _gmlp_kernel(h_ref, w_ref, vn_ref, ws_ref, bsb_ref, o_ref, p_sc, *, tm):
    p_sc[...] = jnp.dot(h_ref[...], w_ref[...], preferred_element_type=F32)
    gd = A_WIDTH // A_GROUPS
    for g in range(A_GROUPS):
        wsg = ws_ref[g].astype(BF16)
        bias = bsb_ref[g]
        gain = vn_ref[:, g * gd:(g + 1) * gd]
        for n in range(tm // CHUNK):
            rows = slice(n * CHUNK, (n + 1) * CHUNK)
            u = _gelu_tanh(p_sc[rows, g * gd:(g + 1) * gd])
            v = _gelu_tanh(p_sc[rows, A_WIDTH + g * gd:A_WIDTH + (g + 1) * gd])
            v = _rms(v, NORM_EPS) * gain
            sv = jnp.dot(wsg, v.astype(BF16), preferred_element_type=F32) + bias
            o_ref[rows, g * gd:(g + 1) * gd] = (u * sv).astype(o_ref.dtype)


def _gmlp(h, w_uv, vn, ws, bsb, n_tiles):
    m, d = h.shape
    return pl.pallas_call(
        functools.partial(_gmlp_kernel, tm=TM),
        grid=(n_tiles,),
        in_specs=[
            pl.BlockSpec((TM, d), lambda i: (i, 0)),
            pl.BlockSpec((d, 2 * A_WIDTH), lambda i: (0, 0)),
            pl.BlockSpec((1, A_WIDTH), lambda i: (0, 0)),
            pl.BlockSpec((A_GROUPS, CHUNK, CHUNK), lambda i: (0, 0, 0)),
            pl.BlockSpec((A_GROUPS, CHUNK, A_WIDTH // A_GROUPS), lambda i: (0, 0, 0)),
        ],
        out_specs=pl.BlockSpec((TM, A_WIDTH), lambda i: (i, 0)),
        out_shape=jax.ShapeDtypeStruct((n_tiles * TM, A_WIDTH), BF16),
        scratch_shapes=[pltpu.VMEM((TM, 2 * A_WIDTH), F32)],
        compiler_params=_params(("parallel",)),
        name="gmlp",
    )(h, w_uv, vn, ws, bsb)


def _qkv_kernel(h_ref, w_ref, gain_ref, cos_ref, sin_ref, g64_ref, o_ref, vx_ref, p_sc):
    j = pl.program_id(1)
    p_sc[...] = jnp.dot(h_ref[...], w_ref[...], preferred_element_type=F32)

    @pl.when(j == 2)
    def _():
        ones = jnp.ones((p_sc.shape[0], LANES), BF16)
        for hs in range(C_HEADS):
            vx_ref[:, 2 * hs * LANES:(2 * hs + 1) * LANES] = p_sc[:, hs * LANES:(hs + 1) * LANES].astype(BF16)
            vx_ref[:, (2 * hs + 1) * LANES:(2 * hs + 2) * LANES] = ones

    @pl.when(j < 2)
    def _():
        lane = lax.broadcasted_iota(jnp.int32, (1, LANES), 1)
        first = (lane % ROPE_AXIS_DIM) < (ROPE_AXIS_DIM // 2)
        cos = cos_ref[...]
        sin = sin_ref[...]
        gain = gain_ref[...]
        g64 = g64_ref[...]
        for hs in range(C_HEADS):
            x = p_sc[:, hs * LANES:(hs + 1) * LANES]
            ms = _gsum(x * x, g64) * (1.0 / C_HEAD_DIM)
            y = x * lax.rsqrt(ms + NORM_EPS) * gain
            half = ROPE_AXIS_DIM // 2
            rot = jnp.where(first, pltpu.roll(y, LANES - half, 1), pltpu.roll(y, half, 1))
            o_ref[:, hs * LANES:(hs + 1) * LANES] = (y * cos + rot * sin).astype(BF16)


def _qkv(h, w_qkv, gains, cos_t, sin_t, g64):
    m, d = h.shape
    return pl.pallas_call(
        _qkv_kernel,
        grid=(m // TM, 3),
        in_specs=[
            pl.BlockSpec((TM, d), lambda i, j: (i, 0)),
            pl.BlockSpec((d, C_WIDTH), lambda i, j: (0, j)),
            pl.BlockSpec((None, 1, LANES), lambda i, j: (j, 0, 0)),
            pl.BlockSpec((TM, LANES), lambda i, j: (i, 0)),
            pl.BlockSpec((TM, LANES), lambda i, j: (i, 0)),
            pl.BlockSpec((LANES, LANES), lambda i, j: (0, 0)),
        ],
        out_specs=[pl.BlockSpec((None, TM, C_WIDTH), lambda i, j: (jnp.minimum(j, 1), i, 0)),
                   pl.BlockSpec((TM, 2 * C_WIDTH), lambda i, j: (i, 0))],
        out_shape=[jax.ShapeDtypeStruct((2, m, C_WIDTH), BF16), jax.ShapeDtypeStruct((m, 2 * C_WIDTH), BF16)],
        scratch_shapes=[pltpu.VMEM((TM, C_WIDTH), F32)],
        compiler_params=_params(("parallel", "arbitrary")),
        name="qkv",
    )(h, w_qkv, gains, cos_t, sin_t, g64)


def _attn_kernel(lam_ref, sub_ref, q_ref, *refs, nseg, lam_init):
    kv = refs[:2 * nseg]
    o_ref = refs[2 * nseg]
    lv = lam_ref[...]
    lam = (jnp.exp(jnp.sum(lv[0:1] * lv[1:2], axis=-1, keepdims=True))
           - jnp.exp(jnp.sum(lv[2:3] * lv[3:4], axis=-1, keepdims=True)) + lam_init)
    q = q_ref[...]
    lane = lax.broadcasted_iota(jnp.int32, (1, LANES), 1)
    zero = jnp.zeros_like(q)
    outs = []
    for j in range(2):
        in_map = (lane < C_HEAD_DIM) if j == 0 else (lane >= C_HEAD_DIM)
        qj = jnp.where(in_map, q, zero)
        m = None
        acc = None
        for si in range(nseg):
            k_ref, v_ref = kv[2 * si], kv[2 * si + 1]
            ls = k_ref.shape[0]
            for c0 in range(0, ls, ATT_KC):
                kc = min(ATT_KC, ls - c0)
                s = lax.dot_general(qj, k_ref[c0:c0 + kc, :], (((1,), (1,)), ((), ())),
                                    preferred_element_type=F32)
                cm = jnp.max(s, axis=-1, keepdims=True)
                m_new = cm if m is None else jnp.maximum(m, cm)
                e = jnp.exp2((s - m_new).astype(BF16))
                pv = jnp.dot(e, v_ref[c0:c0 + kc, :], preferred_element_type=F32)
                acc = pv if acc is None else acc * jnp.exp2(m - m_new) + pv
                m = m_new
        outs.append(acc[:, :LANES] / acc[:, LANES:])
    o = outs[0] - lam * outs[1]
    o = _rms(o, NORM_EPS) * sub_ref[...]
    o_ref[...] = (o * (1.0 - lam_init)).astype(o_ref.dtype)


def _attn(qk, vx, lam, subln, yc_prev, *, b, t, c, lam_init, ctx_queries):
    m = qk.shape[1]
    if ctx_queries:
        tq, nq = c, 1
        q_blk = lambda bi, qi: (b * t) // c + bi
        segs = [(c, lambda bi: (b * t) // c + bi)]
    else:
        tq = ATT_TQ
        nq = t // tq
        q_blk = lambda bi, qi: bi * nq + qi
        segs = [(t, lambda bi: bi), (c, lambda bi: (b * t) // c + bi)]
    in_specs = [
        pl.BlockSpec((4, C_HEAD_DIM), lambda bi, h, qi: (0, 0)),
        pl.BlockSpec((1, LANES), lambda bi, h, qi: (0, 0)),
        pl.BlockSpec((None, tq, LANES), lambda bi, h, qi: (0, q_blk(bi, qi), h)),
    ]
    args = [lam, subln, qk]
    for ls, blk in segs:
        in_specs.append(pl.BlockSpec((None, ls, LANES), lambda bi, h, qi, blk=blk: (1, blk(bi), h)))
        in_specs.append(pl.BlockSpec((ls, 2 * LANES), lambda bi, h, qi, blk=blk: (blk(bi), h)))
        args += [qk, vx]
    aliases = {}
    if yc_prev is not None:
        in_specs.append(pl.BlockSpec(memory_space=pl.ANY))
        args.append(yc_prev)
        aliases = {len(args) - 1: 0}

    def body(*refs):
        n_in = 3 + 2 * len(segs)
        _attn_kernel(*refs[:n_in], refs[-1], nseg=len(segs), lam_init=lam_init)

    return pl.pallas_call(
        body,
        grid=(b, C_HEADS, nq),
        in_specs=in_specs,
        out_specs=pl.BlockSpec((tq, LANES), lambda bi, h, qi: (q_blk(bi, qi), h)),
        out_shape=jax.ShapeDtypeStruct((m, C_WIDTH), BF16),
        input_output_aliases=aliases,
        compiler_params=_params(("parallel", "parallel", "arbitrary")),
        name="diff_attn_ctx" if ctx_queries else "diff_attn",
    )(*args)


KQ = 4
NKL = B_HEAD_DIM // KQ
MERGED = 2 * B_WIDTH
BH = 2 * B_HEADS
LORA_MERGED = 2 * GATE_LORA + 8 * LORA_PAD
PROJ_MERGED = 3 * MERGED + LORA_MERGED


PROJ_UNMERGED = 3 * B_WIDTH + GATE_LORA + 4 * LORA_PAD


def _mm_kernel(a_ref, w_ref, o_ref):
    o_ref[...] = jnp.dot(a_ref[...], w_ref[...], preferred_element_type=F32)


def _rkv_proj(h, w):
    m, d = h.shape
    tn = PROJ_UNMERGED // 3
    return pl.pallas_call(
        _mm_kernel,
        grid=(PROJ_UNMERGED // tn, m // TM),
        in_specs=[pl.BlockSpec((TM, d), lambda j, i: (i, 0)),
                  pl.BlockSpec((d, tn), lambda j, i: (0, j))],
        out_specs=pl.BlockSpec((TM, tn), lambda j, i: (i, j)),
        out_shape=jax.ShapeDtypeStruct((m, PROJ_UNMERGED), F32),
        compiler_params=_params(("parallel", "arbitrary")),
        name="rkv_proj",
    )(h, w)


def _bmerge_kernel(p0_ref, p1_ref, e_ref, o_ref):
    n_kv = 3 * B_WIDTH // LANES
    for bi, p_ref in enumerate((p0_ref, p1_ref)):
        for s in range(n_kv):
            x = p_ref[:, s * LANES:(s + 1) * LANES]
            hi = x.astype(BF16)
            lo = (x - hi.astype(F32)).astype(BF16)
            for half in range(2):
                e = e_ref[bi, half]
                part = (jnp.dot(hi, e, preferred_element_type=F32) + jnp.dot(lo, e, preferred_element_type=F32))
                cs = slice((2 * s + half) * LANES, (2 * s + half + 1) * LANES)
                if bi == 0:
                    o_ref[:, cs] = part
                else:
                    o_ref[:, cs] += part
        src = 3 * B_WIDTH
        dst = 3 * MERGED
        for width in (GATE_LORA,) + (LORA_PAD,) * 4:
            o_ref[:, dst + bi * width:dst + (bi + 1) * width] = p_ref[:, src:src + width]
            src += width
            dst += 2 * width


def _bmerge(p, e, *, t, c):
    tm = 256
    s = t + c
    nlat = t // tm
    hblk = lambda bi, i: jnp.where(i < nlat, bi * nlat + i, (2 * t + bi * c) // tm + (i - nlat))
    return pl.pallas_call(
        _bmerge_kernel,
        grid=(s // tm,),
        in_specs=[
            pl.BlockSpec((tm, PROJ_UNMERGED), lambda i: (hblk(0, i), 0)),
            pl.BlockSpec((tm, PROJ_UNMERGED), lambda i: (hblk(1, i), 0)),
            pl.BlockSpec((2, 2, LANES, LANES), lambda i: (0, 0, 0, 0)),
        ],
        out_specs=pl.BlockSpec((tm, PROJ_MERGED), lambda i: (i, 0)),
        out_shape=jax.ShapeDtypeStruct((s, PROJ_MERGED), F32),
        compiler_params=_params(("parallel",)),
        name="batch_merge",
    )(p, p, e)


class _SlabWriter:
    def __init__(self, ref, lead=()):
        self.ref, self.lead, self.parts, self.base = ref, lead, [], 0

    def add(self, slab):
        self.parts.append(slab)
        if len(self.parts) == SUBLANES:
            block = jnp.swapaxes(jnp.stack(self.parts, axis=0), 0, 1)
            self.ref[self.lead + (slice(None), slice(self.base, self.base + SUBLANES), slice(None))] = block
            self.parts, self.base = [], self.base + SUBLANES


def _rwkv_prep_kernel(p_ref, prev_ref, next_ref, lo_ref, conv_ref, w0_ref, wup_ref, a0_ref, aup_ref,
                      kkg_ref, ka_ref, rk_ref, g32_ref, rep_ref,
                      r_o, kk_o, w_o, kd_o, b_o, vrep_o, v_o, bonus_o, k_sc, r_sc, *, tm, starts, ends):
    i = pl.program_id(0)
    is_start = functools.reduce(jnp.logical_or, [i == s for s in starts])
    is_end = functools.reduce(jnp.logical_or, [i == s for s in ends])
    row = lax.broadcasted_iota(jnp.int32, (tm, 1), 0)

    def conv(c0):
        cs = slice(c0, c0 + LANES)
        x = p_ref[:, cs]
        pm = jnp.where(is_start, 0.0, prev_ref[SUBLANES - 1:SUBLANES, cs])
        nx = jnp.where(is_end, 0.0, next_ref[0:1, cs])
        xm1 = jnp.where(row == 0, pm, pltpu.roll(x, 1, 0))
        xp1 = jnp.where(row == tm - 1, nx, pltpu.roll(x, tm - 1, 0))
        return xm1 * conv_ref[0:1, cs] + x * conv_ref[1:2, cs] + xp1 * conv_ref[2:3, cs]

    r_out, kk_out, vrep_out = _SlabWriter(r_o), _SlabWriter(kk_o), _SlabWriter(vrep_o)
    w_out = [_SlabWriter(w_o, (d,)) for d in range(2)]
    kd_out = [_SlabWriter(kd_o, (d,)) for d in range(2)]
    b_out = [_SlabWriter(b_o, (d,)) for d in range(2)]

    ss = None
    for m in range(NKL):
        cs = slice(m * LANES, (m + 1) * LANES)
        r = conv(m * LANES)
        r_out.add(r)
        r_sc[:, cs] = r
        k = conv(MERGED + m * LANES)
        k_sc[:, cs] = k
        kkr = k * kkg_ref[:, cs]
        ss = kkr * kkr if ss is None else ss + kkr * kkr
    rs = lax.rsqrt(_gsum(ss, g32_ref[...]) + 1e-12)

    w_off = 2 * GATE_LORA
    a_off = w_off + 4 * LORA_PAD
    def split2(x):
        hi = x.astype(BF16)
        return hi, (x - hi.astype(F32)).astype(BF16)

    def lora(x2, up_ref, d, cs):
        return (jnp.dot(x2[0], up_ref[0, d, :, cs], preferred_element_type=F32)
                + jnp.dot(x2[1], up_ref[0, d, :, cs], preferred_element_type=F32)
                + jnp.dot(x2[0], up_ref[1, d, :, cs], preferred_element_type=F32))

    tw = [split2(jnp.tanh(lo_ref[:, w_off + 2 * d * LORA_PAD:w_off + 2 * (d + 1) * LORA_PAD])) for d in range(2)]
    pa = [split2(lo_ref[:, a_off + 2 * d * LORA_PAD:a_off + 2 * (d + 1) * LORA_PAD]) for d in range(2)]
    bonus = None
    for m in range(NKL):
        cs = slice(m * LANES, (m + 1) * LANES)
        k = k_sc[:, cs]
        kk = k * kkg_ref[:, cs] * rs
        kk_out.add(kk)
        kd_sum = None
        for d in range(2):
            w_lo = w0_ref[d:d + 1, cs] + lora(tw[d], wup_ref, d, cs)
            w_out[d].add(jnp.exp(-jax.nn.sigmoid(w_lo) * math.exp(-0.5)))
            a = jax.nn.sigmoid(a0_ref[d:d + 1, cs] + lora(pa[d], aup_ref, d, cs))
            kd = k * (1.0 + (a - 1.0) * ka_ref[:, cs])
            kd_out[d].add(kd)
            kd_sum = kd if kd_sum is None else kd_sum + kd
            b_out[d].add(kk * a)
        term = r_sc[:, cs] * (0.5 * kd_sum) * rk_ref[:, cs]
        bonus = term if bonus is None else bonus + term
    bonus_o[...] = bonus

    per_slab = LANES // BH
    for j in range(MERGED // LANES):
        vs = conv(2 * MERGED + j * LANES)
        v_o[:, j * LANES:(j + 1) * LANES] = vs
        for vi in range(per_slab):
            vrep_out.add(_gsum(vs, rep_ref[vi]))


def _rwkv_prep(p, conv_m, w0_m, wup_m, a0_m, aup_m, kk_m, ka_m, rk_m, g32, rep, *, t, c):
    s = t + c
    tm = 128
    nt = s // tm
    starts = [0, t // tm]
    ends = [t // tm - 1, nt - 1]
    hb = tm // SUBLANES
    rkv_w = 3 * MERGED
    full2 = lambda i: (0, 0)
    full3 = lambda i: (0, 0, 0)
    k3 = jax.ShapeDtypeStruct((s, NKL, LANES), F32)
    k4 = jax.ShapeDtypeStruct((2, s, NKL, LANES), F32)
    k3s = pl.BlockSpec((tm, NKL, LANES), lambda i: (i, 0, 0))
    k4s = pl.BlockSpec((2, tm, NKL, LANES), lambda i: (0, i, 0, 0))
    return pl.pallas_call(
        functools.partial(_rwkv_prep_kernel, tm=tm, starts=starts, ends=ends),
        grid=(nt,),
        in_specs=[
            pl.BlockSpec((tm, rkv_w), lambda i: (i, 0)),
            pl.BlockSpec((SUBLANES, rkv_w), lambda i: (jnp.maximum(i * hb - 1, 0), 0)),
            pl.BlockSpec((SUBLANES, rkv_w), lambda i: (jnp.minimum((i + 1) * hb, s // SUBLANES - 1), 0)),
            pl.BlockSpec((tm, LORA_MERGED), lambda i: (i, rkv_w // LORA_MERGED)),
            pl.BlockSpec((3, rkv_w), full2),
            pl.BlockSpec((2, MERGED), full2),
            pl.BlockSpec((2, 2, 2 * LORA_PAD, MERGED), lambda i: (0, 0, 0, 0)),
            pl.BlockSpec((2, MERGED), full2),
            pl.BlockSpec((2, 2, 2 * LORA_PAD, MERGED), lambda i: (0, 0, 0, 0)),
            pl.BlockSpec((1, MERGED), full2),
            pl.BlockSpec((1, MERGED), full2),
            pl.BlockSpec((1, MERGED), full2),
            pl.BlockSpec((LANES, LANES), full2),
            pl.BlockSpec((LANES // BH, LANES, LANES), full3),
        ],
        out_specs=[k3s, k3s, k4s, k4s, k4s,
                   pl.BlockSpec((tm, B_HEAD_DIM, LANES), lambda i: (i, 0, 0)),
                   pl.BlockSpec((tm, MERGED), lambda i: (i, 0)),
                   pl.BlockSpec((tm, LANES), lambda i: (i, 0))],
        out_shape=[k3, k3, k4, k4, k4, jax.ShapeDtypeStruct((s, B_HEAD_DIM, LANES), F32),
                   jax.ShapeDtypeStruct((s, MERGED), F32), jax.ShapeDtypeStruct((s, LANES), F32)],
        scratch_shapes=[pltpu.VMEM((tm, MERGED), F32), pltpu.VMEM((tm, MERGED), F32)],
        compiler_params=_params(("parallel",)),
        name="rwkv_prep",
    )(p, p, p, p, conv_m, w0_m, wup_m, a0_m, aup_m, kk_m, ka_m, rk_m, g32, rep)


def _scan_kernel(*refs, ts):
    ins, (yf_ref, yb_ref, s_sc, y_sc, sa_sc) = refs[:12], refs[12:]
    streams = [ins[:6] + (yf_ref,), ins[6:] + (yb_ref,)]
    nkl = s_sc.shape[1]
    nvb = s_sc.shape[2] // SUBLANES

    @pl.when(pl.program_id(0) == 0)
    def _():
        s_sc[...] = jnp.zeros_like(s_sc)
        y_sc[...] = jnp.zeros_like(y_sc)

    def vsl(vb):
        return slice(vb * SUBLANES, (vb + 1) * SUBLANES)

    diag = (lax.broadcasted_iota(jnp.int32, (SUBLANES, LANES), 1) // BH
            == lax.broadcasted_iota(jnp.int32, (SUBLANES, LANES), 0) % KQ)

    def emit_y(d, y_ref, row):
        for vb in range(nvb):
            y = y_sc[d, vsl(vb), :]
            y = y + pltpu.roll(y, BH, 1)
            y = y + pltpu.roll(y, 2 * BH, 1)
            y = jnp.where(diag, y, 0.0)
            y = y + pltpu.roll(y, 1, 0)
            y = y + pltpu.roll(y, 2, 0)
            for half in range(SUBLANES // KQ):
                srow = half * KQ + KQ - 1
                y_ref[row, 2 * vb + half:2 * vb + half + 1, :] = y[srow:srow + 1, :]

    def allreduce(x):
        x = x + pltpu.roll(x, BH, 1)
        return x + pltpu.roll(x, 2 * BH, 1)

    def step(j, carry):
        ts_of = (j, ts - 1 - j)
        prev = (jnp.maximum(j - 1, 0), jnp.minimum(ts - j, ts - 1))
        for d in range(2):
            emit_y(d, streams[d][-1], prev[d])
        nxt = (jnp.minimum(j + 1, ts - 1), jnp.maximum(ts - 2 - j, 0))
        for d, (w_ref, kd_ref, b_ref, kk_ref, r_ref, v_ref, y_ref) in enumerate(streams):
            t = ts_of[d]
            sa = [sa_sc[d, vsl(vb), :] if d == 0 else allreduce(sa_sc[d, vsl(vb), :]) for vb in range(nvb)]
            ys = [None] * nvb
            sa_next = [None] * nvb
            for kl in range(nkl):
                wrow = w_ref[t, kl:kl + 1, :]
                brow = b_ref[t, kl:kl + 1, :]
                kdrow = kd_ref[t, kl:kl + 1, :]
                rrow = r_ref[t, kl:kl + 1, :]
                kknext = kk_ref[nxt[d], kl:kl + 1, :]
                for vb in range(nvb):
                    sn = s_sc[d, kl, vsl(vb), :] * wrow - sa[vb] * brow + v_ref[t, vsl(vb), :] * kdrow
                    s_sc[d, kl, vsl(vb), :] = sn
                    term = sn * rrow
                    ys[vb] = term if ys[vb] is None else ys[vb] + term
                    term = sn * kknext
                    sa_next[vb] = term if sa_next[vb] is None else sa_next[vb] + term
            for vb in range(nvb):
                y_sc[d, vsl(vb), :] = ys[vb]
                sa_sc[d, vsl(vb), :] = allreduce(sa_next[vb]) if d == 0 else sa_next[vb]
        return carry

    for d, first in enumerate((0, ts - 1)):
        kk_ref = streams[d][3]
        for vb in range(nvb):
            acc = None
            for kl in range(nkl):
                term = s_sc[d, kl, vsl(vb), :] * kk_ref[first, kl:kl + 1, :]
                acc = term if acc is None else acc + term
            sa_sc[d, vsl(vb), :] = allreduce(acc) if d == 0 else acc

    lax.fori_loop(0, ts, step, 0)
    emit_y(0, yf_ref, ts - 1)
    emit_y(1, yb_ref, 0)


def _scan(w4, kd4, b4, kk3, r3, vrep, *, t, c):
    s = t + c
    ts = SCAN_TS
    nctx = c // ts
    nlat = t // ts
    fblk = lambda g: jnp.where(g < nctx, nlat + g, g - nctx)
    bblk = lambda g: jnp.where(g < nctx, nlat + (nctx - 1 - g), nlat - 1 - (g - nctx))
    in_specs = []
    args = []
    for d, blk in enumerate((fblk, bblk)):
        for a in (w4, kd4, b4):
            in_specs.append(pl.BlockSpec((None, ts, NKL, LANES), lambda g, d=d, blk=blk: (d, blk(g), 0, 0)))
            args.append(a)
        for a in (kk3, r3):
            in_specs.append(pl.BlockSpec((ts, NKL, LANES), lambda g, blk=blk: (blk(g), 0, 0)))
            args.append(a)
        in_specs.append(pl.BlockSpec((ts, B_HEAD_DIM, LANES), lambda g, blk=blk: (blk(g), 0, 0)))
        args.append(vrep)
    nslab = MERGED // LANES
    yshape = jax.ShapeDtypeStruct((s, nslab, LANES), F32)
    return pl.pallas_call(
        functools.partial(_scan_kernel, ts=ts),
        grid=(s // ts,),
        in_specs=in_specs,
        out_specs=[pl.BlockSpec((ts, nslab, LANES), lambda g: (fblk(g), 0, 0)),
                   pl.BlockSpec((ts, nslab, LANES), lambda g: (bblk(g), 0, 0))],
        out_shape=[yshape, yshape],
        scratch_shapes=[pltpu.VMEM((2, NKL, B_HEAD_DIM, LANES), F32),
                        pltpu.VMEM((2, B_HEAD_DIM, LANES), F32),
                        pltpu.VMEM((2, B_HEAD_DIM, LANES), F32)],
        compiler_params=_params(("arbitrary",)),
        name="wkv7_scan",
    )(*args)


def _rwkv_out_kernel(yf_ref, yb_ref, v_ref, bonus_ref, pg_ref, lng_ref, lnb_ref, gup_ref,
                     selv_ref, g32_ref, o_ref):
    g32 = g32_ref[...]
    gate = jnp.dot(jax.nn.sigmoid(pg_ref[...]).astype(BF16), gup_ref[...], preferred_element_type=F32)
    bonus = _gsum(bonus_ref[...], g32)
    nslab = MERGED // LANES
    ys = []
    for g in range(nslab // SUBLANES):
        rows = slice(g * SUBLANES, (g + 1) * SUBLANES)
        both = jnp.swapaxes(yf_ref[:, rows, :] + yb_ref[:, rows, :], 0, 1)
        ys += [both[j] for j in range(SUBLANES)]
    inv = 1.0 / B_HEAD_DIM
    mu = _gsum(functools.reduce(lambda a, b_: a + b_, ys), g32) * inv
    ds = [y - mu for y in ys]
    var = _gsum(functools.reduce(lambda a, b_: a + b_, [x * x for x in ds]), g32) * inv
    rstd = lax.rsqrt(var + LN_X_EPS)
    outs = []
    for j in range(nslab):
        cs = slice(j * LANES, (j + 1) * LANES)
        yn = ds[j] * rstd * lng_ref[:, cs] + lnb_ref[:, cs]
        outs.append(((yn + bonus * v_ref[:, cs]) * gate[:, cs]).astype(BF16))
    for j in range(nslab // 2):
        pair = jnp.concatenate([outs[2 * j], outs[2 * j + 1]], axis=1)
        for bi in range(2):
            o_ref[bi, :, j * LANES:(j + 1) * LANES] = jnp.dot(
                pair, selv_ref[bi], preferred_element_type=F32).astype(o_ref.dtype)


def _rwkv_out(yf, yb, v2d, bonus, p, lng_m, lnb_m, gup_m, selv, g32, *, rows):
    tm = 256
    nslab = MERGED // LANES
    y3s = pl.BlockSpec((tm, nslab, LANES), lambda i: (i, 0, 0))
    full2 = lambda i: (0, 0)
    return pl.pallas_call(
        _rwkv_out_kernel,
        grid=(rows // tm,),
        in_specs=[
            y3s, y3s,
            pl.BlockSpec((tm, MERGED), lambda i: (i, 0)),
            pl.BlockSpec((tm, LANES), lambda i: (i, 0)),
            pl.BlockSpec((tm, 2 * GATE_LORA), lambda i: (i, 3 * MERGED // (2 * GATE_LORA))),
            pl.BlockSpec((1, MERGED), full2),
            pl.BlockSpec((1, MERGED), full2),
            pl.BlockSpec((2 * GATE_LORA, MERGED), full2),
            pl.BlockSpec((2, 2 * LANES, LANES), lambda i: (0, 0, 0)),
            pl.BlockSpec((LANES, LANES), full2),
        ],
        out_specs=pl.BlockSpec((2, tm, B_WIDTH), lambda i: (0, i, 0)),
        out_shape=jax.ShapeDtypeStruct((2, rows, B_WIDTH), BF16),
        compiler_params=_params(("parallel",)),
        name="rwkv_out",
    )(yf, yb, v2d, bonus, p, lng_m, lnb_m, gup_m, selv, g32)


def _kmerge_cols(w):
    r = w.shape[0]
    wt = w.reshape(r, B_HEADS, NKL, KQ).transpose(0, 2, 3, 1)
    z = jnp.zeros_like(wt)
    return jnp.stack([jnp.stack([wt, z], axis=3), jnp.stack([z, wt], axis=3)]).reshape(2, r, MERGED)


def _vmerge_cols(w):
    r = w.shape[0]
    wt = w.reshape(r, B_HEADS, B_HEAD_DIM).transpose(0, 2, 1)
    z = jnp.zeros_like(wt)
    return jnp.stack([jnp.stack([wt, z], axis=2), jnp.stack([z, wt], axis=2)]).reshape(2, r, MERGED)


def _both(m):
    return m[0] + m[1]


def _kperm_cols(w):
    r = w.shape[0]
    return w.reshape(r, B_HEADS, NKL, KQ).transpose(0, 2, 3, 1).reshape(r, B_WIDTH)


def _vperm_cols(w):
    r = w.shape[0]
    return w.reshape(r, B_HEADS, B_HEAD_DIM).transpose(0, 2, 1).reshape(r, B_WIDTH)


def _selectors():
    ci = jnp.arange(LANES)[None, :]
    bsel = jnp.arange(2)[:, None, None]
    c_vi, c_h = ci // B_HEADS, ci % B_HEADS
    r2 = jnp.arange(2 * LANES)[:, None]
    selv = ((r2 // BH == c_vi) & ((r2 % BH) // B_HEADS == bsel) & (r2 % B_HEADS == c_h)).astype(BF16)
    r1 = jnp.arange(LANES)[:, None]
    g32 = (r1 % BH == ci % BH).astype(BF16)
    rep = jnp.stack([((r1 // BH == vi) & (r1 % BH == ci % BH)) for vi in range(LANES // BH)]).astype(BF16)
    half = jnp.arange(2)[None, :, None, None]
    merge = ((r1 // (LANES // 2) == half) & ((r1 % (LANES // 2)) // B_HEADS == ci // BH)
             & ((ci % BH) // B_HEADS == bsel[:, None]) & (r1 % B_HEADS == c_h)).astype(BF16)
    return selv, g32, rep, merge


def _merge_kernel(x_ref, h_ref, ya_ref, yb_ref, yc_ref, mod_ref, wg_ref, bg_ref, wb_ref, wo_ref,
                  o_ref, *, nz):
    j = pl.program_id(1)
    h = h_ref[...]
    z = None
    for i, y_ref in enumerate((ya_ref, yb_ref, yc_ref)):
        gate = jax.nn.sigmoid(jnp.dot(h, wg_ref[i], preferred_element_type=F32) + bg_ref[i])
        term = gate * jnp.dot(y_ref[...], wb_ref[i], preferred_element_type=F32)
        z = term if z is None else z + term
    part = jnp.dot(z.astype(BF16), wo_ref[...], preferred_element_type=F32)

    @pl.when(j == 0)
    def _():
        o_ref[...] = part

    @pl.when(j > 0)
    def _():
        o_ref[...] += part

    @pl.when(j == nz - 1)
    def _():
        o_ref[...] = x_ref[...] + mod_ref[5:6, :] * o_ref[...]


def _merge(xs, h, ya, yb, yc, mod, wg, bg, wb, wo, mod_idx, n_tiles):
    d = xs.shape[1]
    tz = 512
    nz = d // tz
    rowt = lambda i, j: (i, 0)
    return pl.pallas_call(
        functools.partial(_merge_kernel, nz=nz),
        grid=(n_tiles, nz),
        in_specs=[
            pl.BlockSpec((TM, d), rowt),
            pl.BlockSpec((TM, d), rowt),
            pl.BlockSpec((TM, A_WIDTH), rowt),
            pl.BlockSpec((TM, B_WIDTH), rowt),
            pl.BlockSpec((TM, C_WIDTH), rowt),
            pl.BlockSpec((None, N_MOD, d), lambda i, j: (mod_idx(i), 0, 0)),
            pl.BlockSpec((N_BRANCH, d, tz), lambda i, j: (0, 0, j)),
            pl.BlockSpec((N_BRANCH, 1, tz), lambda i, j: (0, 0, j)),
            pl.BlockSpec((N_BRANCH, A_WIDTH, tz), lambda i, j: (0, 0, j)),
            pl.BlockSpec((tz, d), lambda i, j: (j, 0)),
        ],
        out_specs=pl.BlockSpec((TM, d), rowt),
        out_shape=jax.ShapeDtypeStruct((n_tiles * TM, d), F32),
        compiler_params=_params(("parallel", "arbitrary")),
        name="merge",
    )(xs, h, ya, yb, yc, mod, wg, bg, wb, wo)


def _rope_tables(b, t, c):
    rows = t // GRID_W
    rowp = jnp.repeat(jnp.arange(rows), GRID_W).astype(F32)
    colp = jnp.tile(jnp.arange(GRID_W), rows).astype(F32)
    inv = 1.0 / (ROPE_BASE ** (jnp.arange(0, ROPE_AXIS_DIM, 2, dtype=F32) / ROPE_AXIS_DIM))
    ar, ac = rowp[:, None] * inv, colp[:, None] * inv
    cr, sr, cc, sc = jnp.cos(ar), jnp.sin(ar), jnp.cos(ac), jnp.sin(ac)
    cos64 = jnp.concatenate([cr, cr, cc, cc], axis=-1)
    sin64 = jnp.concatenate([-sr, sr, -sc, sc], axis=-1)
    cos_t = jnp.tile(cos64, (b, 2))
    sin_t = jnp.tile(sin64, (b, 2))
    cos_t = jnp.concatenate([cos_t, jnp.ones((b * c, LANES), F32)])
    sin_t = jnp.concatenate([sin_t, jnp.zeros((b * c, LANES), F32)])
    return cos_t, sin_t


def kernel(x, c, ctx, c_ctx, w_ada, b_ada, norm_g, ffn_w_in, ffn_w_out, w_in, gm_v_norm, gm_ws, gm_bs,
           rw_conv, rw_w0, rw_w_up, rw_a0, rw_a_up, rw_g_up, rw_k_k, rw_k_a, rw_r_k, rw_ln_g, rw_ln_b,
           da_q_norm, da_k_norm, da_lam, da_subln, w_branch, b_gate, w_out):
    b, t, d = x.shape
    cl = ctx.shape[1]
    depth = w_ada.shape[0]
    d_ff = ffn_w_out.shape[2]
    assert b == 2 and b * cl == TM and t % TM == 0 and cl % 256 == 0
    n_lat = (b * t) // TM
    n_all = n_lat + 1
    tiles_per_batch = t // TM
    mod_idx = lambda i: jnp.where(i < n_lat, i // tiles_per_batch, b)

    xs = jnp.concatenate([x.reshape(b * t, d), ctx.reshape(b * cl, d)])
    cvec = jnp.zeros((SUBLANES, d), F32).at[:b].set(c).at[b].set(c_ctx)
    b_ada3 = b_ada.reshape(depth, 1, N_MOD * d)
    cos_t, sin_t = _rope_tables(b, t, cl)
    lane = jnp.arange(LANES)
    g64 = (lane[:, None] // C_HEAD_DIM == lane[None, :] // C_HEAD_DIM).astype(BF16)
    selv, g32, rep, merge_sel = _selectors()
    w_gu = jnp.transpose(ffn_w_in.reshape(depth, 2, d, 2, d_ff), (0, 1, 3, 2, 4)).astype(BF16)
    w_dn = ffn_w_out.astype(BF16)

    o = 0
    offs = []
    for n in (A_WIDTH, A_WIDTH, 3 * B_WIDTH, GATE_LORA, 2 * DECAY_LORA, 2 * ICL_LORA, 3 * C_WIDTH, N_BRANCH * d):
        offs.append((o, o + n))
        o += n

    for l in range(depth):
        last = l == depth - 1
        lam_init = 0.8 - 0.6 * math.exp(-0.3 * l)
        mod = _ada(cvec, w_ada, b_ada3, l)[:b + 1].reshape(b + 1, N_MOD, d)

        xs, h = _ffn(xs, mod, norm_g[l], w_gu, w_dn, l, 0, mod_idx, n_tiles=n_all, mi=0, emit_h=True)

        wl = w_in[l]
        sl = lambda i: wl[:, offs[i][0]:offs[i][1]]
        n_mix = n_lat if last else n_all

        w_uv = jnp.concatenate([sl(0), sl(1)], axis=1).astype(BF16)
        bsb = jnp.broadcast_to(gm_bs[l][:, :, None], (A_GROUPS, CHUNK, A_WIDTH // A_GROUPS))
        ya = _gmlp(h, w_uv, gm_v_norm[l].reshape(1, A_WIDTH), gm_ws[l], bsb, n_mix)

        w_rkv, w_dec, w_icl = sl(2), sl(4), sl(5)
        pad_lora = lambda w: jnp.pad(w, ((0, 0), (0, LORA_PAD - w.shape[1])))
        w_b = jnp.concatenate(
            [_kperm_cols(w_rkv[:, :B_WIDTH]), _kperm_cols(w_rkv[:, B_WIDTH:2 * B_WIDTH]),
             _vperm_cols(w_rkv[:, 2 * B_WIDTH:]), sl(3),
             pad_lora(w_dec[:, :DECAY_LORA]), pad_lora(w_dec[:, DECAY_LORA:]),
             pad_lora(w_icl[:, :ICL_LORA]), pad_lora(w_icl[:, ICL_LORA:])], axis=1).astype(BF16)
        p = _bmerge(_rkv_proj(h, w_b), merge_sel, t=t, c=cl)
        conv = rw_conv[l]
        conv_m = jnp.concatenate([_both(_kmerge_cols(conv[:, :B_WIDTH])),
                                  _both(_kmerge_cols(conv[:, B_WIDTH:2 * B_WIDTH])),
                                  _both(_vmerge_cols(conv[:, 2 * B_WIDTH:]))], axis=1)
        def up_m(w):
            w32 = jnp.stack([
                _kmerge_cols(jnp.pad(w[dd], ((0, LORA_PAD - w.shape[1]), (0, 0)))).reshape(2 * LORA_PAD, MERGED)
                for dd in range(2)])
            hi = w32.astype(BF16)
            return jnp.stack([hi, (w32 - hi.astype(F32)).astype(BF16)])
        r3, kk3, w4, kd4, b4, vrep, v2d, bonus = _rwkv_prep(
            p, conv_m, _both(_kmerge_cols(rw_w0[l])), up_m(rw_w_up[l]), _both(_kmerge_cols(rw_a0[l])),
            up_m(rw_a_up[l]), _both(_kmerge_cols(rw_k_k[l].reshape(1, B_WIDTH))),
            _both(_kmerge_cols(rw_k_a[l].reshape(1, B_WIDTH))),
            _both(_kmerge_cols(rw_r_k[l].reshape(1, B_WIDTH))), g32, rep, t=t, c=cl)
        yf, ybk = _scan(w4, kd4, b4, kk3, r3, vrep, t=t, c=cl)
        yb2 = _rwkv_out(yf, ybk, v2d, bonus, p, _both(_vmerge_cols(rw_ln_g[l].reshape(1, B_WIDTH))),
                        _both(_vmerge_cols(rw_ln_b[l].reshape(1, B_WIDTH))),
                        _vmerge_cols(rw_g_up[l]).reshape(2 * GATE_LORA, MERGED).astype(BF16),
                        selv, g32, rows=t if last else t + cl)
        if last:
            yb = yb2.reshape(b * t, B_WIDTH)
        else:
            yb = jnp.concatenate([yb2[0, :t], yb2[1, :t], yb2[0, t:], yb2[1, t:]])

        gains = jnp.stack([jnp.tile(da_q_norm[l], 2) * (C_HEAD_DIM ** -0.5 * math.log2(math.e)),
                           jnp.tile(da_k_norm[l], 2), jnp.ones((LANES,), F32)]).reshape(3, 1, LANES)
        qk, vx = _qkv(h, sl(6).astype(BF16), gains, cos_t, sin_t, g64)
        sub = da_subln[l].reshape(1, LANES)
        yc = _attn(qk, vx, da_lam[l], sub, None, b=b, t=t, c=cl, lam_init=lam_init, ctx_queries=False)
        if not last:
            yc = _attn(qk, vx, da_lam[l], sub, yc, b=b, t=t, c=cl, lam_init=lam_init, ctx_queries=True)

        wg = jnp.transpose(sl(7).reshape(d, N_BRANCH, d), (1, 0, 2)).astype(BF16)
        wb = w_branch[l]
        wb_b = wb[1].reshape(B_HEADS, B_HEAD_DIM, d).transpose(1, 0, 2).reshape(B_WIDTH, d)
        wbr = jnp.stack([wb[0], wb_b, wb[2]]).astype(BF16)
        xs = _merge(xs, h, ya, yb, yc, mod, wg, b_gate[l].reshape(N_BRANCH, 1, d),
                    wbr, w_out[l].astype(BF16), mod_idx, n_mix)

        xs = _ffn(xs, mod, norm_g[l], w_gu, w_dn, l, 1, mod_idx, n_tiles=n_mix, mi=2, emit_h=False)

    return xs[:b * t].reshape(b, t, d)
```

```python
import functools
import math

import jax
import jax.numpy as jnp
from jax import lax
from jax.experimental import pallas as pl
from jax.experimental.pallas import tpu as pltpu

F32 = jnp.float32
BF16 = jnp.bfloat16

N_MOD = 9
CHUNK = 128
A_WIDTH = 1024
A_GROUPS = 8
B_WIDTH = 1024
B_HEAD_DIM = 64
B_HEADS = 16
DECAY_LORA = 96
ICL_LORA = 96
GATE_LORA = 256
C_HEADS = 8
C_HEAD_DIM = 64
C_WIDTH = 1024
N_BRANCH = 3
GRID_W = 64
ROPE_BASE = 10000.0
ROPE_AXIS_DIM = 32
NORM_EPS = 1e-6
LN_X_EPS = 64e-5

LANES = 128
SUBLANES = 8
VMEM_LIMIT = 56 * 1024 * 1024

TM = 512
TF = 512
LORA_PAD = 128
SCAN_TS = 64
ATT_TQ = 512
ATT_KC = 512


def _params(sem):
    return pltpu.CompilerParams(dimension_semantics=sem, vmem_limit_bytes=VMEM_LIMIT)


def _rms(x, eps):
    return x * lax.rsqrt(jnp.mean(x * x, axis=-1, keepdims=True) + eps)


def _split3(s):
    hi = s.astype(BF16)
    r = s - hi.astype(F32)
    mid = r.astype(BF16)
    lo = (r - mid.astype(F32)).astype(BF16)
    return hi, mid, lo


def _gsum(s, g_bf16):
    out = None
    for piece in _split3(s):
        d = jnp.dot(piece, g_bf16, preferred_element_type=F32)
        out = d if out is None else out + d
    return out


def _gelu_tanh(x):
    cdf = 0.5 * (1.0 + jnp.tanh(math.sqrt(2.0 / math.pi) * (x + 0.044715 * (x * x * x))))
    return x * cdf


def _ada_kernel(c_ref, w_ref, b_ref, o_ref):
    s = c_ref[...]
    s = s * jax.nn.sigmoid(s)
    w = w_ref[...]
    w_hi = w.astype(BF16)
    w_lo = (w - w_hi.astype(F32)).astype(BF16)
    s3 = _split3(s)
    acc = b_ref[...]
    for piece in s3:
        acc = acc + jnp.dot(piece, w_hi, preferred_element_type=F32)
    for piece in s3[:2]:
        acc = acc + jnp.dot(piece, w_lo, preferred_element_type=F32)
    o_ref[...] = acc


def _ada(cvec, w_ada, b_ada3, l):
    d = cvec.shape[1]
    n = w_ada.shape[2]
    tn = 1024
    return pl.pallas_call(
        _ada_kernel,
        grid=(n // tn,),
        in_specs=[
            pl.BlockSpec((SUBLANES, d), lambda j: (0, 0)),
            pl.BlockSpec((None, d, tn), lambda j: (l, 0, j)),
            pl.BlockSpec((None, 1, tn), lambda j: (l, 0, j)),
        ],
        out_specs=pl.BlockSpec((SUBLANES, tn), lambda j: (0, j)),
        out_shape=jax.ShapeDtypeStruct((SUBLANES, n), F32),
        compiler_params=_params(("arbitrary",)),
        name="ada",
    )(cvec, w_ada, b_ada3)


def _ffn_kernel(x_ref, mod_ref, g_ref, win_ref, wout_ref, *rest, tf, nf, last_valid, mi, emit_h):
    if emit_h:
        o_ref, h_ref, hn_sc, acc_sc = rest
    else:
        o_ref, hn_sc, acc_sc = rest
    f = pl.program_id(1)

    @pl.when(f == 0)
    def _():
        xn = _rms(x_ref[...], NORM_EPS) * g_ref[mi:mi + 1, :]
        hn = xn * (1.0 + mod_ref[3 * mi + 1:3 * mi + 2, :]) + mod_ref[3 * mi:3 * mi + 1, :]
        hn_sc[...] = hn.astype(BF16)
        acc_sc[...] = jnp.zeros_like(acc_sc)

    def hidden_tile(valid):
        hn = hn_sc[...]
        g = jnp.dot(hn, win_ref[0, :, :valid], preferred_element_type=F32)
        u = jnp.dot(hn, win_ref[1, :, :valid], preferred_element_type=F32)
        act = (g * jax.nn.sigmoid(g) * u).astype(BF16)
        acc_sc[...] += jnp.dot(act, wout_ref[:valid, :], preferred_element_type=F32)

    if last_valid == tf:
        hidden_tile(tf)
    else:
        pl.when(f < nf - 1)(lambda: hidden_tile(tf))
        pl.when(f == nf - 1)(lambda: hidden_tile(last_valid))

    @pl.when(f == nf - 1)
    def _():
        out = x_ref[...] + 0.5 * mod_ref[3 * mi + 2:3 * mi + 3, :] * acc_sc[...]
        o_ref[...] = out
        if emit_h:
            hn = _rms(out, NORM_EPS) * g_ref[1:2, :]
            h_ref[...] = (hn * (1.0 + mod_ref[4:5, :]) + mod_ref[3:4, :]).astype(BF16)


def _ffn(xs, mod, norm_g, w_gu, w_out, l, w, mod_idx, *, n_tiles, mi, emit_h):
    m, d = xs.shape
    d_ff = w_out.shape[2]
    assert d_ff % LANES == 0
    nf = pl.cdiv(d_ff, TF)
    out_rows = n_tiles * TM
    out_shape = [jax.ShapeDtypeStruct((out_rows, d), F32)]
    out_specs = [pl.BlockSpec((TM, d), lambda i, f: (i, 0))]
    if emit_h:
        out_shape.append(jax.ShapeDtypeStruct((out_rows, d), BF16))
        out_specs.append(pl.BlockSpec((TM, d), lambda i, f: (i, 0)))
    res = pl.pallas_call(
        functools.partial(_ffn_kernel, tf=TF, nf=nf, last_valid=d_ff - (nf - 1) * TF, mi=mi, emit_h=emit_h),
        grid=(n_tiles, nf),
        in_specs=[
            pl.BlockSpec((TM, d), lambda i, f: (i, 0)),
            pl.BlockSpec((None, N_MOD, d), lambda i, f: (mod_idx(i), 0, 0)),
            pl.BlockSpec((3, d), lambda i, f: (0, 0)),
            pl.BlockSpec((None, None, 2, d, TF), lambda i, f: (l, w, 0, 0, f)),
            pl.BlockSpec((None, None, TF, d), lambda i, f: (l, w, f, 0)),
        ],
        out_specs=out_specs,
        out_shape=out_shape,
        scratch_shapes=[pltpu.VMEM((TM, d), BF16), pltpu.VMEM((TM, d), F32)],
        compiler_params=_params(("parallel", "arbitrary")),
        name="ffn",
    )(xs, mod, norm_g, w_gu, w_out)
    return res if emit_h else res[0]


def _gmlp_kernel(h_ref, w_ref, vn_ref, ws_ref, bsb_ref, o_ref, p_sc, *, tm):
    p_sc[...] = jnp.dot(h_ref[...], w_ref[...], preferred_element_type=F32)
    gd = A_WIDTH // A_GROUPS
    for g in range(A_GROUPS):
        wsg = ws_ref[g].astype(BF16)
        bias = bsb_ref[g]
        gain = vn_ref[:, g * gd:(g + 1) * gd]
        for n in range(tm // CHUNK):
            rows = slice(n * CHUNK, (n + 1) * CHUNK)
            u = _gelu_tanh(p_sc[rows, g * gd:(g + 1) * gd])
            v = _gelu_tanh(p_sc[rows, A_WIDTH + g * gd:A_WIDTH + (g + 1) * gd])
            v = _rms(v, NORM_EPS) * gain
            sv = jnp.dot(wsg, v.astype(BF16), preferred_element_type=F32) + bias
            o_ref[rows, g * gd:(g + 1) * gd] = (u * sv).astype(o_ref.dtype)


def _gmlp(h, w_uv, vn, ws, bsb, n_tiles):
    m, d = h.shape
    return pl.pallas_call(
        functools.partial(_gmlp_kernel, tm=TM),
        grid=(n_tiles,),
        in_specs=[
            pl.BlockSpec((TM, d), lambda i: (i, 0)),
            pl.BlockSpec((d, 2 * A_WIDTH), lambda i: (0, 0)),
            pl.BlockSpec((1, A_WIDTH), lambda i: (0, 0)),
            pl.BlockSpec((A_GROUPS, CHUNK, CHUNK), lambda i: (0, 0, 0)),
            pl.BlockSpec((A_GROUPS, CHUNK, A_WIDTH // A_GROUPS), lambda i: (0, 0, 0)),
        ],
        out_specs=pl.BlockSpec((TM, A_WIDTH), lambda i: (i, 0)),
        out_shape=jax.ShapeDtypeStruct((n_tiles * TM, A_WIDTH), BF16),
        scratch_shapes=[pltpu.VMEM((TM, 2 * A_WIDTH), F32)],
        compiler_params=_params(("parallel",)),
        name="gmlp",
    )(h, w_uv, vn, ws, bsb)


def _qkv_kernel(h_ref, w_ref, gain_ref, cos_ref, sin_ref, g64_ref, o_ref, vx_ref, p_sc):
    j = pl.program_id(1)
    p_sc[...] = jnp.dot(h_ref[...], w_ref[...], preferred_element_type=F32)

    @pl.when(j == 2)
    def _():
        ones = jnp.ones((p_sc.shape[0], LANES), BF16)
        for hs in range(C_HEADS):
            vx_ref[:, 2 * hs * LANES:(2 * hs + 1) * LANES] = p_sc[:, hs * LANES:(hs + 1) * LANES].astype(BF16)
            vx_ref[:, (2 * hs + 1) * LANES:(2 * hs + 2) * LANES] = ones

    @pl.when(j < 2)
    def _():
        lane = lax.broadcasted_iota(jnp.int32, (1, LANES), 1)
        first = (lane % ROPE_AXIS_DIM) < (ROPE_AXIS_DIM // 2)
        cos = cos_ref[...]
        sin = sin_ref[...]
        gain = gain_ref[...]
        g64 = g64_ref[...]
        for hs in range(C_HEADS):
            x = p_sc[:, hs * LANES:(hs + 1) * LANES]
            ms = _gsum(x * x, g64) * (1.0 / C_HEAD_DIM)
            y = x * lax.rsqrt(ms + NORM_EPS) * gain
            half = ROPE_AXIS_DIM // 2
            rot = jnp.where(first, pltpu.roll(y, LANES - half, 1), pltpu.roll(y, half, 1))
            o_ref[:, hs * LANES:(hs + 1) * LANES] = (y * cos + rot * sin).astype(BF16)


def _qkv(h, w_qkv, gains, cos_t, sin_t, g64):
    m, d = h.shape
    return pl.pallas_call(
        _qkv_kernel,
        grid=(m // TM, 3),
        in_specs=[
            pl.BlockSpec((TM, d), lambda i, j: (i, 0)),
            pl.BlockSpec((d, C_WIDTH), lambda i, j: (0, j)),
            pl.BlockSpec((None, 1, LANES), lambda i, j: (j, 0, 0)),
            pl.BlockSpec((TM, LANES), lambda i, j: (i, 0)),
            pl.BlockSpec((TM, LANES), lambda i, j: (i, 0)),
            pl.BlockSpec((LANES, LANES), lambda i, j: (0, 0)),
        ],
        out_specs=[pl.BlockSpec((None, TM, C_WIDTH), lambda i, j: (jnp.minimum(j, 1), i, 0)),
                   pl.BlockSpec((TM, 2 * C_WIDTH), lambda i, j: (i, 0))],
        out_shape=[jax.ShapeDtypeStruct((2, m, C_WIDTH), BF16), jax.ShapeDtypeStruct((m, 2 * C_WIDTH), BF16)],
        scratch_shapes=[pltpu.VMEM((TM, C_WIDTH), F32)],
        compiler_params=_params(("parallel", "arbitrary")),
        name="qkv",
    )(h, w_qkv, gains, cos_t, sin_t, g64)


def _attn_kernel(lam_ref, sub_ref, q_ref, *refs, nseg, lam_init):
    kv = refs[:2 * nseg]
    o_ref = refs[2 * nseg]
    lv = lam_ref[...]
    lam = (jnp.exp(jnp.sum(lv[0:1] * lv[1:2], axis=-1, keepdims=True))
           - jnp.exp(jnp.sum(lv[2:3] * lv[3:4], axis=-1, keepdims=True)) + lam_init)
    q = q_ref[...]
    lane = lax.broadcasted_iota(jnp.int32, (1, LANES), 1)
    zero = jnp.zeros_like(q)
    outs = []
    for j in range(2):
        in_map = (lane < C_HEAD_DIM) if j == 0 else (lane >= C_HEAD_DIM)
        qj = jnp.where(in_map, q, zero)
        m = None
        acc = None
        for si in range(nseg):
            k_ref, v_ref = kv[2 * si], kv[2 * si + 1]
            ls = k_ref.shape[0]
            for c0 in range(0, ls, ATT_KC):
                kc = min(ATT_KC, ls - c0)
                s = lax.dot_general(qj, k_ref[c0:c0 + kc, :], (((1,), (1,)), ((), ())),
                                    preferred_element_type=F32)
                cm = jnp.max(s, axis=-1, keepdims=True)
                m_new = cm if m is None else jnp.maximum(m, cm)
                e = jnp.exp2((s - m_new).astype(BF16))
                pv = jnp.dot(e, v_ref[c0:c0 + kc, :], preferred_element_type=F32)
                acc = pv if acc is None else acc * jnp.exp2(m - m_new) + pv
                m = m_new
        outs.append(acc[:, :LANES] / acc[:, LANES:])
    o = outs[0] - lam * outs[1]
    o = _rms(o, NORM_EPS) * sub_ref[...]
    o_ref[...] = (o * (1.0 - lam_init)).astype(o_ref.dtype)


def _attn(qk, vx, lam, subln, yc_prev, *, b, t, c, lam_init, ctx_queries):
    m = qk.shape[1]
    if ctx_queries:
        tq, nq = c, 1
        q_blk = lambda bi, qi: (b * t) // c + bi
        segs = [(c, lambda bi: (b * t) // c + bi)]
    else:
        tq = ATT_TQ
        nq = t // tq
        q_blk = lambda bi, qi: bi * nq + qi
        segs = [(t, lambda bi: bi), (c, lambda bi: (b * t) // c + bi)]
    in_specs = [
        pl.BlockSpec((4, C_HEAD_DIM), lambda bi, h, qi: (0, 0)),
        pl.BlockSpec((1, LANES), lambda bi, h, qi: (0, 0)),
        pl.BlockSpec((None, tq, LANES), lambda bi, h, qi: (0, q_blk(bi, qi), h)),
    ]
    args = [lam, subln, qk]
    for ls, blk in segs:
        in_specs.append(pl.BlockSpec((None, ls, LANES), lambda bi, h, qi, blk=blk: (1, blk(bi), h)))
        in_specs.append(pl.BlockSpec((ls, 2 * LANES), lambda bi, h, qi, blk=blk: (blk(bi), h)))
        args += [qk, vx]
    aliases = {}
    if yc_prev is not None:
        in_specs.append(pl.BlockSpec(memory_space=pl.ANY))
        args.append(yc_prev)
        aliases = {len(args) - 1: 0}

    def body(*refs):
        n_in = 3 + 2 * len(segs)
        _attn_kernel(*refs[:n_in], refs[-1], nseg=len(segs), lam_init=lam_init)

    return pl.pallas_call(
        body,
        grid=(b, C_HEADS, nq),
        in_specs=in_specs,
        out_specs=pl.BlockSpec((tq, LANES), lambda bi, h, qi: (q_blk(bi, qi), h)),
        out_shape=jax.ShapeDtypeStruct((m, C_WIDTH), BF16),
        input_output_aliases=aliases,
        compiler_params=_params(("parallel", "parallel", "arbitrary")),
        name="diff_attn_ctx" if ctx_queries else "diff_attn",
    )(*args)


KQ = 4
NKL = B_HEAD_DIM // KQ
MERGED = 2 * B_WIDTH
BH = 2 * B_HEADS
LORA_MERGED = 2 * GATE_LORA + 8 * LORA_PAD
PROJ_MERGED = 3 * MERGED + LORA_MERGED


PROJ_UNMERGED = 3 * B_WIDTH + GATE_LORA + 4 * LORA_PAD


def _mm_kernel(a_ref, w_ref, o_ref):
    o_ref[...] = jnp.dot(a_ref[...], w_ref[...], preferred_element_type=F32)


def _rkv_proj(h, w):
    m, d = h.shape
    tn = PROJ_UNMERGED // 3
    return pl.pallas_call(
        _mm_kernel,
        grid=(PROJ_UNMERGED // tn, m // TM),
        in_specs=[pl.BlockSpec((TM, d), lambda j, i: (i, 0)),
                  pl.BlockSpec((d, tn), lambda j, i: (0, j))],
        out_specs=pl.BlockSpec((TM, tn), lambda j, i: (i, j)),
        out_shape=jax.ShapeDtypeStruct((m, PROJ_UNMERGED), F32),
        compiler_params=_params(("parallel", "arbitrary")),
        name="rkv_proj",
    )(h, w)


def _bmerge_kernel(p0_ref, p1_ref, e_ref, o_ref):
    n_kv = 3 * B_WIDTH // LANES
    for bi, p_ref in enumerate((p0_ref, p1_ref)):
        for s in range(n_kv):
            x = p_ref[:, s * LANES:(s + 1) * LANES]
            hi = x.astype(BF16)
            lo = (x - hi.astype(F32)).astype(BF16)
            for half in range(2):
                e = e_ref[bi, half]
                part = (jnp.dot(hi, e, preferred_element_type=F32) + jnp.dot(lo, e, preferred_element_type=F32))
                cs = slice((2 * s + half) * LANES, (2 * s + half + 1) * LANES)
                if bi == 0:
                    o_ref[:, cs] = part
                else:
                    o_ref[:, cs] += part
        src = 3 * B_WIDTH
        dst = 3 * MERGED
        for width in (GATE_LORA,) + (LORA_PAD,) * 4:
            o_ref[:, dst + bi * width:dst + (bi + 1) * width] = p_ref[:, src:src + width]
            src += width
            dst += 2 * width


def _bmerge(p, e, *, t, c):
    tm = 256
    s = t + c
    nlat = t // tm
    hblk = lambda bi, i: jnp.where(i < nlat, bi * nlat + i, (2 * t + bi * c) // tm + (i - nlat))
    return pl.pallas_call(
        _bmerge_kernel,
        grid=(s // tm,),
        in_specs=[
            pl.BlockSpec((tm, PROJ_UNMERGED), lambda i: (hblk(0, i), 0)),
            pl.BlockSpec((tm, PROJ_UNMERGED), lambda i: (hblk(1, i), 0)),
            pl.BlockSpec((2, 2, LANES, LANES), lambda i: (0, 0, 0, 0)),
        ],
        out_specs=pl.BlockSpec((tm, PROJ_MERGED), lambda i: (i, 0)),
        out_shape=jax.ShapeDtypeStruct((s, PROJ_MERGED), F32),
        compiler_params=_params(("parallel",)),
        name="batch_merge",
    )(p, p, e)


class _SlabWriter:
    def __init__(self, ref, lead=()):
        self.ref, self.lead, self.parts, self.base = ref, lead, [], 0

    def add(self, slab):
        self.parts.append(slab)
        if len(self.parts) == SUBLANES:
            block = jnp.swapaxes(jnp.stack(self.parts, axis=0), 0, 1)
            self.ref[self.lead + (slice(None), slice(self.base, self.base + SUBLANES), slice(None))] = block
            self.parts, self.base = [], self.base + SUBLANES


def _rwkv_prep_kernel(p_ref, prev_ref, next_ref, lo_ref, conv_ref, w0_ref, wup_ref, a0_ref, aup_ref,
                      kkg_ref, ka_ref, rk_ref, g32_ref, rep_ref,
                      r_o, kk_o, kd_o, b_o, ptot_o, vrep_o, v_o, bonus_o, k_sc, r_sc, *, tm, ts, starts, ends):
    i = pl.program_id(0)
    is_start = functools.reduce(jnp.logical_or, [i == s for s in starts])
    is_end = functools.reduce(jnp.logical_or, [i == s for s in ends])
    row = lax.broadcasted_iota(jnp.int32, (tm, 1), 0)

    def conv(c0):
        cs = slice(c0, c0 + LANES)
        x = p_ref[:, cs]
        pm = jnp.where(is_start, 0.0, prev_ref[SUBLANES - 1:SUBLANES, cs])
        nx = jnp.where(is_end, 0.0, next_ref[0:1, cs])
        xm1 = jnp.where(row == 0, pm, pltpu.roll(x, 1, 0))
        xp1 = jnp.where(row == tm - 1, nx, pltpu.roll(x, tm - 1, 0))
        return xm1 * conv_ref[0:1, cs] + x * conv_ref[1:2, cs] + xp1 * conv_ref[2:3, cs]

    vrep_out = _SlabWriter(vrep_o)
    r_out, kk_out, kd_out, b_out = ([_SlabWriter(o, (d,)) for d in range(2)] for o in (r_o, kk_o, kd_o, b_o))

    ss = None
    for m in range(NKL):
        cs = slice(m * LANES, (m + 1) * LANES)
        r_sc[:, cs] = conv(m * LANES)
        k = conv(MERGED + m * LANES)
        k_sc[:, cs] = k
        kkr = k * kkg_ref[:, cs]
        ss = kkr * kkr if ss is None else ss + kkr * kkr
    rs = lax.rsqrt(_gsum(ss, g32_ref[...]) + 1e-12)

    w_off = 2 * GATE_LORA
    a_off = w_off + 4 * LORA_PAD
    def split2(x):
        hi = x.astype(BF16)
        return hi, (x - hi.astype(F32)).astype(BF16)

    def lora(x2, up_ref, d, cs):
        return (jnp.dot(x2[0], up_ref[0, d, :, cs], preferred_element_type=F32)
                + jnp.dot(x2[1], up_ref[0, d, :, cs], preferred_element_type=F32)
                + jnp.dot(x2[0], up_ref[1, d, :, cs], preferred_element_type=F32))

    tw = [split2(jnp.tanh(lo_ref[:, w_off + 2 * d * LORA_PAD:w_off + 2 * (d + 1) * LORA_PAD])) for d in range(2)]
    pa = [split2(lo_ref[:, a_off + 2 * d * LORA_PAD:a_off + 2 * (d + 1) * LORA_PAD]) for d in range(2)]
    bonus = None
    for m in range(NKL):
        cs = slice(m * LANES, (m + 1) * LANES)
        k = k_sc[:, cs]
        kk = k * kkg_ref[:, cs] * rs
        r = r_sc[:, cs]
        kd_sum = None
        for d in range(2):
            w_lo = w0_ref[d:d + 1, cs] + lora(tw[d], wup_ref, d, cs)
            lw = -jax.nn.sigmoid(w_lo) * math.exp(-0.5)
            cum = lw
            shift = 1
            while shift < ts:
                if d == 0:
                    cum = cum + jnp.where(row % ts >= shift, pltpu.roll(cum, shift, 0), 0.0)
                else:
                    cum = cum + jnp.where(row % ts < ts - shift, pltpu.roll(cum, tm - shift, 0), 0.0)
                shift *= 2
            p_incl = jnp.exp(cum)
            p_inv = jnp.exp(-cum)
            p_excl = jnp.exp(cum - lw)
            a = jax.nn.sigmoid(a0_ref[d:d + 1, cs] + lora(pa[d], aup_ref, d, cs))
            kd = k * (1.0 + (a - 1.0) * ka_ref[:, cs])
            kd_sum = kd if kd_sum is None else kd_sum + kd
            r_out[d].add(r * p_incl)
            kk_out[d].add(kk * p_excl)
            kd_out[d].add(kd * p_inv)
            b_out[d].add(kk * a * p_inv)
            for g in range(tm // ts):
                last = g * ts + (ts - 1 if d == 0 else 0)
                ptot_o[d, g, m:m + 1, :] = p_incl[last:last + 1, :]
        term = r * (0.5 * kd_sum) * rk_ref[:, cs]
        bonus = term if bonus is None else bonus + term
    bonus_o[...] = bonus

    per_slab = LANES // BH
    for j in range(MERGED // LANES):
        vs = conv(2 * MERGED + j * LANES)
        v_o[:, j * LANES:(j + 1) * LANES] = vs
        for vi in range(per_slab):
            vrep_out.add(_gsum(vs, rep_ref[vi]))


def _rwkv_prep(p, conv_m, w0_m, wup_m, a0_m, aup_m, kk_m, ka_m, rk_m, g32, rep, *, t, c):
    s = t + c
    tm = 128
    nt = s // tm
    starts = [0, t // tm]
    ends = [t // tm - 1, nt - 1]
    hb = tm // SUBLANES
    rkv_w = 3 * MERGED
    full2 = lambda i: (0, 0)
    full3 = lambda i: (0, 0, 0)
    ts = SCAN_TS
    k4 = jax.ShapeDtypeStruct((2, s, NKL, LANES), F32)
    k4s = pl.BlockSpec((2, tm, NKL, LANES), lambda i: (0, i, 0, 0))
    return pl.pallas_call(
        functools.partial(_rwkv_prep_kernel, tm=tm, ts=ts, starts=starts, ends=ends),
        grid=(nt,),
        in_specs=[
            pl.BlockSpec((tm, rkv_w), lambda i: (i, 0)),
            pl.BlockSpec((SUBLANES, rkv_w), lambda i: (jnp.maximum(i * hb - 1, 0), 0)),
            pl.BlockSpec((SUBLANES, rkv_w), lambda i: (jnp.minimum((i + 1) * hb, s // SUBLANES - 1), 0)),
            pl.BlockSpec((tm, LORA_MERGED), lambda i: (i, rkv_w // LORA_MERGED)),
            pl.BlockSpec((3, rkv_w), full2),
            pl.BlockSpec((2, MERGED), full2),
            pl.BlockSpec((2, 2, 2 * LORA_PAD, MERGED), lambda i: (0, 0, 0, 0)),
            pl.BlockSpec((2, MERGED), full2),
            pl.BlockSpec((2, 2, 2 * LORA_PAD, MERGED), lambda i: (0, 0, 0, 0)),
            pl.BlockSpec((1, MERGED), full2),
            pl.BlockSpec((1, MERGED), full2),
            pl.BlockSpec((1, MERGED), full2),
            pl.BlockSpec((LANES, LANES), full2),
            pl.BlockSpec((LANES // BH, LANES, LANES), full3),
        ],
        out_specs=[k4s, k4s, k4s, k4s,
                   pl.BlockSpec((2, tm // ts, NKL, LANES), lambda i: (0, i, 0, 0)),
                   pl.BlockSpec((tm, B_HEAD_DIM, LANES), lambda i: (i, 0, 0)),
                   pl.BlockSpec((tm, MERGED), lambda i: (i, 0)),
                   pl.BlockSpec((tm, LANES), lambda i: (i, 0))],
        out_shape=[k4, k4, k4, k4, jax.ShapeDtypeStruct((2, s // ts, NKL, LANES), F32),
                   jax.ShapeDtypeStruct((s, B_HEAD_DIM, LANES), F32),
                   jax.ShapeDtypeStruct((s, MERGED), F32), jax.ShapeDtypeStruct((s, LANES), F32)],
        scratch_shapes=[pltpu.VMEM((tm, MERGED), F32), pltpu.VMEM((tm, MERGED), F32)],
        compiler_params=_params(("parallel",)),
        name="rwkv_prep",
    )(p, p, p, p, conv_m, w0_m, wup_m, a0_m, aup_m, kk_m, ka_m, rk_m, g32, rep)


def _scan_kernel(*refs, ts):
    ins, (yf_ref, yb_ref, s_sc, y_sc, sa_sc) = refs[:12], refs[12:]
    streams = [ins[:6] + (yf_ref,), ins[6:] + (yb_ref,)]
    nkl = s_sc.shape[1]
    nvb = s_sc.shape[2] // SUBLANES

    @pl.when(pl.program_id(0) == 0)
    def _():
        s_sc[...] = jnp.zeros_like(s_sc)
        y_sc[...] = jnp.zeros_like(y_sc)

    def vsl(vb):
        return slice(vb * SUBLANES, (vb + 1) * SUBLANES)

    diag = (lax.broadcasted_iota(jnp.int32, (SUBLANES, LANES), 1) // BH
            == lax.broadcasted_iota(jnp.int32, (SUBLANES, LANES), 0) % KQ)

    def emit_y(d, y_ref, row):
        for vb in range(nvb):
            y = y_sc[d, vsl(vb), :]
            y = y + pltpu.roll(y, BH, 1)
            y = y + pltpu.roll(y, 2 * BH, 1)
            y = jnp.where(diag, y, 0.0)
            y = y + pltpu.roll(y, 1, 0)
            y = y + pltpu.roll(y, 2, 0)
            for half in range(SUBLANES // KQ):
                srow = half * KQ + KQ - 1
                y_ref[row, 2 * vb + half:2 * vb + half + 1, :] = y[srow:srow + 1, :]

    def allreduce(x):
        x = x + pltpu.roll(x, BH, 1)
        return x + pltpu.roll(x, 2 * BH, 1)

    def step(j, carry):
        ts_of = (j, ts - 1 - j)
        prev = (jnp.maximum(j - 1, 0), jnp.minimum(ts - j, ts - 1))
        for d in range(2):
            emit_y(d, streams[d][-1], prev[d])
        nxt = (jnp.minimum(j + 1, ts - 1), jnp.maximum(ts - 2 - j, 0))
        for d, (kd_ref, b_ref, kk_ref, r_ref, v_ref, ptot_ref, y_ref) in enumerate(streams):
            t = ts_of[d]
            sa = [sa_sc[d, vsl(vb), :] if d == 0 else allreduce(sa_sc[d, vsl(vb), :]) for vb in range(nvb)]
            ys = [None] * nvb
            sa_next = [None] * nvb
            for kl in range(nkl):
                brow = b_ref[t, kl:kl + 1, :]
                kdrow = kd_ref[t, kl:kl + 1, :]
                rrow = r_ref[t, kl:kl + 1, :]
                kknext = kk_ref[nxt[d], kl:kl + 1, :]
                for vb in range(nvb):
                    sn = s_sc[d, kl, vsl(vb), :] - sa[vb] * brow + v_ref[t, vsl(vb), :] * kdrow
                    s_sc[d, kl, vsl(vb), :] = sn
                    term = sn * rrow
                    ys[vb] = term if ys[vb] is None else ys[vb] + term
                    term = sn * kknext
                    sa_next[vb] = term if sa_next[vb] is None else sa_next[vb] + term
            for vb in range(nvb):
                y_sc[d, vsl(vb), :] = ys[vb]
                sa_sc[d, vsl(vb), :] = allreduce(sa_next[vb]) if d == 0 else sa_next[vb]
        return carry

    for d, first in enumerate((0, ts - 1)):
        kk_ref = streams[d][2]
        for vb in range(nvb):
            acc = None
            for kl in range(nkl):
                term = s_sc[d, kl, vsl(vb), :] * kk_ref[first, kl:kl + 1, :]
                acc = term if acc is None else acc + term
            sa_sc[d, vsl(vb), :] = allreduce(acc) if d == 0 else acc

    lax.fori_loop(0, ts, step, 0)
    emit_y(0, yf_ref, ts - 1)
    emit_y(1, yb_ref, 0)
    for d in range(2):
        ptot_ref = streams[d][5]
        for kl in range(nkl):
            prow = ptot_ref[0, kl:kl + 1, :]
            for vb in range(nvb):
                s_sc[d, kl, vsl(vb), :] = s_sc[d, kl, vsl(vb), :] * prow


def _scan(kd4, b4, kk4, r4, vrep, ptot, *, t, c):
    s = t + c
    ts = SCAN_TS
    nctx = c // ts
    nlat = t // ts
    fblk = lambda g: jnp.where(g < nctx, nlat + g, g - nctx)
    bblk = lambda g: jnp.where(g < nctx, nlat + (nctx - 1 - g), nlat - 1 - (g - nctx))
    in_specs = []
    args = []
    for d, blk in enumerate((fblk, bblk)):
        for a in (kd4, b4, kk4, r4):
            in_specs.append(pl.BlockSpec((None, ts, NKL, LANES), lambda g, d=d, blk=blk: (d, blk(g), 0, 0)))
            args.append(a)
        in_specs.append(pl.BlockSpec((ts, B_HEAD_DIM, LANES), lambda g, blk=blk: (blk(g), 0, 0)))
        args.append(vrep)
        in_specs.append(pl.BlockSpec((None, 1, NKL, LANES), lambda g, d=d, blk=blk: (d, blk(g), 0, 0)))
        args.append(ptot)
    nslab = MERGED // LANES
    yshape = jax.ShapeDtypeStruct((s, nslab, LANES), F32)
    return pl.pallas_call(
        functools.partial(_scan_kernel, ts=ts),
        grid=(s // ts,),
        in_specs=in_specs,
        out_specs=[pl.BlockSpec((ts, nslab, LANES), lambda g: (fblk(g), 0, 0)),
                   pl.BlockSpec((ts, nslab, LANES), lambda g: (bblk(g), 0, 0))],
        out_shape=[yshape, yshape],
        scratch_shapes=[pltpu.VMEM((2, NKL, B_HEAD_DIM, LANES), F32),
                        pltpu.VMEM((2, B_HEAD_DIM, LANES), F32),
                        pltpu.VMEM((2, B_HEAD_DIM, LANES), F32)],
        compiler_params=_params(("arbitrary",)),
        name="wkv7_scan",
    )(*args)


def _rwkv_out_kernel(yf_ref, yb_ref, v_ref, bonus_ref, pg_ref, lng_ref, lnb_ref, gup_ref,
                     selv_ref, g32_ref, o_ref):
    g32 = g32_ref[...]
    gate = jnp.dot(jax.nn.sigmoid(pg_ref[...]).astype(BF16), gup_ref[...], preferred_element_type=F32)
    bonus = _gsum(bonus_ref[...], g32)
    nslab = MERGED // LANES
    ys = []
    for g in range(nslab // SUBLANES):
        rows = slice(g * SUBLANES, (g + 1) * SUBLANES)
        both = jnp.swapaxes(yf_ref[:, rows, :] + yb_ref[:, rows, :], 0, 1)
        ys += [both[j] for j in range(SUBLANES)]
    inv = 1.0 / B_HEAD_DIM
    mu = _gsum(functools.reduce(lambda a, b_: a + b_, ys), g32) * inv
    ds = [y - mu for y in ys]
    var = _gsum(functools.reduce(lambda a, b_: a + b_, [x * x for x in ds]), g32) * inv
    rstd = lax.rsqrt(var + LN_X_EPS)
    outs = []
    for j in range(nslab):
        cs = slice(j * LANES, (j + 1) * LANES)
        yn = ds[j] * rstd * lng_ref[:, cs] + lnb_ref[:, cs]
        outs.append(((yn + bonus * v_ref[:, cs]) * gate[:, cs]).astype(BF16))
    for j in range(nslab // 2):
        pair = jnp.concatenate([outs[2 * j], outs[2 * j + 1]], axis=1)
        for bi in range(2):
            o_ref[bi, :, j * LANES:(j + 1) * LANES] = jnp.dot(
                pair, selv_ref[bi], preferred_element_type=F32).astype(o_ref.dtype)


def _rwkv_out(yf, yb, v2d, bonus, p, lng_m, lnb_m, gup_m, selv, g32, *, rows):
    tm = 256
    nslab = MERGED // LANES
    y3s = pl.BlockSpec((tm, nslab, LANES), lambda i: (i, 0, 0))
    full2 = lambda i: (0, 0)
    return pl.pallas_call(
        _rwkv_out_kernel,
        grid=(rows // tm,),
        in_specs=[
            y3s, y3s,
            pl.BlockSpec((tm, MERGED), lambda i: (i, 0)),
            pl.BlockSpec((tm, LANES), lambda i: (i, 0)),
            pl.BlockSpec((tm, 2 * GATE_LORA), lambda i: (i, 3 * MERGED // (2 * GATE_LORA))),
            pl.BlockSpec((1, MERGED), full2),
            pl.BlockSpec((1, MERGED), full2),
            pl.BlockSpec((2 * GATE_LORA, MERGED), full2),
            pl.BlockSpec((2, 2 * LANES, LANES), lambda i: (0, 0, 0)),
            pl.BlockSpec((LANES, LANES), full2),
        ],
        out_specs=pl.BlockSpec((2, tm, B_WIDTH), lambda i: (0, i, 0)),
        out_shape=jax.ShapeDtypeStruct((2, rows, B_WIDTH), BF16),
        compiler_params=_params(("parallel",)),
        name="rwkv_out",
    )(yf, yb, v2d, bonus, p, lng_m, lnb_m, gup_m, selv, g32)


def _kmerge_cols(w):
    r = w.shape[0]
    wt = w.reshape(r, B_HEADS, NKL, KQ).transpose(0, 2, 3, 1)
    z = jnp.zeros_like(wt)
    return jnp.stack([jnp.stack([wt, z], axis=3), jnp.stack([z, wt], axis=3)]).reshape(2, r, MERGED)


def _vmerge_cols(w):
    r = w.shape[0]
    wt = w.reshape(r, B_HEADS, B_HEAD_DIM).transpose(0, 2, 1)
    z = jnp.zeros_like(wt)
    return jnp.stack([jnp.stack([wt, z], axis=2), jnp.stack([z, wt], axis=2)]).reshape(2, r, MERGED)


def _both(m):
    return m[0] + m[1]


def _kperm_cols(w):
    r = w.shape[0]
    return w.reshape(r, B_HEADS, NKL, KQ).transpose(0, 2, 3, 1).reshape(r, B_WIDTH)


def _vperm_cols(w):
    r = w.shape[0]
    return w.reshape(r, B_HEADS, B_HEAD_DIM).transpose(0, 2, 1).reshape(r, B_WIDTH)


def _selectors():
    ci = jnp.arange(LANES)[None, :]
    bsel = jnp.arange(2)[:, None, None]
    c_vi, c_h = ci // B_HEADS, ci % B_HEADS
    r2 = jnp.arange(2 * LANES)[:, None]
    selv = ((r2 // BH == c_vi) & ((r2 % BH) // B_HEADS == bsel) & (r2 % B_HEADS == c_h)).astype(BF16)
    r1 = jnp.arange(LANES)[:, None]
    g32 = (r1 % BH == ci % BH).astype(BF16)
    rep = jnp.stack([((r1 // BH == vi) & (r1 % BH == ci % BH)) for vi in range(LANES // BH)]).astype(BF16)
    half = jnp.arange(2)[None, :, None, None]
    merge = ((r1 // (LANES // 2) == half) & ((r1 % (LANES // 2)) // B_HEADS == ci // BH)
             & ((ci % BH) // B_HEADS == bsel[:, None]) & (r1 % B_HEADS == c_h)).astype(BF16)
    return selv, g32, rep, merge


def _merge_kernel(x_ref, h_ref, ya_ref, yb_ref, yc_ref, mod_ref, wg_ref, bg_ref, wb_ref, wo_ref,
                  o_ref, *, nz):
    j = pl.program_id(1)
    h = h_ref[...]
    z = None
    for i, y_ref in enumerate((ya_ref, yb_ref, yc_ref)):
        gate = jax.nn.sigmoid(jnp.dot(h, wg_ref[i], preferred_element_type=F32) + bg_ref[i])
        term = gate * jnp.dot(y_ref[...], wb_ref[i], preferred_element_type=F32)
        z = term if z is None else z + term
    part = jnp.dot(z.astype(BF16), wo_ref[...], preferred_element_type=F32)

    @pl.when(j == 0)
    def _():
        o_ref[...] = part

    @pl.when(j > 0)
    def _():
        o_ref[...] += part

    @pl.when(j == nz - 1)
    def _():
        o_ref[...] = x_ref[...] + mod_ref[5:6, :] * o_ref[...]


def _merge(xs, h, ya, yb, yc, mod, wg, bg, wb, wo, mod_idx, n_tiles):
    d = xs.shape[1]
    tz = 512
    nz = d // tz
    rowt = lambda i, j: (i, 0)
    return pl.pallas_call(
        functools.partial(_merge_kernel, nz=nz),
        grid=(n_tiles, nz),
        in_specs=[
            pl.BlockSpec((TM, d), rowt),
            pl.BlockSpec((TM, d), rowt),
            pl.BlockSpec((TM, A_WIDTH), rowt),
            pl.BlockSpec((TM, B_WIDTH), rowt),
            pl.BlockSpec((TM, C_WIDTH), rowt),
            pl.BlockSpec((None, N_MOD, d), lambda i, j: (mod_idx(i), 0, 0)),
            pl.BlockSpec((N_BRANCH, d, tz), lambda i, j: (0, 0, j)),
            pl.BlockSpec((N_BRANCH, 1, tz), lambda i, j: (0, 0, j)),
            pl.BlockSpec((N_BRANCH, A_WIDTH, tz), lambda i, j: (0, 0, j)),
            pl.BlockSpec((tz, d), lambda i, j: (j, 0)),
        ],
        out_specs=pl.BlockSpec((TM, d), rowt),
        out_shape=jax.ShapeDtypeStruct((n_tiles * TM, d), F32),
        compiler_params=_params(("parallel", "arbitrary")),
        name="merge",
    )(xs, h, ya, yb, yc, mod, wg, bg, wb, wo)


def _rope_tables(b, t, c):
    rows = t // GRID_W
    rowp = jnp.repeat(jnp.arange(rows), GRID_W).astype(F32)
    colp = jnp.tile(jnp.arange(GRID_W), rows).astype(F32)
    inv = 1.0 / (ROPE_BASE ** (jnp.arange(0, ROPE_AXIS_DIM, 2, dtype=F32) / ROPE_AXIS_DIM))
    ar, ac = rowp[:, None] * inv, colp[:, None] * inv
    cr, sr, cc, sc = jnp.cos(ar), jnp.sin(ar), jnp.cos(ac), jnp.sin(ac)
    cos64 = jnp.concatenate([cr, cr, cc, cc], axis=-1)
    sin64 = jnp.concatenate([-sr, sr, -sc, sc], axis=-1)
    cos_t = jnp.tile(cos64, (b, 2))
    sin_t = jnp.tile(sin64, (b, 2))
    cos_t = jnp.concatenate([cos_t, jnp.ones((b * c, LANES), F32)])
    sin_t = jnp.concatenate([sin_t, jnp.zeros((b * c, LANES), F32)])
    return cos_t, sin_t


def kernel(x, c, ctx, c_ctx, w_ada, b_ada, norm_g, ffn_w_in, ffn_w_out, w_in, gm_v_norm, gm_ws, gm_bs,
           rw_conv, rw_w0, rw_w_up, rw_a0, rw_a_up, rw_g_up, rw_k_k, rw_k_a, rw_r_k, rw_ln_g, rw_ln_b,
           da_q_norm, da_k_norm, da_lam, da_subln, w_branch, b_gate, w_out):
    b, t, d = x.shape
    cl = ctx.shape[1]
    depth = w_ada.shape[0]
    d_ff = ffn_w_out.shape[2]
    assert b == 2 and b * cl == TM and t % TM == 0 and cl % 256 == 0
    n_lat = (b * t) // TM
    n_all = n_lat + 1
    tiles_per_batch = t // TM
    mod_idx = lambda i: jnp.where(i < n_lat, i // tiles_per_batch, b)

    xs = jnp.concatenate([x.reshape(b * t, d), ctx.reshape(b * cl, d)])
    cvec = jnp.zeros((SUBLANES, d), F32).at[:b].set(c).at[b].set(c_ctx)
    b_ada3 = b_ada.reshape(depth, 1, N_MOD * d)
    cos_t, sin_t = _rope_tables(b, t, cl)
    lane = jnp.arange(LANES)
    g64 = (lane[:, None] // C_HEAD_DIM == lane[None, :] // C_HEAD_DIM).astype(BF16)
    selv, g32, rep, merge_sel = _selectors()
    w_gu = jnp.transpose(ffn_w_in.reshape(depth, 2, d, 2, d_ff), (0, 1, 3, 2, 4)).astype(BF16)
    w_dn = ffn_w_out.astype(BF16)

    o = 0
    offs = []
    for n in (A_WIDTH, A_WIDTH, 3 * B_WIDTH, GATE_LORA, 2 * DECAY_LORA, 2 * ICL_LORA, 3 * C_WIDTH, N_BRANCH * d):
        offs.append((o, o + n))
        o += n

    for l in range(depth):
        last = l == depth - 1
        lam_init = 0.8 - 0.6 * math.exp(-0.3 * l)
        mod = _ada(cvec, w_ada, b_ada3, l)[:b + 1].reshape(b + 1, N_MOD, d)

        xs, h = _ffn(xs, mod, norm_g[l], w_gu, w_dn, l, 0, mod_idx, n_tiles=n_all, mi=0, emit_h=True)

        wl = w_in[l]
        sl = lambda i: wl[:, offs[i][0]:offs[i][1]]
        n_mix = n_lat if last else n_all

        w_uv = jnp.concatenate([sl(0), sl(1)], axis=1).astype(BF16)
        bsb = jnp.broadcast_to(gm_bs[l][:, :, None], (A_GROUPS, CHUNK, A_WIDTH // A_GROUPS))
        ya = _gmlp(h, w_uv, gm_v_norm[l].reshape(1, A_WIDTH), gm_ws[l], bsb, n_mix)

        w_rkv, w_dec, w_icl = sl(2), sl(4), sl(5)
        pad_lora = lambda w: jnp.pad(w, ((0, 0), (0, LORA_PAD - w.shape[1])))
        w_b = jnp.concatenate(
            [_kperm_cols(w_rkv[:, :B_WIDTH]), _kperm_cols(w_rkv[:, B_WIDTH:2 * B_WIDTH]),
             _vperm_cols(w_rkv[:, 2 * B_WIDTH:]), sl(3),
             pad_lora(w_dec[:, :DECAY_LORA]), pad_lora(w_dec[:, DECAY_LORA:]),
             pad_lora(w_icl[:, :ICL_LORA]), pad_lora(w_icl[:, ICL_LORA:])], axis=1).astype(BF16)
        p = _bmerge(_rkv_proj(h, w_b), merge_sel, t=t, c=cl)
        conv = rw_conv[l]
        conv_m = jnp.concatenate([_both(_kmerge_cols(conv[:, :B_WIDTH])),
                                  _both(_kmerge_cols(conv[:, B_WIDTH:2 * B_WIDTH])),
                                  _both(_vmerge_cols(conv[:, 2 * B_WIDTH:]))], axis=1)
        def up_m(w):
            w32 = jnp.stack([
                _kmerge_cols(jnp.pad(w[dd], ((0, LORA_PAD - w.shape[1]), (0, 0)))).reshape(2 * LORA_PAD, MERGED)
                for dd in range(2)])
            hi = w32.astype(BF16)
            return jnp.stack([hi, (w32 - hi.astype(F32)).astype(BF16)])
        r4, kk4, kd4, b4, ptot, vrep, v2d, bonus = _rwkv_prep(
            p, conv_m, _both(_kmerge_cols(rw_w0[l])), up_m(rw_w_up[l]), _both(_kmerge_cols(rw_a0[l])),
            up_m(rw_a_up[l]), _both(_kmerge_cols(rw_k_k[l].reshape(1, B_WIDTH))),
            _both(_kmerge_cols(rw_k_a[l].reshape(1, B_WIDTH))),
            _both(_kmerge_cols(rw_r_k[l].reshape(1, B_WIDTH))), g32, rep, t=t, c=cl)
        yf, ybk = _scan(kd4, b4, kk4, r4, vrep, ptot, t=t, c=cl)
        yb2 = _rwkv_out(yf, ybk, v2d, bonus, p, _both(_vmerge_cols(rw_ln_g[l].reshape(1, B_WIDTH))),
                        _both(_vmerge_cols(rw_ln_b[l].reshape(1, B_WIDTH))),
                        _vmerge_cols(rw_g_up[l]).reshape(2 * GATE_LORA, MERGED).astype(BF16),
                        selv, g32, rows=t if last else t + cl)
        if last:
            yb = yb2.reshape(b * t, B_WIDTH)
        else:
            yb = jnp.concatenate([yb2[0, :t], yb2[1, :t], yb2[0, t:], yb2[1, t:]])

        gains = jnp.stack([jnp.tile(da_q_norm[l], 2) * (C_HEAD_DIM ** -0.5 * math.log2(math.e)),
                           jnp.tile(da_k_norm[l], 2), jnp.ones((LANES,), F32)]).reshape(3, 1, LANES)
        qk, vx = _qkv(h, sl(6).astype(BF16), gains, cos_t, sin_t, g64)
        sub = da_subln[l].reshape(1, LANES)
        yc = _attn(qk, vx, da_lam[l], sub, None, b=b, t=t, c=cl, lam_init=lam_init, ctx_queries=False)
        if not last:
            yc = _attn(qk, vx, da_lam[l], sub, yc, b=b, t=t, c=cl, lam_init=lam_init, ctx_queries=True)

        wg = jnp.transpose(sl(7).reshape(d, N_BRANCH, d), (1, 0, 2)).astype(BF16)
        wb = w_branch[l]
        wb_b = wb[1].reshape(B_HEADS, B_HEAD_DIM, d).transpose(1, 0, 2).reshape(B_WIDTH, d)
        wbr = jnp.stack([wb[0], wb_b, wb[2]]).astype(BF16)
        xs = _merge(xs, h, ya, yb, yc, mod, wg, b_gate[l].reshape(N_BRANCH, 1, d),
                    wbr, w_out[l].astype(BF16), mod_idx, n_mix)

        xs = _ffn(xs, mod, norm_g[l], w_gu, w_dn, l, 1, mod_idx, n_tiles=n_mix, mi=2, emit_h=False)

    return xs[:b * t].reshape(b, t, d)
```

```python
import functools
import math

import jax
import jax.numpy as jnp
from jax import lax
from jax.experimental import pallas as pl
from jax.experimental.pallas import tpu as pltpu

F32 = jnp.float32
BF16 = jnp.bfloat16

N_MOD = 9
CHUNK = 128
A_WIDTH = 1024
A_GROUPS = 8
B_WIDTH = 1024
B_HEAD_DIM = 64
B_HEADS = 16
DECAY_LORA = 96
ICL_LORA = 96
GATE_LORA = 256
C_HEADS = 8
C_HEAD_DIM = 64
C_WIDTH = 1024
N_BRANCH = 3
GRID_W = 64
ROPE_BASE = 10000.0
ROPE_AXIS_DIM = 32
NORM_EPS = 1e-6
LN_X_EPS = 64e-5

LANES = 128
SUBLANES = 8
VMEM_LIMIT = 56 * 1024 * 1024

TM = 512
TF = 512
LORA_PAD = 128
SCAN_TS = 64
ATT_TQ = 512
ATT_KC = 512


def _params(sem):
    return pltpu.CompilerParams(dimension_semantics=sem, vmem_limit_bytes=VMEM_LIMIT)


def _rms(x, eps):
    return x * lax.rsqrt(jnp.mean(x * x, axis=-1, keepdims=True) + eps)


def _split3(s):
    hi = s.astype(BF16)
    r = s - hi.astype(F32)
    mid = r.astype(BF16)
    lo = (r - mid.astype(F32)).astype(BF16)
    return hi, mid, lo


def _gsum(s, g_bf16):
    out = None
    for piece in _split3(s):
        d = jnp.dot(piece, g_bf16, preferred_element_type=F32)
        out = d if out is None else out + d
    return out


def _gelu_tanh(x):
    cdf = 0.5 * (1.0 + jnp.tanh(math.sqrt(2.0 / math.pi) * (x + 0.044715 * (x * x * x))))
    return x * cdf


def _ada_kernel(c_ref, w_ref, b_ref, o_ref):
    s = c_ref[...]
    s = s * jax.nn.sigmoid(s)
    w = w_ref[...].astype(BF16)
    acc = b_ref[...]
    for piece in _split3(s):
        acc = acc + jnp.dot(piece, w, preferred_element_type=F32)
    o_ref[...] = acc


def _ada(cvec, w_ada, b_ada3, l):
    d = cvec.shape[1]
    n = w_ada.shape[2]
    tn = 2048 if n % 2048 == 0 else 1024
    return pl.pallas_call(
        _ada_kernel,
        grid=(n // tn,),
        in_specs=[
            pl.BlockSpec((SUBLANES, d), lambda j: (0, 0)),
            pl.BlockSpec((None, d, tn), lambda j: (l, 0, j)),
            pl.BlockSpec((None, 1, tn), lambda j: (l, 0, j)),
        ],
        out_specs=pl.BlockSpec((SUBLANES, tn), lambda j: (0, j)),
        out_shape=jax.ShapeDtypeStruct((SUBLANES, n), F32),
        compiler_params=_params(("arbitrary",)),
        name="ada",
    )(cvec, w_ada, b_ada3)


def _ffn_kernel(x_ref, mod_ref, g_ref, win_ref, wout_ref, *rest, tf, nf, last_valid, mi, emit_h):
    if emit_h:
        o_ref, h_ref, hn_sc, acc_sc = rest
    else:
        o_ref, hn_sc, acc_sc = rest
    f = pl.program_id(1)

    @pl.when(f == 0)
    def _():
        xn = _rms(x_ref[...], NORM_EPS) * g_ref[mi:mi + 1, :]
        hn = xn * (1.0 + mod_ref[3 * mi + 1:3 * mi + 2, :]) + mod_ref[3 * mi:3 * mi + 1, :]
        hn_sc[...] = hn.astype(BF16)
        acc_sc[...] = jnp.zeros_like(acc_sc)

    def hidden_tile(valid):
        hn = hn_sc[...]
        g = jnp.dot(hn, win_ref[0, :, :valid], preferred_element_type=F32)
        u = jnp.dot(hn, win_ref[1, :, :valid], preferred_element_type=F32)
        act = (g * jax.nn.sigmoid(g) * u).astype(BF16)
        acc_sc[...] += jnp.dot(act, wout_ref[:valid, :], preferred_element_type=F32)

    if last_valid == tf:
        hidden_tile(tf)
    else:
        pl.when(f < nf - 1)(lambda: hidden_tile(tf))
        pl.when(f == nf - 1)(lambda: hidden_tile(last_valid))

    @pl.when(f == nf - 1)
    def _():
        out = x_ref[...] + 0.5 * mod_ref[3 * mi + 2:3 * mi + 3, :] * acc_sc[...]
        o_ref[...] = out
        if emit_h:
            hn = _rms(out, NORM_EPS) * g_ref[1:2, :]
            h_ref[...] = (hn * (1.0 + mod_ref[4:5, :]) + mod_ref[3:4, :]).astype(BF16)


def _ffn(xs, mod, norm_g, w_gu, w_out, l, w, mod_idx, *, n_tiles, mi, emit_h):
    m, d = xs.shape
    d_ff = w_out.shape[2]
    assert d_ff % LANES == 0
    nf = pl.cdiv(d_ff, TF)
    out_rows = n_tiles * TM
    out_shape = [jax.ShapeDtypeStruct((out_rows, d), F32)]
    out_specs = [pl.BlockSpec((TM, d), lambda i, f: (i, 0))]
    if emit_h:
        out_shape.append(jax.ShapeDtypeStruct((out_rows, d), BF16))
        out_specs.append(pl.BlockSpec((TM, d), lambda i, f: (i, 0)))
    res = pl.pallas_call(
        functools.partial(_ffn_kernel, tf=TF, nf=nf, last_valid=d_ff - (nf - 1) * TF, mi=mi, emit_h=emit_h),
        grid=(n_tiles, nf),
        in_specs=[
            pl.BlockSpec((TM, d), lambda i, f: (i, 0)),
            pl.BlockSpec((None, N_MOD, d), lambda i, f: (mod_idx(i), 0, 0)),
            pl.BlockSpec((3, d), lambda i, f: (0, 0)),
            pl.BlockSpec((None, None, 2, d, TF), lambda i, f: (l, w, 0, 0, f)),
            pl.BlockSpec((None, None, TF, d), lambda i, f: (l, w, f, 0)),
        ],
        out_specs=out_specs,
        out_shape=out_shape,
        scratch_shapes=[pltpu.VMEM((TM, d), BF16), pltpu.VMEM((TM, d), F32)],
        compiler_params=_params(("parallel", "arbitrary")),
        name="ffn",
    )(xs, mod, norm_g, w_gu, w_out)
    return res if emit_h else res[0]


def _gmlp_kernel(h_ref, w_ref, vn_ref, ws_ref, bsb_ref, o_ref, p_sc, *, tm):
    p_sc[...] = jnp.dot(h_ref[...], w_ref[...], preferred_element_type=F32)
    gd = A_WIDTH // A_GROUPS
    for g in range(A_GROUPS):
        wsg = ws_ref[g].astype(BF16)
        bias = bsb_ref[g]
        gain = vn_ref[:, g * gd:(g + 1) * gd]
        for n in range(tm // CHUNK):
            rows = slice(n * CHUNK, (n + 1) * CHUNK)
            u = _gelu_tanh(p_sc[rows, g * gd:(g + 1) * gd])
            v = _gelu_tanh(p_sc[rows, A_WIDTH + g * gd:A_WIDTH + (g + 1) * gd])
            v = _rms(v, NORM_EPS) * gain
            sv = jnp.dot(wsg, v.astype(BF16), preferred_element_type=F32) + bias
            o_ref[rows, g * gd:(g + 1) * gd] = (u * sv).astype(o_ref.dtype)


def _gmlp(h, w_uv, vn, ws, bsb, n_tiles):
    m, d = h.shape
    return pl.pallas_call(
        functools.partial(_gmlp_kernel, tm=TM),
        grid=(n_tiles,),
        in_specs=[
            pl.BlockSpec((TM, d), lambda i: (i, 0)),
            pl.BlockSpec((d, 2 * A_WIDTH), lambda i: (0, 0)),
            pl.BlockSpec((1, A_WIDTH), lambda i: (0, 0)),
            pl.BlockSpec((A_GROUPS, CHUNK, CHUNK), lambda i: (0, 0, 0)),
            pl.BlockSpec((A_GROUPS, CHUNK, A_WIDTH // A_GROUPS), lambda i: (0, 0, 0)),
        ],
        out_specs=pl.BlockSpec((TM, A_WIDTH), lambda i: (i, 0)),
        out_shape=jax.ShapeDtypeStruct((n_tiles * TM, A_WIDTH), BF16),
        scratch_shapes=[pltpu.VMEM((TM, 2 * A_WIDTH), F32)],
        compiler_params=_params(("parallel",)),
        name="gmlp",
    )(h, w_uv, vn, ws, bsb)


def _qkv_kernel(h_ref, w_ref, gain_ref, cos_ref, sin_ref, g64_ref, o_ref, vx_ref):
    j = pl.program_id(1)
    n_parts = 2
    heads_per = C_HEADS // n_parts

    def heads():
        for part in range(n_parts):
            cols = slice(part * heads_per * LANES, (part + 1) * heads_per * LANES)
            p = jnp.dot(h_ref[...], w_ref[:, cols], preferred_element_type=F32)
            for i in range(heads_per):
                yield part * heads_per + i, p[:, i * LANES:(i + 1) * LANES]

    @pl.when(j == 2)
    def _():
        ones = jnp.ones((h_ref.shape[0], LANES), BF16)
        for hs, x in heads():
            vx_ref[:, 2 * hs * LANES:(2 * hs + 1) * LANES] = x.astype(BF16)
            vx_ref[:, (2 * hs + 1) * LANES:(2 * hs + 2) * LANES] = ones

    @pl.when(j < 2)
    def _():
        lane = lax.broadcasted_iota(jnp.int32, (1, LANES), 1)
        first = (lane % ROPE_AXIS_DIM) < (ROPE_AXIS_DIM // 2)
        cos = cos_ref[...]
        sin = sin_ref[...]
        gain = gain_ref[...]
        g64 = g64_ref[...]
        for hs, x in heads():
            ms = _gsum(x * x, g64) * (1.0 / C_HEAD_DIM)
            y = x * lax.rsqrt(ms + NORM_EPS) * gain
            half = ROPE_AXIS_DIM // 2
            rot = jnp.where(first, pltpu.roll(y, LANES - half, 1), pltpu.roll(y, half, 1))
            o_ref[:, hs * LANES:(hs + 1) * LANES] = (y * cos + rot * sin).astype(BF16)


def _qkv(h, w_qkv, gains, cos_t, sin_t, g64):
    m, d = h.shape
    return pl.pallas_call(
        _qkv_kernel,
        grid=(m // TM, 3),
        in_specs=[
            pl.BlockSpec((TM, d), lambda i, j: (i, 0)),
            pl.BlockSpec((d, C_WIDTH), lambda i, j: (0, j)),
            pl.BlockSpec((None, 1, LANES), lambda i, j: (j, 0, 0)),
            pl.BlockSpec((TM, LANES), lambda i, j: (i, 0)),
            pl.BlockSpec((TM, LANES), lambda i, j: (i, 0)),
            pl.BlockSpec((LANES, LANES), lambda i, j: (0, 0)),
        ],
        out_specs=[pl.BlockSpec((None, TM, C_WIDTH), lambda i, j: (jnp.minimum(j, 1), i, 0)),
                   pl.BlockSpec((TM, 2 * C_WIDTH), lambda i, j: (i, 0))],
        out_shape=[jax.ShapeDtypeStruct((2, m, C_WIDTH), BF16), jax.ShapeDtypeStruct((m, 2 * C_WIDTH), BF16)],
        compiler_params=_params(("parallel", "arbitrary")),
        name="qkv",
    )(h, w_qkv, gains, cos_t, sin_t, g64)


def _attn_kernel(lam_ref, sub_ref, q_ref, *refs, nseg, lam_init):
    kv = refs[:2 * nseg]
    o_ref = refs[2 * nseg]
    lv = lam_ref[...]
    lam = (jnp.exp(jnp.sum(lv[0:1] * lv[1:2], axis=-1, keepdims=True))
           - jnp.exp(jnp.sum(lv[2:3] * lv[3:4], axis=-1, keepdims=True)) + lam_init)
    q = q_ref[...]
    lane = lax.broadcasted_iota(jnp.int32, (1, LANES), 1)
    zero = jnp.zeros_like(q)
    outs = []
    for j in range(2):
        in_map = (lane < C_HEAD_DIM) if j == 0 else (lane >= C_HEAD_DIM)
        qj = jnp.where(in_map, q, zero)
        m = None
        acc = None
        for si in range(nseg):
            k_ref, v_ref = kv[2 * si], kv[2 * si + 1]
            ls = k_ref.shape[0]
            for c0 in range(0, ls, ATT_KC):
                kc = min(ATT_KC, ls - c0)
                s = lax.dot_general(qj, k_ref[c0:c0 + kc, :], (((1,), (1,)), ((), ())),
                                    preferred_element_type=F32)
                cm = jnp.max(s, axis=-1, keepdims=True)
                m_new = cm if m is None else jnp.maximum(m, cm)
                e = jnp.exp2((s - m_new).astype(BF16))
                pv = jnp.dot(e, v_ref[c0:c0 + kc, :], preferred_element_type=F32)
                acc = pv if acc is None else acc * jnp.exp2(m - m_new) + pv
                m = m_new
        outs.append(acc[:, :LANES] / acc[:, LANES:])
    o = outs[0] - lam * outs[1]
    o = _rms(o, NORM_EPS) * sub_ref[...]
    o_ref[...] = (o * (1.0 - lam_init)).astype(o_ref.dtype)


def _attn(qk, vx, lam, subln, yc_prev, *, b, t, c, lam_init, ctx_queries):
    m = qk.shape[1]
    if ctx_queries:
        tq, nq = c, 1
        q_blk = lambda bi, qi: (b * t) // c + bi
        segs = [(c, lambda bi: (b * t) // c + bi)]
    else:
        tq = ATT_TQ
        nq = t // tq
        q_blk = lambda bi, qi: bi * nq + qi
        segs = [(t, lambda bi: bi), (c, lambda bi: (b * t) // c + bi)]
    in_specs = [
        pl.BlockSpec((4, C_HEAD_DIM), lambda bi, h, qi: (0, 0)),
        pl.BlockSpec((1, LANES), lambda bi, h, qi: (0, 0)),
        pl.BlockSpec((None, tq, LANES), lambda bi, h, qi: (0, q_blk(bi, qi), h)),
    ]
    args = [lam, subln, qk]
    for ls, blk in segs:
        in_specs.append(pl.BlockSpec((None, ls, LANES), lambda bi, h, qi, blk=blk: (1, blk(bi), h)))
        in_specs.append(pl.BlockSpec((ls, 2 * LANES), lambda bi, h, qi, blk=blk: (blk(bi), h)))
        args += [qk, vx]
    aliases = {}
    if yc_prev is not None:
        in_specs.append(pl.BlockSpec(memory_space=pl.ANY))
        args.append(yc_prev)
        aliases = {len(args) - 1: 0}

    def body(*refs):
        n_in = 3 + 2 * len(segs)
        _attn_kernel(*refs[:n_in], refs[-1], nseg=len(segs), lam_init=lam_init)

    return pl.pallas_call(
        body,
        grid=(b, C_HEADS, nq),
        in_specs=in_specs,
        out_specs=pl.BlockSpec((tq, LANES), lambda bi, h, qi: (q_blk(bi, qi), h)),
        out_shape=jax.ShapeDtypeStruct((m, C_WIDTH), BF16),
        input_output_aliases=aliases,
        compiler_params=_params(("parallel", "parallel", "arbitrary")),
        name="diff_attn_ctx" if ctx_queries else "diff_attn",
    )(*args)


KQ = 4
NKL = B_HEAD_DIM // KQ
MERGED = 2 * B_WIDTH
BH = 2 * B_HEADS
LORA_MERGED = 2 * GATE_LORA + 8 * LORA_PAD
PROJ_MERGED = 3 * MERGED + LORA_MERGED


PROJ_UNMERGED = 3 * B_WIDTH + GATE_LORA + 4 * LORA_PAD


def _mm_kernel(a_ref, w_ref, o_ref):
    o_ref[...] = jnp.dot(a_ref[...], w_ref[...], preferred_element_type=F32)


def _rkv_proj(h, w):
    m, d = h.shape
    tn = PROJ_UNMERGED // 3
    return pl.pallas_call(
        _mm_kernel,
        grid=(PROJ_UNMERGED // tn, m // TM),
        in_specs=[pl.BlockSpec((TM, d), lambda j, i: (i, 0)),
                  pl.BlockSpec((d, tn), lambda j, i: (0, j))],
        out_specs=pl.BlockSpec((TM, tn), lambda j, i: (i, j)),
        out_shape=jax.ShapeDtypeStruct((m, PROJ_UNMERGED), F32),
        compiler_params=_params(("parallel", "arbitrary")),
        name="rkv_proj",
    )(h, w)


def _bmerge_kernel(p0_ref, p1_ref, e_ref, o_ref):
    n_kv = 3 * B_WIDTH // LANES
    for bi, p_ref in enumerate((p0_ref, p1_ref)):
        for s in range(n_kv):
            x = p_ref[:, s * LANES:(s + 1) * LANES]
            hi = x.astype(BF16)
            lo = (x - hi.astype(F32)).astype(BF16)
            for half in range(2):
                e = e_ref[bi, half]
                part = (jnp.dot(hi, e, preferred_element_type=F32) + jnp.dot(lo, e, preferred_element_type=F32))
                cs = slice((2 * s + half) * LANES, (2 * s + half + 1) * LANES)
                if bi == 0:
                    o_ref[:, cs] = part
                else:
                    o_ref[:, cs] += part
        src = 3 * B_WIDTH
        dst = 3 * MERGED
        for width in (GATE_LORA,) + (LORA_PAD,) * 4:
            o_ref[:, dst + bi * width:dst + (bi + 1) * width] = p_ref[:, src:src + width]
            src += width
            dst += 2 * width


def _bmerge(p, e, *, t, c):
    tm = 256
    s = t + c
    nlat = t // tm
    hblk = lambda bi, i: jnp.where(i < nlat, bi * nlat + i, (2 * t + bi * c) // tm + (i - nlat))
    return pl.pallas_call(
        _bmerge_kernel,
        grid=(s // tm,),
        in_specs=[
            pl.BlockSpec((tm, PROJ_UNMERGED), lambda i: (hblk(0, i), 0)),
            pl.BlockSpec((tm, PROJ_UNMERGED), lambda i: (hblk(1, i), 0)),
            pl.BlockSpec((2, 2, LANES, LANES), lambda i: (0, 0, 0, 0)),
        ],
        out_specs=pl.BlockSpec((tm, PROJ_MERGED), lambda i: (i, 0)),
        out_shape=jax.ShapeDtypeStruct((s, PROJ_MERGED), F32),
        compiler_params=_params(("parallel",)),
        name="batch_merge",
    )(p, p, e)


class _SlabWriter:
    def __init__(self, ref, lead=()):
        self.ref, self.lead, self.parts, self.base = ref, lead, [], 0

    def add(self, slab):
        self.parts.append(slab)
        if len(self.parts) == SUBLANES:
            block = jnp.swapaxes(jnp.stack(self.parts, axis=0), 0, 1)
            self.ref[self.lead + (slice(None), slice(self.base, self.base + SUBLANES), slice(None))] = block
            self.parts, self.base = [], self.base + SUBLANES


def _rwkv_prep_kernel(p_ref, prev_ref, next_ref, lo_ref, conv_ref, w0_ref, wup_ref, a0_ref, aup_ref,
                      kkg_ref, ka_ref, rk_ref, g32_ref, rep_ref,
                      r_o, kk_o, kd_o, b_o, ptot_o, vrep_o, v_o, bonus_o, k_sc, r_sc, *, tm, ts, starts, ends):
    i = pl.program_id(0)
    is_start = functools.reduce(jnp.logical_or, [i == s for s in starts])
    is_end = functools.reduce(jnp.logical_or, [i == s for s in ends])
    row = lax.broadcasted_iota(jnp.int32, (tm, 1), 0)

    def conv(c0):
        cs = slice(c0, c0 + LANES)
        x = p_ref[:, cs]
        pm = jnp.where(is_start, 0.0, prev_ref[SUBLANES - 1:SUBLANES, cs])
        nx = jnp.where(is_end, 0.0, next_ref[0:1, cs])
        xm1 = jnp.where(row == 0, pm, pltpu.roll(x, 1, 0))
        xp1 = jnp.where(row == tm - 1, nx, pltpu.roll(x, tm - 1, 0))
        return xm1 * conv_ref[0:1, cs] + x * conv_ref[1:2, cs] + xp1 * conv_ref[2:3, cs]

    vrep_out = _SlabWriter(vrep_o)
    r_out, kk_out, kd_out, b_out = ([_SlabWriter(o, (d,)) for d in range(2)] for o in (r_o, kk_o, kd_o, b_o))

    ss = None
    for m in range(NKL):
        cs = slice(m * LANES, (m + 1) * LANES)
        r_sc[:, cs] = conv(m * LANES)
        k = conv(MERGED + m * LANES)
        k_sc[:, cs] = k
        kkr = k * kkg_ref[:, cs]
        ss = kkr * kkr if ss is None else ss + kkr * kkr
    rs = lax.rsqrt(_gsum(ss, g32_ref[...]) + 1e-12)

    w_off = 2 * GATE_LORA
    a_off = w_off + 4 * LORA_PAD
    def split2(x):
        hi = x.astype(BF16)
        return hi, (x - hi.astype(F32)).astype(BF16)

    def lora(x2, up_ref, d, cs):
        return (jnp.dot(x2[0], up_ref[0, d, :, cs], preferred_element_type=F32)
                + jnp.dot(x2[1], up_ref[0, d, :, cs], preferred_element_type=F32)
                + jnp.dot(x2[0], up_ref[1, d, :, cs], preferred_element_type=F32))

    tw = [split2(jnp.tanh(lo_ref[:, w_off + 2 * d * LORA_PAD:w_off + 2 * (d + 1) * LORA_PAD])) for d in range(2)]
    pa = [split2(lo_ref[:, a_off + 2 * d * LORA_PAD:a_off + 2 * (d + 1) * LORA_PAD]) for d in range(2)]
    bonus = None
    for m in range(NKL):
        cs = slice(m * LANES, (m + 1) * LANES)
        k = k_sc[:, cs]
        kk = k * kkg_ref[:, cs] * rs
        r = r_sc[:, cs]
        kd_sum = None
        for d in range(2):
            w_lo = w0_ref[d:d + 1, cs] + lora(tw[d], wup_ref, d, cs)
            lw = -jax.nn.sigmoid(w_lo) * math.exp(-0.5)
            cum = lw
            shift = 1
            while shift < ts:
                if d == 0:
                    cum = cum + jnp.where(row % ts >= shift, pltpu.roll(cum, shift, 0), 0.0)
                else:
                    cum = cum + jnp.where(row % ts < ts - shift, pltpu.roll(cum, tm - shift, 0), 0.0)
                shift *= 2
            p_incl = jnp.exp(cum)
            p_inv = jnp.exp(-cum)
            p_excl = jnp.exp(cum - lw)
            a = jax.nn.sigmoid(a0_ref[d:d + 1, cs] + lora(pa[d], aup_ref, d, cs))
            kd = k * (1.0 + (a - 1.0) * ka_ref[:, cs])
            kd_sum = kd if kd_sum is None else kd_sum + kd
            r_out[d].add(r * p_incl)
            kk_out[d].add(kk * p_excl)
            kd_out[d].add(kd * p_inv)
            b_out[d].add(kk * a * p_inv)
            for g in range(tm // ts):
                last = g * ts + (ts - 1 if d == 0 else 0)
                ptot_o[d, g, m:m + 1, :] = p_incl[last:last + 1, :]
        term = r * (0.5 * kd_sum) * rk_ref[:, cs]
        bonus = term if bonus is None else bonus + term
    bonus_o[...] = bonus

    per_slab = LANES // BH
    for j in range(MERGED // LANES):
        vs = conv(2 * MERGED + j * LANES)
        v_o[:, j * LANES:(j + 1) * LANES] = vs
        pieces = _split3(vs)
        for vi in range(per_slab):
            vrep_out.add(sum(jnp.dot(piece, rep_ref[vi], preferred_element_type=F32) for piece in pieces))


def _rwkv_prep(p, conv_m, w0_m, wup_m, a0_m, aup_m, kk_m, ka_m, rk_m, g32, rep, *, t, c):
    s = t + c
    tm = 128
    nt = s // tm
    starts = [0, t // tm]
    ends = [t // tm - 1, nt - 1]
    hb = tm // SUBLANES
    rkv_w = 3 * MERGED
    full2 = lambda i: (0, 0)
    full3 = lambda i: (0, 0, 0)
    ts = SCAN_TS
    k4 = jax.ShapeDtypeStruct((2, s, NKL, LANES), F32)
    k4s = pl.BlockSpec((2, tm, NKL, LANES), lambda i: (0, i, 0, 0))
    return pl.pallas_call(
        functools.partial(_rwkv_prep_kernel, tm=tm, ts=ts, starts=starts, ends=ends),
        grid=(nt,),
        in_specs=[
            pl.BlockSpec((tm, rkv_w), lambda i: (i, 0)),
            pl.BlockSpec((SUBLANES, rkv_w), lambda i: (jnp.maximum(i * hb - 1, 0), 0)),
            pl.BlockSpec((SUBLANES, rkv_w), lambda i: (jnp.minimum((i + 1) * hb, s // SUBLANES - 1), 0)),
            pl.BlockSpec((tm, LORA_MERGED), lambda i: (i, rkv_w // LORA_MERGED)),
            pl.BlockSpec((3, rkv_w), full2),
            pl.BlockSpec((2, MERGED), full2),
            pl.BlockSpec((2, 2, 2 * LORA_PAD, MERGED), lambda i: (0, 0, 0, 0)),
            pl.BlockSpec((2, MERGED), full2),
            pl.BlockSpec((2, 2, 2 * LORA_PAD, MERGED), lambda i: (0, 0, 0, 0)),
            pl.BlockSpec((1, MERGED), full2),
            pl.BlockSpec((1, MERGED), full2),
            pl.BlockSpec((1, MERGED), full2),
            pl.BlockSpec((LANES, LANES), full2),
            pl.BlockSpec((LANES // BH, LANES, LANES), full3),
        ],
        out_specs=[k4s, k4s, k4s, k4s,
                   pl.BlockSpec((2, tm // ts, NKL, LANES), lambda i: (0, i, 0, 0)),
                   pl.BlockSpec((tm, B_HEAD_DIM, LANES), lambda i: (i, 0, 0)),
                   pl.BlockSpec((tm, MERGED), lambda i: (i, 0)),
                   pl.BlockSpec((tm, LANES), lambda i: (i, 0))],
        out_shape=[k4, k4, k4, k4, jax.ShapeDtypeStruct((2, s // ts, NKL, LANES), F32),
                   jax.ShapeDtypeStruct((s, B_HEAD_DIM, LANES), F32),
                   jax.ShapeDtypeStruct((s, MERGED), F32), jax.ShapeDtypeStruct((s, LANES), F32)],
        scratch_shapes=[pltpu.VMEM((tm, MERGED), F32), pltpu.VMEM((tm, MERGED), F32)],
        compiler_params=_params(("parallel",)),
        name="rwkv_prep",
    )(p, p, p, p, conv_m, w0_m, wup_m, a0_m, aup_m, kk_m, ka_m, rk_m, g32, rep)


def _scan_kernel(*refs, ts):
    ins, (yf_ref, yb_ref, s_sc, y_sc, sa_sc) = refs[:12], refs[12:]
    streams = [ins[:6] + (yf_ref,), ins[6:] + (yb_ref,)]
    nkl = s_sc.shape[1]
    nvb = s_sc.shape[2] // SUBLANES

    @pl.when(pl.program_id(0) == 0)
    def _():
        s_sc[...] = jnp.zeros_like(s_sc)
        y_sc[...] = jnp.zeros_like(y_sc)

    def vsl(vb):
        return slice(vb * SUBLANES, (vb + 1) * SUBLANES)

    diag = (lax.broadcasted_iota(jnp.int32, (SUBLANES, LANES), 1) // BH
            == lax.broadcasted_iota(jnp.int32, (SUBLANES, LANES), 0) % KQ)

    def emit_y(d, y_ref, row):
        for vb in range(nvb):
            y = y_sc[d, vsl(vb), :]
            y = y + pltpu.roll(y, BH, 1)
            y = y + pltpu.roll(y, 2 * BH, 1)
            y = jnp.where(diag, y, 0.0)
            y = y + pltpu.roll(y, 1, 0)
            y = y + pltpu.roll(y, 2, 0)
            for half in range(SUBLANES // KQ):
                srow = half * KQ + KQ - 1
                y_ref[row, 2 * vb + half:2 * vb + half + 1, :] = y[srow:srow + 1, :]

    def allreduce(x):
        x = x + pltpu.roll(x, BH, 1)
        return x + pltpu.roll(x, 2 * BH, 1)

    def step(j, carry):
        ts_of = (j, ts - 1 - j)
        prev = (jnp.maximum(j - 1, 0), jnp.minimum(ts - j, ts - 1))
        for d in range(2):
            emit_y(d, streams[d][-1], prev[d])
        nxt = (jnp.minimum(j + 1, ts - 1), jnp.maximum(ts - 2 - j, 0))
        for d, (kd_ref, b_ref, kk_ref, r_ref, v_ref, ptot_ref, y_ref) in enumerate(streams):
            t = ts_of[d]
            sa = [sa_sc[d, vsl(vb), :] if d == 0 else allreduce(sa_sc[d, vsl(vb), :]) for vb in range(nvb)]
            ys = [None] * nvb
            sa_next = [None] * nvb
            for kl in range(nkl):
                brow = b_ref[t, kl:kl + 1, :]
                kdrow = kd_ref[t, kl:kl + 1, :]
                rrow = r_ref[t, kl:kl + 1, :]
                kknext = kk_ref[nxt[d], kl:kl + 1, :]
                for vb in range(nvb):
                    sn = s_sc[d, kl, vsl(vb), :] - sa[vb] * brow + v_ref[t, vsl(vb), :] * kdrow
                    s_sc[d, kl, vsl(vb), :] = sn
                    term = sn * rrow
                    ys[vb] = term if ys[vb] is None else ys[vb] + term
                    term = sn * kknext
                    sa_next[vb] = term if sa_next[vb] is None else sa_next[vb] + term
            for vb in range(nvb):
                y_sc[d, vsl(vb), :] = ys[vb]
                sa_sc[d, vsl(vb), :] = allreduce(sa_next[vb]) if d == 0 else sa_next[vb]
        return carry

    for d, first in enumerate((0, ts - 1)):
        kk_ref = streams[d][2]
        for vb in range(nvb):
            acc = None
            for kl in range(nkl):
                term = s_sc[d, kl, vsl(vb), :] * kk_ref[first, kl:kl + 1, :]
                acc = term if acc is None else acc + term
            sa_sc[d, vsl(vb), :] = allreduce(acc) if d == 0 else acc

    lax.fori_loop(0, ts, step, 0)
    emit_y(0, yf_ref, ts - 1)
    emit_y(1, yb_ref, 0)
    for d in range(2):
        ptot_ref = streams[d][5]
        for kl in range(nkl):
            prow = ptot_ref[0, kl:kl + 1, :]
            for vb in range(nvb):
                s_sc[d, kl, vsl(vb), :] = s_sc[d, kl, vsl(vb), :] * prow


def _scan(kd4, b4, kk4, r4, vrep, ptot, *, t, c):
    s = t + c
    ts = SCAN_TS
    nctx = c // ts
    nlat = t // ts
    fblk = lambda g: jnp.where(g < nctx, nlat + g, g - nctx)
    bblk = lambda g: jnp.where(g < nctx, nlat + (nctx - 1 - g), nlat - 1 - (g - nctx))
    in_specs = []
    args = []
    for d, blk in enumerate((fblk, bblk)):
        for a in (kd4, b4, kk4, r4):
            in_specs.append(pl.BlockSpec((None, ts, NKL, LANES), lambda g, d=d, blk=blk: (d, blk(g), 0, 0)))
            args.append(a)
        in_specs.append(pl.BlockSpec((ts, B_HEAD_DIM, LANES), lambda g, blk=blk: (blk(g), 0, 0)))
        args.append(vrep)
        in_specs.append(pl.BlockSpec((None, 1, NKL, LANES), lambda g, d=d, blk=blk: (d, blk(g), 0, 0)))
        args.append(ptot)
    nslab = MERGED // LANES
    yshape = jax.ShapeDtypeStruct((s, nslab, LANES), F32)
    return pl.pallas_call(
        functools.partial(_scan_kernel, ts=ts),
        grid=(s // ts,),
        in_specs=in_specs,
        out_specs=[pl.BlockSpec((ts, nslab, LANES), lambda g: (fblk(g), 0, 0)),
                   pl.BlockSpec((ts, nslab, LANES), lambda g: (bblk(g), 0, 0))],
        out_shape=[yshape, yshape],
        scratch_shapes=[pltpu.VMEM((2, NKL, B_HEAD_DIM, LANES), F32),
                        pltpu.VMEM((2, B_HEAD_DIM, LANES), F32),
                        pltpu.VMEM((2, B_HEAD_DIM, LANES), F32)],
        compiler_params=_params(("arbitrary",)),
        name="wkv7_scan",
    )(*args)


def _rwkv_out_kernel(yf_ref, yb_ref, v_ref, bonus_ref, pg_ref, lng_ref, lnb_ref, gup_ref,
                     selv_ref, g32_ref, o_ref):
    g32 = g32_ref[...]
    gate = jnp.dot(jax.nn.sigmoid(pg_ref[...]).astype(BF16), gup_ref[...], preferred_element_type=F32)
    bonus = _gsum(bonus_ref[...], g32)
    nslab = MERGED // LANES
    ys = []
    for g in range(nslab // SUBLANES):
        rows = slice(g * SUBLANES, (g + 1) * SUBLANES)
        both = jnp.swapaxes(yf_ref[:, rows, :] + yb_ref[:, rows, :], 0, 1)
        ys += [both[j] for j in range(SUBLANES)]
    inv = 1.0 / B_HEAD_DIM
    mu = _gsum(functools.reduce(lambda a, b_: a + b_, ys), g32) * inv
    ds = [y - mu for y in ys]
    var = _gsum(functools.reduce(lambda a, b_: a + b_, [x * x for x in ds]), g32) * inv
    rstd = lax.rsqrt(var + LN_X_EPS)
    outs = []
    for j in range(nslab):
        cs = slice(j * LANES, (j + 1) * LANES)
        yn = ds[j] * rstd * lng_ref[:, cs] + lnb_ref[:, cs]
        outs.append(((yn + bonus * v_ref[:, cs]) * gate[:, cs]).astype(BF16))
    for j in range(nslab // 2):
        pair = jnp.concatenate([outs[2 * j], outs[2 * j + 1]], axis=1)
        for bi in range(2):
            o_ref[bi, :, j * LANES:(j + 1) * LANES] = jnp.dot(
                pair, selv_ref[bi], preferred_element_type=F32).astype(o_ref.dtype)


def _rwkv_out(yf, yb, v2d, bonus, p, lng_m, lnb_m, gup_m, selv, g32, *, rows):
    tm = 256
    nslab = MERGED // LANES
    y3s = pl.BlockSpec((tm, nslab, LANES), lambda i: (i, 0, 0))
    full2 = lambda i: (0, 0)
    return pl.pallas_call(
        _rwkv_out_kernel,
        grid=(rows // tm,),
        in_specs=[
            y3s, y3s,
            pl.BlockSpec((tm, MERGED), lambda i: (i, 0)),
            pl.BlockSpec((tm, LANES), lambda i: (i, 0)),
            pl.BlockSpec((tm, 2 * GATE_LORA), lambda i: (i, 3 * MERGED // (2 * GATE_LORA))),
            pl.BlockSpec((1, MERGED), full2),
            pl.BlockSpec((1, MERGED), full2),
            pl.BlockSpec((2 * GATE_LORA, MERGED), full2),
            pl.BlockSpec((2, 2 * LANES, LANES), lambda i: (0, 0, 0)),
            pl.BlockSpec((LANES, LANES), full2),
        ],
        out_specs=pl.BlockSpec((2, tm, B_WIDTH), lambda i: (0, i, 0)),
        out_shape=jax.ShapeDtypeStruct((2, rows, B_WIDTH), BF16),
        compiler_params=_params(("parallel",)),
        name="rwkv_out",
    )(yf, yb, v2d, bonus, p, lng_m, lnb_m, gup_m, selv, g32)


def _kmerge_cols(w):
    r = w.shape[0]
    wt = w.reshape(r, B_HEADS, NKL, KQ).transpose(0, 2, 3, 1)
    z = jnp.zeros_like(wt)
    return jnp.stack([jnp.stack([wt, z], axis=3), jnp.stack([z, wt], axis=3)]).reshape(2, r, MERGED)


def _vmerge_cols(w):
    r = w.shape[0]
    wt = w.reshape(r, B_HEADS, B_HEAD_DIM).transpose(0, 2, 1)
    z = jnp.zeros_like(wt)
    return jnp.stack([jnp.stack([wt, z], axis=2), jnp.stack([z, wt], axis=2)]).reshape(2, r, MERGED)


def _both(m):
    return m[0] + m[1]


def _kperm_cols(w):
    r = w.shape[0]
    return w.reshape(r, B_HEADS, NKL, KQ).transpose(0, 2, 3, 1).reshape(r, B_WIDTH)


def _vperm_cols(w):
    r = w.shape[0]
    return w.reshape(r, B_HEADS, B_HEAD_DIM).transpose(0, 2, 1).reshape(r, B_WIDTH)


def _selectors():
    ci = jnp.arange(LANES)[None, :]
    bsel = jnp.arange(2)[:, None, None]
    c_vi, c_h = ci // B_HEADS, ci % B_HEADS
    r2 = jnp.arange(2 * LANES)[:, None]
    selv = ((r2 // BH == c_vi) & ((r2 % BH) // B_HEADS == bsel) & (r2 % B_HEADS == c_h)).astype(BF16)
    r1 = jnp.arange(LANES)[:, None]
    g32 = (r1 % BH == ci % BH).astype(BF16)
    rep = jnp.stack([((r1 // BH == vi) & (r1 % BH == ci % BH)) for vi in range(LANES // BH)]).astype(BF16)
    half = jnp.arange(2)[None, :, None, None]
    merge = ((r1 // (LANES // 2) == half) & ((r1 % (LANES // 2)) // B_HEADS == ci // BH)
             & ((ci % BH) // B_HEADS == bsel[:, None]) & (r1 % B_HEADS == c_h)).astype(BF16)
    return selv, g32, rep, merge


def _merge_kernel(x_ref, h_ref, ya_ref, yb_ref, yc_ref, mod_ref, wg_ref, bg_ref, wb_ref, wo_ref,
                  o_ref, *, nz):
    j = pl.program_id(1)
    h = h_ref[...]
    z = None
    for i, y_ref in enumerate((ya_ref, yb_ref, yc_ref)):
        gate = jax.nn.sigmoid(jnp.dot(h, wg_ref[i], preferred_element_type=F32) + bg_ref[i])
        term = gate * jnp.dot(y_ref[...], wb_ref[i], preferred_element_type=F32)
        z = term if z is None else z + term
    part = jnp.dot(z.astype(BF16), wo_ref[...], preferred_element_type=F32)

    @pl.when(j == 0)
    def _():
        o_ref[...] = part

    @pl.when(j > 0)
    def _():
        o_ref[...] += part

    @pl.when(j == nz - 1)
    def _():
        o_ref[...] = x_ref[...] + mod_ref[5:6, :] * o_ref[...]


def _merge(xs, h, ya, yb, yc, mod, wg, bg, wb, wo, mod_idx, n_tiles):
    d = xs.shape[1]
    tz = 512
    nz = d // tz
    rowt = lambda i, j: (i, 0)
    return pl.pallas_call(
        functools.partial(_merge_kernel, nz=nz),
        grid=(n_tiles, nz),
        in_specs=[
            pl.BlockSpec((TM, d), rowt),
            pl.BlockSpec((TM, d), rowt),
            pl.BlockSpec((TM, A_WIDTH), rowt),
            pl.BlockSpec((TM, B_WIDTH), rowt),
            pl.BlockSpec((TM, C_WIDTH), rowt),
            pl.BlockSpec((None, N_MOD, d), lambda i, j: (mod_idx(i), 0, 0)),
            pl.BlockSpec((N_BRANCH, d, tz), lambda i, j: (0, 0, j)),
            pl.BlockSpec((N_BRANCH, 1, tz), lambda i, j: (0, 0, j)),
            pl.BlockSpec((N_BRANCH, A_WIDTH, tz), lambda i, j: (0, 0, j)),
            pl.BlockSpec((tz, d), lambda i, j: (j, 0)),
        ],
        out_specs=pl.BlockSpec((TM, d), rowt),
        out_shape=jax.ShapeDtypeStruct((n_tiles * TM, d), F32),
        compiler_params=_params(("parallel", "arbitrary")),
        name="merge",
    )(xs, h, ya, yb, yc, mod, wg, bg, wb, wo)


def _rope_tables(b, t, c):
    rows = t // GRID_W
    rowp = jnp.repeat(jnp.arange(rows), GRID_W).astype(F32)
    colp = jnp.tile(jnp.arange(GRID_W), rows).astype(F32)
    inv = 1.0 / (ROPE_BASE ** (jnp.arange(0, ROPE_AXIS_DIM, 2, dtype=F32) / ROPE_AXIS_DIM))
    ar, ac = rowp[:, None] * inv, colp[:, None] * inv
    cr, sr, cc, sc = jnp.cos(ar), jnp.sin(ar), jnp.cos(ac), jnp.sin(ac)
    cos64 = jnp.concatenate([cr, cr, cc, cc], axis=-1)
    sin64 = jnp.concatenate([-sr, sr, -sc, sc], axis=-1)
    cos_t = jnp.tile(cos64, (b, 2))
    sin_t = jnp.tile(sin64, (b, 2))
    cos_t = jnp.concatenate([cos_t, jnp.ones((b * c, LANES), F32)])
    sin_t = jnp.concatenate([sin_t, jnp.zeros((b * c, LANES), F32)])
    return cos_t, sin_t


def kernel(x, c, ctx, c_ctx, w_ada, b_ada, norm_g, ffn_w_in, ffn_w_out, w_in, gm_v_norm, gm_ws, gm_bs,
           rw_conv, rw_w0, rw_w_up, rw_a0, rw_a_up, rw_g_up, rw_k_k, rw_k_a, rw_r_k, rw_ln_g, rw_ln_b,
           da_q_norm, da_k_norm, da_lam, da_subln, w_branch, b_gate, w_out):
    b, t, d = x.shape
    cl = ctx.shape[1]
    depth = w_ada.shape[0]
    d_ff = ffn_w_out.shape[2]
    assert b == 2 and b * cl == TM and t % TM == 0 and cl % 256 == 0
    n_lat = (b * t) // TM
    n_all = n_lat + 1
    tiles_per_batch = t // TM
    mod_idx = lambda i: jnp.where(i < n_lat, i // tiles_per_batch, b)

    xs = jnp.concatenate([x.reshape(b * t, d), ctx.reshape(b * cl, d)])
    cvec = jnp.zeros((SUBLANES, d), F32).at[:b].set(c).at[b].set(c_ctx)
    b_ada3 = b_ada.reshape(depth, 1, N_MOD * d)
    cos_t, sin_t = _rope_tables(b, t, cl)
    lane = jnp.arange(LANES)
    g64 = (lane[:, None] // C_HEAD_DIM == lane[None, :] // C_HEAD_DIM).astype(BF16)
    selv, g32, rep, merge_sel = _selectors()
    w_gu = jnp.transpose(ffn_w_in.reshape(depth, 2, d, 2, d_ff), (0, 1, 3, 2, 4)).astype(BF16)
    w_dn = ffn_w_out.astype(BF16)

    o = 0
    offs = []
    for n in (A_WIDTH, A_WIDTH, 3 * B_WIDTH, GATE_LORA, 2 * DECAY_LORA, 2 * ICL_LORA, 3 * C_WIDTH, N_BRANCH * d):
        offs.append((o, o + n))
        o += n

    for l in range(depth):
        last = l == depth - 1
        lam_init = 0.8 - 0.6 * math.exp(-0.3 * l)
        mod = _ada(cvec, w_ada, b_ada3, l)[:b + 1].reshape(b + 1, N_MOD, d)

        xs, h = _ffn(xs, mod, norm_g[l], w_gu, w_dn, l, 0, mod_idx, n_tiles=n_all, mi=0, emit_h=True)

        wl = w_in[l]
        sl = lambda i: wl[:, offs[i][0]:offs[i][1]]
        n_mix = n_lat if last else n_all

        w_uv = jnp.concatenate([sl(0), sl(1)], axis=1).astype(BF16)
        bsb = jnp.broadcast_to(gm_bs[l][:, :, None], (A_GROUPS, CHUNK, A_WIDTH // A_GROUPS))
        ya = _gmlp(h, w_uv, gm_v_norm[l].reshape(1, A_WIDTH), gm_ws[l], bsb, n_mix)

        w_rkv, w_dec, w_icl = sl(2), sl(4), sl(5)
        pad_lora = lambda w: jnp.pad(w, ((0, 0), (0, LORA_PAD - w.shape[1])))
        w_b = jnp.concatenate(
            [_kperm_cols(w_rkv[:, :B_WIDTH]), _kperm_cols(w_rkv[:, B_WIDTH:2 * B_WIDTH]),
             _vperm_cols(w_rkv[:, 2 * B_WIDTH:]), sl(3),
             pad_lora(w_dec[:, :DECAY_LORA]), pad_lora(w_dec[:, DECAY_LORA:]),
             pad_lora(w_icl[:, :ICL_LORA]), pad_lora(w_icl[:, ICL_LORA:])], axis=1).astype(BF16)
        p = _bmerge(_rkv_proj(h, w_b), merge_sel, t=t, c=cl)
        conv = rw_conv[l]
        conv_m = jnp.concatenate([_both(_kmerge_cols(conv[:, :B_WIDTH])),
                                  _both(_kmerge_cols(conv[:, B_WIDTH:2 * B_WIDTH])),
                                  _both(_vmerge_cols(conv[:, 2 * B_WIDTH:]))], axis=1)
        def up_m(w):
            w32 = jnp.stack([
                _kmerge_cols(jnp.pad(w[dd], ((0, LORA_PAD - w.shape[1]), (0, 0)))).reshape(2 * LORA_PAD, MERGED)
                for dd in range(2)])
            hi = w32.astype(BF16)
            return jnp.stack([hi, (w32 - hi.astype(F32)).astype(BF16)])
        r4, kk4, kd4, b4, ptot, vrep, v2d, bonus = _rwkv_prep(
            p, conv_m, _both(_kmerge_cols(rw_w0[l])), up_m(rw_w_up[l]), _both(_kmerge_cols(rw_a0[l])),
            up_m(rw_a_up[l]), _both(_kmerge_cols(rw_k_k[l].reshape(1, B_WIDTH))),
            _both(_kmerge_cols(rw_k_a[l].reshape(1, B_WIDTH))),
            _both(_kmerge_cols(rw_r_k[l].reshape(1, B_WIDTH))), g32, rep, t=t, c=cl)
        yf, ybk = _scan(kd4, b4, kk4, r4, vrep, ptot, t=t, c=cl)
        yb2 = _rwkv_out(yf, ybk, v2d, bonus, p, _both(_vmerge_cols(rw_ln_g[l].reshape(1, B_WIDTH))),
                        _both(_vmerge_cols(rw_ln_b[l].reshape(1, B_WIDTH))),
                        _vmerge_cols(rw_g_up[l]).reshape(2 * GATE_LORA, MERGED).astype(BF16),
                        selv, g32, rows=t if last else t + cl)
        if last:
            yb = yb2.reshape(b * t, B_WIDTH)
        else:
            yb = jnp.concatenate([yb2[0, :t], yb2[1, :t], yb2[0, t:], yb2[1, t:]])

        gains = jnp.stack([jnp.tile(da_q_norm[l], 2) * (C_HEAD_DIM ** -0.5 * math.log2(math.e)),
                           jnp.tile(da_k_norm[l], 2), jnp.ones((LANES,), F32)]).reshape(3, 1, LANES)
        qk, vx = _qkv(h, sl(6).astype(BF16), gains, cos_t, sin_t, g64)
        sub = da_subln[l].reshape(1, LANES)
        yc = _attn(qk, vx, da_lam[l], sub, None, b=b, t=t, c=cl, lam_init=lam_init, ctx_queries=False)
        if not last:
            yc = _attn(qk, vx, da_lam[l], sub, yc, b=b, t=t, c=cl, lam_init=lam_init, ctx_queries=True)

        wg = jnp.transpose(sl(7).reshape(d, N_BRANCH, d), (1, 0, 2)).astype(BF16)
        wb = w_branch[l]
        wb_b = wb[1].reshape(B_HEADS, B_HEAD_DIM, d).transpose(1, 0, 2).reshape(B_WIDTH, d)
        wbr = jnp.stack([wb[0], wb_b, wb[2]]).astype(BF16)
        xs = _merge(xs, h, ya, yb, yc, mod, wg, b_gate[l].reshape(N_BRANCH, 1, d),
                    wbr, w_out[l].astype(BF16), mod_idx, n_mix)

        xs = _ffn(xs, mod, norm_g[l], w_gu, w_dn, l, 1, mod_idx, n_tiles=n_mix, mi=2, emit_h=False)

    return xs[:b * t].reshape(b, t, d)
```

```python
import functools
import math

import jax
import jax.numpy as jnp
from jax import lax
from jax.experimental import pallas as pl
from jax.experimental.pallas import tpu as pltpu

F32 = jnp.float32
BF16 = jnp.bfloat16

N_MOD = 9
CHUNK = 128
A_WIDTH = 1024
A_GROUPS = 8
B_WIDTH = 1024
B_HEAD_DIM = 64
B_HEADS = 16
DECAY_LORA = 96
ICL_LORA = 96
GATE_LORA = 256
C_HEADS = 8
C_HEAD_DIM = 64
C_WIDTH = 1024
N_BRANCH = 3
GRID_W = 64
ROPE_BASE = 10000.0
ROPE_AXIS_DIM = 32
NORM_EPS = 1e-6
LN_X_EPS = 64e-5

LANES = 128
SUBLANES = 8
VMEM_LIMIT = 56 * 1024 * 1024

TM = 512
TF = 512
LORA_PAD = 128
SCAN_TS = 64
ATT_TQ = 512
ATT_KC = 512
ADA_BANDS = 4


def _params(sem):
    return pltpu.CompilerParams(dimension_semantics=sem, vmem_limit_bytes=VMEM_LIMIT)


def _rms(x, eps):
    return x * lax.rsqrt(jnp.mean(x * x, axis=-1, keepdims=True) + eps)


def _split3(s):
    hi = s.astype(BF16)
    r = s - hi.astype(F32)
    mid = r.astype(BF16)
    lo = (r - mid.astype(F32)).astype(BF16)
    return hi, mid, lo


def _gsum(s, g_bf16):
    out = None
    for piece in _split3(s):
        d = jnp.dot(piece, g_bf16, preferred_element_type=F32)
        out = d if out is None else out + d
    return out


def _gelu_tanh(x):
    cdf = 0.5 * (1.0 + jnp.tanh(math.sqrt(2.0 / math.pi) * (x + 0.044715 * (x * x * x))))
    return x * cdf


def _ada_kernel(c_ref, *refs):
    w_refs, (b_ref, o_ref) = refs[:-2], refs[-2:]
    s = c_ref[...]
    s = s * jax.nn.sigmoid(s)
    pieces = _split3(s)
    band = s.shape[1] // len(w_refs)
    acc = b_ref[...]
    for i, w_ref in enumerate(w_refs):
        w = w_ref[...].astype(BF16)
        for piece in pieces:
            acc = acc + jnp.dot(piece[:, i * band:(i + 1) * band], w, preferred_element_type=F32)
    o_ref[...] = acc


def _ada(cvec, w_ada, b_ada3, l):
    d = cvec.shape[1]
    n = w_ada.shape[2]
    tn = 2048 if n % 2048 == 0 else 1024
    return pl.pallas_call(
        _ada_kernel,
        grid=(n // tn,),
        in_specs=[pl.BlockSpec((SUBLANES, d), lambda j: (0, 0))]
        + [pl.BlockSpec((None, d // ADA_BANDS, tn), lambda j, i=i: (l, i, j)) for i in range(ADA_BANDS)]
        + [pl.BlockSpec((None, 1, tn), lambda j: (l, 0, j))],
        out_specs=pl.BlockSpec((SUBLANES, tn), lambda j: (0, j)),
        out_shape=jax.ShapeDtypeStruct((SUBLANES, n), F32),
        compiler_params=_params(("arbitrary",)),
        name="ada",
    )(cvec, *([w_ada] * ADA_BANDS), b_ada3)


def _ffn_kernel(x_ref, mod_ref, g_ref, wg_ref, *rest, tf, nf, last_valid, mi, emit_h):
    wu_refs, wout_ref, rest = rest[:tf // LANES], rest[tf // LANES], rest[tf // LANES + 1:]
    if emit_h:
        o_ref, h_ref, hn_sc, acc_sc = rest
    else:
        o_ref, hn_sc, acc_sc = rest
    f = pl.program_id(1)

    @pl.when(f == 0)
    def _():
        xn = _rms(x_ref[...], NORM_EPS) * g_ref[mi:mi + 1, :]
        hn = xn * (1.0 + mod_ref[3 * mi + 1:3 * mi + 2, :]) + mod_ref[3 * mi:3 * mi + 1, :]
        hn_sc[...] = hn.astype(BF16)
        acc_sc[...] = jnp.zeros_like(acc_sc)

    def hidden_tile(valid):
        hn = hn_sc[...]
        g = jnp.dot(hn, wg_ref[:, :valid], preferred_element_type=F32)
        wu = jnp.concatenate([r[...] for r in wu_refs[:valid // LANES]], axis=1)
        u = jnp.dot(hn, wu, preferred_element_type=F32)
        act = (g * jax.nn.sigmoid(g) * u).astype(BF16)
        acc_sc[...] += jnp.dot(act, wout_ref[:valid, :], preferred_element_type=F32)

    if last_valid == tf:
        hidden_tile(tf)
    else:
        pl.when(f < nf - 1)(lambda: hidden_tile(tf))
        pl.when(f == nf - 1)(lambda: hidden_tile(last_valid))

    @pl.when(f == nf - 1)
    def _():
        out = x_ref[...] + 0.5 * mod_ref[3 * mi + 2:3 * mi + 3, :] * acc_sc[...]
        o_ref[...] = out
        if emit_h:
            hn = _rms(out, NORM_EPS) * g_ref[1:2, :]
            h_ref[...] = (hn * (1.0 + mod_ref[4:5, :]) + mod_ref[3:4, :]).astype(BF16)


def _ffn(xs, mod, norm_g, w_gu, w_out, l, w, mod_idx, *, n_tiles, mi, emit_h):
    m, d = xs.shape
    d_ff = w_out.shape[2]
    assert d_ff % LANES == 0
    up0 = d_ff // LANES
    nf = pl.cdiv(d_ff, TF)
    out_rows = n_tiles * TM
    out_shape = [jax.ShapeDtypeStruct((out_rows, d), F32)]
    out_specs = [pl.BlockSpec((TM, d), lambda i, f: (i, 0))]
    if emit_h:
        out_shape.append(jax.ShapeDtypeStruct((out_rows, d), BF16))
        out_specs.append(pl.BlockSpec((TM, d), lambda i, f: (i, 0)))
    res = pl.pallas_call(
        functools.partial(_ffn_kernel, tf=TF, nf=nf, last_valid=d_ff - (nf - 1) * TF, mi=mi, emit_h=emit_h),
        grid=(n_tiles, nf),
        in_specs=[
            pl.BlockSpec((TM, d), lambda i, f: (i, 0)),
            pl.BlockSpec((None, N_MOD, d), lambda i, f: (mod_idx(i), 0, 0)),
            pl.BlockSpec((3, d), lambda i, f: (0, 0)),
            pl.BlockSpec((None, None, d, TF), lambda i, f: (l, w, 0, f)),
        ] + [
            pl.BlockSpec((None, None, d, LANES),
                         lambda i, f, k=k: (l, w, 0, jnp.minimum(up0 + f * (TF // LANES) + k, 2 * up0 - 1)))
            for k in range(TF // LANES)
        ] + [
            pl.BlockSpec((None, None, TF, d), lambda i, f: (l, w, f, 0)),
        ],
        out_specs=out_specs,
        out_shape=out_shape,
        scratch_shapes=[pltpu.VMEM((TM, d), BF16), pltpu.VMEM((TM, d), F32)],
        compiler_params=_params(("parallel", "arbitrary")),
        name="ffn",
    )(xs, mod, norm_g, w_gu, *([w_gu] * (TF // LANES)), w_out)
    return res if emit_h else res[0]


def _gmlp_kernel(h_ref, w_ref, vn_ref, ws_ref, bsb_ref, o_ref, p_sc, *, tm):
    p_sc[...] = jnp.dot(h_ref[...], w_ref[...], preferred_element_type=F32)
    gd = A_WIDTH // A_GROUPS
    for g in range(A_GROUPS):
        wsg = ws_ref[g].astype(BF16)
        bias = bsb_ref[g]
        gain = vn_ref[:, g * gd:(g + 1) * gd]
        for n in range(tm // CHUNK):
            rows = slice(n * CHUNK, (n + 1) * CHUNK)
            u = _gelu_tanh(p_sc[rows, g * gd:(g + 1) * gd])
            v = _gelu_tanh(p_sc[rows, A_WIDTH + g * gd:A_WIDTH + (g + 1) * gd])
            v = _rms(v, NORM_EPS) * gain
            sv = jnp.dot(wsg, v.astype(BF16), preferred_element_type=F32) + bias
            o_ref[rows, g * gd:(g + 1) * gd] = (u * sv).astype(o_ref.dtype)


def _gmlp(h, w_uv, vn, ws, bsb, n_tiles):
    m, d = h.shape
    return pl.pallas_call(
        functools.partial(_gmlp_kernel, tm=TM),
        grid=(n_tiles,),
        in_specs=[
            pl.BlockSpec((TM, d), lambda i: (i, 0)),
            pl.BlockSpec((d, 2 * A_WIDTH), lambda i: (0, 0)),
            pl.BlockSpec((1, A_WIDTH), lambda i: (0, 0)),
            pl.BlockSpec((A_GROUPS, CHUNK, CHUNK), lambda i: (0, 0, 0)),
            pl.BlockSpec((A_GROUPS, CHUNK, A_WIDTH // A_GROUPS), lambda i: (0, 0, 0)),
        ],
        out_specs=pl.BlockSpec((TM, A_WIDTH), lambda i: (i, 0)),
        out_shape=jax.ShapeDtypeStruct((n_tiles * TM, A_WIDTH), BF16),
        scratch_shapes=[pltpu.VMEM((TM, 2 * A_WIDTH), F32)],
        compiler_params=_params(("parallel",)),
        name="gmlp",
    )(h, w_uv, vn, ws, bsb)


def _qkv_kernel(h_ref, w_ref, gain_ref, cos_ref, sin_ref, g64_ref, o_ref, vx_ref):
    j = pl.program_id(1)
    n_parts = 2
    heads_per = C_HEADS // n_parts

    def heads():
        for part in range(n_parts):
            cols = slice(part * heads_per * LANES, (part + 1) * heads_per * LANES)
            p = jnp.dot(h_ref[...], w_ref[:, cols], preferred_element_type=F32)
            for i in range(heads_per):
                yield part * heads_per + i, p[:, i * LANES:(i + 1) * LANES]

    @pl.when(j == 2)
    def _():
        ones = jnp.ones((h_ref.shape[0], LANES), BF16)
        for hs, x in heads():
            vx_ref[:, 2 * hs * LANES:(2 * hs + 1) * LANES] = x.astype(BF16)
            vx_ref[:, (2 * hs + 1) * LANES:(2 * hs + 2) * LANES] = ones

    @pl.when(j < 2)
    def _():
        lane = lax.broadcasted_iota(jnp.int32, (1, LANES), 1)
        first = (lane % ROPE_AXIS_DIM) < (ROPE_AXIS_DIM // 2)
        cos = cos_ref[...]
        sin = sin_ref[...]
        gain = gain_ref[...]
        g64 = g64_ref[...]
        for hs, x in heads():
            ms = _gsum(x * x, g64) * (1.0 / C_HEAD_DIM)
            y = x * lax.rsqrt(ms + NORM_EPS) * gain
            half = ROPE_AXIS_DIM // 2
            rot = jnp.where(first, pltpu.roll(y, LANES - half, 1), pltpu.roll(y, half, 1))
            o_ref[:, hs * LANES:(hs + 1) * LANES] = (y * cos + rot * sin).astype(BF16)


def _qkv(h, w_qkv, gains, cos_t, sin_t, g64):
    m, d = h.shape
    return pl.pallas_call(
        _qkv_kernel,
        grid=(m // TM, 3),
        in_specs=[
            pl.BlockSpec((TM, d), lambda i, j: (i, 0)),
            pl.BlockSpec((d, C_WIDTH), lambda i, j: (0, j)),
            pl.BlockSpec((None, 1, LANES), lambda i, j: (j, 0, 0)),
            pl.BlockSpec((TM, LANES), lambda i, j: (i, 0)),
            pl.BlockSpec((TM, LANES), lambda i, j: (i, 0)),
            pl.BlockSpec((LANES, LANES), lambda i, j: (0, 0)),
        ],
        out_specs=[pl.BlockSpec((None, TM, C_WIDTH), lambda i, j: (jnp.minimum(j, 1), i, 0)),
                   pl.BlockSpec((TM, 2 * C_WIDTH), lambda i, j: (i, 0))],
        out_shape=[jax.ShapeDtypeStruct((2, m, C_WIDTH), BF16), jax.ShapeDtypeStruct((m, 2 * C_WIDTH), BF16)],
        compiler_params=_params(("parallel", "arbitrary")),
        name="qkv",
    )(h, w_qkv, gains, cos_t, sin_t, g64)


def _attn_kernel(lam_ref, sub_ref, q_ref, *refs, nseg, lam_init):
    kv = refs[:2 * nseg]
    o_ref = refs[2 * nseg]
    lv = lam_ref[...]
    lam = (jnp.exp(jnp.sum(lv[0:1] * lv[1:2], axis=-1, keepdims=True))
           - jnp.exp(jnp.sum(lv[2:3] * lv[3:4], axis=-1, keepdims=True)) + lam_init)
    q = q_ref[...]
    lane = lax.broadcasted_iota(jnp.int32, (1, LANES), 1)
    zero = jnp.zeros_like(q)
    outs = []
    for j in range(2):
        in_map = (lane < C_HEAD_DIM) if j == 0 else (lane >= C_HEAD_DIM)
        qj = jnp.where(in_map, q, zero)
        m = None
        acc = None
        for si in range(nseg):
            k_ref, v_ref = kv[2 * si], kv[2 * si + 1]
            ls = k_ref.shape[0]
            for c0 in range(0, ls, ATT_KC):
                kc = min(ATT_KC, ls - c0)
                s = lax.dot_general(qj, k_ref[c0:c0 + kc, :], (((1,), (1,)), ((), ())),
                                    preferred_element_type=F32)
                cm = jnp.max(s, axis=-1, keepdims=True)
                m_new = cm if m is None else jnp.maximum(m, cm)
                e = jnp.exp2((s - m_new).astype(BF16))
                pv = jnp.dot(e, v_ref[c0:c0 + kc, :], preferred_element_type=F32)
                acc = pv if acc is None else acc * jnp.exp2(m - m_new) + pv
                m = m_new
        outs.append(acc[:, :LANES] / acc[:, LANES:])
    o = outs[0] - lam * outs[1]
    o = _rms(o, NORM_EPS) * sub_ref[...]
    o_ref[...] = (o * (1.0 - lam_init)).astype(o_ref.dtype)


def _attn(qk, vx, lam, subln, yc_prev, *, b, t, c, lam_init, ctx_queries):
    m = qk.shape[1]
    if ctx_queries:
        tq, nq = c, 1
        q_blk = lambda bi, qi: (b * t) // c + bi
        segs = [(c, lambda bi: (b * t) // c + bi)]
    else:
        tq = ATT_TQ
        nq = t // tq
        q_blk = lambda bi, qi: bi * nq + qi
        segs = [(t, lambda bi: bi), (c, lambda bi: (b * t) // c + bi)]
    in_specs = [
        pl.BlockSpec((4, C_HEAD_DIM), lambda bi, h, qi: (0, 0)),
        pl.BlockSpec((1, LANES), lambda bi, h, qi: (0, 0)),
        pl.BlockSpec((None, tq, LANES), lambda bi, h, qi: (0, q_blk(bi, qi), h)),
    ]
    args = [lam, subln, qk]
    for ls, blk in segs:
        in_specs.append(pl.BlockSpec((None, ls, LANES), lambda bi, h, qi, blk=blk: (1, blk(bi), h)))
        in_specs.append(pl.BlockSpec((ls, 2 * LANES), lambda bi, h, qi, blk=blk: (blk(bi), h)))
        args += [qk, vx]
    aliases = {}
    if yc_prev is not None:
        in_specs.append(pl.BlockSpec(memory_space=pl.ANY))
        args.append(yc_prev)
        aliases = {len(args) - 1: 0}

    def body(*refs):
        n_in = 3 + 2 * len(segs)
        _attn_kernel(*refs[:n_in], refs[-1], nseg=len(segs), lam_init=lam_init)

    return pl.pallas_call(
        body,
        grid=(b, C_HEADS, nq),
        in_specs=in_specs,
        out_specs=pl.BlockSpec((tq, LANES), lambda bi, h, qi: (q_blk(bi, qi), h)),
        out_shape=jax.ShapeDtypeStruct((m, C_WIDTH), BF16),
        input_output_aliases=aliases,
        compiler_params=_params(("parallel", "parallel", "arbitrary")),
        name="diff_attn_ctx" if ctx_queries else "diff_attn",
    )(*args)


KQ = 4
NKL = B_HEAD_DIM // KQ
MERGED = 2 * B_WIDTH
BH = 2 * B_HEADS
LORA_MERGED = 2 * GATE_LORA + 8 * LORA_PAD
PROJ_MERGED = 3 * MERGED + LORA_MERGED


PROJ_UNMERGED = 3 * B_WIDTH + GATE_LORA + 4 * LORA_PAD


def _mm_kernel(a_ref, w_ref, o_ref):
    o_ref[...] = jnp.dot(a_ref[...], w_ref[...], preferred_element_type=F32)


def _rkv_proj(h, w):
    m, d = h.shape
    tn = PROJ_UNMERGED // 3
    return pl.pallas_call(
        _mm_kernel,
        grid=(PROJ_UNMERGED // tn, m // TM),
        in_specs=[pl.BlockSpec((TM, d), lambda j, i: (i, 0)),
                  pl.BlockSpec((d, tn), lambda j, i: (0, j))],
        out_specs=pl.BlockSpec((TM, tn), lambda j, i: (i, j)),
        out_shape=jax.ShapeDtypeStruct((m, PROJ_UNMERGED), F32),
        compiler_params=_params(("parallel", "arbitrary")),
        name="rkv_proj",
    )(h, w)


def _bmerge_kernel(p0_ref, p1_ref, e_ref, o_ref):
    n_kv = 3 * B_WIDTH // LANES
    for bi, p_ref in enumerate((p0_ref, p1_ref)):
        for s in range(n_kv):
            x = p_ref[:, s * LANES:(s + 1) * LANES]
            hi = x.astype(BF16)
            lo = (x - hi.astype(F32)).astype(BF16)
            for half in range(2):
                e = e_ref[bi, half]
                part = (jnp.dot(hi, e, preferred_element_type=F32) + jnp.dot(lo, e, preferred_element_type=F32))
                cs = slice((2 * s + half) * LANES, (2 * s + half + 1) * LANES)
                if bi == 0:
                    o_ref[:, cs] = part
                else:
                    o_ref[:, cs] += part
        src = 3 * B_WIDTH
        dst = 3 * MERGED
        for width in (GATE_LORA,) + (LORA_PAD,) * 4:
            o_ref[:, dst + bi * width:dst + (bi + 1) * width] = p_ref[:, src:src + width]
            src += width
            dst += 2 * width


def _bmerge(p, e, *, t, c):
    tm = 256
    s = t + c
    nlat = t // tm
    hblk = lambda bi, i: jnp.where(i < nlat, bi * nlat + i, (2 * t + bi * c) // tm + (i - nlat))
    return pl.pallas_call(
        _bmerge_kernel,
        grid=(s // tm,),
        in_specs=[
            pl.BlockSpec((tm, PROJ_UNMERGED), lambda i: (hblk(0, i), 0)),
            pl.BlockSpec((tm, PROJ_UNMERGED), lambda i: (hblk(1, i), 0)),
            pl.BlockSpec((2, 2, LANES, LANES), lambda i: (0, 0, 0, 0)),
        ],
        out_specs=pl.BlockSpec((tm, PROJ_MERGED), lambda i: (i, 0)),
        out_shape=jax.ShapeDtypeStruct((s, PROJ_MERGED), F32),
        compiler_params=_params(("parallel",)),
        name="batch_merge",
    )(p, p, e)


class _SlabWriter:
    def __init__(self, ref, lead=()):
        self.ref, self.lead, self.parts, self.base = ref, lead, [], 0

    def add(self, slab):
        self.parts.append(slab)
        if len(self.parts) == SUBLANES:
            block = jnp.swapaxes(jnp.stack(self.parts, axis=0), 0, 1)
            self.ref[self.lead + (slice(None), slice(self.base, self.base + SUBLANES), slice(None))] = block
            self.parts, self.base = [], self.base + SUBLANES


def _rwkv_prep_kernel(p_ref, prev_ref, next_ref, lo_ref, conv_ref, w0_ref, wup_ref, a0_ref, aup_ref,
                      kkg_ref, ka_ref, rk_ref, g32_ref, rep_ref,
                      r_o, kk_o, kd_o, b_o, ptot_o, vrep_o, v_o, bonus_o, k_sc, r_sc, *, tm, ts, starts, ends):
    i = pl.program_id(0)
    is_start = functools.reduce(jnp.logical_or, [i == s for s in starts])
    is_end = functools.reduce(jnp.logical_or, [i == s for s in ends])
    row = lax.broadcasted_iota(jnp.int32, (tm, 1), 0)

    def conv(c0):
        cs = slice(c0, c0 + LANES)
        x = p_ref[:, cs]
        pm = jnp.where(is_start, 0.0, prev_ref[SUBLANES - 1:SUBLANES, cs])
        nx = jnp.where(is_end, 0.0, next_ref[0:1, cs])
        xm1 = jnp.where(row == 0, pm, pltpu.roll(x, 1, 0))
        xp1 = jnp.where(row == tm - 1, nx, pltpu.roll(x, tm - 1, 0))
        return xm1 * conv_ref[0:1, cs] + x * conv_ref[1:2, cs] + xp1 * conv_ref[2:3, cs]

    vrep_out = _SlabWriter(vrep_o)
    r_out, kk_out, kd_out, b_out = ([_SlabWriter(o, (d,)) for d in range(2)] for o in (r_o, kk_o, kd_o, b_o))

    ss = None
    for m in range(NKL):
        cs = slice(m * LANES, (m + 1) * LANES)
        r_sc[:, cs] = conv(m * LANES)
        k = conv(MERGED + m * LANES)
        k_sc[:, cs] = k
        kkr = k * kkg_ref[:, cs]
        ss = kkr * kkr if ss is None else ss + kkr * kkr
    rs = lax.rsqrt(_gsum(ss, g32_ref[...]) + 1e-12)

    w_off = 2 * GATE_LORA
    a_off = w_off + 4 * LORA_PAD
    def split2(x):
        hi = x.astype(BF16)
        return hi, (x - hi.astype(F32)).astype(BF16)

    def lora(x2, up_ref, d, cs):
        return (jnp.dot(x2[0], up_ref[0, d, :, cs], preferred_element_type=F32)
                + jnp.dot(x2[1], up_ref[0, d, :, cs], preferred_element_type=F32)
                + jnp.dot(x2[0], up_ref[1, d, :, cs], preferred_element_type=F32))

    tw = [split2(jnp.tanh(lo_ref[:, w_off + 2 * d * LORA_PAD:w_off + 2 * (d + 1) * LORA_PAD])) for d in range(2)]
    pa = [split2(lo_ref[:, a_off + 2 * d * LORA_PAD:a_off + 2 * (d + 1) * LORA_PAD]) for d in range(2)]
    bonus = None
    for m in range(NKL):
        cs = slice(m * LANES, (m + 1) * LANES)
        k = k_sc[:, cs]
        kk = k * kkg_ref[:, cs] * rs
        r = r_sc[:, cs]
        kd_sum = None
        for d in range(2):
            w_lo = w0_ref[d:d + 1, cs] + lora(tw[d], wup_ref, d, cs)
            lw = -jax.nn.sigmoid(w_lo) * math.exp(-0.5)
            cum = lw
            shift = 1
            while shift < ts:
                if d == 0:
                    cum = cum + jnp.where(row % ts >= shift, pltpu.roll(cum, shift, 0), 0.0)
                else:
                    cum = cum + jnp.where(row % ts < ts - shift, pltpu.roll(cum, tm - shift, 0), 0.0)
                shift *= 2
            p_incl = jnp.exp(cum)
            p_inv = jnp.exp(-cum)
            p_excl = jnp.exp(cum - lw)
            a = jax.nn.sigmoid(a0_ref[d:d + 1, cs] + lora(pa[d], aup_ref, d, cs))
            kd = k * (1.0 + (a - 1.0) * ka_ref[:, cs])
            kd_sum = kd if kd_sum is None else kd_sum + kd
            r_out[d].add(r * p_incl)
            kk_out[d].add(kk * p_excl)
            kd_out[d].add(kd * p_inv)
            b_out[d].add(kk * a * p_inv)
            for g in range(tm // ts):
                last = g * ts + (ts - 1 if d == 0 else 0)
                ptot_o[d, g, m:m + 1, :] = p_incl[last:last + 1, :]
        term = r * (0.5 * kd_sum) * rk_ref[:, cs]
        bonus = term if bonus is None else bonus + term
    bonus_o[...] = bonus

    per_slab = LANES // BH
    for j in range(MERGED // LANES):
        vs = conv(2 * MERGED + j * LANES)
        v_o[:, j * LANES:(j + 1) * LANES] = vs
        pieces = _split3(vs)
        for vi in range(per_slab):
            vrep_out.add(sum(jnp.dot(piece, rep_ref[vi], preferred_element_type=F32) for piece in pieces))


def _rwkv_prep(p, conv_m, w0_m, wup_m, a0_m, aup_m, kk_m, ka_m, rk_m, g32, rep, *, t, c):
    s = t + c
    tm = 128
    nt = s // tm
    starts = [0, t // tm]
    ends = [t // tm - 1, nt - 1]
    hb = tm // SUBLANES
    rkv_w = 3 * MERGED
    full2 = lambda i: (0, 0)
    full3 = lambda i: (0, 0, 0)
    ts = SCAN_TS
    k4 = jax.ShapeDtypeStruct((2, s, NKL, LANES), F32)
    k4s = pl.BlockSpec((2, tm, NKL, LANES), lambda i: (0, i, 0, 0))
    return pl.pallas_call(
        functools.partial(_rwkv_prep_kernel, tm=tm, ts=ts, starts=starts, ends=ends),
        grid=(nt,),
        in_specs=[
            pl.BlockSpec((tm, rkv_w), lambda i: (i, 0)),
            pl.BlockSpec((SUBLANES, rkv_w), lambda i: (jnp.maximum(i * hb - 1, 0), 0)),
            pl.BlockSpec((SUBLANES, rkv_w), lambda i: (jnp.minimum((i + 1) * hb, s // SUBLANES - 1), 0)),
            pl.BlockSpec((tm, LORA_MERGED), lambda i: (i, rkv_w // LORA_MERGED)),
            pl.BlockSpec((3, rkv_w), full2),
            pl.BlockSpec((2, MERGED), full2),
            pl.BlockSpec((2, 2, 2 * LORA_PAD, MERGED), lambda i: (0, 0, 0, 0)),
            pl.BlockSpec((2, MERGED), full2),
            pl.BlockSpec((2, 2, 2 * LORA_PAD, MERGED), lambda i: (0, 0, 0, 0)),
            pl.BlockSpec((1, MERGED), full2),
            pl.BlockSpec((1, MERGED), full2),
            pl.BlockSpec((1, MERGED), full2),
            pl.BlockSpec((LANES, LANES), full2),
            pl.BlockSpec((LANES // BH, LANES, LANES), full3),
        ],
        out_specs=[k4s, k4s, k4s, k4s,
                   pl.BlockSpec((2, tm // ts, NKL, LANES), lambda i: (0, i, 0, 0)),
                   pl.BlockSpec((tm, B_HEAD_DIM, LANES), lambda i: (i, 0, 0)),
                   pl.BlockSpec((tm, MERGED), lambda i: (i, 0)),
                   pl.BlockSpec((tm, LANES), lambda i: (i, 0))],
        out_shape=[k4, k4, k4, k4, jax.ShapeDtypeStruct((2, s // ts, NKL, LANES), F32),
                   jax.ShapeDtypeStruct((s, B_HEAD_DIM, LANES), F32),
                   jax.ShapeDtypeStruct((s, MERGED), F32), jax.ShapeDtypeStruct((s, LANES), F32)],
        scratch_shapes=[pltpu.VMEM((tm, MERGED), F32), pltpu.VMEM((tm, MERGED), F32)],
        compiler_params=_params(("parallel",)),
        name="rwkv_prep",
    )(p, p, p, p, conv_m, w0_m, wup_m, a0_m, aup_m, kk_m, ka_m, rk_m, g32, rep)


def _scan_kernel(*refs, ts):
    ins, (yf_ref, yb_ref, s_sc, y_sc, sa_sc) = refs[:12], refs[12:]
    streams = [ins[:6] + (yf_ref,), ins[6:] + (yb_ref,)]
    nkl = s_sc.shape[1]
    nvb = s_sc.shape[2] // SUBLANES

    @pl.when(pl.program_id(0) == 0)
    def _():
        s_sc[...] = jnp.zeros_like(s_sc)
        y_sc[...] = jnp.zeros_like(y_sc)

    def vsl(vb):
        return slice(vb * SUBLANES, (vb + 1) * SUBLANES)

    diag = (lax.broadcasted_iota(jnp.int32, (SUBLANES, LANES), 1) // BH
            == lax.broadcasted_iota(jnp.int32, (SUBLANES, LANES), 0) % KQ)

    def emit_y(d, y_ref, row):
        for vb in range(nvb):
            y = y_sc[d, vsl(vb), :]
            y = y + pltpu.roll(y, BH, 1)
            y = y + pltpu.roll(y, 2 * BH, 1)
            y = jnp.where(diag, y, 0.0)
            y = y + pltpu.roll(y, 1, 0)
            y = y + pltpu.roll(y, 2, 0)
            for half in range(SUBLANES // KQ):
                srow = half * KQ + KQ - 1
                y_ref[row, 2 * vb + half:2 * vb + half + 1, :] = y[srow:srow + 1, :]

    def allreduce(x):
        x = x + pltpu.roll(x, BH, 1)
        return x + pltpu.roll(x, 2 * BH, 1)

    def step(j, carry):
        ts_of = (j, ts - 1 - j)
        prev = (jnp.maximum(j - 1, 0), jnp.minimum(ts - j, ts - 1))
        for d in range(2):
            emit_y(d, streams[d][-1], prev[d])
        nxt = (jnp.minimum(j + 1, ts - 1), jnp.maximum(ts - 2 - j, 0))
        for d, (kd_ref, b_ref, kk_ref, r_ref, v_ref, ptot_ref, y_ref) in enumerate(streams):
            t = ts_of[d]
            sa = [sa_sc[d, vsl(vb), :] if d == 0 else allreduce(sa_sc[d, vsl(vb), :]) for vb in range(nvb)]
            ys = [None] * nvb
            sa_next = [None] * nvb
            for kl in range(nkl):
                brow = b_ref[t, kl:kl + 1, :]
                kdrow = kd_ref[t, kl:kl + 1, :]
                rrow = r_ref[t, kl:kl + 1, :]
                kknext = kk_ref[nxt[d], kl:kl + 1, :]
                for vb in range(nvb):
                    sn = s_sc[d, kl, vsl(vb), :] - sa[vb] * brow + v_ref[t, vsl(vb), :] * kdrow
                    s_sc[d, kl, vsl(vb), :] = sn
                    term = sn * rrow
                    ys[vb] = term if ys[vb] is None else ys[vb] + term
                    term = sn * kknext
                    sa_next[vb] = term if sa_next[vb] is None else sa_next[vb] + term
            for vb in range(nvb):
                y_sc[d, vsl(vb), :] = ys[vb]
                sa_sc[d, vsl(vb), :] = allreduce(sa_next[vb]) if d == 0 else sa_next[vb]
        return carry

    for d, first in enumerate((0, ts - 1)):
        kk_ref = streams[d][2]
        for vb in range(nvb):
            acc = None
            for kl in range(nkl):
                term = s_sc[d, kl, vsl(vb), :] * kk_ref[first, kl:kl + 1, :]
                acc = term if acc is None else acc + term
            sa_sc[d, vsl(vb), :] = allreduce(acc) if d == 0 else acc

    lax.fori_loop(0, ts, step, 0)
    emit_y(0, yf_ref, ts - 1)
    emit_y(1, yb_ref, 0)
    for d in range(2):
        ptot_ref = streams[d][5]
        for kl in range(nkl):
            prow = ptot_ref[0, kl:kl + 1, :]
            for vb in range(nvb):
                s_sc[d, kl, vsl(vb), :] = s_sc[d, kl, vsl(vb), :] * prow


def _scan(kd4, b4, kk4, r4, vrep, ptot, *, t, c):
    s = t + c
    ts = SCAN_TS
    nctx = c // ts
    nlat = t // ts
    fblk = lambda g: jnp.where(g < nctx, nlat + g, g - nctx)
    bblk = lambda g: jnp.where(g < nctx, nlat + (nctx - 1 - g), nlat - 1 - (g - nctx))
    in_specs = []
    args = []
    for d, blk in enumerate((fblk, bblk)):
        for a in (kd4, b4, kk4, r4):
            in_specs.append(pl.BlockSpec((None, ts, NKL, LANES), lambda g, d=d, blk=blk: (d, blk(g), 0, 0)))
            args.append(a)
        in_specs.append(pl.BlockSpec((ts, B_HEAD_DIM, LANES), lambda g, blk=blk: (blk(g), 0, 0)))
        args.append(vrep)
        in_specs.append(pl.BlockSpec((None, 1, NKL, LANES), lambda g, d=d, blk=blk: (d, blk(g), 0, 0)))
        args.append(ptot)
    nslab = MERGED // LANES
    yshape = jax.ShapeDtypeStruct((s, nslab, LANES), F32)
    return pl.pallas_call(
        functools.partial(_scan_kernel, ts=ts),
        grid=(s // ts,),
        in_specs=in_specs,
        out_specs=[pl.BlockSpec((ts, nslab, LANES), lambda g: (fblk(g), 0, 0)),
                   pl.BlockSpec((ts, nslab, LANES), lambda g: (bblk(g), 0, 0))],
        out_shape=[yshape, yshape],
        scratch_shapes=[pltpu.VMEM((2, NKL, B_HEAD_DIM, LANES), F32),
                        pltpu.VMEM((2, B_HEAD_DIM, LANES), F32),
                        pltpu.VMEM((2, B_HEAD_DIM, LANES), F32)],
        compiler_params=_params(("arbitrary",)),
        name="wkv7_scan",
    )(*args)


def _rwkv_out_kernel(yf_ref, yb_ref, v_ref, bonus_ref, pg_ref, lng_ref, lnb_ref, gup_ref,
                     selv_ref, g32_ref, o_ref):
    g32 = g32_ref[...]
    gate = jnp.dot(jax.nn.sigmoid(pg_ref[...]).astype(BF16), gup_ref[...], preferred_element_type=F32)
    bonus = _gsum(bonus_ref[...], g32)
    nslab = MERGED // LANES
    ys = []
    for g in range(nslab // SUBLANES):
        rows = slice(g * SUBLANES, (g + 1) * SUBLANES)
        both = jnp.swapaxes(yf_ref[:, rows, :] + yb_ref[:, rows, :], 0, 1)
        ys += [both[j] for j in range(SUBLANES)]
    inv = 1.0 / B_HEAD_DIM
    mu = _gsum(functools.reduce(lambda a, b_: a + b_, ys), g32) * inv
    ds = [y - mu for y in ys]
    var = _gsum(functools.reduce(lambda a, b_: a + b_, [x * x for x in ds]), g32) * inv
    rstd = lax.rsqrt(var + LN_X_EPS)
    outs = []
    for j in range(nslab):
        cs = slice(j * LANES, (j + 1) * LANES)
        yn = ds[j] * rstd * lng_ref[:, cs] + lnb_ref[:, cs]
        outs.append(((yn + bonus * v_ref[:, cs]) * gate[:, cs]).astype(BF16))
    for j in range(nslab // 2):
        pair = jnp.concatenate([outs[2 * j], outs[2 * j + 1]], axis=1)
        for bi in range(2):
            o_ref[bi, :, j * LANES:(j + 1) * LANES] = jnp.dot(
                pair, selv_ref[bi], preferred_element_type=F32).astype(o_ref.dtype)


def _rwkv_out(yf, yb, v2d, bonus, p, lng_m, lnb_m, gup_m, selv, g32, *, rows):
    tm = 256
    nslab = MERGED // LANES
    y3s = pl.BlockSpec((tm, nslab, LANES), lambda i: (i, 0, 0))
    full2 = lambda i: (0, 0)
    return pl.pallas_call(
        _rwkv_out_kernel,
        grid=(rows // tm,),
        in_specs=[
            y3s, y3s,
            pl.BlockSpec((tm, MERGED), lambda i: (i, 0)),
            pl.BlockSpec((tm, LANES), lambda i: (i, 0)),
            pl.BlockSpec((tm, 2 * GATE_LORA), lambda i: (i, 3 * MERGED // (2 * GATE_LORA))),
            pl.BlockSpec((1, MERGED), full2),
            pl.BlockSpec((1, MERGED), full2),
            pl.BlockSpec((2 * GATE_LORA, MERGED), full2),
            pl.BlockSpec((2, 2 * LANES, LANES), lambda i: (0, 0, 0)),
            pl.BlockSpec((LANES, LANES), full2),
        ],
        out_specs=pl.BlockSpec((2, tm, B_WIDTH), lambda i: (0, i, 0)),
        out_shape=jax.ShapeDtypeStruct((2, rows, B_WIDTH), BF16),
        compiler_params=_params(("parallel",)),
        name="rwkv_out",
    )(yf, yb, v2d, bonus, p, lng_m, lnb_m, gup_m, selv, g32)


def _kmerge_cols(w):
    r = w.shape[0]
    wt = w.reshape(r, B_HEADS, NKL, KQ).transpose(0, 2, 3, 1)
    z = jnp.zeros_like(wt)
    return jnp.stack([jnp.stack([wt, z], axis=3), jnp.stack([z, wt], axis=3)]).reshape(2, r, MERGED)


def _vmerge_cols(w):
    r = w.shape[0]
    wt = w.reshape(r, B_HEADS, B_HEAD_DIM).transpose(0, 2, 1)
    z = jnp.zeros_like(wt)
    return jnp.stack([jnp.stack([wt, z], axis=2), jnp.stack([z, wt], axis=2)]).reshape(2, r, MERGED)


def _both(m):
    return m[0] + m[1]


def _kperm_cols(w):
    r = w.shape[0]
    return w.reshape(r, B_HEADS, NKL, KQ).transpose(0, 2, 3, 1).reshape(r, B_WIDTH)


def _vperm_cols(w):
    r = w.shape[0]
    return w.reshape(r, B_HEADS, B_HEAD_DIM).transpose(0, 2, 1).reshape(r, B_WIDTH)


def _selectors():
    ci = jnp.arange(LANES)[None, :]
    bsel = jnp.arange(2)[:, None, None]
    c_vi, c_h = ci // B_HEADS, ci % B_HEADS
    r2 = jnp.arange(2 * LANES)[:, None]
    selv = ((r2 // BH == c_vi) & ((r2 % BH) // B_HEADS == bsel) & (r2 % B_HEADS == c_h)).astype(BF16)
    r1 = jnp.arange(LANES)[:, None]
    g32 = (r1 % BH == ci % BH).astype(BF16)
    rep = jnp.stack([((r1 // BH == vi) & (r1 % BH == ci % BH)) for vi in range(LANES // BH)]).astype(BF16)
    half = jnp.arange(2)[None, :, None, None]
    merge = ((r1 // (LANES // 2) == half) & ((r1 % (LANES // 2)) // B_HEADS == ci // BH)
             & ((ci % BH) // B_HEADS == bsel[:, None]) & (r1 % B_HEADS == c_h)).astype(BF16)
    return selv, g32, rep, merge


def _merge_kernel(x_ref, h_ref, ya_ref, yb_ref, yc_ref, mod_ref, wg_ref, bg_ref, wb_ref, wo_ref,
                  o_ref, *, nz):
    j = pl.program_id(1)
    h = h_ref[...]
    z = None
    for i, y_ref in enumerate((ya_ref, yb_ref, yc_ref)):
        gate = jax.nn.sigmoid(jnp.dot(h, wg_ref[i], preferred_element_type=F32) + bg_ref[i])
        term = gate * jnp.dot(y_ref[...], wb_ref[i], preferred_element_type=F32)
        z = term if z is None else z + term
    part = jnp.dot(z.astype(BF16), wo_ref[...], preferred_element_type=F32)

    @pl.when(j == 0)
    def _():
        o_ref[...] = part

    @pl.when(j > 0)
    def _():
        o_ref[...] += part

    @pl.when(j == nz - 1)
    def _():
        o_ref[...] = x_ref[...] + mod_ref[5:6, :] * o_ref[...]


def _merge(xs, h, ya, yb, yc, mod, wg, bg, wb, wo, mod_idx, n_tiles):
    d = xs.shape[1]
    tz = 512
    nz = d // tz
    rowt = lambda i, j: (i, 0)
    return pl.pallas_call(
        functools.partial(_merge_kernel, nz=nz),
        grid=(n_tiles, nz),
        in_specs=[
            pl.BlockSpec((TM, d), rowt),
            pl.BlockSpec((TM, d), rowt),
            pl.BlockSpec((TM, A_WIDTH), rowt),
            pl.BlockSpec((TM, B_WIDTH), rowt),
            pl.BlockSpec((TM, C_WIDTH), rowt),
            pl.BlockSpec((None, N_MOD, d), lambda i, j: (mod_idx(i), 0, 0)),
            pl.BlockSpec((N_BRANCH, d, tz), lambda i, j: (0, 0, j)),
            pl.BlockSpec((N_BRANCH, 1, tz), lambda i, j: (0, 0, j)),
            pl.BlockSpec((N_BRANCH, A_WIDTH, tz), lambda i, j: (0, 0, j)),
            pl.BlockSpec((tz, d), lambda i, j: (j, 0)),
        ],
        out_specs=pl.BlockSpec((TM, d), rowt),
        out_shape=jax.ShapeDtypeStruct((n_tiles * TM, d), F32),
        compiler_params=_params(("parallel", "arbitrary")),
        name="merge",
    )(xs, h, ya, yb, yc, mod, wg, bg, wb, wo)


def _rope_tables(b, t, c):
    rows = t // GRID_W
    rowp = jnp.repeat(jnp.arange(rows), GRID_W).astype(F32)
    colp = jnp.tile(jnp.arange(GRID_W), rows).astype(F32)
    inv = 1.0 / (ROPE_BASE ** (jnp.arange(0, ROPE_AXIS_DIM, 2, dtype=F32) / ROPE_AXIS_DIM))
    ar, ac = rowp[:, None] * inv, colp[:, None] * inv
    cr, sr, cc, sc = jnp.cos(ar), jnp.sin(ar), jnp.cos(ac), jnp.sin(ac)
    cos64 = jnp.concatenate([cr, cr, cc, cc], axis=-1)
    sin64 = jnp.concatenate([-sr, sr, -sc, sc], axis=-1)
    cos_t = jnp.tile(cos64, (b, 2))
    sin_t = jnp.tile(sin64, (b, 2))
    cos_t = jnp.concatenate([cos_t, jnp.ones((b * c, LANES), F32)])
    sin_t = jnp.concatenate([sin_t, jnp.zeros((b * c, LANES), F32)])
    return cos_t, sin_t


def kernel(x, c, ctx, c_ctx, w_ada, b_ada, norm_g, ffn_w_in, ffn_w_out, w_in, gm_v_norm, gm_ws, gm_bs,
           rw_conv, rw_w0, rw_w_up, rw_a0, rw_a_up, rw_g_up, rw_k_k, rw_k_a, rw_r_k, rw_ln_g, rw_ln_b,
           da_q_norm, da_k_norm, da_lam, da_subln, w_branch, b_gate, w_out):
    b, t, d = x.shape
    cl = ctx.shape[1]
    depth = w_ada.shape[0]
    d_ff = ffn_w_out.shape[2]
    assert b == 2 and b * cl == TM and t % TM == 0 and cl % 256 == 0
    n_lat = (b * t) // TM
    n_all = n_lat + 1
    tiles_per_batch = t // TM
    mod_idx = lambda i: jnp.where(i < n_lat, i // tiles_per_batch, b)

    xs = jnp.concatenate([x.reshape(b * t, d), ctx.reshape(b * cl, d)])
    cvec = jnp.zeros((SUBLANES, d), F32).at[:b].set(c).at[b].set(c_ctx)
    b_ada3 = b_ada.reshape(depth, 1, N_MOD * d)
    cos_t, sin_t = _rope_tables(b, t, cl)
    lane = jnp.arange(LANES)
    g64 = (lane[:, None] // C_HEAD_DIM == lane[None, :] // C_HEAD_DIM).astype(BF16)
    selv, g32, rep, merge_sel = _selectors()
    w_gu = ffn_w_in.astype(BF16)
    w_dn = ffn_w_out.astype(BF16)

    o = 0
    offs = []
    for n in (A_WIDTH, A_WIDTH, 3 * B_WIDTH, GATE_LORA, 2 * DECAY_LORA, 2 * ICL_LORA, 3 * C_WIDTH, N_BRANCH * d):
        offs.append((o, o + n))
        o += n

    for l in range(depth):
        last = l == depth - 1
        lam_init = 0.8 - 0.6 * math.exp(-0.3 * l)
        mod = _ada(cvec, w_ada, b_ada3, l)[:b + 1].reshape(b + 1, N_MOD, d)

        xs, h = _ffn(xs, mod, norm_g[l], w_gu, w_dn, l, 0, mod_idx, n_tiles=n_all, mi=0, emit_h=True)

        wl = w_in[l]
        sl = lambda i: wl[:, offs[i][0]:offs[i][1]]
        n_mix = n_lat if last else n_all

        w_uv = jnp.concatenate([sl(0), sl(1)], axis=1).astype(BF16)
        bsb = jnp.broadcast_to(gm_bs[l][:, :, None], (A_GROUPS, CHUNK, A_WIDTH // A_GROUPS))
        ya = _gmlp(h, w_uv, gm_v_norm[l].reshape(1, A_WIDTH), gm_ws[l], bsb, n_mix)

        w_rkv, w_dec, w_icl = sl(2), sl(4), sl(5)
        pad_lora = lambda w: jnp.pad(w, ((0, 0), (0, LORA_PAD - w.shape[1])))
        w_b = jnp.concatenate(
            [_kperm_cols(w_rkv[:, :B_WIDTH]), _kperm_cols(w_rkv[:, B_WIDTH:2 * B_WIDTH]),
             _vperm_cols(w_rkv[:, 2 * B_WIDTH:]), sl(3),
             pad_lora(w_dec[:, :DECAY_LORA]), pad_lora(w_dec[:, DECAY_LORA:]),
             pad_lora(w_icl[:, :ICL_LORA]), pad_lora(w_icl[:, ICL_LORA:])], axis=1).astype(BF16)
        p = _bmerge(_rkv_proj(h, w_b), merge_sel, t=t, c=cl)
        conv = rw_conv[l]
        conv_m = jnp.concatenate([_both(_kmerge_cols(conv[:, :B_WIDTH])),
                                  _both(_kmerge_cols(conv[:, B_WIDTH:2 * B_WIDTH])),
                                  _both(_vmerge_cols(conv[:, 2 * B_WIDTH:]))], axis=1)
        def up_m(w):
            w32 = jnp.stack([
                _kmerge_cols(jnp.pad(w[dd], ((0, LORA_PAD - w.shape[1]), (0, 0)))).reshape(2 * LORA_PAD, MERGED)
                for dd in range(2)])
            hi = w32.astype(BF16)
            return jnp.stack([hi, (w32 - hi.astype(F32)).astype(BF16)])
        r4, kk4, kd4, b4, ptot, vrep, v2d, bonus = _rwkv_prep(
            p, conv_m, _both(_kmerge_cols(rw_w0[l])), up_m(rw_w_up[l]), _both(_kmerge_cols(rw_a0[l])),
            up_m(rw_a_up[l]), _both(_kmerge_cols(rw_k_k[l].reshape(1, B_WIDTH))),
            _both(_kmerge_cols(rw_k_a[l].reshape(1, B_WIDTH))),
            _both(_kmerge_cols(rw_r_k[l].reshape(1, B_WIDTH))), g32, rep, t=t, c=cl)
        yf, ybk = _scan(kd4, b4, kk4, r4, vrep, ptot, t=t, c=cl)
        yb2 = _rwkv_out(yf, ybk, v2d, bonus, p, _both(_vmerge_cols(rw_ln_g[l].reshape(1, B_WIDTH))),
                        _both(_vmerge_cols(rw_ln_b[l].reshape(1, B_WIDTH))),
                        _vmerge_cols(rw_g_up[l]).reshape(2 * GATE_LORA, MERGED).astype(BF16),
                        selv, g32, rows=t if last else t + cl)
        if last:
            yb = yb2.reshape(b * t, B_WIDTH)
        else:
            yb = jnp.concatenate([yb2[0, :t], yb2[1, :t], yb2[0, t:], yb2[1, t:]])

        gains = jnp.stack([jnp.tile(da_q_norm[l], 2) * (C_HEAD_DIM ** -0.5 * math.log2(math.e)),
                           jnp.tile(da_k_norm[l], 2), jnp.ones((LANES,), F32)]).reshape(3, 1, LANES)
        qk, vx = _qkv(h, sl(6).astype(BF16), gains, cos_t, sin_t, g64)
        sub = da_subln[l].reshape(1, LANES)
        yc = _attn(qk, vx, da_lam[l], sub, None, b=b, t=t, c=cl, lam_init=lam_init, ctx_queries=False)
        if not last:
            yc = _attn(qk, vx, da_lam[l], sub, yc, b=b, t=t, c=cl, lam_init=lam_init, ctx_queries=True)

        wg = jnp.transpose(sl(7).reshape(d, N_BRANCH, d), (1, 0, 2)).astype(BF16)
        wb = w_branch[l]
        wb_b = wb[1].reshape(B_HEADS, B_HEAD_DIM, d).transpose(1, 0, 2).reshape(B_WIDTH, d)
        wbr = jnp.stack([wb[0], wb_b, wb[2]]).astype(BF16)
        xs = _merge(xs, h, ya, yb, yc, mod, wg, b_gate[l].reshape(N_BRANCH, 1, d),
                    wbr, w_out[l].astype(BF16), mod_idx, n_mix)

        xs = _ffn(xs, mod, norm_g[l], w_gu, w_dn, l, 1, mod_idx, n_tiles=n_mix, mi=2, emit_h=False)

    return xs[:b * t].reshape(b, t, d)
```

```python
import functools
import math

import jax
import jax.numpy as jnp
from jax import lax
from jax.experimental import pallas as pl
from jax.experimental.pallas import tpu as pltpu

F32 = jnp.float32
BF16 = jnp.bfloat16

N_MOD = 9
CHUNK = 128
A_WIDTH = 1024
A_GROUPS = 8
B_WIDTH = 1024
B_HEAD_DIM = 64
B_HEADS = 16
DECAY_LORA = 96
ICL_LORA = 96
GATE_LORA = 256
C_HEADS = 8
C_HEAD_DIM = 64
C_WIDTH = 1024
N_BRANCH = 3
GRID_W = 64
ROPE_BASE = 10000.0
ROPE_AXIS_DIM = 32
NORM_EPS = 1e-6
LN_X_EPS = 64e-5

LANES = 128
SUBLANES = 8
VMEM_LIMIT = 56 * 1024 * 1024

TM = 512
TF = 1024
LORA_PAD = 128
SCAN_TS = 64
ATT_TQ = 512
ATT_KC = 256
ADA_BANDS = 4


def _params(sem):
    return pltpu.CompilerParams(dimension_semantics=sem, vmem_limit_bytes=VMEM_LIMIT)


def _rms(x, eps):
    return x * lax.rsqrt(jnp.mean(x * x, axis=-1, keepdims=True) + eps)


def _split3(s):
    hi = s.astype(BF16)
    r = s - hi.astype(F32)
    mid = r.astype(BF16)
    lo = (r - mid.astype(F32)).astype(BF16)
    return hi, mid, lo


def _gsum(s, g_bf16):
    out = None
    for piece in _split3(s):
        d = jnp.dot(piece, g_bf16, preferred_element_type=F32)
        out = d if out is None else out + d
    return out


def _gelu_tanh(x):
    cdf = 0.5 * (1.0 + jnp.tanh(math.sqrt(2.0 / math.pi) * (x + 0.044715 * (x * x * x))))
    return x * cdf


def _ada_kernel(c_ref, *refs):
    w_refs, (b_ref, o_ref) = refs[:-2], refs[-2:]
    s = c_ref[...]
    s = s * jax.nn.sigmoid(s)
    pieces = _split3(s)
    band = s.shape[1] // len(w_refs)
    acc = b_ref[...]
    for i, w_ref in enumerate(w_refs):
        w = w_ref[...].astype(BF16)
        for piece in pieces:
            acc = acc + jnp.dot(piece[:, i * band:(i + 1) * band], w, preferred_element_type=F32)
    o_ref[...] = acc


def _ada(cvec, w_ada, b_ada3, l):
    d = cvec.shape[1]
    n = w_ada.shape[2]
    tn = 2048 if n % 2048 == 0 else 1024
    return pl.pallas_call(
        _ada_kernel,
        grid=(n // tn,),
        in_specs=[pl.BlockSpec((SUBLANES, d), lambda j: (0, 0))]
        + [pl.BlockSpec((None, d // ADA_BANDS, tn), lambda j, i=i: (l, i, j)) for i in range(ADA_BANDS)]
        + [pl.BlockSpec((None, 1, tn), lambda j: (l, 0, j))],
        out_specs=pl.BlockSpec((SUBLANES, tn), lambda j: (0, j)),
        out_shape=jax.ShapeDtypeStruct((SUBLANES, n), F32),
        compiler_params=_params(("arbitrary",)),
        name="ada",
    )(cvec, *([w_ada] * ADA_BANDS), b_ada3)


def _ffn_kernel(x_ref, mod_ref, g_ref, wg_ref, *rest, tf, nf, last_valid, mi, emit_h):
    wu_refs, wout_ref, rest = rest[:tf // LANES], rest[tf // LANES], rest[tf // LANES + 1:]
    if emit_h:
        o_ref, h_ref, hn_sc, acc_sc = rest
    else:
        o_ref, hn_sc, acc_sc = rest
    f = pl.program_id(1)

    @pl.when(f == 0)
    def _():
        xn = _rms(x_ref[...], NORM_EPS) * g_ref[mi:mi + 1, :]
        hn = xn * (1.0 + mod_ref[3 * mi + 1:3 * mi + 2, :]) + mod_ref[3 * mi:3 * mi + 1, :]
        hn_sc[...] = hn.astype(BF16)
        acc_sc[...] = jnp.zeros_like(acc_sc)

    def hidden_tile(valid):
        hn = hn_sc[...]
        g = jnp.dot(hn, wg_ref[:, :valid], preferred_element_type=F32)
        wu = jnp.concatenate([r[...] for r in wu_refs[:valid // LANES]], axis=1)
        u = jnp.dot(hn, wu, preferred_element_type=F32)
        act = (g * jax.nn.sigmoid(g) * u).astype(BF16)
        acc_sc[...] += jnp.dot(act, wout_ref[:valid, :], preferred_element_type=F32)

    if last_valid == tf:
        hidden_tile(tf)
    else:
        pl.when(f < nf - 1)(lambda: hidden_tile(tf))
        pl.when(f == nf - 1)(lambda: hidden_tile(last_valid))

    @pl.when(f == nf - 1)
    def _():
        out = x_ref[...] + 0.5 * mod_ref[3 * mi + 2:3 * mi + 3, :] * acc_sc[...]
        o_ref[...] = out
        if emit_h:
            hn = _rms(out, NORM_EPS) * g_ref[1:2, :]
            h_ref[...] = (hn * (1.0 + mod_ref[4:5, :]) + mod_ref[3:4, :]).astype(BF16)


def _ffn(xs, mod, norm_g, w_gu, w_out, l, w, mod_idx, *, n_tiles, mi, emit_h):
    m, d = xs.shape
    d_ff = w_out.shape[2]
    assert d_ff % LANES == 0
    up0 = d_ff // LANES
    nf = pl.cdiv(d_ff, TF)
    out_rows = n_tiles * TM
    out_shape = [jax.ShapeDtypeStruct((out_rows, d), F32)]
    out_specs = [pl.BlockSpec((TM, d), lambda i, f: (i, 0))]
    if emit_h:
        out_shape.append(jax.ShapeDtypeStruct((out_rows, d), BF16))
        out_specs.append(pl.BlockSpec((TM, d), lambda i, f: (i, 0)))
    res = pl.pallas_call(
        functools.partial(_ffn_kernel, tf=TF, nf=nf, last_valid=d_ff - (nf - 1) * TF, mi=mi, emit_h=emit_h),
        grid=(n_tiles, nf),
        in_specs=[
            pl.BlockSpec((TM, d), lambda i, f: (i, 0)),
            pl.BlockSpec((None, N_MOD, d), lambda i, f: (mod_idx(i), 0, 0)),
            pl.BlockSpec((3, d), lambda i, f: (0, 0)),
            pl.BlockSpec((None, None, d, TF), lambda i, f: (l, w, 0, f)),
        ] + [
            pl.BlockSpec((None, None, d, LANES),
                         lambda i, f, k=k: (l, w, 0, jnp.minimum(up0 + f * (TF // LANES) + k, 2 * up0 - 1)))
            for k in range(TF // LANES)
        ] + [
            pl.BlockSpec((None, None, TF, d), lambda i, f: (l, w, f, 0)),
        ],
        out_specs=out_specs,
        out_shape=out_shape,
        scratch_shapes=[pltpu.VMEM((TM, d), BF16), pltpu.VMEM((TM, d), F32)],
        compiler_params=_params(("parallel", "arbitrary")),
        name="ffn",
    )(xs, mod, norm_g, w_gu, *([w_gu] * (TF // LANES)), w_out)
    return res if emit_h else res[0]


def _gmlp_kernel(h_ref, w_ref, vn_ref, ws_ref, bsb_ref, o_ref, p_sc, *, tm):
    p_sc[...] = jnp.dot(h_ref[...], w_ref[...], preferred_element_type=F32)
    gd = A_WIDTH // A_GROUPS
    for g in range(A_GROUPS):
        wsg = ws_ref[g].astype(BF16)
        bias = bsb_ref[g]
        gain = vn_ref[:, g * gd:(g + 1) * gd]
        for n in range(tm // CHUNK):
            rows = slice(n * CHUNK, (n + 1) * CHUNK)
            u = _gelu_tanh(p_sc[rows, g * gd:(g + 1) * gd])
            v = _gelu_tanh(p_sc[rows, A_WIDTH + g * gd:A_WIDTH + (g + 1) * gd])
            v = _rms(v, NORM_EPS) * gain
            sv = jnp.dot(wsg, v.astype(BF16), preferred_element_type=F32) + bias
            o_ref[rows, g * gd:(g + 1) * gd] = (u * sv).astype(o_ref.dtype)


def _gmlp(h, w_uv, vn, ws, bsb, n_tiles):
    m, d = h.shape
    return pl.pallas_call(
        functools.partial(_gmlp_kernel, tm=TM),
        grid=(n_tiles,),
        in_specs=[
            pl.BlockSpec((TM, d), lambda i: (i, 0)),
            pl.BlockSpec((d, 2 * A_WIDTH), lambda i: (0, 0)),
            pl.BlockSpec((1, A_WIDTH), lambda i: (0, 0)),
            pl.BlockSpec((A_GROUPS, CHUNK, CHUNK), lambda i: (0, 0, 0)),
            pl.BlockSpec((A_GROUPS, CHUNK, A_WIDTH // A_GROUPS), lambda i: (0, 0, 0)),
        ],
        out_specs=pl.BlockSpec((TM, A_WIDTH), lambda i: (i, 0)),
        out_shape=jax.ShapeDtypeStruct((n_tiles * TM, A_WIDTH), BF16),
        scratch_shapes=[pltpu.VMEM((TM, 2 * A_WIDTH), F32)],
        compiler_params=_params(("parallel",)),
        name="gmlp",
    )(h, w_uv, vn, ws, bsb)


def _qkv_kernel(h_ref, w_ref, gain_ref, cos_ref, sin_ref, g64_ref, o_ref, vx_ref):
    j = pl.program_id(1)
    n_parts = 2
    heads_per = C_HEADS // n_parts

    def heads():
        for part in range(n_parts):
            cols = slice(part * heads_per * LANES, (part + 1) * heads_per * LANES)
            p = jnp.dot(h_ref[...], w_ref[:, cols], preferred_element_type=F32)
            for i in range(heads_per):
                yield part * heads_per + i, p[:, i * LANES:(i + 1) * LANES]

    @pl.when(j == 2)
    def _():
        ones = jnp.ones((h_ref.shape[0], LANES), BF16)
        for hs, x in heads():
            vx_ref[:, 2 * hs * LANES:(2 * hs + 1) * LANES] = x.astype(BF16)
            vx_ref[:, (2 * hs + 1) * LANES:(2 * hs + 2) * LANES] = ones

    @pl.when(j < 2)
    def _():
        lane = lax.broadcasted_iota(jnp.int32, (1, LANES), 1)
        first = (lane % ROPE_AXIS_DIM) < (ROPE_AXIS_DIM // 2)
        cos = cos_ref[...]
        sin = sin_ref[...]
        gain = gain_ref[...]
        g64 = g64_ref[...]
        for hs, x in heads():
            ms = _gsum(x * x, g64) * (1.0 / C_HEAD_DIM)
            y = x * lax.rsqrt(ms + NORM_EPS) * gain
            half = ROPE_AXIS_DIM // 2
            rot = jnp.where(first, pltpu.roll(y, LANES - half, 1), pltpu.roll(y, half, 1))
            o_ref[:, hs * LANES:(hs + 1) * LANES] = (y * cos + rot * sin).astype(BF16)


def _qkv(h, w_qkv, gains, cos_t, sin_t, g64):
    m, d = h.shape
    return pl.pallas_call(
        _qkv_kernel,
        grid=(m // TM, 3),
        in_specs=[
            pl.BlockSpec((TM, d), lambda i, j: (i, 0)),
            pl.BlockSpec((d, C_WIDTH), lambda i, j: (0, j)),
            pl.BlockSpec((None, 1, LANES), lambda i, j: (j, 0, 0)),
            pl.BlockSpec((TM, LANES), lambda i, j: (i, 0)),
            pl.BlockSpec((TM, LANES), lambda i, j: (i, 0)),
            pl.BlockSpec((LANES, LANES), lambda i, j: (0, 0)),
        ],
        out_specs=[pl.BlockSpec((None, TM, C_WIDTH), lambda i, j: (jnp.minimum(j, 1), i, 0)),
                   pl.BlockSpec((TM, 2 * C_WIDTH), lambda i, j: (i, 0))],
        out_shape=[jax.ShapeDtypeStruct((2, m, C_WIDTH), BF16), jax.ShapeDtypeStruct((m, 2 * C_WIDTH), BF16)],
        compiler_params=_params(("parallel", "arbitrary")),
        name="qkv",
    )(h, w_qkv, gains, cos_t, sin_t, g64)


def _attn_kernel(lam_ref, sub_ref, q_ref, *refs, nseg, lam_init):
    kv = refs[:2 * nseg]
    o_ref = refs[2 * nseg]
    lv = lam_ref[...]
    lam = (jnp.exp(jnp.sum(lv[0:1] * lv[1:2], axis=-1, keepdims=True))
           - jnp.exp(jnp.sum(lv[2:3] * lv[3:4], axis=-1, keepdims=True)) + lam_init)
    q = q_ref[...]
    lane = lax.broadcasted_iota(jnp.int32, (1, LANES), 1)
    zero = jnp.zeros_like(q)
    outs = []
    for j in range(2):
        in_map = (lane < C_HEAD_DIM) if j == 0 else (lane >= C_HEAD_DIM)
        qj = jnp.where(in_map, q, zero)
        m = None
        acc = None
        for si in range(nseg):
            k_ref, v_ref = kv[2 * si], kv[2 * si + 1]
            ls = k_ref.shape[0]
            for c0 in range(0, ls, ATT_KC):
                kc = min(ATT_KC, ls - c0)
                s = lax.dot_general(qj, k_ref[c0:c0 + kc, :], (((1,), (1,)), ((), ())),
                                    preferred_element_type=F32)
                cm = jnp.max(s, axis=-1, keepdims=True)
                m_new = cm if m is None else jnp.maximum(m, cm)
                e = jnp.exp2((s - m_new).astype(BF16))
                pv = jnp.dot(e, v_ref[c0:c0 + kc, :], preferred_element_type=F32)
                acc = pv if acc is None else acc * jnp.exp2(m - m_new) + pv
                m = m_new
        outs.append(acc[:, :LANES] / acc[:, LANES:])
    o = outs[0] - lam * outs[1]
    o = _rms(o, NORM_EPS) * sub_ref[...]
    o_ref[...] = (o * (1.0 - lam_init)).astype(o_ref.dtype)


def _attn(qk, vx, lam, subln, yc_prev, *, b, t, c, lam_init, ctx_queries):
    m = qk.shape[1]
    if ctx_queries:
        tq, nq = c, 1
        q_blk = lambda bi, qi: (b * t) // c + bi
        segs = [(c, lambda bi: (b * t) // c + bi)]
    else:
        tq = ATT_TQ
        nq = t // tq
        q_blk = lambda bi, qi: bi * nq + qi
        segs = [(t, lambda bi: bi), (c, lambda bi: (b * t) // c + bi)]
    in_specs = [
        pl.BlockSpec((4, C_HEAD_DIM), lambda bi, h, qi: (0, 0)),
        pl.BlockSpec((1, LANES), lambda bi, h, qi: (0, 0)),
        pl.BlockSpec((None, tq, LANES), lambda bi, h, qi: (0, q_blk(bi, qi), h)),
    ]
    args = [lam, subln, qk]
    for ls, blk in segs:
        in_specs.append(pl.BlockSpec((None, ls, LANES), lambda bi, h, qi, blk=blk: (1, blk(bi), h)))
        in_specs.append(pl.BlockSpec((ls, 2 * LANES), lambda bi, h, qi, blk=blk: (blk(bi), h)))
        args += [qk, vx]
    aliases = {}
    if yc_prev is not None:
        in_specs.append(pl.BlockSpec(memory_space=pl.ANY))
        args.append(yc_prev)
        aliases = {len(args) - 1: 0}

    def body(*refs):
        n_in = 3 + 2 * len(segs)
        _attn_kernel(*refs[:n_in], refs[-1], nseg=len(segs), lam_init=lam_init)

    return pl.pallas_call(
        body,
        grid=(b, C_HEADS, nq),
        in_specs=in_specs,
        out_specs=pl.BlockSpec((tq, LANES), lambda bi, h, qi: (q_blk(bi, qi), h)),
        out_shape=jax.ShapeDtypeStruct((m, C_WIDTH), BF16),
        input_output_aliases=aliases,
        compiler_params=_params(("parallel", "parallel", "arbitrary")),
        name="diff_attn_ctx" if ctx_queries else "diff_attn",
    )(*args)


KQ = 4
NKL = B_HEAD_DIM // KQ
MERGED = 2 * B_WIDTH
BH = 2 * B_HEADS
LORA_MERGED = 2 * GATE_LORA + 8 * LORA_PAD
PROJ_MERGED = 3 * MERGED + LORA_MERGED


PROJ_UNMERGED = 3 * B_WIDTH + GATE_LORA + 4 * LORA_PAD


def _mm_kernel(a_ref, w_ref, o_ref):
    o_ref[...] = jnp.dot(a_ref[...], w_ref[...], preferred_element_type=F32)


def _rkv_proj(h, w):
    m, d = h.shape
    tn = PROJ_UNMERGED // 3
    return pl.pallas_call(
        _mm_kernel,
        grid=(PROJ_UNMERGED // tn, m // TM),
        in_specs=[pl.BlockSpec((TM, d), lambda j, i: (i, 0)),
                  pl.BlockSpec((d, tn), lambda j, i: (0, j))],
        out_specs=pl.BlockSpec((TM, tn), lambda j, i: (i, j)),
        out_shape=jax.ShapeDtypeStruct((m, PROJ_UNMERGED), F32),
        compiler_params=_params(("parallel", "arbitrary")),
        name="rkv_proj",
    )(h, w)


def _bmerge_kernel(p0_ref, p1_ref, e_ref, o_ref):
    n_kv = 3 * B_WIDTH // LANES
    for bi, p_ref in enumerate((p0_ref, p1_ref)):
        for s in range(n_kv):
            x = p_ref[:, s * LANES:(s + 1) * LANES]
            hi = x.astype(BF16)
            lo = (x - hi.astype(F32)).astype(BF16)
            for half in range(2):
                e = e_ref[bi, half]
                part = (jnp.dot(hi, e, preferred_element_type=F32) + jnp.dot(lo, e, preferred_element_type=F32))
                cs = slice((2 * s + half) * LANES, (2 * s + half + 1) * LANES)
                if bi == 0:
                    o_ref[:, cs] = part
                else:
                    o_ref[:, cs] += part
        src = 3 * B_WIDTH
        dst = 3 * MERGED
        for width in (GATE_LORA,) + (LORA_PAD,) * 4:
            o_ref[:, dst + bi * width:dst + (bi + 1) * width] = p_ref[:, src:src + width]
            src += width
            dst += 2 * width


def _bmerge(p, e, *, t, c):
    tm = 256
    s = t + c
    nlat = t // tm
    hblk = lambda bi, i: jnp.where(i < nlat, bi * nlat + i, (2 * t + bi * c) // tm + (i - nlat))
    return pl.pallas_call(
        _bmerge_kernel,
        grid=(s // tm,),
        in_specs=[
            pl.BlockSpec((tm, PROJ_UNMERGED), lambda i: (hblk(0, i), 0)),
            pl.BlockSpec((tm, PROJ_UNMERGED), lambda i: (hblk(1, i), 0)),
            pl.BlockSpec((2, 2, LANES, LANES), lambda i: (0, 0, 0, 0)),
        ],
        out_specs=pl.BlockSpec((tm, PROJ_MERGED), lambda i: (i, 0)),
        out_shape=jax.ShapeDtypeStruct((s, PROJ_MERGED), F32),
        compiler_params=_params(("parallel",)),
        name="batch_merge",
    )(p, p, e)


class _SlabWriter:
    def __init__(self, ref, lead=()):
        self.ref, self.lead, self.parts, self.base = ref, lead, [], 0

    def add(self, slab):
        self.parts.append(slab)
        if len(self.parts) == SUBLANES:
            block = jnp.swapaxes(jnp.stack(self.parts, axis=0), 0, 1)
            self.ref[self.lead + (slice(None), slice(self.base, self.base + SUBLANES), slice(None))] = block
            self.parts, self.base = [], self.base + SUBLANES


def _rwkv_prep_kernel(p_ref, prev_ref, next_ref, lo_ref, conv_ref, w0_ref, wup_ref, a0_ref, aup_ref,
                      kkg_ref, ka_ref, rk_ref, g32_ref, rep_ref,
                      r_o, kk_o, kd_o, b_o, ptot_o, vrep_o, v_o, bonus_o, k_sc, r_sc, *, tm, ts, starts, ends):
    i = pl.program_id(0)
    is_start = functools.reduce(jnp.logical_or, [i == s for s in starts])
    is_end = functools.reduce(jnp.logical_or, [i == s for s in ends])
    row = lax.broadcasted_iota(jnp.int32, (tm, 1), 0)

    def conv(c0):
        cs = slice(c0, c0 + LANES)
        x = p_ref[:, cs]
        pm = jnp.where(is_start, 0.0, prev_ref[SUBLANES - 1:SUBLANES, cs])
        nx = jnp.where(is_end, 0.0, next_ref[0:1, cs])
        xm1 = jnp.where(row == 0, pm, pltpu.roll(x, 1, 0))
        xp1 = jnp.where(row == tm - 1, nx, pltpu.roll(x, tm - 1, 0))
        return xm1 * conv_ref[0:1, cs] + x * conv_ref[1:2, cs] + xp1 * conv_ref[2:3, cs]

    vrep_out = _SlabWriter(vrep_o)
    r_out, kk_out, kd_out, b_out = ([_SlabWriter(o, (d,)) for d in range(2)] for o in (r_o, kk_o, kd_o, b_o))

    ss = None
    for m in range(NKL):
        cs = slice(m * LANES, (m + 1) * LANES)
        r_sc[:, cs] = conv(m * LANES)
        k = conv(MERGED + m * LANES)
        k_sc[:, cs] = k
        kkr = k * kkg_ref[:, cs]
        ss = kkr * kkr if ss is None else ss + kkr * kkr
    rs = lax.rsqrt(_gsum(ss, g32_ref[...]) + 1e-12)

    w_off = 2 * GATE_LORA
    a_off = w_off + 4 * LORA_PAD
    def split2(x):
        hi = x.astype(BF16)
        return hi, (x - hi.astype(F32)).astype(BF16)

    def lora(x2, up_ref, d, cs):
        return (jnp.dot(x2[0], up_ref[0, d, :, cs], preferred_element_type=F32)
                + jnp.dot(x2[1], up_ref[0, d, :, cs], preferred_element_type=F32)
                + jnp.dot(x2[0], up_ref[1, d, :, cs], preferred_element_type=F32))

    tw = [split2(jnp.tanh(lo_ref[:, w_off + 2 * d * LORA_PAD:w_off + 2 * (d + 1) * LORA_PAD])) for d in range(2)]
    pa = [split2(lo_ref[:, a_off + 2 * d * LORA_PAD:a_off + 2 * (d + 1) * LORA_PAD]) for d in range(2)]
    bonus = None
    for m in range(NKL):
        cs = slice(m * LANES, (m + 1) * LANES)
        k = k_sc[:, cs]
        kk = k * kkg_ref[:, cs] * rs
        r = r_sc[:, cs]
        kd_sum = None
        for d in range(2):
            w_lo = w0_ref[d:d + 1, cs] + lora(tw[d], wup_ref, d, cs)
            lw = -jax.nn.sigmoid(w_lo) * math.exp(-0.5)
            cum = lw
            shift = 1
            while shift < ts:
                if d == 0:
                    cum = cum + jnp.where(row % ts >= shift, pltpu.roll(cum, shift, 0), 0.0)
                else:
                    cum = cum + jnp.where(row % ts < ts - shift, pltpu.roll(cum, tm - shift, 0), 0.0)
                shift *= 2
            p_incl = jnp.exp(cum)
            p_inv = jnp.exp(-cum)
            p_excl = jnp.exp(cum - lw)
            a = jax.nn.sigmoid(a0_ref[d:d + 1, cs] + lora(pa[d], aup_ref, d, cs))
            kd = k * (1.0 + (a - 1.0) * ka_ref[:, cs])
            kd_sum = kd if kd_sum is None else kd_sum + kd
            r_out[d].add(r * p_incl)
            kk_out[d].add(kk * p_excl)
            kd_out[d].add(kd * p_inv)
            b_out[d].add(kk * a * p_inv)
            for g in range(tm // ts):
                last = g * ts + (ts - 1 if d == 0 else 0)
                ptot_o[d, g, m:m + 1, :] = p_incl[last:last + 1, :]
        term = r * (0.5 * kd_sum) * rk_ref[:, cs]
        bonus = term if bonus is None else bonus + term
    bonus_o[...] = bonus

    per_slab = LANES // BH
    for j in range(MERGED // LANES):
        vs = conv(2 * MERGED + j * LANES)
        v_o[:, j * LANES:(j + 1) * LANES] = vs
        pieces = _split3(vs)
        for vi in range(per_slab):
            vrep_out.add(sum(jnp.dot(piece, rep_ref[vi], preferred_element_type=F32) for piece in pieces))


def _rwkv_prep(p, conv_m, w0_m, wup_m, a0_m, aup_m, kk_m, ka_m, rk_m, g32, rep, *, t, c):
    s = t + c
    tm = 128
    nt = s // tm
    starts = [0, t // tm]
    ends = [t // tm - 1, nt - 1]
    hb = tm // SUBLANES
    rkv_w = 3 * MERGED
    full2 = lambda i: (0, 0)
    full3 = lambda i: (0, 0, 0)
    ts = SCAN_TS
    k4 = jax.ShapeDtypeStruct((2, s, NKL, LANES), F32)
    k4s = pl.BlockSpec((2, tm, NKL, LANES), lambda i: (0, i, 0, 0))
    return pl.pallas_call(
        functools.partial(_rwkv_prep_kernel, tm=tm, ts=ts, starts=starts, ends=ends),
        grid=(nt,),
        in_specs=[
            pl.BlockSpec((tm, rkv_w), lambda i: (i, 0)),
            pl.BlockSpec((SUBLANES, rkv_w), lambda i: (jnp.maximum(i * hb - 1, 0), 0)),
            pl.BlockSpec((SUBLANES, rkv_w), lambda i: (jnp.minimum((i + 1) * hb, s // SUBLANES - 1), 0)),
            pl.BlockSpec((tm, LORA_MERGED), lambda i: (i, rkv_w // LORA_MERGED)),
            pl.BlockSpec((3, rkv_w), full2),
            pl.BlockSpec((2, MERGED), full2),
            pl.BlockSpec((2, 2, 2 * LORA_PAD, MERGED), lambda i: (0, 0, 0, 0)),
            pl.BlockSpec((2, MERGED), full2),
            pl.BlockSpec((2, 2, 2 * LORA_PAD, MERGED), lambda i: (0, 0, 0, 0)),
            pl.BlockSpec((1, MERGED), full2),
            pl.BlockSpec((1, MERGED), full2),
            pl.BlockSpec((1, MERGED), full2),
            pl.BlockSpec((LANES, LANES), full2),
            pl.BlockSpec((LANES // BH, LANES, LANES), full3),
        ],
        out_specs=[k4s, k4s, k4s, k4s,
                   pl.BlockSpec((2, tm // ts, NKL, LANES), lambda i: (0, i, 0, 0)),
                   pl.BlockSpec((tm, B_HEAD_DIM, LANES), lambda i: (i, 0, 0)),
                   pl.BlockSpec((tm, MERGED), lambda i: (i, 0)),
                   pl.BlockSpec((tm, LANES), lambda i: (i, 0))],
        out_shape=[k4, k4, k4, k4, jax.ShapeDtypeStruct((2, s // ts, NKL, LANES), F32),
                   jax.ShapeDtypeStruct((s, B_HEAD_DIM, LANES), F32),
                   jax.ShapeDtypeStruct((s, MERGED), F32), jax.ShapeDtypeStruct((s, LANES), F32)],
        scratch_shapes=[pltpu.VMEM((tm, MERGED), F32), pltpu.VMEM((tm, MERGED), F32)],
        compiler_params=_params(("parallel",)),
        name="rwkv_prep",
    )(p, p, p, p, conv_m, w0_m, wup_m, a0_m, aup_m, kk_m, ka_m, rk_m, g32, rep)


def _scan_kernel(*refs, ts):
    ins, (yf_ref, yb_ref, s_sc, y_sc, sa_sc) = refs[:12], refs[12:]
    streams = [ins[:6] + (yf_ref,), ins[6:] + (yb_ref,)]
    nkl = s_sc.shape[1]
    nvb = s_sc.shape[2] // SUBLANES

    @pl.when(pl.program_id(0) == 0)
    def _():
        s_sc[...] = jnp.zeros_like(s_sc)
        y_sc[...] = jnp.zeros_like(y_sc)

    def vsl(vb):
        return slice(vb * SUBLANES, (vb + 1) * SUBLANES)

    diag = (lax.broadcasted_iota(jnp.int32, (SUBLANES, LANES), 1) // BH
            == lax.broadcasted_iota(jnp.int32, (SUBLANES, LANES), 0) % KQ)

    def emit_y(d, y_ref, row):
        for vb in range(nvb):
            y = y_sc[d, vsl(vb), :]
            y = y + pltpu.roll(y, BH, 1)
            y = y + pltpu.roll(y, 2 * BH, 1)
            y = jnp.where(diag, y, 0.0)
            y = y + pltpu.roll(y, 1, 0)
            y = y + pltpu.roll(y, 2, 0)
            for half in range(SUBLANES // KQ):
                srow = half * KQ + KQ - 1
                y_ref[row, 2 * vb + half:2 * vb + half + 1, :] = y[srow:srow + 1, :]

    def allreduce(x):
        x = x + pltpu.roll(x, BH, 1)
        return x + pltpu.roll(x, 2 * BH, 1)

    def step(j, carry):
        ts_of = (j, ts - 1 - j)
        prev = (jnp.maximum(j - 1, 0), jnp.minimum(ts - j, ts - 1))
        for d in range(2):
            emit_y(d, streams[d][-1], prev[d])
        nxt = (jnp.minimum(j + 1, ts - 1), jnp.maximum(ts - 2 - j, 0))
        for d, (kd_ref, b_ref, kk_ref, r_ref, v_ref, ptot_ref, y_ref) in enumerate(streams):
            t = ts_of[d]
            sa = [sa_sc[d, vsl(vb), :] if d == 0 else allreduce(sa_sc[d, vsl(vb), :]) for vb in range(nvb)]
            ys = [None] * nvb
            sa_next = [None] * nvb
            for kl in range(nkl):
                brow = b_ref[t, kl:kl + 1, :]
                kdrow = kd_ref[t, kl:kl + 1, :]
                rrow = r_ref[t, kl:kl + 1, :]
                kknext = kk_ref[nxt[d], kl:kl + 1, :]
                for vb in range(nvb):
                    sn = s_sc[d, kl, vsl(vb), :] - sa[vb] * brow + v_ref[t, vsl(vb), :] * kdrow
                    s_sc[d, kl, vsl(vb), :] = sn
                    term = sn * rrow
                    ys[vb] = term if ys[vb] is None else ys[vb] + term
                    term = sn * kknext
                    sa_next[vb] = term if sa_next[vb] is None else sa_next[vb] + term
            for vb in range(nvb):
                y_sc[d, vsl(vb), :] = ys[vb]
                sa_sc[d, vsl(vb), :] = allreduce(sa_next[vb]) if d == 0 else sa_next[vb]
        return carry

    for d, first in enumerate((0, ts - 1)):
        kk_ref = streams[d][2]
        for vb in range(nvb):
            acc = None
            for kl in range(nkl):
                term = s_sc[d, kl, vsl(vb), :] * kk_ref[first, kl:kl + 1, :]
                acc = term if acc is None else acc + term
            sa_sc[d, vsl(vb), :] = allreduce(acc) if d == 0 else acc

    lax.fori_loop(0, ts, step, 0)
    emit_y(0, yf_ref, ts - 1)
    emit_y(1, yb_ref, 0)
    for d in range(2):
        ptot_ref = streams[d][5]
        for kl in range(nkl):
            prow = ptot_ref[0, kl:kl + 1, :]
            for vb in range(nvb):
                s_sc[d, kl, vsl(vb), :] = s_sc[d, kl, vsl(vb), :] * prow


def _scan(kd4, b4, kk4, r4, vrep, ptot, *, t, c):
    s = t + c
    ts = SCAN_TS
    nctx = c // ts
    nlat = t // ts
    fblk = lambda g: jnp.where(g < nctx, nlat + g, g - nctx)
    bblk = lambda g: jnp.where(g < nctx, nlat + (nctx - 1 - g), nlat - 1 - (g - nctx))
    in_specs = []
    args = []
    for d, blk in enumerate((fblk, bblk)):
        for a in (kd4, b4, kk4, r4):
            in_specs.append(pl.BlockSpec((None, ts, NKL, LANES), lambda g, d=d, blk=blk: (d, blk(g), 0, 0)))
            args.append(a)
        in_specs.append(pl.BlockSpec((ts, B_HEAD_DIM, LANES), lambda g, blk=blk: (blk(g), 0, 0)))
        args.append(vrep)
        in_specs.append(pl.BlockSpec((None, 1, NKL, LANES), lambda g, d=d, blk=blk: (d, blk(g), 0, 0)))
        args.append(ptot)
    nslab = MERGED // LANES
    yshape = jax.ShapeDtypeStruct((s, nslab, LANES), F32)
    return pl.pallas_call(
        functools.partial(_scan_kernel, ts=ts),
        grid=(s // ts,),
        in_specs=in_specs,
        out_specs=[pl.BlockSpec((ts, nslab, LANES), lambda g: (fblk(g), 0, 0)),
                   pl.BlockSpec((ts, nslab, LANES), lambda g: (bblk(g), 0, 0))],
        out_shape=[yshape, yshape],
        scratch_shapes=[pltpu.VMEM((2, NKL, B_HEAD_DIM, LANES), F32),
                        pltpu.VMEM((2, B_HEAD_DIM, LANES), F32),
                        pltpu.VMEM((2, B_HEAD_DIM, LANES), F32)],
        compiler_params=_params(("arbitrary",)),
        name="wkv7_scan",
    )(*args)


def _rwkv_out_kernel(yf_ref, yb_ref, v_ref, bonus_ref, pg_ref, lng_ref, lnb_ref, gup_ref,
                     selv_ref, g32_ref, o_ref):
    g32 = g32_ref[...]
    gate = jnp.dot(jax.nn.sigmoid(pg_ref[...]).astype(BF16), gup_ref[...], preferred_element_type=F32)
    bonus = _gsum(bonus_ref[...], g32)
    nslab = MERGED // LANES
    ys = []
    for g in range(nslab // SUBLANES):
        rows = slice(g * SUBLANES, (g + 1) * SUBLANES)
        both = jnp.swapaxes(yf_ref[:, rows, :] + yb_ref[:, rows, :], 0, 1)
        ys += [both[j] for j in range(SUBLANES)]
    inv = 1.0 / B_HEAD_DIM
    mu = _gsum(functools.reduce(lambda a, b_: a + b_, ys), g32) * inv
    ds = [y - mu for y in ys]
    var = _gsum(functools.reduce(lambda a, b_: a + b_, [x * x for x in ds]), g32) * inv
    rstd = lax.rsqrt(var + LN_X_EPS)
    outs = []
    for j in range(nslab):
        cs = slice(j * LANES, (j + 1) * LANES)
        yn = ds[j] * rstd * lng_ref[:, cs] + lnb_ref[:, cs]
        outs.append(((yn + bonus * v_ref[:, cs]) * gate[:, cs]).astype(BF16))
    for j in range(nslab // 2):
        pair = jnp.concatenate([outs[2 * j], outs[2 * j + 1]], axis=1)
        for bi in range(2):
            o_ref[bi, :, j * LANES:(j + 1) * LANES] = jnp.dot(
                pair, selv_ref[bi], preferred_element_type=F32).astype(o_ref.dtype)


def _rwkv_out(yf, yb, v2d, bonus, p, lng_m, lnb_m, gup_m, selv, g32, *, rows):
    tm = 256
    nslab = MERGED // LANES
    y3s = pl.BlockSpec((tm, nslab, LANES), lambda i: (i, 0, 0))
    full2 = lambda i: (0, 0)
    return pl.pallas_call(
        _rwkv_out_kernel,
        grid=(rows // tm,),
        in_specs=[
            y3s, y3s,
            pl.BlockSpec((tm, MERGED), lambda i: (i, 0)),
            pl.BlockSpec((tm, LANES), lambda i: (i, 0)),
            pl.BlockSpec((tm, 2 * GATE_LORA), lambda i: (i, 3 * MERGED // (2 * GATE_LORA))),
            pl.BlockSpec((1, MERGED), full2),
            pl.BlockSpec((1, MERGED), full2),
            pl.BlockSpec((2 * GATE_LORA, MERGED), full2),
            pl.BlockSpec((2, 2 * LANES, LANES), lambda i: (0, 0, 0)),
            pl.BlockSpec((LANES, LANES), full2),
        ],
        out_specs=pl.BlockSpec((2, tm, B_WIDTH), lambda i: (0, i, 0)),
        out_shape=jax.ShapeDtypeStruct((2, rows, B_WIDTH), BF16),
        compiler_params=_params(("parallel",)),
        name="rwkv_out",
    )(yf, yb, v2d, bonus, p, lng_m, lnb_m, gup_m, selv, g32)


def _kmerge_cols(w):
    r = w.shape[0]
    wt = w.reshape(r, B_HEADS, NKL, KQ).transpose(0, 2, 3, 1)
    z = jnp.zeros_like(wt)
    return jnp.stack([jnp.stack([wt, z], axis=3), jnp.stack([z, wt], axis=3)]).reshape(2, r, MERGED)


def _vmerge_cols(w):
    r = w.shape[0]
    wt = w.reshape(r, B_HEADS, B_HEAD_DIM).transpose(0, 2, 1)
    z = jnp.zeros_like(wt)
    return jnp.stack([jnp.stack([wt, z], axis=2), jnp.stack([z, wt], axis=2)]).reshape(2, r, MERGED)


def _both(m):
    return m[0] + m[1]


def _kperm_cols(w):
    r = w.shape[0]
    return w.reshape(r, B_HEADS, NKL, KQ).transpose(0, 2, 3, 1).reshape(r, B_WIDTH)


def _vperm_cols(w):
    r = w.shape[0]
    return w.reshape(r, B_HEADS, B_HEAD_DIM).transpose(0, 2, 1).reshape(r, B_WIDTH)


def _selectors():
    ci = jnp.arange(LANES)[None, :]
    bsel = jnp.arange(2)[:, None, None]
    c_vi, c_h = ci // B_HEADS, ci % B_HEADS
    r2 = jnp.arange(2 * LANES)[:, None]
    selv = ((r2 // BH == c_vi) & ((r2 % BH) // B_HEADS == bsel) & (r2 % B_HEADS == c_h)).astype(BF16)
    r1 = jnp.arange(LANES)[:, None]
    g32 = (r1 % BH == ci % BH).astype(BF16)
    rep = jnp.stack([((r1 // BH == vi) & (r1 % BH == ci % BH)) for vi in range(LANES // BH)]).astype(BF16)
    half = jnp.arange(2)[None, :, None, None]
    merge = ((r1 // (LANES // 2) == half) & ((r1 % (LANES // 2)) // B_HEADS == ci // BH)
             & ((ci % BH) // B_HEADS == bsel[:, None]) & (r1 % B_HEADS == c_h)).astype(BF16)
    return selv, g32, rep, merge


def _merge_kernel(x_ref, h_ref, ya_ref, yb_ref, yc_ref, mod_ref, wg_ref, bg_ref, wb_ref, wo_ref,
                  o_ref, *, nz):
    j = pl.program_id(1)
    h = h_ref[...]
    z = None
    for i, y_ref in enumerate((ya_ref, yb_ref, yc_ref)):
        gate = jax.nn.sigmoid(jnp.dot(h, wg_ref[i], preferred_element_type=F32) + bg_ref[i])
        term = gate * jnp.dot(y_ref[...], wb_ref[i], preferred_element_type=F32)
        z = term if z is None else z + term
    part = jnp.dot(z.astype(BF16), wo_ref[...], preferred_element_type=F32)

    @pl.when(j == 0)
    def _():
        o_ref[...] = part

    @pl.when(j > 0)
    def _():
        o_ref[...] += part

    @pl.when(j == nz - 1)
    def _():
        o_ref[...] = x_ref[...] + mod_ref[5:6, :] * o_ref[...]


def _merge(xs, h, ya, yb, yc, mod, wg, bg, wb, wo, mod_idx, n_tiles):
    d = xs.shape[1]
    tz = 512
    nz = d // tz
    rowt = lambda i, j: (i, 0)
    return pl.pallas_call(
        functools.partial(_merge_kernel, nz=nz),
        grid=(n_tiles, nz),
        in_specs=[
            pl.BlockSpec((TM, d), rowt),
            pl.BlockSpec((TM, d), rowt),
            pl.BlockSpec((TM, A_WIDTH), rowt),
            pl.BlockSpec((TM, B_WIDTH), rowt),
            pl.BlockSpec((TM, C_WIDTH), rowt),
            pl.BlockSpec((None, N_MOD, d), lambda i, j: (mod_idx(i), 0, 0)),
            pl.BlockSpec((N_BRANCH, d, tz), lambda i, j: (0, 0, j)),
            pl.BlockSpec((N_BRANCH, 1, tz), lambda i, j: (0, 0, j)),
            pl.BlockSpec((N_BRANCH, A_WIDTH, tz), lambda i, j: (0, 0, j)),
            pl.BlockSpec((tz, d), lambda i, j: (j, 0)),
        ],
        out_specs=pl.BlockSpec((TM, d), rowt),
        out_shape=jax.ShapeDtypeStruct((n_tiles * TM, d), F32),
        compiler_params=_params(("parallel", "arbitrary")),
        name="merge",
    )(xs, h, ya, yb, yc, mod, wg, bg, wb, wo)


def _rope_tables(b, t, c):
    rows = t // GRID_W
    rowp = jnp.repeat(jnp.arange(rows), GRID_W).astype(F32)
    colp = jnp.tile(jnp.arange(GRID_W), rows).astype(F32)
    inv = 1.0 / (ROPE_BASE ** (jnp.arange(0, ROPE_AXIS_DIM, 2, dtype=F32) / ROPE_AXIS_DIM))
    ar, ac = rowp[:, None] * inv, colp[:, None] * inv
    cr, sr, cc, sc = jnp.cos(ar), jnp.sin(ar), jnp.cos(ac), jnp.sin(ac)
    cos64 = jnp.concatenate([cr, cr, cc, cc], axis=-1)
    sin64 = jnp.concatenate([-sr, sr, -sc, sc], axis=-1)
    cos_t = jnp.tile(cos64, (b, 2))
    sin_t = jnp.tile(sin64, (b, 2))
    cos_t = jnp.concatenate([cos_t, jnp.ones((b * c, LANES), F32)])
    sin_t = jnp.concatenate([sin_t, jnp.zeros((b * c, LANES), F32)])
    return cos_t, sin_t


def kernel(x, c, ctx, c_ctx, w_ada, b_ada, norm_g, ffn_w_in, ffn_w_out, w_in, gm_v_norm, gm_ws, gm_bs,
           rw_conv, rw_w0, rw_w_up, rw_a0, rw_a_up, rw_g_up, rw_k_k, rw_k_a, rw_r_k, rw_ln_g, rw_ln_b,
           da_q_norm, da_k_norm, da_lam, da_subln, w_branch, b_gate, w_out):
    b, t, d = x.shape
    cl = ctx.shape[1]
    depth = w_ada.shape[0]
    d_ff = ffn_w_out.shape[2]
    assert b == 2 and b * cl == TM and t % TM == 0 and cl % 256 == 0
    n_lat = (b * t) // TM
    n_all = n_lat + 1
    tiles_per_batch = t // TM
    mod_idx = lambda i: jnp.where(i < n_lat, i // tiles_per_batch, b)

    xs = jnp.concatenate([x.reshape(b * t, d), ctx.reshape(b * cl, d)])
    cvec = jnp.zeros((SUBLANES, d), F32).at[:b].set(c).at[b].set(c_ctx)
    b_ada3 = b_ada.reshape(depth, 1, N_MOD * d)
    cos_t, sin_t = _rope_tables(b, t, cl)
    lane = jnp.arange(LANES)
    g64 = (lane[:, None] // C_HEAD_DIM == lane[None, :] // C_HEAD_DIM).astype(BF16)
    selv, g32, rep, merge_sel = _selectors()
    w_gu = ffn_w_in.astype(BF16)
    w_dn = ffn_w_out.astype(BF16)

    o = 0
    offs = []
    for n in (A_WIDTH, A_WIDTH, 3 * B_WIDTH, GATE_LORA, 2 * DECAY_LORA, 2 * ICL_LORA, 3 * C_WIDTH, N_BRANCH * d):
        offs.append((o, o + n))
        o += n

    for l in range(depth):
        last = l == depth - 1
        lam_init = 0.8 - 0.6 * math.exp(-0.3 * l)
        mod = _ada(cvec, w_ada, b_ada3, l)[:b + 1].reshape(b + 1, N_MOD, d)

        xs, h = _ffn(xs, mod, norm_g[l], w_gu, w_dn, l, 0, mod_idx, n_tiles=n_all, mi=0, emit_h=True)

        wl = w_in[l]
        sl = lambda i: wl[:, offs[i][0]:offs[i][1]]
        n_mix = n_lat if last else n_all

        w_uv = jnp.concatenate([sl(0), sl(1)], axis=1).astype(BF16)
        bsb = jnp.broadcast_to(gm_bs[l][:, :, None], (A_GROUPS, CHUNK, A_WIDTH // A_GROUPS))
        ya = _gmlp(h, w_uv, gm_v_norm[l].reshape(1, A_WIDTH), gm_ws[l], bsb, n_mix)

        w_rkv, w_dec, w_icl = sl(2), sl(4), sl(5)
        pad_lora = lambda w: jnp.pad(w, ((0, 0), (0, LORA_PAD - w.shape[1])))
        w_b = jnp.concatenate(
            [_kperm_cols(w_rkv[:, :B_WIDTH]), _kperm_cols(w_rkv[:, B_WIDTH:2 * B_WIDTH]),
             _vperm_cols(w_rkv[:, 2 * B_WIDTH:]), sl(3),
             pad_lora(w_dec[:, :DECAY_LORA]), pad_lora(w_dec[:, DECAY_LORA:]),
             pad_lora(w_icl[:, :ICL_LORA]), pad_lora(w_icl[:, ICL_LORA:])], axis=1).astype(BF16)
        p = _bmerge(_rkv_proj(h, w_b), merge_sel, t=t, c=cl)
        conv = rw_conv[l]
        conv_m = jnp.concatenate([_both(_kmerge_cols(conv[:, :B_WIDTH])),
                                  _both(_kmerge_cols(conv[:, B_WIDTH:2 * B_WIDTH])),
                                  _both(_vmerge_cols(conv[:, 2 * B_WIDTH:]))], axis=1)
        def up_m(w):
            w32 = jnp.stack([
                _kmerge_cols(jnp.pad(w[dd], ((0, LORA_PAD - w.shape[1]), (0, 0)))).reshape(2 * LORA_PAD, MERGED)
                for dd in range(2)])
            hi = w32.astype(BF16)
            return jnp.stack([hi, (w32 - hi.astype(F32)).astype(BF16)])
        r4, kk4, kd4, b4, ptot, vrep, v2d, bonus = _rwkv_prep(
            p, conv_m, _both(_kmerge_cols(rw_w0[l])), up_m(rw_w_up[l]), _both(_kmerge_cols(rw_a0[l])),
            up_m(rw_a_up[l]), _both(_kmerge_cols(rw_k_k[l].reshape(1, B_WIDTH))),
            _both(_kmerge_cols(rw_k_a[l].reshape(1, B_WIDTH))),
            _both(_kmerge_cols(rw_r_k[l].reshape(1, B_WIDTH))), g32, rep, t=t, c=cl)
        yf, ybk = _scan(kd4, b4, kk4, r4, vrep, ptot, t=t, c=cl)
        yb2 = _rwkv_out(yf, ybk, v2d, bonus, p, _both(_vmerge_cols(rw_ln_g[l].reshape(1, B_WIDTH))),
                        _both(_vmerge_cols(rw_ln_b[l].reshape(1, B_WIDTH))),
                        _vmerge_cols(rw_g_up[l]).reshape(2 * GATE_LORA, MERGED).astype(BF16),
                        selv, g32, rows=t if last else t + cl)
        if last:
            yb = yb2.reshape(b * t, B_WIDTH)
        else:
            yb = jnp.concatenate([yb2[0, :t], yb2[1, :t], yb2[0, t:], yb2[1, t:]])

        gains = jnp.stack([jnp.tile(da_q_norm[l], 2) * (C_HEAD_DIM ** -0.5 * math.log2(math.e)),
                           jnp.tile(da_k_norm[l], 2), jnp.ones((LANES,), F32)]).reshape(3, 1, LANES)
        qk, vx = _qkv(h, sl(6).astype(BF16), gains, cos_t, sin_t, g64)
        sub = da_subln[l].reshape(1, LANES)
        yc = _attn(qk, vx, da_lam[l], sub, None, b=b, t=t, c=cl, lam_init=lam_init, ctx_queries=False)
        if not last:
            yc = _attn(qk, vx, da_lam[l], sub, yc, b=b, t=t, c=cl, lam_init=lam_init, ctx_queries=True)

        wg = jnp.transpose(sl(7).reshape(d, N_BRANCH, d), (1, 0, 2)).astype(BF16)
        wb = w_branch[l]
        wb_b = wb[1].reshape(B_HEADS, B_HEAD_DIM, d).transpose(1, 0, 2).reshape(B_WIDTH, d)
        wbr = jnp.stack([wb[0], wb_b, wb[2]]).astype(BF16)
        xs = _merge(xs, h, ya, yb, yc, mod, wg, b_gate[l].reshape(N_BRANCH, 1, d),
                    wbr, w_out[l].astype(BF16), mod_idx, n_mix)

        xs = _ffn(xs, mod, norm_g[l], w_gu, w_dn, l, 1, mod_idx, n_tiles=n_mix, mi=2, emit_h=False)

    return xs[:b * t].reshape(b, t, d)
```

```python
import functools
import math

import jax
import jax.numpy as jnp
from jax import lax
from jax.experimental import pallas as pl
from jax.experimental.pallas import tpu as pltpu

F32 = jnp.float32
BF16 = jnp.bfloat16

N_MOD = 9
CHUNK = 128
A_WIDTH = 1024
A_GROUPS = 8
B_WIDTH = 1024
B_HEAD_DIM = 64
B_HEADS = 16
DECAY_LORA = 96
ICL_LORA = 96
GATE_LORA = 256
C_HEADS = 8
C_HEAD_DIM = 64
C_WIDTH = 1024
N_BRANCH = 3
GRID_W = 64
ROPE_BASE = 10000.0
ROPE_AXIS_DIM = 32
NORM_EPS = 1e-6
LN_X_EPS = 64e-5

LANES = 128
SUBLANES = 8
VMEM_LIMIT = 56 * 1024 * 1024

TM = 512
TF = 512
LORA_PAD = 128
SCAN_TS = 64
ATT_TQ = 512
ATT_KC = 512
ADA_BANDS = 4


def _params(sem):
    return pltpu.CompilerParams(dimension_semantics=sem, vmem_limit_bytes=VMEM_LIMIT)


def _rms(x, eps):
    return x * lax.rsqrt(jnp.mean(x * x, axis=-1, keepdims=True) + eps)


def _split3(s):
    hi = s.astype(BF16)
    r = s - hi.astype(F32)
    mid = r.astype(BF16)
    lo = (r - mid.astype(F32)).astype(BF16)
    return hi, mid, lo


def _gsum(s, g_bf16):
    out = None
    for piece in _split3(s):
        d = jnp.dot(piece, g_bf16, preferred_element_type=F32)
        out = d if out is None else out + d
    return out


def _gelu_tanh(x):
    cdf = 0.5 * (1.0 + jnp.tanh(math.sqrt(2.0 / math.pi) * (x + 0.044715 * (x * x * x))))
    return x * cdf


def _ada_kernel(c_ref, *refs):
    w_refs, (b_ref, o_ref) = refs[:-2], refs[-2:]
    s = c_ref[...]
    s = s * jax.nn.sigmoid(s)
    pieces = _split3(s)
    band = s.shape[1] // len(w_refs)
    acc = b_ref[...]
    for i, w_ref in enumerate(w_refs):
        w = w_ref[...].astype(BF16)
        for piece in pieces:
            acc = acc + jnp.dot(piece[:, i * band:(i + 1) * band], w, preferred_element_type=F32)
    o_ref[...] = acc


def _ada(cvec, w_ada, b_ada3, l):
    d = cvec.shape[1]
    n = w_ada.shape[2]
    tn = 2048 if n % 2048 == 0 else 1024
    return pl.pallas_call(
        _ada_kernel,
        grid=(n // tn,),
        in_specs=[pl.BlockSpec((SUBLANES, d), lambda j: (0, 0))]
        + [pl.BlockSpec((None, d // ADA_BANDS, tn), lambda j, i=i: (l, i, j)) for i in range(ADA_BANDS)]
        + [pl.BlockSpec((None, 1, tn), lambda j: (l, 0, j))],
        out_specs=pl.BlockSpec((SUBLANES, tn), lambda j: (0, j)),
        out_shape=jax.ShapeDtypeStruct((SUBLANES, n), F32),
        compiler_params=_params(("arbitrary",)),
        name="ada",
    )(cvec, *([w_ada] * ADA_BANDS), b_ada3)


def _ffn_kernel(x_ref, mod_ref, g_ref, wg_ref, *rest, tf, nf, last_valid, mi, emit_h):
    wu_refs, wout_ref, rest = rest[:tf // LANES], rest[tf // LANES], rest[tf // LANES + 1:]
    if emit_h:
        o_ref, h_ref, hn_sc, acc_sc = rest
    else:
        o_ref, hn_sc, acc_sc = rest
    f = pl.program_id(1)

    @pl.when(f == 0)
    def _():
        xn = _rms(x_ref[...], NORM_EPS) * g_ref[mi:mi + 1, :]
        hn = xn * (1.0 + mod_ref[3 * mi + 1:3 * mi + 2, :]) + mod_ref[3 * mi:3 * mi + 1, :]
        hn_sc[...] = hn.astype(BF16)
        acc_sc[...] = jnp.zeros_like(acc_sc)

    def hidden_tile(valid):
        hn = hn_sc[...]
        g = jnp.dot(hn, wg_ref[:, :valid], preferred_element_type=F32)
        wu = jnp.concatenate([r[...] for r in wu_refs[:valid // LANES]], axis=1)
        u = jnp.dot(hn, wu, preferred_element_type=F32)
        act = (g * jax.nn.sigmoid(g) * u).astype(BF16)
        acc_sc[...] += jnp.dot(act, wout_ref[:valid, :], preferred_element_type=F32)

    if last_valid == tf:
        hidden_tile(tf)
    else:
        pl.when(f < nf - 1)(lambda: hidden_tile(tf))
        pl.when(f == nf - 1)(lambda: hidden_tile(last_valid))

    @pl.when(f == nf - 1)
    def _():
        out = x_ref[...] + 0.5 * mod_ref[3 * mi + 2:3 * mi + 3, :] * acc_sc[...]
        o_ref[...] = out
        if emit_h:
            hn = _rms(out, NORM_EPS) * g_ref[1:2, :]
            h_ref[...] = (hn * (1.0 + mod_ref[4:5, :]) + mod_ref[3:4, :]).astype(BF16)


def _ffn(xs, mod, norm_g, w_gu, w_out, l, w, mod_idx, *, n_tiles, mi, emit_h):
    m, d = xs.shape
    d_ff = w_out.shape[2]
    assert d_ff % LANES == 0
    up0 = d_ff // LANES
    nf = pl.cdiv(d_ff, TF)
    out_rows = n_tiles * TM
    out_shape = [jax.ShapeDtypeStruct((out_rows, d), F32)]
    out_specs = [pl.BlockSpec((TM, d), lambda i, f: (i, 0))]
    if emit_h:
        out_shape.append(jax.ShapeDtypeStruct((out_rows, d), BF16))
        out_specs.append(pl.BlockSpec((TM, d), lambda i, f: (i, 0)))
    res = pl.pallas_call(
        functools.partial(_ffn_kernel, tf=TF, nf=nf, last_valid=d_ff - (nf - 1) * TF, mi=mi, emit_h=emit_h),
        grid=(n_tiles, nf),
        in_specs=[
            pl.BlockSpec((TM, d), lambda i, f: (i, 0)),
            pl.BlockSpec((None, N_MOD, d), lambda i, f: (mod_idx(i), 0, 0)),
            pl.BlockSpec((3, d), lambda i, f: (0, 0)),
            pl.BlockSpec((None, None, d, TF), lambda i, f: (l, w, 0, f)),
        ] + [
            pl.BlockSpec((None, None, d, LANES),
                         lambda i, f, k=k: (l, w, 0, jnp.minimum(up0 + f * (TF // LANES) + k, 2 * up0 - 1)))
            for k in range(TF // LANES)
        ] + [
            pl.BlockSpec((None, None, TF, d), lambda i, f: (l, w, f, 0)),
        ],
        out_specs=out_specs,
        out_shape=out_shape,
        scratch_shapes=[pltpu.VMEM((TM, d), BF16), pltpu.VMEM((TM, d), F32)],
        compiler_params=_params(("parallel", "arbitrary")),
        name="ffn",
    )(xs, mod, norm_g, w_gu, *([w_gu] * (TF // LANES)), w_out)
    return res if emit_h else res[0]


def _gmlp_kernel(h_ref, w_ref, vn_ref, ws_ref, bsb_ref, o_ref, p_sc, *, tm):
    p_sc[...] = jnp.dot(h_ref[...], w_ref[...], preferred_element_type=F32)
    gd = A_WIDTH // A_GROUPS
    for g in range(A_GROUPS):
        wsg = ws_ref[g].astype(BF16)
        bias = bsb_ref[g]
        gain = vn_ref[:, g * gd:(g + 1) * gd]
        for n in range(tm // CHUNK):
            rows = slice(n * CHUNK, (n + 1) * CHUNK)
            u = _gelu_tanh(p_sc[rows, g * gd:(g + 1) * gd])
            v = _gelu_tanh(p_sc[rows, A_WIDTH + g * gd:A_WIDTH + (g + 1) * gd])
            v = _rms(v, NORM_EPS) * gain
            sv = jnp.dot(wsg, v.astype(BF16), preferred_element_type=F32) + bias
            o_ref[rows, g * gd:(g + 1) * gd] = (u * sv).astype(o_ref.dtype)


def _gmlp(h, w_uv, vn, ws, bsb, n_tiles):
    m, d = h.shape
    return pl.pallas_call(
        functools.partial(_gmlp_kernel, tm=TM),
        grid=(n_tiles,),
        in_specs=[
            pl.BlockSpec((TM, d), lambda i: (i, 0)),
            pl.BlockSpec((d, 2 * A_WIDTH), lambda i: (0, 0)),
            pl.BlockSpec((1, A_WIDTH), lambda i: (0, 0)),
            pl.BlockSpec((A_GROUPS, CHUNK, CHUNK), lambda i: (0, 0, 0)),
            pl.BlockSpec((A_GROUPS, CHUNK, A_WIDTH // A_GROUPS), lambda i: (0, 0, 0)),
        ],
        out_specs=pl.BlockSpec((TM, A_WIDTH), lambda i: (i, 0)),
        out_shape=jax.ShapeDtypeStruct((n_tiles * TM, A_WIDTH), BF16),
        scratch_shapes=[pltpu.VMEM((TM, 2 * A_WIDTH), F32)],
        compiler_params=_params(("parallel",)),
        name="gmlp",
    )(h, w_uv, vn, ws, bsb)


def _qkv_kernel(h_ref, w_ref, gain_ref, cos_ref, sin_ref, g64_ref, o_ref, vx_ref):
    j = pl.program_id(1)
    n_parts = 2
    heads_per = C_HEADS // n_parts

    def heads():
        for part in range(n_parts):
            cols = slice(part * heads_per * LANES, (part + 1) * heads_per * LANES)
            p = jnp.dot(h_ref[...], w_ref[:, cols], preferred_element_type=F32)
            for i in range(heads_per):
                yield part * heads_per + i, p[:, i * LANES:(i + 1) * LANES]

    @pl.when(j == 2)
    def _():
        ones = jnp.ones((h_ref.shape[0], LANES), BF16)
        for hs, x in heads():
            vx_ref[:, 2 * hs * LANES:(2 * hs + 1) * LANES] = x.astype(BF16)
            vx_ref[:, (2 * hs + 1) * LANES:(2 * hs + 2) * LANES] = ones

    @pl.when(j < 2)
    def _():
        lane = lax.broadcasted_iota(jnp.int32, (1, LANES), 1)
        first = (lane % ROPE_AXIS_DIM) < (ROPE_AXIS_DIM // 2)
        cos = cos_ref[...]
        sin = sin_ref[...]
        gain = gain_ref[...]
        g64 = g64_ref[...]
        for hs, x in heads():
            ms = _gsum(x * x, g64) * (1.0 / C_HEAD_DIM)
            y = x * lax.rsqrt(ms + NORM_EPS) * gain
            half = ROPE_AXIS_DIM // 2
            rot = jnp.where(first, pltpu.roll(y, LANES - half, 1), pltpu.roll(y, half, 1))
            o_ref[:, hs * LANES:(hs + 1) * LANES] = (y * cos + rot * sin).astype(BF16)


def _qkv(h, w_qkv, gains, cos_t, sin_t, g64):
    m, d = h.shape
    return pl.pallas_call(
        _qkv_kernel,
        grid=(m // TM, 3),
        in_specs=[
            pl.BlockSpec((TM, d), lambda i, j: (i, 0)),
            pl.BlockSpec((d, C_WIDTH), lambda i, j: (0, j)),
            pl.BlockSpec((None, 1, LANES), lambda i, j: (j, 0, 0)),
            pl.BlockSpec((TM, LANES), lambda i, j: (i, 0)),
            pl.BlockSpec((TM, LANES), lambda i, j: (i, 0)),
            pl.BlockSpec((LANES, LANES), lambda i, j: (0, 0)),
        ],
        out_specs=[pl.BlockSpec((None, TM, C_WIDTH), lambda i, j: (jnp.minimum(j, 1), i, 0)),
                   pl.BlockSpec((TM, 2 * C_WIDTH), lambda i, j: (i, 0))],
        out_shape=[jax.ShapeDtypeStruct((2, m, C_WIDTH), BF16), jax.ShapeDtypeStruct((m, 2 * C_WIDTH), BF16)],
        compiler_params=_params(("parallel", "arbitrary")),
        name="qkv",
    )(h, w_qkv, gains, cos_t, sin_t, g64)


def _attn_kernel(lam_ref, sub_ref, q_ref, *refs, nseg, lam_init):
    kv = refs[:2 * nseg]
    o_ref = refs[2 * nseg]
    lv = lam_ref[...]
    lam = (jnp.exp(jnp.sum(lv[0:1] * lv[1:2], axis=-1, keepdims=True))
           - jnp.exp(jnp.sum(lv[2:3] * lv[3:4], axis=-1, keepdims=True)) + lam_init)
    q = q_ref[...]
    lane = lax.broadcasted_iota(jnp.int32, (1, LANES), 1)
    zero = jnp.zeros_like(q)
    outs = []
    for j in range(2):
        in_map = (lane < C_HEAD_DIM) if j == 0 else (lane >= C_HEAD_DIM)
        qj = jnp.where(in_map, q, zero)
        m = None
        acc = None
        for si in range(nseg):
            k_ref, v_ref = kv[2 * si], kv[2 * si + 1]
            ls = k_ref.shape[0]
            for c0 in range(0, ls, ATT_KC):
                kc = min(ATT_KC, ls - c0)
                s = lax.dot_general(qj, k_ref[c0:c0 + kc, :], (((1,), (1,)), ((), ())),
                                    preferred_element_type=F32)
                cm = jnp.max(s, axis=-1, keepdims=True)
                m_new = cm if m is None else jnp.maximum(m, cm)
                e = jnp.exp2((s - m_new).astype(BF16))
                pv = jnp.dot(e, v_ref[c0:c0 + kc, :], preferred_element_type=F32)
                acc = pv if acc is None else acc * jnp.exp2(m - m_new) + pv
                m = m_new
        outs.append(acc[:, :LANES] / acc[:, LANES:])
    o = outs[0] - lam * outs[1]
    o = _rms(o, NORM_EPS) * sub_ref[...]
    o_ref[...] = (o * (1.0 - lam_init)).astype(o_ref.dtype)


def _attn(qk, vx, lam, subln, yc_prev, *, b, t, c, lam_init, ctx_queries):
    m = qk.shape[1]
    if ctx_queries:
        tq, nq = c, 1
        q_blk = lambda bi, qi: (b * t) // c + bi
        segs = [(c, lambda bi: (b * t) // c + bi)]
    else:
        tq = ATT_TQ
        nq = t // tq
        q_blk = lambda bi, qi: bi * nq + qi
        segs = [(t, lambda bi: bi), (c, lambda bi: (b * t) // c + bi)]
    in_specs = [
        pl.BlockSpec((4, C_HEAD_DIM), lambda bi, h, qi: (0, 0)),
        pl.BlockSpec((1, LANES), lambda bi, h, qi: (0, 0)),
        pl.BlockSpec((None, tq, LANES), lambda bi, h, qi: (0, q_blk(bi, qi), h)),
    ]
    args = [lam, subln, qk]
    for ls, blk in segs:
        in_specs.append(pl.BlockSpec((None, ls, LANES), lambda bi, h, qi, blk=blk: (1, blk(bi), h)))
        in_specs.append(pl.BlockSpec((ls, 2 * LANES), lambda bi, h, qi, blk=blk: (blk(bi), h)))
        args += [qk, vx]
    aliases = {}
    if yc_prev is not None:
        in_specs.append(pl.BlockSpec(memory_space=pl.ANY))
        args.append(yc_prev)
        aliases = {len(args) - 1: 0}

    def body(*refs):
        n_in = 3 + 2 * len(segs)
        _attn_kernel(*refs[:n_in], refs[-1], nseg=len(segs), lam_init=lam_init)

    return pl.pallas_call(
        body,
        grid=(b, C_HEADS, nq),
        in_specs=in_specs,
        out_specs=pl.BlockSpec((tq, LANES), lambda bi, h, qi: (q_blk(bi, qi), h)),
        out_shape=jax.ShapeDtypeStruct((m, C_WIDTH), BF16),
        input_output_aliases=aliases,
        compiler_params=_params(("parallel", "parallel", "arbitrary")),
        name="diff_attn_ctx" if ctx_queries else "diff_attn",
    )(*args)


KQ = 4
NKL = B_HEAD_DIM // KQ
MERGED = 2 * B_WIDTH
BH = 2 * B_HEADS
LORA_MERGED = 2 * GATE_LORA + 8 * LORA_PAD
PROJ_MERGED = 3 * MERGED + LORA_MERGED


PROJ_UNMERGED = 3 * B_WIDTH + GATE_LORA + 4 * LORA_PAD


def _mm_kernel(a_ref, w_ref, o_ref):
    o_ref[...] = jnp.dot(a_ref[...], w_ref[...], preferred_element_type=F32)


def _rkv_proj(h, w):
    m, d = h.shape
    tn = PROJ_UNMERGED // 3
    return pl.pallas_call(
        _mm_kernel,
        grid=(PROJ_UNMERGED // tn, m // TM),
        in_specs=[pl.BlockSpec((TM, d), lambda j, i: (i, 0)),
                  pl.BlockSpec((d, tn), lambda j, i: (0, j))],
        out_specs=pl.BlockSpec((TM, tn), lambda j, i: (i, j)),
        out_shape=jax.ShapeDtypeStruct((m, PROJ_UNMERGED), F32),
        compiler_params=_params(("parallel", "arbitrary")),
        name="rkv_proj",
    )(h, w)


def _bmerge_kernel(p0_ref, p1_ref, e_ref, o_ref):
    n_kv = 3 * B_WIDTH // LANES
    for bi, p_ref in enumerate((p0_ref, p1_ref)):
        for s in range(n_kv):
            x = p_ref[:, s * LANES:(s + 1) * LANES]
            hi = x.astype(BF16)
            lo = (x - hi.astype(F32)).astype(BF16)
            for half in range(2):
                e = e_ref[bi, half]
                part = (jnp.dot(hi, e, preferred_element_type=F32) + jnp.dot(lo, e, preferred_element_type=F32))
                cs = slice((2 * s + half) * LANES, (2 * s + half + 1) * LANES)
                if bi == 0:
                    o_ref[:, cs] = part
                else:
                    o_ref[:, cs] += part
        src = 3 * B_WIDTH
        dst = 3 * MERGED
        for width in (GATE_LORA,) + (LORA_PAD,) * 4:
            o_ref[:, dst + bi * width:dst + (bi + 1) * width] = p_ref[:, src:src + width]
            src += width
            dst += 2 * width


def _bmerge(p, e, *, t, c):
    tm = 256
    s = t + c
    nlat = t // tm
    hblk = lambda bi, i: jnp.where(i < nlat, bi * nlat + i, (2 * t + bi * c) // tm + (i - nlat))
    return pl.pallas_call(
        _bmerge_kernel,
        grid=(s // tm,),
        in_specs=[
            pl.BlockSpec((tm, PROJ_UNMERGED), lambda i: (hblk(0, i), 0)),
            pl.BlockSpec((tm, PROJ_UNMERGED), lambda i: (hblk(1, i), 0)),
            pl.BlockSpec((2, 2, LANES, LANES), lambda i: (0, 0, 0, 0)),
        ],
        out_specs=pl.BlockSpec((tm, PROJ_MERGED), lambda i: (i, 0)),
        out_shape=jax.ShapeDtypeStruct((s, PROJ_MERGED), F32),
        compiler_params=_params(("parallel",)),
        name="batch_merge",
    )(p, p, e)


class _SlabWriter:
    def __init__(self, ref, lead=()):
        self.ref, self.lead, self.parts, self.base = ref, lead, [], 0

    def add(self, slab):
        self.parts.append(slab)
        if len(self.parts) == SUBLANES:
            block = jnp.swapaxes(jnp.stack(self.parts, axis=0), 0, 1)
            self.ref[self.lead + (slice(None), slice(self.base, self.base + SUBLANES), slice(None))] = block
            self.parts, self.base = [], self.base + SUBLANES


def _rwkv_prep_kernel(p_ref, prev_ref, next_ref, lo_ref, conv_ref, w0_ref, wup_ref, a0_ref, aup_ref,
                      kkg_ref, ka_ref, rk_ref, g32_ref, rep_ref,
                      r_o, kk_o, kd_o, b_o, ptot_o, vrep_o, v_o, bonus_o, k_sc, r_sc, *, tm, ts, starts, ends):
    i = pl.program_id(0)
    is_start = functools.reduce(jnp.logical_or, [i == s for s in starts])
    is_end = functools.reduce(jnp.logical_or, [i == s for s in ends])
    row = lax.broadcasted_iota(jnp.int32, (tm, 1), 0)

    def conv(c0):
        cs = slice(c0, c0 + LANES)
        x = p_ref[:, cs]
        pm = jnp.where(is_start, 0.0, prev_ref[SUBLANES - 1:SUBLANES, cs])
        nx = jnp.where(is_end, 0.0, next_ref[0:1, cs])
        xm1 = jnp.where(row == 0, pm, pltpu.roll(x, 1, 0))
        xp1 = jnp.where(row == tm - 1, nx, pltpu.roll(x, tm - 1, 0))
        return xm1 * conv_ref[0:1, cs] + x * conv_ref[1:2, cs] + xp1 * conv_ref[2:3, cs]

    vrep_out = _SlabWriter(vrep_o)
    r_out, kk_out, kd_out, b_out = ([_SlabWriter(o, (d,)) for d in range(2)] for o in (r_o, kk_o, kd_o, b_o))

    ss = None
    for m in range(NKL):
        cs = slice(m * LANES, (m + 1) * LANES)
        r_sc[:, cs] = conv(m * LANES)
        k = conv(MERGED + m * LANES)
        k_sc[:, cs] = k
        kkr = k * kkg_ref[:, cs]
        ss = kkr * kkr if ss is None else ss + kkr * kkr
    rs = lax.rsqrt(_gsum(ss, g32_ref[...]) + 1e-12)

    w_off = 2 * GATE_LORA
    a_off = w_off + 4 * LORA_PAD
    def split2(x):
        hi = x.astype(BF16)
        return hi, (x - hi.astype(F32)).astype(BF16)

    def lora(x2, up_ref, d, cs):
        return (jnp.dot(x2[0], up_ref[0, d, :, cs], preferred_element_type=F32)
                + jnp.dot(x2[1], up_ref[0, d, :, cs], preferred_element_type=F32)
                + jnp.dot(x2[0], up_ref[1, d, :, cs], preferred_element_type=F32))

    tw = [split2(jnp.tanh(lo_ref[:, w_off + 2 * d * LORA_PAD:w_off + 2 * (d + 1) * LORA_PAD])) for d in range(2)]
    pa = [split2(lo_ref[:, a_off + 2 * d * LORA_PAD:a_off + 2 * (d + 1) * LORA_PAD]) for d in range(2)]
    bonus = None
    for m in range(NKL):
        cs = slice(m * LANES, (m + 1) * LANES)
        k = k_sc[:, cs]
        kk = k * kkg_ref[:, cs] * rs
        r = r_sc[:, cs]
        kd_sum = None
        for d in range(2):
            w_lo = w0_ref[d:d + 1, cs] + lora(tw[d], wup_ref, d, cs)
            lw = -jax.nn.sigmoid(w_lo) * math.exp(-0.5)
            cum = lw
            shift = 1
            while shift < ts:
                if d == 0:
                    cum = cum + jnp.where(row % ts >= shift, pltpu.roll(cum, shift, 0), 0.0)
                else:
                    cum = cum + jnp.where(row % ts < ts - shift, pltpu.roll(cum, tm - shift, 0), 0.0)
                shift *= 2
            p_incl = jnp.exp(cum)
            p_inv = jnp.exp(-cum)
            p_excl = jnp.exp(cum - lw)
            a = jax.nn.sigmoid(a0_ref[d:d + 1, cs] + lora(pa[d], aup_ref, d, cs))
            kd = k * (1.0 + (a - 1.0) * ka_ref[:, cs])
            kd_sum = kd if kd_sum is None else kd_sum + kd
            r_out[d].add(r * p_incl)
            kk_out[d].add(kk * p_excl)
            kd_out[d].add(kd * p_inv)
            b_out[d].add(kk * a * p_inv)
            for g in range(tm // ts):
                last = g * ts + (ts - 1 if d == 0 else 0)
                ptot_o[d, g, m:m + 1, :] = p_incl[last:last + 1, :]
        term = r * (0.5 * kd_sum) * rk_ref[:, cs]
        bonus = term if bonus is None else bonus + term
    bonus_o[...] = bonus

    per_slab = LANES // BH
    for j in range(MERGED // LANES):
        vs = conv(2 * MERGED + j * LANES)
        v_o[:, j * LANES:(j + 1) * LANES] = vs
        pieces = _split3(vs)
        for vi in range(per_slab):
            vrep_out.add(sum(jnp.dot(piece, rep_ref[vi], preferred_element_type=F32) for piece in pieces))


def _rwkv_prep(p, conv_m, w0_m, wup_m, a0_m, aup_m, kk_m, ka_m, rk_m, g32, rep, *, t, c):
    s = t + c
    tm = 128
    nt = s // tm
    starts = [0, t // tm]
    ends = [t // tm - 1, nt - 1]
    hb = tm // SUBLANES
    rkv_w = 3 * MERGED
    full2 = lambda i: (0, 0)
    full3 = lambda i: (0, 0, 0)
    ts = SCAN_TS
    k4 = jax.ShapeDtypeStruct((2, s, NKL, LANES), F32)
    k4s = pl.BlockSpec((2, tm, NKL, LANES), lambda i: (0, i, 0, 0))
    return pl.pallas_call(
        functools.partial(_rwkv_prep_kernel, tm=tm, ts=ts, starts=starts, ends=ends),
        grid=(nt,),
        in_specs=[
            pl.BlockSpec((tm, rkv_w), lambda i: (i, 0)),
            pl.BlockSpec((SUBLANES, rkv_w), lambda i: (jnp.maximum(i * hb - 1, 0), 0)),
            pl.BlockSpec((SUBLANES, rkv_w), lambda i: (jnp.minimum((i + 1) * hb, s // SUBLANES - 1), 0)),
            pl.BlockSpec((tm, LORA_MERGED), lambda i: (i, rkv_w // LORA_MERGED)),
            pl.BlockSpec((3, rkv_w), full2),
            pl.BlockSpec((2, MERGED), full2),
            pl.BlockSpec((2, 2, 2 * LORA_PAD, MERGED), lambda i: (0, 0, 0, 0)),
            pl.BlockSpec((2, MERGED), full2),
            pl.BlockSpec((2, 2, 2 * LORA_PAD, MERGED), lambda i: (0, 0, 0, 0)),
            pl.BlockSpec((1, MERGED), full2),
            pl.BlockSpec((1, MERGED), full2),
            pl.BlockSpec((1, MERGED), full2),
            pl.BlockSpec((LANES, LANES), full2),
            pl.BlockSpec((LANES // BH, LANES, LANES), full3),
        ],
        out_specs=[k4s, k4s, k4s, k4s,
                   pl.BlockSpec((2, tm // ts, NKL, LANES), lambda i: (0, i, 0, 0)),
                   pl.BlockSpec((tm, B_HEAD_DIM, LANES), lambda i: (i, 0, 0)),
                   pl.BlockSpec((tm, MERGED), lambda i: (i, 0)),
                   pl.BlockSpec((tm, LANES), lambda i: (i, 0))],
        out_shape=[k4, k4, k4, k4, jax.ShapeDtypeStruct((2, s // ts, NKL, LANES), F32),
                   jax.ShapeDtypeStruct((s, B_HEAD_DIM, LANES), F32),
                   jax.ShapeDtypeStruct((s, MERGED), F32), jax.ShapeDtypeStruct((s, LANES), F32)],
        scratch_shapes=[pltpu.VMEM((tm, MERGED), F32), pltpu.VMEM((tm, MERGED), F32)],
        compiler_params=_params(("parallel",)),
        name="rwkv_prep",
    )(p, p, p, p, conv_m, w0_m, wup_m, a0_m, aup_m, kk_m, ka_m, rk_m, g32, rep)


def _scan_kernel(*refs, ts):
    ins, (yf_ref, yb_ref, s_sc, y_sc, sa_sc) = refs[:12], refs[12:]
    streams = [ins[:6] + (yf_ref,), ins[6:] + (yb_ref,)]
    nkl = s_sc.shape[1]
    nvb = s_sc.shape[2] // SUBLANES

    @pl.when(pl.program_id(0) == 0)
    def _():
        s_sc[...] = jnp.zeros_like(s_sc)
        y_sc[...] = jnp.zeros_like(y_sc)

    def vsl(vb):
        return slice(vb * SUBLANES, (vb + 1) * SUBLANES)

    diag = (lax.broadcasted_iota(jnp.int32, (SUBLANES, LANES), 1) // BH
            == lax.broadcasted_iota(jnp.int32, (SUBLANES, LANES), 0) % KQ)

    def emit_y(d, y_ref, row):
        for vb in range(nvb):
            y = y_sc[d, vsl(vb), :]
            y = y + pltpu.roll(y, BH, 1)
            y = y + pltpu.roll(y, 2 * BH, 1)
            y = jnp.where(diag, y, 0.0)
            y = y + pltpu.roll(y, 1, 0)
            y = y + pltpu.roll(y, 2, 0)
            for half in range(SUBLANES // KQ):
                srow = half * KQ + KQ - 1
                y_ref[row, 2 * vb + half:2 * vb + half + 1, :] = y[srow:srow + 1, :]

    def allreduce(x):
        x = x + pltpu.roll(x, BH, 1)
        return x + pltpu.roll(x, 2 * BH, 1)

    def step(j, carry):
        ts_of = (j, ts - 1 - j)
        prev = (jnp.maximum(j - 1, 0), jnp.minimum(ts - j, ts - 1))
        for d in range(2):
            emit_y(d, streams[d][-1], prev[d])
        nxt = (jnp.minimum(j + 1, ts - 1), jnp.maximum(ts - 2 - j, 0))
        for d, (kd_ref, b_ref, kk_ref, r_ref, v_ref, ptot_ref, y_ref) in enumerate(streams):
            t = ts_of[d]
            sa = [sa_sc[d, vsl(vb), :] if d == 0 else allreduce(sa_sc[d, vsl(vb), :]) for vb in range(nvb)]
            ys = [None] * nvb
            sa_next = [None] * nvb
            for kl in range(nkl):
                brow = b_ref[t, kl:kl + 1, :]
                kdrow = kd_ref[t, kl:kl + 1, :]
                rrow = r_ref[t, kl:kl + 1, :]
                kknext = kk_ref[nxt[d], kl:kl + 1, :]
                for vb in range(nvb):
                    sn = s_sc[d, kl, vsl(vb), :] - sa[vb] * brow + v_ref[t, vsl(vb), :] * kdrow
                    s_sc[d, kl, vsl(vb), :] = sn
                    term = sn * rrow
                    ys[vb] = term if ys[vb] is None else ys[vb] + term
                    term = sn * kknext
                    sa_next[vb] = term if sa_next[vb] is None else sa_next[vb] + term
            for vb in range(nvb):
                y_sc[d, vsl(vb), :] = ys[vb]
                sa_sc[d, vsl(vb), :] = allreduce(sa_next[vb]) if d == 0 else sa_next[vb]
        return carry

    for d, first in enumerate((0, ts - 1)):
        kk_ref = streams[d][2]
        for vb in range(nvb):
            acc = None
            for kl in range(nkl):
                term = s_sc[d, kl, vsl(vb), :] * kk_ref[first, kl:kl + 1, :]
                acc = term if acc is None else acc + term
            sa_sc[d, vsl(vb), :] = allreduce(acc) if d == 0 else acc

    lax.fori_loop(0, ts, step, 0, unroll=2)
    emit_y(0, yf_ref, ts - 1)
    emit_y(1, yb_ref, 0)
    for d in range(2):
        ptot_ref = streams[d][5]
        for kl in range(nkl):
            prow = ptot_ref[0, kl:kl + 1, :]
            for vb in range(nvb):
                s_sc[d, kl, vsl(vb), :] = s_sc[d, kl, vsl(vb), :] * prow


def _scan(kd4, b4, kk4, r4, vrep, ptot, *, t, c):
    s = t + c
    ts = SCAN_TS
    nctx = c // ts
    nlat = t // ts
    fblk = lambda g: jnp.where(g < nctx, nlat + g, g - nctx)
    bblk = lambda g: jnp.where(g < nctx, nlat + (nctx - 1 - g), nlat - 1 - (g - nctx))
    in_specs = []
    args = []
    for d, blk in enumerate((fblk, bblk)):
        for a in (kd4, b4, kk4, r4):
            in_specs.append(pl.BlockSpec((None, ts, NKL, LANES), lambda g, d=d, blk=blk: (d, blk(g), 0, 0)))
            args.append(a)
        in_specs.append(pl.BlockSpec((ts, B_HEAD_DIM, LANES), lambda g, blk=blk: (blk(g), 0, 0)))
        args.append(vrep)
        in_specs.append(pl.BlockSpec((None, 1, NKL, LANES), lambda g, d=d, blk=blk: (d, blk(g), 0, 0)))
        args.append(ptot)
    nslab = MERGED // LANES
    yshape = jax.ShapeDtypeStruct((s, nslab, LANES), F32)
    return pl.pallas_call(
        functools.partial(_scan_kernel, ts=ts),
        grid=(s // ts,),
        in_specs=in_specs,
        out_specs=[pl.BlockSpec((ts, nslab, LANES), lambda g: (fblk(g), 0, 0)),
                   pl.BlockSpec((ts, nslab, LANES), lambda g: (bblk(g), 0, 0))],
        out_shape=[yshape, yshape],
        scratch_shapes=[pltpu.VMEM((2, NKL, B_HEAD_DIM, LANES), F32),
                        pltpu.VMEM((2, B_HEAD_DIM, LANES), F32),
                        pltpu.VMEM((2, B_HEAD_DIM, LANES), F32)],
        compiler_params=_params(("arbitrary",)),
        name="wkv7_scan",
    )(*args)


def _rwkv_out_kernel(yf_ref, yb_ref, v_ref, bonus_ref, pg_ref, lng_ref, lnb_ref, gup_ref,
                     selv_ref, g32_ref, o_ref):
    g32 = g32_ref[...]
    gate = jnp.dot(jax.nn.sigmoid(pg_ref[...]).astype(BF16), gup_ref[...], preferred_element_type=F32)
    bonus = _gsum(bonus_ref[...], g32)
    nslab = MERGED // LANES
    ys = []
    for g in range(nslab // SUBLANES):
        rows = slice(g * SUBLANES, (g + 1) * SUBLANES)
        both = jnp.swapaxes(yf_ref[:, rows, :] + yb_ref[:, rows, :], 0, 1)
        ys += [both[j] for j in range(SUBLANES)]
    inv = 1.0 / B_HEAD_DIM
    mu = _gsum(functools.reduce(lambda a, b_: a + b_, ys), g32) * inv
    ds = [y - mu for y in ys]
    var = _gsum(functools.reduce(lambda a, b_: a + b_, [x * x for x in ds]), g32) * inv
    rstd = lax.rsqrt(var + LN_X_EPS)
    outs = []
    for j in range(nslab):
        cs = slice(j * LANES, (j + 1) * LANES)
        yn = ds[j] * rstd * lng_ref[:, cs] + lnb_ref[:, cs]
        outs.append(((yn + bonus * v_ref[:, cs]) * gate[:, cs]).astype(BF16))
    for j in range(nslab // 2):
        pair = jnp.concatenate([outs[2 * j], outs[2 * j + 1]], axis=1)
        for bi in range(2):
            o_ref[bi, :, j * LANES:(j + 1) * LANES] = jnp.dot(
                pair, selv_ref[bi], preferred_element_type=F32).astype(o_ref.dtype)


def _rwkv_out(yf, yb, v2d, bonus, p, lng_m, lnb_m, gup_m, selv, g32, *, rows):
    tm = 256
    nslab = MERGED // LANES
    y3s = pl.BlockSpec((tm, nslab, LANES), lambda i: (i, 0, 0))
    full2 = lambda i: (0, 0)
    return pl.pallas_call(
        _rwkv_out_kernel,
        grid=(rows // tm,),
        in_specs=[
            y3s, y3s,
            pl.BlockSpec((tm, MERGED), lambda i: (i, 0)),
            pl.BlockSpec((tm, LANES), lambda i: (i, 0)),
            pl.BlockSpec((tm, 2 * GATE_LORA), lambda i: (i, 3 * MERGED // (2 * GATE_LORA))),
            pl.BlockSpec((1, MERGED), full2),
            pl.BlockSpec((1, MERGED), full2),
            pl.BlockSpec((2 * GATE_LORA, MERGED), full2),
            pl.BlockSpec((2, 2 * LANES, LANES), lambda i: (0, 0, 0)),
            pl.BlockSpec((LANES, LANES), full2),
        ],
        out_specs=pl.BlockSpec((2, tm, B_WIDTH), lambda i: (0, i, 0)),
        out_shape=jax.ShapeDtypeStruct((2, rows, B_WIDTH), BF16),
        compiler_params=_params(("parallel",)),
        name="rwkv_out",
    )(yf, yb, v2d, bonus, p, lng_m, lnb_m, gup_m, selv, g32)


def _kmerge_cols(w):
    r = w.shape[0]
    wt = w.reshape(r, B_HEADS, NKL, KQ).transpose(0, 2, 3, 1)
    z = jnp.zeros_like(wt)
    return jnp.stack([jnp.stack([wt, z], axis=3), jnp.stack([z, wt], axis=3)]).reshape(2, r, MERGED)


def _vmerge_cols(w):
    r = w.shape[0]
    wt = w.reshape(r, B_HEADS, B_HEAD_DIM).transpose(0, 2, 1)
    z = jnp.zeros_like(wt)
    return jnp.stack([jnp.stack([wt, z], axis=2), jnp.stack([z, wt], axis=2)]).reshape(2, r, MERGED)


def _both(m):
    return m[0] + m[1]


def _kperm_cols(w):
    r = w.shape[0]
    return w.reshape(r, B_HEADS, NKL, KQ).transpose(0, 2, 3, 1).reshape(r, B_WIDTH)


def _vperm_cols(w):
    r = w.shape[0]
    return w.reshape(r, B_HEADS, B_HEAD_DIM).transpose(0, 2, 1).reshape(r, B_WIDTH)


def _selectors():
    ci = jnp.arange(LANES)[None, :]
    bsel = jnp.arange(2)[:, None, None]
    c_vi, c_h = ci // B_HEADS, ci % B_HEADS
    r2 = jnp.arange(2 * LANES)[:, None]
    selv = ((r2 // BH == c_vi) & ((r2 % BH) // B_HEADS == bsel) & (r2 % B_HEADS == c_h)).astype(BF16)
    r1 = jnp.arange(LANES)[:, None]
    g32 = (r1 % BH == ci % BH).astype(BF16)
    rep = jnp.stack([((r1 // BH == vi) & (r1 % BH == ci % BH)) for vi in range(LANES // BH)]).astype(BF16)
    half = jnp.arange(2)[None, :, None, None]
    merge = ((r1 // (LANES // 2) == half) & ((r1 % (LANES // 2)) // B_HEADS == ci // BH)
             & ((ci % BH) // B_HEADS == bsel[:, None]) & (r1 % B_HEADS == c_h)).astype(BF16)
    return selv, g32, rep, merge


def _merge_kernel(x_ref, h_ref, ya_ref, yb_ref, yc_ref, mod_ref, wg_ref, bg_ref, wb_ref, wo_ref,
                  o_ref, *, nz):
    j = pl.program_id(1)
    h = h_ref[...]
    z = None
    for i, y_ref in enumerate((ya_ref, yb_ref, yc_ref)):
        gate = jax.nn.sigmoid(jnp.dot(h, wg_ref[i], preferred_element_type=F32) + bg_ref[i])
        term = gate * jnp.dot(y_ref[...], wb_ref[i], preferred_element_type=F32)
        z = term if z is None else z + term
    part = jnp.dot(z.astype(BF16), wo_ref[...], preferred_element_type=F32)

    @pl.when(j == 0)
    def _():
        o_ref[...] = part

    @pl.when(j > 0)
    def _():
        o_ref[...] += part

    @pl.when(j == nz - 1)
    def _():
        o_ref[...] = x_ref[...] + mod_ref[5:6, :] * o_ref[...]


def _merge(xs, h, ya, yb, yc, mod, wg, bg, wb, wo, mod_idx, n_tiles):
    d = xs.shape[1]
    tz = 512
    nz = d // tz
    rowt = lambda i, j: (i, 0)
    return pl.pallas_call(
        functools.partial(_merge_kernel, nz=nz),
        grid=(n_tiles, nz),
        in_specs=[
            pl.BlockSpec((TM, d), rowt),
            pl.BlockSpec((TM, d), rowt),
            pl.BlockSpec((TM, A_WIDTH), rowt),
            pl.BlockSpec((TM, B_WIDTH), rowt),
            pl.BlockSpec((TM, C_WIDTH), rowt),
            pl.BlockSpec((None, N_MOD, d), lambda i, j: (mod_idx(i), 0, 0)),
            pl.BlockSpec((N_BRANCH, d, tz), lambda i, j: (0, 0, j)),
            pl.BlockSpec((N_BRANCH, 1, tz), lambda i, j: (0, 0, j)),
            pl.BlockSpec((N_BRANCH, A_WIDTH, tz), lambda i, j: (0, 0, j)),
            pl.BlockSpec((tz, d), lambda i, j: (j, 0)),
        ],
        out_specs=pl.BlockSpec((TM, d), rowt),
        out_shape=jax.ShapeDtypeStruct((n_tiles * TM, d), F32),
        compiler_params=_params(("parallel", "arbitrary")),
        name="merge",
    )(xs, h, ya, yb, yc, mod, wg, bg, wb, wo)


def _rope_tables(b, t, c):
    rows = t // GRID_W
    rowp = jnp.repeat(jnp.arange(rows), GRID_W).astype(F32)
    colp = jnp.tile(jnp.arange(GRID_W), rows).astype(F32)
    inv = 1.0 / (ROPE_BASE ** (jnp.arange(0, ROPE_AXIS_DIM, 2, dtype=F32) / ROPE_AXIS_DIM))
    ar, ac = rowp[:, None] * inv, colp[:, None] * inv
    cr, sr, cc, sc = jnp.cos(ar), jnp.sin(ar), jnp.cos(ac), jnp.sin(ac)
    cos64 = jnp.concatenate([cr, cr, cc, cc], axis=-1)
    sin64 = jnp.concatenate([-sr, sr, -sc, sc], axis=-1)
    cos_t = jnp.tile(cos64, (b, 2))
    sin_t = jnp.tile(sin64, (b, 2))
    cos_t = jnp.concatenate([cos_t, jnp.ones((b * c, LANES), F32)])
    sin_t = jnp.concatenate([sin_t, jnp.zeros((b * c, LANES), F32)])
    return cos_t, sin_t


def kernel(x, c, ctx, c_ctx, w_ada, b_ada, norm_g, ffn_w_in, ffn_w_out, w_in, gm_v_norm, gm_ws, gm_bs,
           rw_conv, rw_w0, rw_w_up, rw_a0, rw_a_up, rw_g_up, rw_k_k, rw_k_a, rw_r_k, rw_ln_g, rw_ln_b,
           da_q_norm, da_k_norm, da_lam, da_subln, w_branch, b_gate, w_out):
    b, t, d = x.shape
    cl = ctx.shape[1]
    depth = w_ada.shape[0]
    d_ff = ffn_w_out.shape[2]
    assert b == 2 and b * cl == TM and t % TM == 0 and cl % 256 == 0
    n_lat = (b * t) // TM
    n_all = n_lat + 1
    tiles_per_batch = t // TM
    mod_idx = lambda i: jnp.where(i < n_lat, i // tiles_per_batch, b)

    xs = jnp.concatenate([x.reshape(b * t, d), ctx.reshape(b * cl, d)])
    cvec = jnp.zeros((SUBLANES, d), F32).at[:b].set(c).at[b].set(c_ctx)
    b_ada3 = b_ada.reshape(depth, 1, N_MOD * d)
    cos_t, sin_t = _rope_tables(b, t, cl)
    lane = jnp.arange(LANES)
    g64 = (lane[:, None] // C_HEAD_DIM == lane[None, :] // C_HEAD_DIM).astype(BF16)
    selv, g32, rep, merge_sel = _selectors()
    w_gu = ffn_w_in.astype(BF16)
    w_dn = ffn_w_out.astype(BF16)

    o = 0
    offs = []
    for n in (A_WIDTH, A_WIDTH, 3 * B_WIDTH, GATE_LORA, 2 * DECAY_LORA, 2 * ICL_LORA, 3 * C_WIDTH, N_BRANCH * d):
        offs.append((o, o + n))
        o += n

    for l in range(depth):
        last = l == depth - 1
        lam_init = 0.8 - 0.6 * math.exp(-0.3 * l)
        mod = _ada(cvec, w_ada, b_ada3, l)[:b + 1].reshape(b + 1, N_MOD, d)

        xs, h = _ffn(xs, mod, norm_g[l], w_gu, w_dn, l, 0, mod_idx, n_tiles=n_all, mi=0, emit_h=True)

        wl = w_in[l]
        sl = lambda i: wl[:, offs[i][0]:offs[i][1]]
        n_mix = n_lat if last else n_all

        w_uv = jnp.concatenate([sl(0), sl(1)], axis=1).astype(BF16)
        bsb = jnp.broadcast_to(gm_bs[l][:, :, None], (A_GROUPS, CHUNK, A_WIDTH // A_GROUPS))
        ya = _gmlp(h, w_uv, gm_v_norm[l].reshape(1, A_WIDTH), gm_ws[l], bsb, n_mix)

        w_rkv, w_dec, w_icl = sl(2), sl(4), sl(5)
        pad_lora = lambda w: jnp.pad(w, ((0, 0), (0, LORA_PAD - w.shape[1])))
        w_b = jnp.concatenate(
            [_kperm_cols(w_rkv[:, :B_WIDTH]), _kperm_cols(w_rkv[:, B_WIDTH:2 * B_WIDTH]),
             _vperm_cols(w_rkv[:, 2 * B_WIDTH:]), sl(3),
             pad_lora(w_dec[:, :DECAY_LORA]), pad_lora(w_dec[:, DECAY_LORA:]),
             pad_lora(w_icl[:, :ICL_LORA]), pad_lora(w_icl[:, ICL_LORA:])], axis=1).astype(BF16)
        p = _bmerge(_rkv_proj(h, w_b), merge_sel, t=t, c=cl)
        conv = rw_conv[l]
        conv_m = jnp.concatenate([_both(_kmerge_cols(conv[:, :B_WIDTH])),
                                  _both(_kmerge_cols(conv[:, B_WIDTH:2 * B_WIDTH])),
                                  _both(_vmerge_cols(conv[:, 2 * B_WIDTH:]))], axis=1)
        def up_m(w):
            w32 = jnp.stack([
                _kmerge_cols(jnp.pad(w[dd], ((0, LORA_PAD - w.shape[1]), (0, 0)))).reshape(2 * LORA_PAD, MERGED)
                for dd in range(2)])
            hi = w32.astype(BF16)
            return jnp.stack([hi, (w32 - hi.astype(F32)).astype(BF16)])
        r4, kk4, kd4, b4, ptot, vrep, v2d, bonus = _rwkv_prep(
            p, conv_m, _both(_kmerge_cols(rw_w0[l])), up_m(rw_w_up[l]), _both(_kmerge_cols(rw_a0[l])),
            up_m(rw_a_up[l]), _both(_kmerge_cols(rw_k_k[l].reshape(1, B_WIDTH))),
            _both(_kmerge_cols(rw_k_a[l].reshape(1, B_WIDTH))),
            _both(_kmerge_cols(rw_r_k[l].reshape(1, B_WIDTH))), g32, rep, t=t, c=cl)
        yf, ybk = _scan(kd4, b4, kk4, r4, vrep, ptot, t=t, c=cl)
        yb2 = _rwkv_out(yf, ybk, v2d, bonus, p, _both(_vmerge_cols(rw_ln_g[l].reshape(1, B_WIDTH))),
                        _both(_vmerge_cols(rw_ln_b[l].reshape(1, B_WIDTH))),
                        _vmerge_cols(rw_g_up[l]).reshape(2 * GATE_LORA, MERGED).astype(BF16),
                        selv, g32, rows=t if last else t + cl)
        if last:
            yb = yb2.reshape(b * t, B_WIDTH)
        else:
            yb = jnp.concatenate([yb2[0, :t], yb2[1, :t], yb2[0, t:], yb2[1, t:]])

        gains = jnp.stack([jnp.tile(da_q_norm[l], 2) * (C_HEAD_DIM ** -0.5 * math.log2(math.e)),
                           jnp.tile(da_k_norm[l], 2), jnp.ones((LANES,), F32)]).reshape(3, 1, LANES)
        qk, vx = _qkv(h, sl(6).astype(BF16), gains, cos_t, sin_t, g64)
        sub = da_subln[l].reshape(1, LANES)
        yc = _attn(qk, vx, da_lam[l], sub, None, b=b, t=t, c=cl, lam_init=lam_init, ctx_queries=False)
        if not last:
            yc = _attn(qk, vx, da_lam[l], sub, yc, b=b, t=t, c=cl, lam_init=lam_init, ctx_queries=True)

        wg = jnp.transpose(sl(7).reshape(d, N_BRANCH, d), (1, 0, 2)).astype(BF16)
        wb = w_branch[l]
        wb_b = wb[1].reshape(B_HEADS, B_HEAD_DIM, d).transpose(1, 0, 2).reshape(B_WIDTH, d)
        wbr = jnp.stack([wb[0], wb_b, wb[2]]).astype(BF16)
        xs = _merge(xs, h, ya, yb, yc, mod, wg, b_gate[l].reshape(N_BRANCH, 1, d),
                    wbr, w_out[l].astype(BF16), mod_idx, n_mix)

        xs = _ffn(xs, mod, norm_g[l], w_gu, w_dn, l, 1, mod_idx, n_tiles=n_mix, mi=2, emit_h=False)

    return xs[:b * t].reshape(b, t, d)
```

```python
import functools
import math

import jax
import jax.numpy as jnp
from jax import lax
from jax.experimental import pallas as pl
from jax.experimental.pallas import tpu as pltpu

F32 = jnp.float32
BF16 = jnp.bfloat16

N_MOD = 9
CHUNK = 128
A_WIDTH = 1024
A_GROUPS = 8
B_WIDTH = 1024
B_HEAD_DIM = 64
B_HEADS = 16
DECAY_LORA = 96
ICL_LORA = 96
GATE_LORA = 256
C_HEADS = 8
C_HEAD_DIM = 64
C_WIDTH = 1024
N_BRANCH = 3
GRID_W = 64
ROPE_BASE = 10000.0
ROPE_AXIS_DIM = 32
NORM_EPS = 1e-6
LN_X_EPS = 64e-5

LANES = 128
SUBLANES = 8
VMEM_LIMIT = 56 * 1024 * 1024

TM = 512
TF = 512
LORA_PAD = 128
SCAN_TS = 64
ATT_TQ = 512
ATT_KC = 512
ADA_BANDS = 4


def _params(sem):
    return pltpu.CompilerParams(dimension_semantics=sem, vmem_limit_bytes=VMEM_LIMIT)


def _rms(x, eps):
    return x * lax.rsqrt(jnp.mean(x * x, axis=-1, keepdims=True) + eps)


def _split3(s):
    hi = s.astype(BF16)
    r = s - hi.astype(F32)
    mid = r.astype(BF16)
    lo = (r - mid.astype(F32)).astype(BF16)
    return hi, mid, lo


def _gsum(s, g_bf16):
    out = None
    for piece in _split3(s):
        d = jnp.dot(piece, g_bf16, preferred_element_type=F32)
        out = d if out is None else out + d
    return out


def _gelu_tanh(x):
    cdf = 0.5 * (1.0 + jnp.tanh(math.sqrt(2.0 / math.pi) * (x + 0.044715 * (x * x * x))))
    return x * cdf


def _ada_kernel(c_ref, *refs):
    w_refs, (b_ref, o_ref) = refs[:-2], refs[-2:]
    s = c_ref[...]
    s = s * jax.nn.sigmoid(s)
    pieces = _split3(s)
    band = s.shape[1] // len(w_refs)
    acc = b_ref[...]
    for i, w_ref in enumerate(w_refs):
        w = w_ref[...].astype(BF16)
        for piece in pieces:
            acc = acc + jnp.dot(piece[:, i * band:(i + 1) * band], w, preferred_element_type=F32)
    o_ref[...] = acc


def _ada(cvec, w_ada, b_ada3, l):
    d = cvec.shape[1]
    n = w_ada.shape[2]
    tn = 2048 if n % 2048 == 0 else 1024
    return pl.pallas_call(
        _ada_kernel,
        grid=(n // tn,),
        in_specs=[pl.BlockSpec((SUBLANES, d), lambda j: (0, 0))]
        + [pl.BlockSpec((None, d // ADA_BANDS, tn), lambda j, i=i: (l, i, j)) for i in range(ADA_BANDS)]
        + [pl.BlockSpec((None, 1, tn), lambda j: (l, 0, j))],
        out_specs=pl.BlockSpec((SUBLANES, tn), lambda j: (0, j)),
        out_shape=jax.ShapeDtypeStruct((SUBLANES, n), F32),
        compiler_params=_params(("arbitrary",)),
        name="ada",
    )(cvec, *([w_ada] * ADA_BANDS), b_ada3)


def _ffn_kernel(x_ref, mod_ref, g_ref, wg_ref, *rest, tf, nf, last_valid, mi, emit_h):
    wu_refs, wout_ref, rest = rest[:tf // LANES], rest[tf // LANES], rest[tf // LANES + 1:]
    if emit_h:
        o_ref, h_ref, hn_sc, acc_sc = rest
    else:
        o_ref, hn_sc, acc_sc = rest
    f = pl.program_id(1)

    @pl.when(f == 0)
    def _():
        xn = _rms(x_ref[...], NORM_EPS) * g_ref[mi:mi + 1, :]
        hn = xn * (1.0 + mod_ref[3 * mi + 1:3 * mi + 2, :]) + mod_ref[3 * mi:3 * mi + 1, :]
        hn_sc[...] = hn.astype(BF16)
        acc_sc[...] = jnp.zeros_like(acc_sc)

    def hidden_tile(valid):
        hn = hn_sc[...]
        g = jnp.dot(hn, wg_ref[:, :valid], preferred_element_type=F32)
        wu = jnp.concatenate([r[...] for r in wu_refs[:valid // LANES]], axis=1)
        u = jnp.dot(hn, wu, preferred_element_type=F32)
        act = (g * jax.nn.sigmoid(g) * u).astype(BF16)
        acc_sc[...] += jnp.dot(act, wout_ref[:valid, :], preferred_element_type=F32)

    if last_valid == tf:
        hidden_tile(tf)
    else:
        pl.when(f < nf - 1)(lambda: hidden_tile(tf))
        pl.when(f == nf - 1)(lambda: hidden_tile(last_valid))

    @pl.when(f == nf - 1)
    def _():
        out = x_ref[...] + 0.5 * mod_ref[3 * mi + 2:3 * mi + 3, :] * acc_sc[...]
        o_ref[...] = out
        if emit_h:
            hn = _rms(out, NORM_EPS) * g_ref[1:2, :]
            h_ref[...] = (hn * (1.0 + mod_ref[4:5, :]) + mod_ref[3:4, :]).astype(BF16)


def _ffn(xs, mod, norm_g, w_gu, w_out, l, w, mod_idx, *, n_tiles, mi, emit_h):
    m, d = xs.shape
    d_ff = w_out.shape[2]
    assert d_ff % LANES == 0
    up0 = d_ff // LANES
    nf = pl.cdiv(d_ff, TF)
    out_rows = n_tiles * TM
    out_shape = [jax.ShapeDtypeStruct((out_rows, d), F32)]
    out_specs = [pl.BlockSpec((TM, d), lambda i, f: (i, 0))]
    if emit_h:
        out_shape.append(jax.ShapeDtypeStruct((out_rows, d), BF16))
        out_specs.append(pl.BlockSpec((TM, d), lambda i, f: (i, 0)))
    res = pl.pallas_call(
        functools.partial(_ffn_kernel, tf=TF, nf=nf, last_valid=d_ff - (nf - 1) * TF, mi=mi, emit_h=emit_h),
        grid=(n_tiles, nf),
        in_specs=[
            pl.BlockSpec((TM, d), lambda i, f: (i, 0)),
            pl.BlockSpec((None, N_MOD, d), lambda i, f: (mod_idx(i), 0, 0)),
            pl.BlockSpec((3, d), lambda i, f: (0, 0)),
            pl.BlockSpec((None, None, d, TF), lambda i, f: (l, w, 0, f)),
        ] + [
            pl.BlockSpec((None, None, d, LANES),
                         lambda i, f, k=k: (l, w, 0, jnp.minimum(up0 + f * (TF // LANES) + k, 2 * up0 - 1)))
            for k in range(TF // LANES)
        ] + [
            pl.BlockSpec((None, None, TF, d), lambda i, f: (l, w, f, 0)),
        ],
        out_specs=out_specs,
        out_shape=out_shape,
        scratch_shapes=[pltpu.VMEM((TM, d), BF16), pltpu.VMEM((TM, d), F32)],
        compiler_params=_params(("parallel", "arbitrary")),
        name="ffn",
    )(xs, mod, norm_g, w_gu, *([w_gu] * (TF // LANES)), w_out)
    return res if emit_h else res[0]


def _gmlp_kernel(h_ref, w_ref, vn_ref, ws_ref, bsb_ref, o_ref, p_sc, *, tm):
    p_sc[...] = jnp.dot(h_ref[...], w_ref[...], preferred_element_type=F32)
    gd = A_WIDTH // A_GROUPS
    for g in range(A_GROUPS):
        wsg = ws_ref[g].astype(BF16)
        bias = bsb_ref[g]
        gain = vn_ref[:, g * gd:(g + 1) * gd]
        for n in range(tm // CHUNK):
            rows = slice(n * CHUNK, (n + 1) * CHUNK)
            u = _gelu_tanh(p_sc[rows, g * gd:(g + 1) * gd])
            v = _gelu_tanh(p_sc[rows, A_WIDTH + g * gd:A_WIDTH + (g + 1) * gd])
            v = _rms(v, NORM_EPS) * gain
            sv = jnp.dot(wsg, v.astype(BF16), preferred_element_type=F32) + bias
            o_ref[rows, g * gd:(g + 1) * gd] = (u * sv).astype(o_ref.dtype)


def _gmlp(h, w_uv, vn, ws, bsb, n_tiles):
    m, d = h.shape
    return pl.pallas_call(
        functools.partial(_gmlp_kernel, tm=TM),
        grid=(n_tiles,),
        in_specs=[
            pl.BlockSpec((TM, d), lambda i: (i, 0)),
            pl.BlockSpec((d, 2 * A_WIDTH), lambda i: (0, 0)),
            pl.BlockSpec((1, A_WIDTH), lambda i: (0, 0)),
            pl.BlockSpec((A_GROUPS, CHUNK, CHUNK), lambda i: (0, 0, 0)),
            pl.BlockSpec((A_GROUPS, CHUNK, A_WIDTH // A_GROUPS), lambda i: (0, 0, 0)),
        ],
        out_specs=pl.BlockSpec((TM, A_WIDTH), lambda i: (i, 0)),
        out_shape=jax.ShapeDtypeStruct((n_tiles * TM, A_WIDTH), BF16),
        scratch_shapes=[pltpu.VMEM((TM, 2 * A_WIDTH), F32)],
        compiler_params=_params(("parallel",)),
        name="gmlp",
    )(h, w_uv, vn, ws, bsb)


def _qkv_kernel(h_ref, w_ref, gain_ref, cos_ref, sin_ref, g64_ref, o_ref, vx_ref):
    n_parts = 2
    heads_per = C_HEADS // n_parts
    lane = lax.broadcasted_iota(jnp.int32, (1, LANES), 1)
    first = (lane % ROPE_AXIS_DIM) < (ROPE_AXIS_DIM // 2)
    half = ROPE_AXIS_DIM // 2
    cos = cos_ref[...]
    sin = sin_ref[...]
    g64 = g64_ref[...]
    ones = jnp.ones((h_ref.shape[0], LANES), BF16)
    for j in range(3):
        for part in range(n_parts):
            c0 = j * C_WIDTH + part * heads_per * LANES
            p = jnp.dot(h_ref[...], w_ref[:, c0:c0 + heads_per * LANES], preferred_element_type=F32)
            for i in range(heads_per):
                hs = part * heads_per + i
                x = p[:, i * LANES:(i + 1) * LANES]
                if j == 2:
                    vx_ref[:, 2 * hs * LANES:(2 * hs + 1) * LANES] = x.astype(BF16)
                    vx_ref[:, (2 * hs + 1) * LANES:(2 * hs + 2) * LANES] = ones
                else:
                    ms = _gsum(x * x, g64) * (1.0 / C_HEAD_DIM)
                    y = x * lax.rsqrt(ms + NORM_EPS) * gain_ref[j]
                    rot = jnp.where(first, pltpu.roll(y, LANES - half, 1), pltpu.roll(y, half, 1))
                    o_ref[j, :, hs * LANES:(hs + 1) * LANES] = (y * cos + rot * sin).astype(BF16)


def _qkv(h, w_qkv, gains, cos_t, sin_t, g64):
    m, d = h.shape
    return pl.pallas_call(
        _qkv_kernel,
        grid=(m // TM,),
        in_specs=[
            pl.BlockSpec((TM, d), lambda i: (i, 0)),
            pl.BlockSpec((d, 3 * C_WIDTH), lambda i: (0, 0)),
            pl.BlockSpec((2, 1, LANES), lambda i: (0, 0, 0)),
            pl.BlockSpec((TM, LANES), lambda i: (i, 0)),
            pl.BlockSpec((TM, LANES), lambda i: (i, 0)),
            pl.BlockSpec((LANES, LANES), lambda i: (0, 0)),
        ],
        out_specs=[pl.BlockSpec((2, TM, C_WIDTH), lambda i: (0, i, 0)),
                   pl.BlockSpec((TM, 2 * C_WIDTH), lambda i: (i, 0))],
        out_shape=[jax.ShapeDtypeStruct((2, m, C_WIDTH), BF16), jax.ShapeDtypeStruct((m, 2 * C_WIDTH), BF16)],
        compiler_params=_params(("parallel",)),
        name="qkv",
    )(h, w_qkv, gains, cos_t, sin_t, g64)


def _attn_kernel(lam_ref, sub_ref, q_ref, *refs, nseg, lam_init):
    kv = refs[:2 * nseg]
    o_ref = refs[2 * nseg]
    lv = lam_ref[...]
    lam = (jnp.exp(jnp.sum(lv[0:1] * lv[1:2], axis=-1, keepdims=True))
           - jnp.exp(jnp.sum(lv[2:3] * lv[3:4], axis=-1, keepdims=True)) + lam_init)
    q = q_ref[...]
    lane = lax.broadcasted_iota(jnp.int32, (1, LANES), 1)
    zero = jnp.zeros_like(q)
    outs = []
    for j in range(2):
        in_map = (lane < C_HEAD_DIM) if j == 0 else (lane >= C_HEAD_DIM)
        qj = jnp.where(in_map, q, zero)
        m = None
        acc = None
        for si in range(nseg):
            k_ref, v_ref = kv[2 * si], kv[2 * si + 1]
            ls = k_ref.shape[0]
            for c0 in range(0, ls, ATT_KC):
                kc = min(ATT_KC, ls - c0)
                s = lax.dot_general(qj, k_ref[c0:c0 + kc, :], (((1,), (1,)), ((), ())),
                                    preferred_element_type=F32)
                cm = jnp.max(s, axis=-1, keepdims=True)
                m_new = cm if m is None else jnp.maximum(m, cm)
                e = jnp.exp2((s - m_new).astype(BF16))
                pv = jnp.dot(e, v_ref[c0:c0 + kc, :], preferred_element_type=F32)
                acc = pv if acc is None else acc * jnp.exp2(m - m_new) + pv
                m = m_new
        outs.append(acc[:, :LANES] / acc[:, LANES:])
    o = outs[0] - lam * outs[1]
    o = _rms(o, NORM_EPS) * sub_ref[...]
    o_ref[...] = (o * (1.0 - lam_init)).astype(o_ref.dtype)


def _attn(qk, vx, lam, subln, yc_prev, *, b, t, c, lam_init, ctx_queries):
    m = qk.shape[1]
    if ctx_queries:
        tq, nq = c, 1
        q_blk = lambda bi, qi: (b * t) // c + bi
        segs = [(c, lambda bi: (b * t) // c + bi)]
    else:
        tq = ATT_TQ
        nq = t // tq
        q_blk = lambda bi, qi: bi * nq + qi
        segs = [(t, lambda bi: bi), (c, lambda bi: (b * t) // c + bi)]
    in_specs = [
        pl.BlockSpec((4, C_HEAD_DIM), lambda bi, h, qi: (0, 0)),
        pl.BlockSpec((1, LANES), lambda bi, h, qi: (0, 0)),
        pl.BlockSpec((None, tq, LANES), lambda bi, h, qi: (0, q_blk(bi, qi), h)),
    ]
    args = [lam, subln, qk]
    for ls, blk in segs:
        in_specs.append(pl.BlockSpec((None, ls, LANES), lambda bi, h, qi, blk=blk: (1, blk(bi), h)))
        in_specs.append(pl.BlockSpec((ls, 2 * LANES), lambda bi, h, qi, blk=blk: (blk(bi), h)))
        args += [qk, vx]
    aliases = {}
    if yc_prev is not None:
        in_specs.append(pl.BlockSpec(memory_space=pl.ANY))
        args.append(yc_prev)
        aliases = {len(args) - 1: 0}

    def body(*refs):
        n_in = 3 + 2 * len(segs)
        _attn_kernel(*refs[:n_in], refs[-1], nseg=len(segs), lam_init=lam_init)

    return pl.pallas_call(
        body,
        grid=(b, C_HEADS, nq),
        in_specs=in_specs,
        out_specs=pl.BlockSpec((tq, LANES), lambda bi, h, qi: (q_blk(bi, qi), h)),
        out_shape=jax.ShapeDtypeStruct((m, C_WIDTH), BF16),
        input_output_aliases=aliases,
        compiler_params=_params(("parallel", "parallel", "arbitrary")),
        name="diff_attn_ctx" if ctx_queries else "diff_attn",
    )(*args)


KQ = 4
NKL = B_HEAD_DIM // KQ
MERGED = 2 * B_WIDTH
BH = 2 * B_HEADS
LORA_MERGED = 2 * GATE_LORA + 8 * LORA_PAD
PROJ_MERGED = 3 * MERGED + LORA_MERGED


PROJ_UNMERGED = 3 * B_WIDTH + GATE_LORA + 4 * LORA_PAD


def _mm_kernel(a_ref, w_ref, o_ref):
    o_ref[...] = jnp.dot(a_ref[...], w_ref[...], preferred_element_type=F32)


def _rkv_proj(h, w):
    m, d = h.shape
    tn = PROJ_UNMERGED // 3
    return pl.pallas_call(
        _mm_kernel,
        grid=(PROJ_UNMERGED // tn, m // TM),
        in_specs=[pl.BlockSpec((TM, d), lambda j, i: (i, 0)),
                  pl.BlockSpec((d, tn), lambda j, i: (0, j))],
        out_specs=pl.BlockSpec((TM, tn), lambda j, i: (i, j)),
        out_shape=jax.ShapeDtypeStruct((m, PROJ_UNMERGED), F32),
        compiler_params=_params(("parallel", "arbitrary")),
        name="rkv_proj",
    )(h, w)


def _bmerge_kernel(p0_ref, p1_ref, e_ref, o_ref):
    n_kv = 3 * B_WIDTH // LANES
    for bi, p_ref in enumerate((p0_ref, p1_ref)):
        for s in range(n_kv):
            x = p_ref[:, s * LANES:(s + 1) * LANES]
            hi = x.astype(BF16)
            lo = (x - hi.astype(F32)).astype(BF16)
            for half in range(2):
                e = e_ref[bi, half]
                part = (jnp.dot(hi, e, preferred_element_type=F32) + jnp.dot(lo, e, preferred_element_type=F32))
                cs = slice((2 * s + half) * LANES, (2 * s + half + 1) * LANES)
                if bi == 0:
                    o_ref[:, cs] = part
                else:
                    o_ref[:, cs] += part
        src = 3 * B_WIDTH
        dst = 3 * MERGED
        for width in (GATE_LORA,) + (LORA_PAD,) * 4:
            o_ref[:, dst + bi * width:dst + (bi + 1) * width] = p_ref[:, src:src + width]
            src += width
            dst += 2 * width


def _bmerge(p, e, *, t, c):
    tm = 256
    s = t + c
    nlat = t // tm
    hblk = lambda bi, i: jnp.where(i < nlat, bi * nlat + i, (2 * t + bi * c) // tm + (i - nlat))
    return pl.pallas_call(
        _bmerge_kernel,
        grid=(s // tm,),
        in_specs=[
            pl.BlockSpec((tm, PROJ_UNMERGED), lambda i: (hblk(0, i), 0)),
            pl.BlockSpec((tm, PROJ_UNMERGED), lambda i: (hblk(1, i), 0)),
            pl.BlockSpec((2, 2, LANES, LANES), lambda i: (0, 0, 0, 0)),
        ],
        out_specs=pl.BlockSpec((tm, PROJ_MERGED), lambda i: (i, 0)),
        out_shape=jax.ShapeDtypeStruct((s, PROJ_MERGED), F32),
        compiler_params=_params(("parallel",)),
        name="batch_merge",
    )(p, p, e)


class _SlabWriter:
    def __init__(self, ref, lead=()):
        self.ref, self.lead, self.parts, self.base = ref, lead, [], 0

    def add(self, slab):
        self.parts.append(slab)
        if len(self.parts) == SUBLANES:
            block = jnp.swapaxes(jnp.stack(self.parts, axis=0), 0, 1)
            self.ref[self.lead + (slice(None), slice(self.base, self.base + SUBLANES), slice(None))] = block
            self.parts, self.base = [], self.base + SUBLANES


def _rwkv_prep_kernel(p_ref, prev_ref, next_ref, lo_ref, conv_ref, w0_ref, wup_ref, a0_ref, aup_ref,
                      kkg_ref, ka_ref, rk_ref, g32_ref, rep_ref,
                      r_o, kk_o, kd_o, b_o, ptot_o, vrep_o, v_o, bonus_o, k_sc, r_sc, *, tm, ts, starts, ends):
    i = pl.program_id(0)
    is_start = functools.reduce(jnp.logical_or, [i == s for s in starts])
    is_end = functools.reduce(jnp.logical_or, [i == s for s in ends])
    row = lax.broadcasted_iota(jnp.int32, (tm, 1), 0)

    def conv(c0):
        cs = slice(c0, c0 + LANES)
        x = p_ref[:, cs]
        pm = jnp.where(is_start, 0.0, prev_ref[SUBLANES - 1:SUBLANES, cs])
        nx = jnp.where(is_end, 0.0, next_ref[0:1, cs])
        xm1 = jnp.where(row == 0, pm, pltpu.roll(x, 1, 0))
        xp1 = jnp.where(row == tm - 1, nx, pltpu.roll(x, tm - 1, 0))
        return xm1 * conv_ref[0:1, cs] + x * conv_ref[1:2, cs] + xp1 * conv_ref[2:3, cs]

    vrep_out = _SlabWriter(vrep_o)
    r_out, kk_out, kd_out, b_out = ([_SlabWriter(o, (d,)) for d in range(2)] for o in (r_o, kk_o, kd_o, b_o))

    ss = None
    for m in range(NKL):
        cs = slice(m * LANES, (m + 1) * LANES)
        r_sc[:, cs] = conv(m * LANES)
        k = conv(MERGED + m * LANES)
        k_sc[:, cs] = k
        kkr = k * kkg_ref[:, cs]
        ss = kkr * kkr if ss is None else ss + kkr * kkr
    rs = lax.rsqrt(_gsum(ss, g32_ref[...]) + 1e-12)

    w_off = 2 * GATE_LORA
    a_off = w_off + 4 * LORA_PAD
    def split2(x):
        hi = x.astype(BF16)
        return hi, (x - hi.astype(F32)).astype(BF16)

    def lora(x2, up_ref, d, cs):
        return (jnp.dot(x2[0], up_ref[0, d, :, cs], preferred_element_type=F32)
                + jnp.dot(x2[1], up_ref[0, d, :, cs], preferred_element_type=F32)
                + jnp.dot(x2[0], up_ref[1, d, :, cs], preferred_element_type=F32))

    tw = [split2(jnp.tanh(lo_ref[:, w_off + 2 * d * LORA_PAD:w_off + 2 * (d + 1) * LORA_PAD])) for d in range(2)]
    pa = [split2(lo_ref[:, a_off + 2 * d * LORA_PAD:a_off + 2 * (d + 1) * LORA_PAD]) for d in range(2)]
    bonus = None
    for m in range(NKL):
        cs = slice(m * LANES, (m + 1) * LANES)
        k = k_sc[:, cs]
        kk = k * kkg_ref[:, cs] * rs
        r = r_sc[:, cs]
        kd_sum = None
        for d in range(2):
            w_lo = w0_ref[d:d + 1, cs] + lora(tw[d], wup_ref, d, cs)
            lw = -jax.nn.sigmoid(w_lo) * math.exp(-0.5)
            cum = lw
            shift = 1
            while shift < ts:
                if d == 0:
                    cum = cum + jnp.where(row % ts >= shift, pltpu.roll(cum, shift, 0), 0.0)
                else:
                    cum = cum + jnp.where(row % ts < ts - shift, pltpu.roll(cum, tm - shift, 0), 0.0)
                shift *= 2
            p_incl = jnp.exp(cum)
            p_inv = jnp.exp(-cum)
            p_excl = jnp.exp(cum - lw)
            a = jax.nn.sigmoid(a0_ref[d:d + 1, cs] + lora(pa[d], aup_ref, d, cs))
            kd = k * (1.0 + (a - 1.0) * ka_ref[:, cs])
            kd_sum = kd if kd_sum is None else kd_sum + kd
            r_out[d].add(r * p_incl)
            kk_out[d].add(kk * p_excl)
            kd_out[d].add(kd * p_inv)
            b_out[d].add(kk * a * p_inv)
            for g in range(tm // ts):
                last = g * ts + (ts - 1 if d == 0 else 0)
                ptot_o[d, g, m:m + 1, :] = p_incl[last:last + 1, :]
        term = r * (0.5 * kd_sum) * rk_ref[:, cs]
        bonus = term if bonus is None else bonus + term
    bonus_o[...] = bonus

    per_slab = LANES // BH
    for j in range(MERGED // LANES):
        vs = conv(2 * MERGED + j * LANES)
        v_o[:, j * LANES:(j + 1) * LANES] = vs
        pieces = _split3(vs)
        for vi in range(per_slab):
            vrep_out.add(sum(jnp.dot(piece, rep_ref[vi], preferred_element_type=F32) for piece in pieces))


def _rwkv_prep(p, conv_m, w0_m, wup_m, a0_m, aup_m, kk_m, ka_m, rk_m, g32, rep, *, t, c):
    s = t + c
    tm = 128
    nt = s // tm
    starts = [0, t // tm]
    ends = [t // tm - 1, nt - 1]
    hb = tm // SUBLANES
    rkv_w = 3 * MERGED
    full2 = lambda i: (0, 0)
    full3 = lambda i: (0, 0, 0)
    ts = SCAN_TS
    k4 = jax.ShapeDtypeStruct((2, s, NKL, LANES), F32)
    k4s = pl.BlockSpec((2, tm, NKL, LANES), lambda i: (0, i, 0, 0))
    return pl.pallas_call(
        functools.partial(_rwkv_prep_kernel, tm=tm, ts=ts, starts=starts, ends=ends),
        grid=(nt,),
        in_specs=[
            pl.BlockSpec((tm, rkv_w), lambda i: (i, 0)),
            pl.BlockSpec((SUBLANES, rkv_w), lambda i: (jnp.maximum(i * hb - 1, 0), 0)),
            pl.BlockSpec((SUBLANES, rkv_w), lambda i: (jnp.minimum((i + 1) * hb, s // SUBLANES - 1), 0)),
            pl.BlockSpec((tm, LORA_MERGED), lambda i: (i, rkv_w // LORA_MERGED)),
            pl.BlockSpec((3, rkv_w), full2),
            pl.BlockSpec((2, MERGED), full2),
            pl.BlockSpec((2, 2, 2 * LORA_PAD, MERGED), lambda i: (0, 0, 0, 0)),
            pl.BlockSpec((2, MERGED), full2),
            pl.BlockSpec((2, 2, 2 * LORA_PAD, MERGED), lambda i: (0, 0, 0, 0)),
            pl.BlockSpec((1, MERGED), full2),
            pl.BlockSpec((1, MERGED), full2),
            pl.BlockSpec((1, MERGED), full2),
            pl.BlockSpec((LANES, LANES), full2),
            pl.BlockSpec((LANES // BH, LANES, LANES), full3),
        ],
        out_specs=[k4s, k4s, k4s, k4s,
                   pl.BlockSpec((2, tm // ts, NKL, LANES), lambda i: (0, i, 0, 0)),
                   pl.BlockSpec((tm, B_HEAD_DIM, LANES), lambda i: (i, 0, 0)),
                   pl.BlockSpec((tm, MERGED), lambda i: (i, 0)),
                   pl.BlockSpec((tm, LANES), lambda i: (i, 0))],
        out_shape=[k4, k4, k4, k4, jax.ShapeDtypeStruct((2, s // ts, NKL, LANES), F32),
                   jax.ShapeDtypeStruct((s, B_HEAD_DIM, LANES), F32),
                   jax.ShapeDtypeStruct((s, MERGED), F32), jax.ShapeDtypeStruct((s, LANES), F32)],
        scratch_shapes=[pltpu.VMEM((tm, MERGED), F32), pltpu.VMEM((tm, MERGED), F32)],
        compiler_params=_params(("parallel",)),
        name="rwkv_prep",
    )(p, p, p, p, conv_m, w0_m, wup_m, a0_m, aup_m, kk_m, ka_m, rk_m, g32, rep)


def _scan_kernel(*refs, ts):
    ins, (yf_ref, yb_ref, s_sc, y_sc, sa_sc) = refs[:12], refs[12:]
    streams = [ins[:6] + (yf_ref,), ins[6:] + (yb_ref,)]
    nkl = s_sc.shape[1]
    nvb = s_sc.shape[2] // SUBLANES

    @pl.when(pl.program_id(0) == 0)
    def _():
        s_sc[...] = jnp.zeros_like(s_sc)
        y_sc[...] = jnp.zeros_like(y_sc)

    def vsl(vb):
        return slice(vb * SUBLANES, (vb + 1) * SUBLANES)

    diag = (lax.broadcasted_iota(jnp.int32, (SUBLANES, LANES), 1) // BH
            == lax.broadcasted_iota(jnp.int32, (SUBLANES, LANES), 0) % KQ)

    def emit_y(d, y_ref, row):
        for vb in range(nvb):
            y = y_sc[d, vsl(vb), :]
            y = y + pltpu.roll(y, BH, 1)
            y = y + pltpu.roll(y, 2 * BH, 1)
            y = jnp.where(diag, y, 0.0)
            y = y + pltpu.roll(y, 1, 0)
            y = y + pltpu.roll(y, 2, 0)
            for half in range(SUBLANES // KQ):
                srow = half * KQ + KQ - 1
                y_ref[row, 2 * vb + half:2 * vb + half + 1, :] = y[srow:srow + 1, :]

    def allreduce(x):
        x = x + pltpu.roll(x, BH, 1)
        return x + pltpu.roll(x, 2 * BH, 1)

    def step(j, carry):
        ts_of = (j, ts - 1 - j)
        prev = (jnp.maximum(j - 1, 0), jnp.minimum(ts - j, ts - 1))
        for d in range(2):
            emit_y(d, streams[d][-1], prev[d])
        nxt = (jnp.minimum(j + 1, ts - 1), jnp.maximum(ts - 2 - j, 0))
        for d, (kd_ref, b_ref, kk_ref, r_ref, v_ref, ptot_ref, y_ref) in enumerate(streams):
            t = ts_of[d]
            sa = [sa_sc[d, vsl(vb), :] if d == 0 else allreduce(sa_sc[d, vsl(vb), :]) for vb in range(nvb)]
            ys = [None] * nvb
            sa_next = [None] * nvb
            for kl in range(nkl):
                brow = b_ref[t, kl:kl + 1, :]
                kdrow = kd_ref[t, kl:kl + 1, :]
                rrow = r_ref[t, kl:kl + 1, :]
                kknext = kk_ref[nxt[d], kl:kl + 1, :]
                for vb in range(nvb):
                    sn = s_sc[d, kl, vsl(vb), :] - sa[vb] * brow + v_ref[t, vsl(vb), :] * kdrow
                    s_sc[d, kl, vsl(vb), :] = sn
                    term = sn * rrow
                    ys[vb] = term if ys[vb] is None else ys[vb] + term
                    term = sn * kknext
                    sa_next[vb] = term if sa_next[vb] is None else sa_next[vb] + term
            for vb in range(nvb):
                y_sc[d, vsl(vb), :] = ys[vb]
                sa_sc[d, vsl(vb), :] = allreduce(sa_next[vb]) if d == 0 else sa_next[vb]
        return carry

    for d, first in enumerate((0, ts - 1)):
        kk_ref = streams[d][2]
        for vb in range(nvb):
            acc = None
            for kl in range(nkl):
                term = s_sc[d, kl, vsl(vb), :] * kk_ref[first, kl:kl + 1, :]
                acc = term if acc is None else acc + term
            sa_sc[d, vsl(vb), :] = allreduce(acc) if d == 0 else acc

    lax.fori_loop(0, ts, step, 0, unroll=2)
    emit_y(0, yf_ref, ts - 1)
    emit_y(1, yb_ref, 0)
    for d in range(2):
        ptot_ref = streams[d][5]
        for kl in range(nkl):
            prow = ptot_ref[0, kl:kl + 1, :]
            for vb in range(nvb):
                s_sc[d, kl, vsl(vb), :] = s_sc[d, kl, vsl(vb), :] * prow


def _scan(kd4, b4, kk4, r4, vrep, ptot, *, t, c):
    s = t + c
    ts = SCAN_TS
    nctx = c // ts
    nlat = t // ts
    fblk = lambda g: jnp.where(g < nctx, nlat + g, g - nctx)
    bblk = lambda g: jnp.where(g < nctx, nlat + (nctx - 1 - g), nlat - 1 - (g - nctx))
    in_specs = []
    args = []
    for d, blk in enumerate((fblk, bblk)):
        for a in (kd4, b4, kk4, r4):
            in_specs.append(pl.BlockSpec((None, ts, NKL, LANES), lambda g, d=d, blk=blk: (d, blk(g), 0, 0)))
            args.append(a)
        in_specs.append(pl.BlockSpec((ts, B_HEAD_DIM, LANES), lambda g, blk=blk: (blk(g), 0, 0)))
        args.append(vrep)
        in_specs.append(pl.BlockSpec((None, 1, NKL, LANES), lambda g, d=d, blk=blk: (d, blk(g), 0, 0)))
        args.append(ptot)
    nslab = MERGED // LANES
    yshape = jax.ShapeDtypeStruct((s, nslab, LANES), F32)
    return pl.pallas_call(
        functools.partial(_scan_kernel, ts=ts),
        grid=(s // ts,),
        in_specs=in_specs,
        out_specs=[pl.BlockSpec((ts, nslab, LANES), lambda g: (fblk(g), 0, 0)),
                   pl.BlockSpec((ts, nslab, LANES), lambda g: (bblk(g), 0, 0))],
        out_shape=[yshape, yshape],
        scratch_shapes=[pltpu.VMEM((2, NKL, B_HEAD_DIM, LANES), F32),
                        pltpu.VMEM((2, B_HEAD_DIM, LANES), F32),
                        pltpu.VMEM((2, B_HEAD_DIM, LANES), F32)],
        compiler_params=_params(("arbitrary",)),
        name="wkv7_scan",
    )(*args)


def _rwkv_out_kernel(yf_ref, yb_ref, v_ref, bonus_ref, pg_ref, lng_ref, lnb_ref, gup_ref,
                     selv_ref, g32_ref, o_ref):
    g32 = g32_ref[...]
    gate = jnp.dot(jax.nn.sigmoid(pg_ref[...]).astype(BF16), gup_ref[...], preferred_element_type=F32)
    bonus = _gsum(bonus_ref[...], g32)
    nslab = MERGED // LANES
    ys = []
    for g in range(nslab // SUBLANES):
        rows = slice(g * SUBLANES, (g + 1) * SUBLANES)
        both = jnp.swapaxes(yf_ref[:, rows, :] + yb_ref[:, rows, :], 0, 1)
        ys += [both[j] for j in range(SUBLANES)]
    inv = 1.0 / B_HEAD_DIM
    mu = _gsum(functools.reduce(lambda a, b_: a + b_, ys), g32) * inv
    ds = [y - mu for y in ys]
    var = _gsum(functools.reduce(lambda a, b_: a + b_, [x * x for x in ds]), g32) * inv
    rstd = lax.rsqrt(var + LN_X_EPS)
    outs = []
    for j in range(nslab):
        cs = slice(j * LANES, (j + 1) * LANES)
        yn = ds[j] * rstd * lng_ref[:, cs] + lnb_ref[:, cs]
        outs.append(((yn + bonus * v_ref[:, cs]) * gate[:, cs]).astype(BF16))
    for j in range(nslab // 2):
        pair = jnp.concatenate([outs[2 * j], outs[2 * j + 1]], axis=1)
        for bi in range(2):
            o_ref[bi, :, j * LANES:(j + 1) * LANES] = jnp.dot(
                pair, selv_ref[bi], preferred_element_type=F32).astype(o_ref.dtype)


def _rwkv_out(yf, yb, v2d, bonus, p, lng_m, lnb_m, gup_m, selv, g32, *, rows):
    tm = 256
    nslab = MERGED // LANES
    y3s = pl.BlockSpec((tm, nslab, LANES), lambda i: (i, 0, 0))
    full2 = lambda i: (0, 0)
    return pl.pallas_call(
        _rwkv_out_kernel,
        grid=(rows // tm,),
        in_specs=[
            y3s, y3s,
            pl.BlockSpec((tm, MERGED), lambda i: (i, 0)),
            pl.BlockSpec((tm, LANES), lambda i: (i, 0)),
            pl.BlockSpec((tm, 2 * GATE_LORA), lambda i: (i, 3 * MERGED // (2 * GATE_LORA))),
            pl.BlockSpec((1, MERGED), full2),
            pl.BlockSpec((1, MERGED), full2),
            pl.BlockSpec((2 * GATE_LORA, MERGED), full2),
            pl.BlockSpec((2, 2 * LANES, LANES), lambda i: (0, 0, 0)),
            pl.BlockSpec((LANES, LANES), full2),
        ],
        out_specs=pl.BlockSpec((2, tm, B_WIDTH), lambda i: (0, i, 0)),
        out_shape=jax.ShapeDtypeStruct((2, rows, B_WIDTH), BF16),
        compiler_params=_params(("parallel",)),
        name="rwkv_out",
    )(yf, yb, v2d, bonus, p, lng_m, lnb_m, gup_m, selv, g32)


def _kmerge_cols(w):
    r = w.shape[0]
    wt = w.reshape(r, B_HEADS, NKL, KQ).transpose(0, 2, 3, 1)
    z = jnp.zeros_like(wt)
    return jnp.stack([jnp.stack([wt, z], axis=3), jnp.stack([z, wt], axis=3)]).reshape(2, r, MERGED)


def _vmerge_cols(w):
    r = w.shape[0]
    wt = w.reshape(r, B_HEADS, B_HEAD_DIM).transpose(0, 2, 1)
    z = jnp.zeros_like(wt)
    return jnp.stack([jnp.stack([wt, z], axis=2), jnp.stack([z, wt], axis=2)]).reshape(2, r, MERGED)


def _both(m):
    return m[0] + m[1]


def _kperm_cols(w):
    r = w.shape[0]
    return w.reshape(r, B_HEADS, NKL, KQ).transpose(0, 2, 3, 1).reshape(r, B_WIDTH)


def _vperm_cols(w):
    r = w.shape[0]
    return w.reshape(r, B_HEADS, B_HEAD_DIM).transpose(0, 2, 1).reshape(r, B_WIDTH)


def _selectors():
    ci = jnp.arange(LANES)[None, :]
    bsel = jnp.arange(2)[:, None, None]
    c_vi, c_h = ci // B_HEADS, ci % B_HEADS
    r2 = jnp.arange(2 * LANES)[:, None]
    selv = ((r2 // BH == c_vi) & ((r2 % BH) // B_HEADS == bsel) & (r2 % B_HEADS == c_h)).astype(BF16)
    r1 = jnp.arange(LANES)[:, None]
    g32 = (r1 % BH == ci % BH).astype(BF16)
    rep = jnp.stack([((r1 // BH == vi) & (r1 % BH == ci % BH)) for vi in range(LANES // BH)]).astype(BF16)
    half = jnp.arange(2)[None, :, None, None]
    merge = ((r1 // (LANES // 2) == half) & ((r1 % (LANES // 2)) // B_HEADS == ci // BH)
             & ((ci % BH) // B_HEADS == bsel[:, None]) & (r1 % B_HEADS == c_h)).astype(BF16)
    return selv, g32, rep, merge


def _merge_kernel(x_ref, h_ref, ya_ref, yb_ref, yc_ref, mod_ref, wg_ref, bg_ref, wb_ref, wo_ref,
                  o_ref, *, nz):
    j = pl.program_id(1)
    h = h_ref[...]
    z = None
    for i, y_ref in enumerate((ya_ref, yb_ref, yc_ref)):
        gate = jax.nn.sigmoid(jnp.dot(h, wg_ref[i], preferred_element_type=F32) + bg_ref[i])
        term = gate * jnp.dot(y_ref[...], wb_ref[i], preferred_element_type=F32)
        z = term if z is None else z + term
    part = jnp.dot(z.astype(BF16), wo_ref[...], preferred_element_type=F32)

    @pl.when(j == 0)
    def _():
        o_ref[...] = part

    @pl.when(j > 0)
    def _():
        o_ref[...] += part

    @pl.when(j == nz - 1)
    def _():
        o_ref[...] = x_ref[...] + mod_ref[5:6, :] * o_ref[...]


def _merge(xs, h, ya, yb, yc, mod, wg, bg, wb, wo, mod_idx, n_tiles):
    d = xs.shape[1]
    tz = 512
    nz = d // tz
    rowt = lambda i, j: (i, 0)
    return pl.pallas_call(
        functools.partial(_merge_kernel, nz=nz),
        grid=(n_tiles, nz),
        in_specs=[
            pl.BlockSpec((TM, d), rowt),
            pl.BlockSpec((TM, d), rowt),
            pl.BlockSpec((TM, A_WIDTH), rowt),
            pl.BlockSpec((TM, B_WIDTH), rowt),
            pl.BlockSpec((TM, C_WIDTH), rowt),
            pl.BlockSpec((None, N_MOD, d), lambda i, j: (mod_idx(i), 0, 0)),
            pl.BlockSpec((N_BRANCH, d, tz), lambda i, j: (0, 0, j)),
            pl.BlockSpec((N_BRANCH, 1, tz), lambda i, j: (0, 0, j)),
            pl.BlockSpec((N_BRANCH, A_WIDTH, tz), lambda i, j: (0, 0, j)),
            pl.BlockSpec((tz, d), lambda i, j: (j, 0)),
        ],
        out_specs=pl.BlockSpec((TM, d), rowt),
        out_shape=jax.ShapeDtypeStruct((n_tiles * TM, d), F32),
        compiler_params=_params(("parallel", "arbitrary")),
        name="merge",
    )(xs, h, ya, yb, yc, mod, wg, bg, wb, wo)


def _rope_tables(b, t, c):
    rows = t // GRID_W
    rowp = jnp.repeat(jnp.arange(rows), GRID_W).astype(F32)
    colp = jnp.tile(jnp.arange(GRID_W), rows).astype(F32)
    inv = 1.0 / (ROPE_BASE ** (jnp.arange(0, ROPE_AXIS_DIM, 2, dtype=F32) / ROPE_AXIS_DIM))
    ar, ac = rowp[:, None] * inv, colp[:, None] * inv
    cr, sr, cc, sc = jnp.cos(ar), jnp.sin(ar), jnp.cos(ac), jnp.sin(ac)
    cos64 = jnp.concatenate([cr, cr, cc, cc], axis=-1)
    sin64 = jnp.concatenate([-sr, sr, -sc, sc], axis=-1)
    cos_t = jnp.tile(cos64, (b, 2))
    sin_t = jnp.tile(sin64, (b, 2))
    cos_t = jnp.concatenate([cos_t, jnp.ones((b * c, LANES), F32)])
    sin_t = jnp.concatenate([sin_t, jnp.zeros((b * c, LANES), F32)])
    return cos_t, sin_t


def kernel(x, c, ctx, c_ctx, w_ada, b_ada, norm_g, ffn_w_in, ffn_w_out, w_in, gm_v_norm, gm_ws, gm_bs,
           rw_conv, rw_w0, rw_w_up, rw_a0, rw_a_up, rw_g_up, rw_k_k, rw_k_a, rw_r_k, rw_ln_g, rw_ln_b,
           da_q_norm, da_k_norm, da_lam, da_subln, w_branch, b_gate, w_out):
    b, t, d = x.shape
    cl = ctx.shape[1]
    depth = w_ada.shape[0]
    d_ff = ffn_w_out.shape[2]
    assert b == 2 and b * cl == TM and t % TM == 0 and cl % 256 == 0
    n_lat = (b * t) // TM
    n_all = n_lat + 1
    tiles_per_batch = t // TM
    mod_idx = lambda i: jnp.where(i < n_lat, i // tiles_per_batch, b)

    xs = jnp.concatenate([x.reshape(b * t, d), ctx.reshape(b * cl, d)])
    cvec = jnp.zeros((SUBLANES, d), F32).at[:b].set(c).at[b].set(c_ctx)
    b_ada3 = b_ada.reshape(depth, 1, N_MOD * d)
    cos_t, sin_t = _rope_tables(b, t, cl)
    lane = jnp.arange(LANES)
    g64 = (lane[:, None] // C_HEAD_DIM == lane[None, :] // C_HEAD_DIM).astype(BF16)
    selv, g32, rep, merge_sel = _selectors()
    w_gu = ffn_w_in.astype(BF16)
    w_dn = ffn_w_out.astype(BF16)

    o = 0
    offs = []
    for n in (A_WIDTH, A_WIDTH, 3 * B_WIDTH, GATE_LORA, 2 * DECAY_LORA, 2 * ICL_LORA, 3 * C_WIDTH, N_BRANCH * d):
        offs.append((o, o + n))
        o += n

    for l in range(depth):
        last = l == depth - 1
        lam_init = 0.8 - 0.6 * math.exp(-0.3 * l)
        mod = _ada(cvec, w_ada, b_ada3, l)[:b + 1].reshape(b + 1, N_MOD, d)

        xs, h = _ffn(xs, mod, norm_g[l], w_gu, w_dn, l, 0, mod_idx, n_tiles=n_all, mi=0, emit_h=True)

        wl = w_in[l]
        sl = lambda i: wl[:, offs[i][0]:offs[i][1]]
        n_mix = n_lat if last else n_all

        w_uv = jnp.concatenate([sl(0), sl(1)], axis=1).astype(BF16)
        bsb = jnp.broadcast_to(gm_bs[l][:, :, None], (A_GROUPS, CHUNK, A_WIDTH // A_GROUPS))
        ya = _gmlp(h, w_uv, gm_v_norm[l].reshape(1, A_WIDTH), gm_ws[l], bsb, n_mix)

        w_rkv, w_dec, w_icl = sl(2), sl(4), sl(5)
        pad_lora = lambda w: jnp.pad(w, ((0, 0), (0, LORA_PAD - w.shape[1])))
        w_b = jnp.concatenate(
            [_kperm_cols(w_rkv[:, :B_WIDTH]), _kperm_cols(w_rkv[:, B_WIDTH:2 * B_WIDTH]),
             _vperm_cols(w_rkv[:, 2 * B_WIDTH:]), sl(3),
             pad_lora(w_dec[:, :DECAY_LORA]), pad_lora(w_dec[:, DECAY_LORA:]),
             pad_lora(w_icl[:, :ICL_LORA]), pad_lora(w_icl[:, ICL_LORA:])], axis=1).astype(BF16)
        p = _bmerge(_rkv_proj(h, w_b), merge_sel, t=t, c=cl)
        conv = rw_conv[l]
        conv_m = jnp.concatenate([_both(_kmerge_cols(conv[:, :B_WIDTH])),
                                  _both(_kmerge_cols(conv[:, B_WIDTH:2 * B_WIDTH])),
                                  _both(_vmerge_cols(conv[:, 2 * B_WIDTH:]))], axis=1)
        def up_m(w):
            w32 = jnp.stack([
                _kmerge_cols(jnp.pad(w[dd], ((0, LORA_PAD - w.shape[1]), (0, 0)))).reshape(2 * LORA_PAD, MERGED)
                for dd in range(2)])
            hi = w32.astype(BF16)
            return jnp.stack([hi, (w32 - hi.astype(F32)).astype(BF16)])
        r4, kk4, kd4, b4, ptot, vrep, v2d, bonus = _rwkv_prep(
            p, conv_m, _both(_kmerge_cols(rw_w0[l])), up_m(rw_w_up[l]), _both(_kmerge_cols(rw_a0[l])),
            up_m(rw_a_up[l]), _both(_kmerge_cols(rw_k_k[l].reshape(1, B_WIDTH))),
            _both(_kmerge_cols(rw_k_a[l].reshape(1, B_WIDTH))),
            _both(_kmerge_cols(rw_r_k[l].reshape(1, B_WIDTH))), g32, rep, t=t, c=cl)
        yf, ybk = _scan(kd4, b4, kk4, r4, vrep, ptot, t=t, c=cl)
        yb2 = _rwkv_out(yf, ybk, v2d, bonus, p, _both(_vmerge_cols(rw_ln_g[l].reshape(1, B_WIDTH))),
                        _both(_vmerge_cols(rw_ln_b[l].reshape(1, B_WIDTH))),
                        _vmerge_cols(rw_g_up[l]).reshape(2 * GATE_LORA, MERGED).astype(BF16),
                        selv, g32, rows=t if last else t + cl)
        if last:
            yb = yb2.reshape(b * t, B_WIDTH)
        else:
            yb = jnp.concatenate([yb2[0, :t], yb2[1, :t], yb2[0, t:], yb2[1, t:]])

        gains = jnp.stack([jnp.tile(da_q_norm[l], 2) * (C_HEAD_DIM ** -0.5 * math.log2(math.e)),
                           jnp.tile(da_k_norm[l], 2)]).reshape(2, 1, LANES)
        qk, vx = _qkv(h, sl(6).astype(BF16), gains, cos_t, sin_t, g64)
        sub = da_subln[l].reshape(1, LANES)
        yc = _attn(qk, vx, da_lam[l], sub, None, b=b, t=t, c=cl, lam_init=lam_init, ctx_queries=False)
        if not last:
            yc = _attn(qk, vx, da_lam[l], sub, yc, b=b, t=t, c=cl, lam_init=lam_init, ctx_queries=True)

        wg = jnp.transpose(sl(7).reshape(d, N_BRANCH, d), (1, 0, 2)).astype(BF16)
        wb = w_branch[l]
        wb_b = wb[1].reshape(B_HEADS, B_HEAD_DIM, d).transpose(1, 0, 2).reshape(B_WIDTH, d)
        wbr = jnp.stack([wb[0], wb_b, wb[2]]).astype(BF16)
        xs = _merge(xs, h, ya, yb, yc, mod, wg, b_gate[l].reshape(N_BRANCH, 1, d),
                    wbr, w_out[l].astype(BF16), mod_idx, n_mix)

        xs = _ffn(xs, mod, norm_g[l], w_gu, w_dn, l, 1, mod_idx, n_tiles=n_mix, mi=2, emit_h=False)

    return xs[:b * t].reshape(b, t, d)
```

```python
import functools
import math

import jax
import jax.numpy as jnp
from jax import lax
from jax.experimental import pallas as pl
from jax.experimental.pallas import tpu as pltpu

F32 = jnp.float32
BF16 = jnp.bfloat16

N_MOD = 9
CHUNK = 128
A_WIDTH = 1024
A_GROUPS = 8
B_WIDTH = 1024
B_HEAD_DIM = 64
B_HEADS = 16
DECAY_LORA = 96
ICL_LORA = 96
GATE_LORA = 256
C_HEADS = 8
C_HEAD_DIM = 64
C_WIDTH = 1024
N_BRANCH = 3
GRID_W = 64
ROPE_BASE = 10000.0
ROPE_AXIS_DIM = 32
NORM_EPS = 1e-6
LN_X_EPS = 64e-5

LANES = 128
SUBLANES = 8
VMEM_LIMIT = 56 * 1024 * 1024

TM = 512
TF = 512
LORA_PAD = 128
SCAN_TS = 64
ATT_TQ = 512
ATT_KC = 512
ADA_BANDS = 4


def _params(sem):
    return pltpu.CompilerParams(dimension_semantics=sem, vmem_limit_bytes=VMEM_LIMIT)


def _rms(x, eps):
    return x * lax.rsqrt(jnp.mean(x * x, axis=-1, keepdims=True) + eps)


def _split3(s):
    hi = s.astype(BF16)
    r = s - hi.astype(F32)
    mid = r.astype(BF16)
    lo = (r - mid.astype(F32)).astype(BF16)
    return hi, mid, lo


def _gsum(s, g_bf16):
    out = None
    for piece in _split3(s):
        d = jnp.dot(piece, g_bf16, preferred_element_type=F32)
        out = d if out is None else out + d
    return out


def _gelu_tanh(x):
    cdf = 0.5 * (1.0 + jnp.tanh(math.sqrt(2.0 / math.pi) * (x + 0.044715 * (x * x * x))))
    return x * cdf


def _ada_kernel(c_ref, *refs):
    w_refs, (b_ref, o_ref) = refs[:-2], refs[-2:]
    s = c_ref[...]
    s = s * jax.nn.sigmoid(s)
    pieces = _split3(s)
    band = s.shape[1] // len(w_refs)
    acc = b_ref[...]
    for i, w_ref in enumerate(w_refs):
        w = w_ref[...].astype(BF16)
        for piece in pieces:
            acc = acc + jnp.dot(piece[:, i * band:(i + 1) * band], w, preferred_element_type=F32)
    o_ref[...] = acc


def _ada(cvec, w_ada, b_ada3, l):
    d = cvec.shape[1]
    n = w_ada.shape[2]
    tn = 2048 if n % 2048 == 0 else 1024
    return pl.pallas_call(
        _ada_kernel,
        grid=(n // tn,),
        in_specs=[pl.BlockSpec((SUBLANES, d), lambda j: (0, 0))]
        + [pl.BlockSpec((None, d // ADA_BANDS, tn), lambda j, i=i: (l, i, j)) for i in range(ADA_BANDS)]
        + [pl.BlockSpec((None, 1, tn), lambda j: (l, 0, j))],
        out_specs=pl.BlockSpec((SUBLANES, tn), lambda j: (0, j)),
        out_shape=jax.ShapeDtypeStruct((SUBLANES, n), F32),
        compiler_params=_params(("arbitrary",)),
        name="ada",
    )(cvec, *([w_ada] * ADA_BANDS), b_ada3)


def _ffn_kernel(x_ref, mod_ref, g_ref, wg_ref, *rest, tf, nf, last_valid, mi, emit_h):
    wu_refs, wout_ref, rest = rest[:tf // LANES], rest[tf // LANES], rest[tf // LANES + 1:]
    if emit_h:
        o_ref, h_ref, hn_sc, acc_sc = rest
    else:
        o_ref, hn_sc, acc_sc = rest
    f = pl.program_id(1)

    @pl.when(f == 0)
    def _():
        xn = _rms(x_ref[...], NORM_EPS) * g_ref[mi:mi + 1, :]
        hn = xn * (1.0 + mod_ref[3 * mi + 1:3 * mi + 2, :]) + mod_ref[3 * mi:3 * mi + 1, :]
        hn_sc[...] = hn.astype(BF16)
        acc_sc[...] = jnp.zeros_like(acc_sc)

    def hidden_tile(valid):
        hn = hn_sc[...]
        g = jnp.dot(hn, wg_ref[:, :valid], preferred_element_type=F32)
        wu = jnp.concatenate([r[...] for r in wu_refs[:valid // LANES]], axis=1)
        u = jnp.dot(hn, wu, preferred_element_type=F32)
        act = (g * jax.nn.sigmoid(g) * u).astype(BF16)
        acc_sc[...] += jnp.dot(act, wout_ref[:valid, :], preferred_element_type=F32)

    if last_valid == tf:
        hidden_tile(tf)
    else:
        pl.when(f < nf - 1)(lambda: hidden_tile(tf))
        pl.when(f == nf - 1)(lambda: hidden_tile(last_valid))

    @pl.when(f == nf - 1)
    def _():
        out = x_ref[...] + 0.5 * mod_ref[3 * mi + 2:3 * mi + 3, :] * acc_sc[...]
        o_ref[...] = out
        if emit_h:
            hn = _rms(out, NORM_EPS) * g_ref[1:2, :]
            h_ref[...] = (hn * (1.0 + mod_ref[4:5, :]) + mod_ref[3:4, :]).astype(BF16)


def _ffn(xs, mod, norm_g, w_gu, w_out, l, w, mod_idx, *, n_tiles, mi, emit_h):
    m, d = xs.shape
    d_ff = w_out.shape[2]
    assert d_ff % LANES == 0
    up0 = d_ff // LANES
    nf = pl.cdiv(d_ff, TF)
    out_rows = n_tiles * TM
    out_shape = [jax.ShapeDtypeStruct((out_rows, d), F32)]
    out_specs = [pl.BlockSpec((TM, d), lambda i, f: (i, 0))]
    if emit_h:
        out_shape.append(jax.ShapeDtypeStruct((out_rows, d), BF16))
        out_specs.append(pl.BlockSpec((TM, d), lambda i, f: (i, 0)))
    res = pl.pallas_call(
        functools.partial(_ffn_kernel, tf=TF, nf=nf, last_valid=d_ff - (nf - 1) * TF, mi=mi, emit_h=emit_h),
        grid=(n_tiles, nf),
        in_specs=[
            pl.BlockSpec((TM, d), lambda i, f: (i, 0)),
            pl.BlockSpec((None, N_MOD, d), lambda i, f: (mod_idx(i), 0, 0)),
            pl.BlockSpec((3, d), lambda i, f: (0, 0)),
            pl.BlockSpec((None, None, d, TF), lambda i, f: (l, w, 0, f)),
        ] + [
            pl.BlockSpec((None, None, d, LANES),
                         lambda i, f, k=k: (l, w, 0, jnp.minimum(up0 + f * (TF // LANES) + k, 2 * up0 - 1)))
            for k in range(TF // LANES)
        ] + [
            pl.BlockSpec((None, None, TF, d), lambda i, f: (l, w, f, 0)),
        ],
        out_specs=out_specs,
        out_shape=out_shape,
        scratch_shapes=[pltpu.VMEM((TM, d), BF16), pltpu.VMEM((TM, d), F32)],
        compiler_params=_params(("parallel", "arbitrary")),
        name="ffn",
    )(xs, mod, norm_g, w_gu, *([w_gu] * (TF // LANES)), w_out)
    return res if emit_h else res[0]


def _gmlp_kernel(h_ref, w_ref, vn_ref, ws_ref, bsb_ref, o_ref, p_sc, *, tm):
    p_sc[...] = jnp.dot(h_ref[...], w_ref[...], preferred_element_type=F32)
    gd = A_WIDTH // A_GROUPS
    for g in range(A_GROUPS):
        wsg = ws_ref[g].astype(BF16)
        bias = bsb_ref[g]
        gain = vn_ref[:, g * gd:(g + 1) * gd]
        for n in range(tm // CHUNK):
            rows = slice(n * CHUNK, (n + 1) * CHUNK)
            u = _gelu_tanh(p_sc[rows, g * gd:(g + 1) * gd])
            v = _gelu_tanh(p_sc[rows, A_WIDTH + g * gd:A_WIDTH + (g + 1) * gd])
            v = _rms(v, NORM_EPS) * gain
            sv = jnp.dot(wsg, v.astype(BF16), preferred_element_type=F32) + bias
            o_ref[rows, g * gd:(g + 1) * gd] = (u * sv).astype(o_ref.dtype)


def _gmlp(h, w_uv, vn, ws, bsb, n_tiles):
    m, d = h.shape
    return pl.pallas_call(
        functools.partial(_gmlp_kernel, tm=TM),
        grid=(n_tiles,),
        in_specs=[
            pl.BlockSpec((TM, d), lambda i: (i, 0)),
            pl.BlockSpec((d, 2 * A_WIDTH), lambda i: (0, 0)),
            pl.BlockSpec((1, A_WIDTH), lambda i: (0, 0)),
            pl.BlockSpec((A_GROUPS, CHUNK, CHUNK), lambda i: (0, 0, 0)),
            pl.BlockSpec((A_GROUPS, CHUNK, A_WIDTH // A_GROUPS), lambda i: (0, 0, 0)),
        ],
        out_specs=pl.BlockSpec((TM, A_WIDTH), lambda i: (i, 0)),
        out_shape=jax.ShapeDtypeStruct((n_tiles * TM, A_WIDTH), BF16),
        scratch_shapes=[pltpu.VMEM((TM, 2 * A_WIDTH), F32)],
        compiler_params=_params(("parallel",)),
        name="gmlp",
    )(h, w_uv, vn, ws, bsb)


def _qkv_kernel(h_ref, w_ref, gain_ref, cos_ref, sin_ref, g64_ref, o_ref, vx_ref):
    n_parts = 2
    heads_per = C_HEADS // n_parts
    lane = lax.broadcasted_iota(jnp.int32, (1, LANES), 1)
    first = (lane % ROPE_AXIS_DIM) < (ROPE_AXIS_DIM // 2)
    half = ROPE_AXIS_DIM // 2
    cos = cos_ref[...]
    sin = sin_ref[...]
    g64 = g64_ref[...]
    ones = jnp.ones((h_ref.shape[0], LANES), BF16)
    for j in range(3):
        for part in range(n_parts):
            c0 = j * C_WIDTH + part * heads_per * LANES
            p = jnp.dot(h_ref[...], w_ref[:, c0:c0 + heads_per * LANES], preferred_element_type=F32)
            for i in range(heads_per):
                hs = part * heads_per + i
                x = p[:, i * LANES:(i + 1) * LANES]
                if j == 2:
                    vx_ref[:, 2 * hs * LANES:(2 * hs + 1) * LANES] = x.astype(BF16)
                    vx_ref[:, (2 * hs + 1) * LANES:(2 * hs + 2) * LANES] = ones
                else:
                    ms = _gsum(x * x, g64) * (1.0 / C_HEAD_DIM)
                    y = x * lax.rsqrt(ms + NORM_EPS) * gain_ref[j]
                    rot = jnp.where(first, pltpu.roll(y, LANES - half, 1), pltpu.roll(y, half, 1))
                    o_ref[j, :, hs * LANES:(hs + 1) * LANES] = (y * cos + rot * sin).astype(BF16)


def _qkv(h, w_qkv, gains, cos_t, sin_t, g64):
    m, d = h.shape
    return pl.pallas_call(
        _qkv_kernel,
        grid=(m // TM,),
        in_specs=[
            pl.BlockSpec((TM, d), lambda i: (i, 0)),
            pl.BlockSpec((d, 3 * C_WIDTH), lambda i: (0, 0)),
            pl.BlockSpec((2, 1, LANES), lambda i: (0, 0, 0)),
            pl.BlockSpec((TM, LANES), lambda i: (i, 0)),
            pl.BlockSpec((TM, LANES), lambda i: (i, 0)),
            pl.BlockSpec((LANES, LANES), lambda i: (0, 0)),
        ],
        out_specs=[pl.BlockSpec((2, TM, C_WIDTH), lambda i: (0, i, 0)),
                   pl.BlockSpec((TM, 2 * C_WIDTH), lambda i: (i, 0))],
        out_shape=[jax.ShapeDtypeStruct((2, m, C_WIDTH), BF16), jax.ShapeDtypeStruct((m, 2 * C_WIDTH), BF16)],
        compiler_params=_params(("parallel",)),
        name="qkv",
    )(h, w_qkv, gains, cos_t, sin_t, g64)


def _attn_kernel(lam_ref, sub_ref, q_ref, *refs, nseg, lam_init):
    kv = refs[:2 * nseg]
    o_ref = refs[2 * nseg]
    lv = lam_ref[...]
    lam = (jnp.exp(jnp.sum(lv[0:1] * lv[1:2], axis=-1, keepdims=True))
           - jnp.exp(jnp.sum(lv[2:3] * lv[3:4], axis=-1, keepdims=True)) + lam_init)
    q = q_ref[...]
    lane = lax.broadcasted_iota(jnp.int32, (1, LANES), 1)
    zero = jnp.zeros_like(q)
    outs = []
    for j in range(2):
        in_map = (lane < C_HEAD_DIM) if j == 0 else (lane >= C_HEAD_DIM)
        qj = jnp.where(in_map, q, zero)
        m = None
        acc = None
        for si in range(nseg):
            k_ref, v_ref = kv[2 * si], kv[2 * si + 1]
            ls = k_ref.shape[0]
            for c0 in range(0, ls, ATT_KC):
                kc = min(ATT_KC, ls - c0)
                s = lax.dot_general(qj, k_ref[c0:c0 + kc, :], (((1,), (1,)), ((), ())),
                                    preferred_element_type=F32)
                cm = jnp.max(s, axis=-1, keepdims=True)
                m_new = cm if m is None else jnp.maximum(m, cm)
                e = jnp.exp2((s - m_new).astype(BF16))
                pv = jnp.dot(e, v_ref[c0:c0 + kc, :], preferred_element_type=F32)
                acc = pv if acc is None else acc * jnp.exp2(m - m_new) + pv
                m = m_new
        outs.append(acc[:, :LANES] / acc[:, LANES:])
    o = outs[0] - lam * outs[1]
    o = _rms(o, NORM_EPS) * sub_ref[...]
    o_ref[...] = (o * (1.0 - lam_init)).astype(o_ref.dtype)


def _attn(qk, vx, lam, subln, *, b, t, c, lam_init, ctx_queries):
    if ctx_queries:
        tq, nq = c, 1
        q_blk = lambda bi, qi: (b * t) // c + bi
        o_blk = lambda bi, qi: bi
        segs = [(c, lambda bi: (b * t) // c + bi)]
    else:
        tq = ATT_TQ
        nq = t // tq
        q_blk = o_blk = lambda bi, qi: bi * nq + qi
        segs = [(t, lambda bi: bi), (c, lambda bi: (b * t) // c + bi)]
    in_specs = [
        pl.BlockSpec((4, C_HEAD_DIM), lambda bi, h, qi: (0, 0)),
        pl.BlockSpec((1, LANES), lambda bi, h, qi: (0, 0)),
        pl.BlockSpec((None, tq, LANES), lambda bi, h, qi: (0, q_blk(bi, qi), h)),
    ]
    args = [lam, subln, qk]
    for ls, blk in segs:
        in_specs.append(pl.BlockSpec((None, ls, LANES), lambda bi, h, qi, blk=blk: (1, blk(bi), h)))
        in_specs.append(pl.BlockSpec((ls, 2 * LANES), lambda bi, h, qi, blk=blk: (blk(bi), h)))
        args += [qk, vx]
    return pl.pallas_call(
        functools.partial(_attn_kernel, nseg=len(segs), lam_init=lam_init),
        grid=(b, C_HEADS, nq),
        in_specs=in_specs,
        out_specs=pl.BlockSpec((tq, LANES), lambda bi, h, qi: (o_blk(bi, qi), h)),
        out_shape=jax.ShapeDtypeStruct((b * nq * tq, C_WIDTH), BF16),
        compiler_params=_params(("parallel", "parallel", "arbitrary")),
        name="diff_attn_ctx" if ctx_queries else "diff_attn",
    )(*args)


KQ = 4
NKL = B_HEAD_DIM // KQ
MERGED = 2 * B_WIDTH
BH = 2 * B_HEADS
LORA_MERGED = 2 * GATE_LORA + 8 * LORA_PAD
PROJ_MERGED = 3 * MERGED + LORA_MERGED


PROJ_UNMERGED = 3 * B_WIDTH + GATE_LORA + 4 * LORA_PAD


def _mm_kernel(a_ref, w_ref, o_ref):
    o_ref[...] = jnp.dot(a_ref[...], w_ref[...], preferred_element_type=F32)


def _rkv_proj(h, w):
    m, d = h.shape
    tn = PROJ_UNMERGED // 3
    return pl.pallas_call(
        _mm_kernel,
        grid=(PROJ_UNMERGED // tn, m // TM),
        in_specs=[pl.BlockSpec((TM, d), lambda j, i: (i, 0)),
                  pl.BlockSpec((d, tn), lambda j, i: (0, j))],
        out_specs=pl.BlockSpec((TM, tn), lambda j, i: (i, j)),
        out_shape=jax.ShapeDtypeStruct((m, PROJ_UNMERGED), F32),
        compiler_params=_params(("parallel", "arbitrary")),
        name="rkv_proj",
    )(h, w)


def _bmerge_kernel(p0_ref, p1_ref, e_ref, o_ref):
    n_kv = 3 * B_WIDTH // LANES
    for bi, p_ref in enumerate((p0_ref, p1_ref)):
        for s in range(n_kv):
            x = p_ref[:, s * LANES:(s + 1) * LANES]
            hi = x.astype(BF16)
            lo = (x - hi.astype(F32)).astype(BF16)
            for half in range(2):
                e = e_ref[bi, half]
                part = (jnp.dot(hi, e, preferred_element_type=F32) + jnp.dot(lo, e, preferred_element_type=F32))
                cs = slice((2 * s + half) * LANES, (2 * s + half + 1) * LANES)
                if bi == 0:
                    o_ref[:, cs] = part
                else:
                    o_ref[:, cs] += part
        src = 3 * B_WIDTH
        dst = 3 * MERGED
        for width in (GATE_LORA,) + (LORA_PAD,) * 4:
            o_ref[:, dst + bi * width:dst + (bi + 1) * width] = p_ref[:, src:src + width]
            src += width
            dst += 2 * width


def _bmerge(p, e, *, t, c):
    tm = 256
    s = t + c
    nlat = t // tm
    hblk = lambda bi, i: jnp.where(i < nlat, bi * nlat + i, (2 * t + bi * c) // tm + (i - nlat))
    return pl.pallas_call(
        _bmerge_kernel,
        grid=(s // tm,),
        in_specs=[
            pl.BlockSpec((tm, PROJ_UNMERGED), lambda i: (hblk(0, i), 0)),
            pl.BlockSpec((tm, PROJ_UNMERGED), lambda i: (hblk(1, i), 0)),
            pl.BlockSpec((2, 2, LANES, LANES), lambda i: (0, 0, 0, 0)),
        ],
        out_specs=pl.BlockSpec((tm, PROJ_MERGED), lambda i: (i, 0)),
        out_shape=jax.ShapeDtypeStruct((s, PROJ_MERGED), F32),
        compiler_params=_params(("parallel",)),
        name="batch_merge",
    )(p, p, e)


class _SlabWriter:
    def __init__(self, ref, lead=()):
        self.ref, self.lead, self.parts, self.base = ref, lead, [], 0

    def add(self, slab):
        self.parts.append(slab)
        if len(self.parts) == SUBLANES:
            block = jnp.swapaxes(jnp.stack(self.parts, axis=0), 0, 1)
            self.ref[self.lead + (slice(None), slice(self.base, self.base + SUBLANES), slice(None))] = block
            self.parts, self.base = [], self.base + SUBLANES


def _rwkv_prep_kernel(p_ref, prev_ref, next_ref, lo_ref, conv_ref, w0_ref, wup_ref, a0_ref, aup_ref,
                      kkg_ref, ka_ref, rk_ref, g32_ref, rep_ref,
                      r_o, kk_o, kd_o, b_o, ptot_o, vrep_o, v_o, bonus_o, k_sc, r_sc, *, tm, ts, starts, ends):
    i = pl.program_id(0)
    is_start = functools.reduce(jnp.logical_or, [i == s for s in starts])
    is_end = functools.reduce(jnp.logical_or, [i == s for s in ends])
    row = lax.broadcasted_iota(jnp.int32, (tm, 1), 0)

    def conv(c0):
        cs = slice(c0, c0 + LANES)
        x = p_ref[:, cs]
        pm = jnp.where(is_start, 0.0, prev_ref[SUBLANES - 1:SUBLANES, cs])
        nx = jnp.where(is_end, 0.0, next_ref[0:1, cs])
        xm1 = jnp.where(row == 0, pm, pltpu.roll(x, 1, 0))
        xp1 = jnp.where(row == tm - 1, nx, pltpu.roll(x, tm - 1, 0))
        return xm1 * conv_ref[0:1, cs] + x * conv_ref[1:2, cs] + xp1 * conv_ref[2:3, cs]

    vrep_out = _SlabWriter(vrep_o)
    r_out, kk_out, kd_out, b_out = ([_SlabWriter(o, (d,)) for d in range(2)] for o in (r_o, kk_o, kd_o, b_o))

    ss = None
    for m in range(NKL):
        cs = slice(m * LANES, (m + 1) * LANES)
        r_sc[:, cs] = conv(m * LANES)
        k = conv(MERGED + m * LANES)
        k_sc[:, cs] = k
        kkr = k * kkg_ref[:, cs]
        ss = kkr * kkr if ss is None else ss + kkr * kkr
    rs = lax.rsqrt(_gsum(ss, g32_ref[...]) + 1e-12)

    w_off = 2 * GATE_LORA
    a_off = w_off + 4 * LORA_PAD
    def split2(x):
        hi = x.astype(BF16)
        return hi, (x - hi.astype(F32)).astype(BF16)

    def lora(x2, up_ref, d, cs):
        return (jnp.dot(x2[0], up_ref[0, d, :, cs], preferred_element_type=F32)
                + jnp.dot(x2[1], up_ref[0, d, :, cs], preferred_element_type=F32)
                + jnp.dot(x2[0], up_ref[1, d, :, cs], preferred_element_type=F32))

    tw = [split2(jnp.tanh(lo_ref[:, w_off + 2 * d * LORA_PAD:w_off + 2 * (d + 1) * LORA_PAD])) for d in range(2)]
    pa = [split2(lo_ref[:, a_off + 2 * d * LORA_PAD:a_off + 2 * (d + 1) * LORA_PAD]) for d in range(2)]
    bonus = None
    for m in range(NKL):
        cs = slice(m * LANES, (m + 1) * LANES)
        k = k_sc[:, cs]
        kk = k * kkg_ref[:, cs] * rs
        r = r_sc[:, cs]
        kd_sum = None
        for d in range(2):
            w_lo = w0_ref[d:d + 1, cs] + lora(tw[d], wup_ref, d, cs)
            lw = -jax.nn.sigmoid(w_lo) * math.exp(-0.5)
            cum = lw
            shift = 1
            while shift < ts:
                if d == 0:
                    cum = cum + jnp.where(row % ts >= shift, pltpu.roll(cum, shift, 0), 0.0)
                else:
                    cum = cum + jnp.where(row % ts < ts - shift, pltpu.roll(cum, tm - shift, 0), 0.0)
                shift *= 2
            p_incl = jnp.exp(cum)
            p_inv = jnp.exp(-cum)
            p_excl = jnp.exp(cum - lw)
            a = jax.nn.sigmoid(a0_ref[d:d + 1, cs] + lora(pa[d], aup_ref, d, cs))
            kd = k * (1.0 + (a - 1.0) * ka_ref[:, cs])
            kd_sum = kd if kd_sum is None else kd_sum + kd
            r_out[d].add(r * p_incl)
            kk_out[d].add(kk * p_excl)
            kd_out[d].add(kd * p_inv)
            b_out[d].add(kk * a * p_inv)
            for g in range(tm // ts):
                last = g * ts + (ts - 1 if d == 0 else 0)
                ptot_o[d, g, m:m + 1, :] = p_incl[last:last + 1, :]
        term = r * (0.5 * kd_sum) * rk_ref[:, cs]
        bonus = term if bonus is None else bonus + term
    bonus_o[...] = bonus

    per_slab = LANES // BH
    for j in range(MERGED // LANES):
        vs = conv(2 * MERGED + j * LANES)
        v_o[:, j * LANES:(j + 1) * LANES] = vs
        pieces = _split3(vs)
        for vi in range(per_slab):
            vrep_out.add(sum(jnp.dot(piece, rep_ref[vi], preferred_element_type=F32) for piece in pieces))


def _rwkv_prep(p, conv_m, w0_m, wup_m, a0_m, aup_m, kk_m, ka_m, rk_m, g32, rep, *, t, c):
    s = t + c
    tm = 128
    nt = s // tm
    starts = [0, t // tm]
    ends = [t // tm - 1, nt - 1]
    hb = tm // SUBLANES
    rkv_w = 3 * MERGED
    full2 = lambda i: (0, 0)
    full3 = lambda i: (0, 0, 0)
    ts = SCAN_TS
    k4 = jax.ShapeDtypeStruct((2, s, NKL, LANES), F32)
    k4s = pl.BlockSpec((2, tm, NKL, LANES), lambda i: (0, i, 0, 0))
    return pl.pallas_call(
        functools.partial(_rwkv_prep_kernel, tm=tm, ts=ts, starts=starts, ends=ends),
        grid=(nt,),
        in_specs=[
            pl.BlockSpec((tm, rkv_w), lambda i: (i, 0)),
            pl.BlockSpec((SUBLANES, rkv_w), lambda i: (jnp.maximum(i * hb - 1, 0), 0)),
            pl.BlockSpec((SUBLANES, rkv_w), lambda i: (jnp.minimum((i + 1) * hb, s // SUBLANES - 1), 0)),
            pl.BlockSpec((tm, LORA_MERGED), lambda i: (i, rkv_w // LORA_MERGED)),
            pl.BlockSpec((3, rkv_w), full2),
            pl.BlockSpec((2, MERGED), full2),
            pl.BlockSpec((2, 2, 2 * LORA_PAD, MERGED), lambda i: (0, 0, 0, 0)),
            pl.BlockSpec((2, MERGED), full2),
            pl.BlockSpec((2, 2, 2 * LORA_PAD, MERGED), lambda i: (0, 0, 0, 0)),
            pl.BlockSpec((1, MERGED), full2),
            pl.BlockSpec((1, MERGED), full2),
            pl.BlockSpec((1, MERGED), full2),
            pl.BlockSpec((LANES, LANES), full2),
            pl.BlockSpec((LANES // BH, LANES, LANES), full3),
        ],
        out_specs=[k4s, k4s, k4s, k4s,
                   pl.BlockSpec((2, tm // ts, NKL, LANES), lambda i: (0, i, 0, 0)),
                   pl.BlockSpec((tm, B_HEAD_DIM, LANES), lambda i: (i, 0, 0)),
                   pl.BlockSpec((tm, MERGED), lambda i: (i, 0)),
                   pl.BlockSpec((tm, LANES), lambda i: (i, 0))],
        out_shape=[k4, k4, k4, k4, jax.ShapeDtypeStruct((2, s // ts, NKL, LANES), F32),
                   jax.ShapeDtypeStruct((s, B_HEAD_DIM, LANES), F32),
                   jax.ShapeDtypeStruct((s, MERGED), F32), jax.ShapeDtypeStruct((s, LANES), F32)],
        scratch_shapes=[pltpu.VMEM((tm, MERGED), F32), pltpu.VMEM((tm, MERGED), F32)],
        compiler_params=_params(("parallel",)),
        name="rwkv_prep",
    )(p, p, p, p, conv_m, w0_m, wup_m, a0_m, aup_m, kk_m, ka_m, rk_m, g32, rep)


def _scan_kernel(*refs, ts):
    ins, (yf_ref, yb_ref, s_sc, y_sc, sa_sc) = refs[:12], refs[12:]
    streams = [ins[:6] + (yf_ref,), ins[6:] + (yb_ref,)]
    nkl = s_sc.shape[1]
    nvb = s_sc.shape[2] // SUBLANES

    @pl.when(pl.program_id(0) == 0)
    def _():
        s_sc[...] = jnp.zeros_like(s_sc)
        y_sc[...] = jnp.zeros_like(y_sc)

    def vsl(vb):
        return slice(vb * SUBLANES, (vb + 1) * SUBLANES)

    diag = (lax.broadcasted_iota(jnp.int32, (SUBLANES, LANES), 1) // BH
            == lax.broadcasted_iota(jnp.int32, (SUBLANES, LANES), 0) % KQ)

    def emit_y(d, y_ref, row):
        for vb in range(nvb):
            y = y_sc[d, vsl(vb), :]
            y = y + pltpu.roll(y, BH, 1)
            y = y + pltpu.roll(y, 2 * BH, 1)
            y = jnp.where(diag, y, 0.0)
            y = y + pltpu.roll(y, 1, 0)
            y = y + pltpu.roll(y, 2, 0)
            for half in range(SUBLANES // KQ):
                srow = half * KQ + KQ - 1
                y_ref[row, 2 * vb + half:2 * vb + half + 1, :] = y[srow:srow + 1, :]

    def allreduce(x):
        x = x + pltpu.roll(x, BH, 1)
        return x + pltpu.roll(x, 2 * BH, 1)

    def step(j, carry):
        ts_of = (j, ts - 1 - j)
        prev = (jnp.maximum(j - 1, 0), jnp.minimum(ts - j, ts - 1))
        for d in range(2):
            emit_y(d, streams[d][-1], prev[d])
        nxt = (jnp.minimum(j + 1, ts - 1), jnp.maximum(ts - 2 - j, 0))
        for d, (kd_ref, b_ref, kk_ref, r_ref, v_ref, ptot_ref, y_ref) in enumerate(streams):
            t = ts_of[d]
            sa = [sa_sc[d, vsl(vb), :] if d == 0 else allreduce(sa_sc[d, vsl(vb), :]) for vb in range(nvb)]
            ys = [None] * nvb
            sa_next = [None] * nvb
            for kl in range(nkl):
                brow = b_ref[t, kl:kl + 1, :]
                kdrow = kd_ref[t, kl:kl + 1, :]
                rrow = r_ref[t, kl:kl + 1, :]
                kknext = kk_ref[nxt[d], kl:kl + 1, :]
                for vb in range(nvb):
                    sn = s_sc[d, kl, vsl(vb), :] - sa[vb] * brow + v_ref[t, vsl(vb), :] * kdrow
                    s_sc[d, kl, vsl(vb), :] = sn
                    term = sn * rrow
                    ys[vb] = term if ys[vb] is None else ys[vb] + term
                    term = sn * kknext
                    sa_next[vb] = term if sa_next[vb] is None else sa_next[vb] + term
            for vb in range(nvb):
                y_sc[d, vsl(vb), :] = ys[vb]
                sa_sc[d, vsl(vb), :] = allreduce(sa_next[vb]) if d == 0 else sa_next[vb]
        return carry

    for d, first in enumerate((0, ts - 1)):
        kk_ref = streams[d][2]
        for vb in range(nvb):
            acc = None
            for kl in range(nkl):
                term = s_sc[d, kl, vsl(vb), :] * kk_ref[first, kl:kl + 1, :]
                acc = term if acc is None else acc + term
            sa_sc[d, vsl(vb), :] = allreduce(acc) if d == 0 else acc

    lax.fori_loop(0, ts, step, 0, unroll=2)
    emit_y(0, yf_ref, ts - 1)
    emit_y(1, yb_ref, 0)
    for d in range(2):
        ptot_ref = streams[d][5]
        for kl in range(nkl):
            prow = ptot_ref[0, kl:kl + 1, :]
            for vb in range(nvb):
                s_sc[d, kl, vsl(vb), :] = s_sc[d, kl, vsl(vb), :] * prow


def _scan(kd4, b4, kk4, r4, vrep, ptot, *, t, c):
    s = t + c
    ts = SCAN_TS
    nctx = c // ts
    nlat = t // ts
    fblk = lambda g: jnp.where(g < nctx, nlat + g, g - nctx)
    bblk = lambda g: jnp.where(g < nctx, nlat + (nctx - 1 - g), nlat - 1 - (g - nctx))
    in_specs = []
    args = []
    for d, blk in enumerate((fblk, bblk)):
        for a in (kd4, b4, kk4, r4):
            in_specs.append(pl.BlockSpec((None, ts, NKL, LANES), lambda g, d=d, blk=blk: (d, blk(g), 0, 0)))
            args.append(a)
        in_specs.append(pl.BlockSpec((ts, B_HEAD_DIM, LANES), lambda g, blk=blk: (blk(g), 0, 0)))
        args.append(vrep)
        in_specs.append(pl.BlockSpec((None, 1, NKL, LANES), lambda g, d=d, blk=blk: (d, blk(g), 0, 0)))
        args.append(ptot)
    nslab = MERGED // LANES
    yshape = jax.ShapeDtypeStruct((s, nslab, LANES), F32)
    return pl.pallas_call(
        functools.partial(_scan_kernel, ts=ts),
        grid=(s // ts,),
        in_specs=in_specs,
        out_specs=[pl.BlockSpec((ts, nslab, LANES), lambda g: (fblk(g), 0, 0)),
                   pl.BlockSpec((ts, nslab, LANES), lambda g: (bblk(g), 0, 0))],
        out_shape=[yshape, yshape],
        scratch_shapes=[pltpu.VMEM((2, NKL, B_HEAD_DIM, LANES), F32),
                        pltpu.VMEM((2, B_HEAD_DIM, LANES), F32),
                        pltpu.VMEM((2, B_HEAD_DIM, LANES), F32)],
        compiler_params=_params(("arbitrary",)),
        name="wkv7_scan",
    )(*args)


def _rwkv_out_kernel(yf_ref, yb_ref, v_ref, bonus_ref, pg_ref, lng_ref, lnb_ref, gup_ref,
                     selv_ref, g32_ref, o_ref):
    g32 = g32_ref[...]
    gate = jnp.dot(jax.nn.sigmoid(pg_ref[...]).astype(BF16), gup_ref[...], preferred_element_type=F32)
    bonus = _gsum(bonus_ref[...], g32)
    nslab = MERGED // LANES
    ys = []
    for g in range(nslab // SUBLANES):
        rows = slice(g * SUBLANES, (g + 1) * SUBLANES)
        both = jnp.swapaxes(yf_ref[:, rows, :] + yb_ref[:, rows, :], 0, 1)
        ys += [both[j] for j in range(SUBLANES)]
    inv = 1.0 / B_HEAD_DIM
    mu = _gsum(functools.reduce(lambda a, b_: a + b_, ys), g32) * inv
    ds = [y - mu for y in ys]
    var = _gsum(functools.reduce(lambda a, b_: a + b_, [x * x for x in ds]), g32) * inv
    rstd = lax.rsqrt(var + LN_X_EPS)
    outs = []
    for j in range(nslab):
        cs = slice(j * LANES, (j + 1) * LANES)
        yn = ds[j] * rstd * lng_ref[:, cs] + lnb_ref[:, cs]
        outs.append(((yn + bonus * v_ref[:, cs]) * gate[:, cs]).astype(BF16))
    for j in range(nslab // 2):
        pair = jnp.concatenate([outs[2 * j], outs[2 * j + 1]], axis=1)
        for bi in range(2):
            o_ref[bi, :, j * LANES:(j + 1) * LANES] = jnp.dot(
                pair, selv_ref[bi], preferred_element_type=F32).astype(o_ref.dtype)


def _rwkv_out(yf, yb, v2d, bonus, p, lng_m, lnb_m, gup_m, selv, g32, *, rows):
    tm = 256
    nslab = MERGED // LANES
    y3s = pl.BlockSpec((tm, nslab, LANES), lambda i: (i, 0, 0))
    full2 = lambda i: (0, 0)
    return pl.pallas_call(
        _rwkv_out_kernel,
        grid=(rows // tm,),
        in_specs=[
            y3s, y3s,
            pl.BlockSpec((tm, MERGED), lambda i: (i, 0)),
            pl.BlockSpec((tm, LANES), lambda i: (i, 0)),
            pl.BlockSpec((tm, 2 * GATE_LORA), lambda i: (i, 3 * MERGED // (2 * GATE_LORA))),
            pl.BlockSpec((1, MERGED), full2),
            pl.BlockSpec((1, MERGED), full2),
            pl.BlockSpec((2 * GATE_LORA, MERGED), full2),
            pl.BlockSpec((2, 2 * LANES, LANES), lambda i: (0, 0, 0)),
            pl.BlockSpec((LANES, LANES), full2),
        ],
        out_specs=pl.BlockSpec((2, tm, B_WIDTH), lambda i: (0, i, 0)),
        out_shape=jax.ShapeDtypeStruct((2, rows, B_WIDTH), BF16),
        compiler_params=_params(("parallel",)),
        name="rwkv_out",
    )(yf, yb, v2d, bonus, p, lng_m, lnb_m, gup_m, selv, g32)


def _kmerge_cols(w):
    r = w.shape[0]
    wt = w.reshape(r, B_HEADS, NKL, KQ).transpose(0, 2, 3, 1)
    z = jnp.zeros_like(wt)
    return jnp.stack([jnp.stack([wt, z], axis=3), jnp.stack([z, wt], axis=3)]).reshape(2, r, MERGED)


def _vmerge_cols(w):
    r = w.shape[0]
    wt = w.reshape(r, B_HEADS, B_HEAD_DIM).transpose(0, 2, 1)
    z = jnp.zeros_like(wt)
    return jnp.stack([jnp.stack([wt, z], axis=2), jnp.stack([z, wt], axis=2)]).reshape(2, r, MERGED)


def _both(m):
    return m[0] + m[1]


def _kperm_cols(w):
    r = w.shape[0]
    return w.reshape(r, B_HEADS, NKL, KQ).transpose(0, 2, 3, 1).reshape(r, B_WIDTH)


def _vperm_cols(w):
    r = w.shape[0]
    return w.reshape(r, B_HEADS, B_HEAD_DIM).transpose(0, 2, 1).reshape(r, B_WIDTH)


def _selectors():
    ci = jnp.arange(LANES)[None, :]
    bsel = jnp.arange(2)[:, None, None]
    c_vi, c_h = ci // B_HEADS, ci % B_HEADS
    r2 = jnp.arange(2 * LANES)[:, None]
    selv = ((r2 // BH == c_vi) & ((r2 % BH) // B_HEADS == bsel) & (r2 % B_HEADS == c_h)).astype(BF16)
    r1 = jnp.arange(LANES)[:, None]
    g32 = (r1 % BH == ci % BH).astype(BF16)
    rep = jnp.stack([((r1 // BH == vi) & (r1 % BH == ci % BH)) for vi in range(LANES // BH)]).astype(BF16)
    half = jnp.arange(2)[None, :, None, None]
    merge = ((r1 // (LANES // 2) == half) & ((r1 % (LANES // 2)) // B_HEADS == ci // BH)
             & ((ci % BH) // B_HEADS == bsel[:, None]) & (r1 % B_HEADS == c_h)).astype(BF16)
    return selv, g32, rep, merge


def _merge_kernel(x_ref, h_ref, ya_ref, yb_ref, yc_ref, mod_ref, wg_ref, bg_ref, wb_ref, wo_ref,
                  o_ref, *, nz):
    j = pl.program_id(1)
    h = h_ref[...]
    z = None
    for i, y_ref in enumerate((ya_ref, yb_ref, yc_ref)):
        gate = jax.nn.sigmoid(jnp.dot(h, wg_ref[i], preferred_element_type=F32) + bg_ref[i])
        term = gate * jnp.dot(y_ref[...], wb_ref[i], preferred_element_type=F32)
        z = term if z is None else z + term
    part = jnp.dot(z.astype(BF16), wo_ref[...], preferred_element_type=F32)

    @pl.when(j == 0)
    def _():
        o_ref[...] = part

    @pl.when(j > 0)
    def _():
        o_ref[...] += part

    @pl.when(j == nz - 1)
    def _():
        o_ref[...] = x_ref[...] + mod_ref[5:6, :] * o_ref[...]


def _merge(xs, h, ya, yb, yc, mod, wg, bg, wb, wo, mod_idx, n_tiles):
    d = xs.shape[1]
    tz = 512
    nz = d // tz
    rowt = lambda i, j: (i, 0)
    return pl.pallas_call(
        functools.partial(_merge_kernel, nz=nz),
        grid=(n_tiles, nz),
        in_specs=[
            pl.BlockSpec((TM, d), rowt),
            pl.BlockSpec((TM, d), rowt),
            pl.BlockSpec((TM, A_WIDTH), rowt),
            pl.BlockSpec((TM, B_WIDTH), rowt),
            pl.BlockSpec((TM, C_WIDTH), rowt),
            pl.BlockSpec((None, N_MOD, d), lambda i, j: (mod_idx(i), 0, 0)),
            pl.BlockSpec((N_BRANCH, d, tz), lambda i, j: (0, 0, j)),
            pl.BlockSpec((N_BRANCH, 1, tz), lambda i, j: (0, 0, j)),
            pl.BlockSpec((N_BRANCH, A_WIDTH, tz), lambda i, j: (0, 0, j)),
            pl.BlockSpec((tz, d), lambda i, j: (j, 0)),
        ],
        out_specs=pl.BlockSpec((TM, d), rowt),
        out_shape=jax.ShapeDtypeStruct((n_tiles * TM, d), F32),
        compiler_params=_params(("parallel", "arbitrary")),
        name="merge",
    )(xs, h, ya, yb, yc, mod, wg, bg, wb, wo)


def _rope_tables(b, t, c):
    rows = t // GRID_W
    rowp = jnp.repeat(jnp.arange(rows), GRID_W).astype(F32)
    colp = jnp.tile(jnp.arange(GRID_W), rows).astype(F32)
    inv = 1.0 / (ROPE_BASE ** (jnp.arange(0, ROPE_AXIS_DIM, 2, dtype=F32) / ROPE_AXIS_DIM))
    ar, ac = rowp[:, None] * inv, colp[:, None] * inv
    cr, sr, cc, sc = jnp.cos(ar), jnp.sin(ar), jnp.cos(ac), jnp.sin(ac)
    cos64 = jnp.concatenate([cr, cr, cc, cc], axis=-1)
    sin64 = jnp.concatenate([-sr, sr, -sc, sc], axis=-1)
    cos_t = jnp.tile(cos64, (b, 2))
    sin_t = jnp.tile(sin64, (b, 2))
    cos_t = jnp.concatenate([cos_t, jnp.ones((b * c, LANES), F32)])
    sin_t = jnp.concatenate([sin_t, jnp.zeros((b * c, LANES), F32)])
    return cos_t, sin_t


def kernel(x, c, ctx, c_ctx, w_ada, b_ada, norm_g, ffn_w_in, ffn_w_out, w_in, gm_v_norm, gm_ws, gm_bs,
           rw_conv, rw_w0, rw_w_up, rw_a0, rw_a_up, rw_g_up, rw_k_k, rw_k_a, rw_r_k, rw_ln_g, rw_ln_b,
           da_q_norm, da_k_norm, da_lam, da_subln, w_branch, b_gate, w_out):
    b, t, d = x.shape
    cl = ctx.shape[1]
    depth = w_ada.shape[0]
    d_ff = ffn_w_out.shape[2]
    assert b == 2 and b * cl == TM and t % TM == 0 and cl % 256 == 0
    n_lat = (b * t) // TM
    n_all = n_lat + 1
    tiles_per_batch = t // TM
    mod_idx = lambda i: jnp.where(i < n_lat, i // tiles_per_batch, b)

    xs = jnp.concatenate([x.reshape(b * t, d), ctx.reshape(b * cl, d)])
    cvec = jnp.zeros((SUBLANES, d), F32).at[:b].set(c).at[b].set(c_ctx)
    b_ada3 = b_ada.reshape(depth, 1, N_MOD * d)
    cos_t, sin_t = _rope_tables(b, t, cl)
    lane = jnp.arange(LANES)
    g64 = (lane[:, None] // C_HEAD_DIM == lane[None, :] // C_HEAD_DIM).astype(BF16)
    selv, g32, rep, merge_sel = _selectors()
    w_gu = ffn_w_in.astype(BF16)
    w_dn = ffn_w_out.astype(BF16)

    o = 0
    offs = []
    for n in (A_WIDTH, A_WIDTH, 3 * B_WIDTH, GATE_LORA, 2 * DECAY_LORA, 2 * ICL_LORA, 3 * C_WIDTH, N_BRANCH * d):
        offs.append((o, o + n))
        o += n

    for l in range(depth):
        last = l == depth - 1
        lam_init = 0.8 - 0.6 * math.exp(-0.3 * l)
        mod = _ada(cvec, w_ada, b_ada3, l)[:b + 1].reshape(b + 1, N_MOD, d)

        xs, h = _ffn(xs, mod, norm_g[l], w_gu, w_dn, l, 0, mod_idx, n_tiles=n_all, mi=0, emit_h=True)

        wl = w_in[l]
        sl = lambda i: wl[:, offs[i][0]:offs[i][1]]
        n_mix = n_lat if last else n_all

        w_uv = jnp.concatenate([sl(0), sl(1)], axis=1).astype(BF16)
        bsb = jnp.broadcast_to(gm_bs[l][:, :, None], (A_GROUPS, CHUNK, A_WIDTH // A_GROUPS))
        ya = _gmlp(h, w_uv, gm_v_norm[l].reshape(1, A_WIDTH), gm_ws[l], bsb, n_mix)

        w_rkv, w_dec, w_icl = sl(2), sl(4), sl(5)
        pad_lora = lambda w: jnp.pad(w, ((0, 0), (0, LORA_PAD - w.shape[1])))
        w_b = jnp.concatenate(
            [_kperm_cols(w_rkv[:, :B_WIDTH]), _kperm_cols(w_rkv[:, B_WIDTH:2 * B_WIDTH]),
             _vperm_cols(w_rkv[:, 2 * B_WIDTH:]), sl(3),
             pad_lora(w_dec[:, :DECAY_LORA]), pad_lora(w_dec[:, DECAY_LORA:]),
             pad_lora(w_icl[:, :ICL_LORA]), pad_lora(w_icl[:, ICL_LORA:])], axis=1).astype(BF16)
        p = _bmerge(_rkv_proj(h, w_b), merge_sel, t=t, c=cl)
        conv = rw_conv[l]
        conv_m = jnp.concatenate([_both(_kmerge_cols(conv[:, :B_WIDTH])),
                                  _both(_kmerge_cols(conv[:, B_WIDTH:2 * B_WIDTH])),
                                  _both(_vmerge_cols(conv[:, 2 * B_WIDTH:]))], axis=1)
        def up_m(w):
            w32 = jnp.stack([
                _kmerge_cols(jnp.pad(w[dd], ((0, LORA_PAD - w.shape[1]), (0, 0)))).reshape(2 * LORA_PAD, MERGED)
                for dd in range(2)])
            hi = w32.astype(BF16)
            return jnp.stack([hi, (w32 - hi.astype(F32)).astype(BF16)])
        r4, kk4, kd4, b4, ptot, vrep, v2d, bonus = _rwkv_prep(
            p, conv_m, _both(_kmerge_cols(rw_w0[l])), up_m(rw_w_up[l]), _both(_kmerge_cols(rw_a0[l])),
            up_m(rw_a_up[l]), _both(_kmerge_cols(rw_k_k[l].reshape(1, B_WIDTH))),
            _both(_kmerge_cols(rw_k_a[l].reshape(1, B_WIDTH))),
            _both(_kmerge_cols(rw_r_k[l].reshape(1, B_WIDTH))), g32, rep, t=t, c=cl)
        yf, ybk = _scan(kd4, b4, kk4, r4, vrep, ptot, t=t, c=cl)
        yb2 = _rwkv_out(yf, ybk, v2d, bonus, p, _both(_vmerge_cols(rw_ln_g[l].reshape(1, B_WIDTH))),
                        _both(_vmerge_cols(rw_ln_b[l].reshape(1, B_WIDTH))),
                        _vmerge_cols(rw_g_up[l]).reshape(2 * GATE_LORA, MERGED).astype(BF16),
                        selv, g32, rows=t if last else t + cl)
        if last:
            yb = yb2.reshape(b * t, B_WIDTH)
        else:
            yb = jnp.concatenate([yb2[0, :t], yb2[1, :t], yb2[0, t:], yb2[1, t:]])

        gains = jnp.stack([jnp.tile(da_q_norm[l], 2) * (C_HEAD_DIM ** -0.5 * math.log2(math.e)),
                           jnp.tile(da_k_norm[l], 2)]).reshape(2, 1, LANES)
        qk, vx = _qkv(h, sl(6).astype(BF16), gains, cos_t, sin_t, g64)
        sub = da_subln[l].reshape(1, LANES)
        yc = _attn(qk, vx, da_lam[l], sub, b=b, t=t, c=cl, lam_init=lam_init, ctx_queries=False)
        if not last:
            yc_ctx = _attn(qk, vx, da_lam[l], sub, b=b, t=t, c=cl, lam_init=lam_init, ctx_queries=True)
            yc = jnp.concatenate([yc, yc_ctx])

        wg = jnp.transpose(sl(7).reshape(d, N_BRANCH, d), (1, 0, 2)).astype(BF16)
        wb = w_branch[l]
        wb_b = wb[1].reshape(B_HEADS, B_HEAD_DIM, d).transpose(1, 0, 2).reshape(B_WIDTH, d)
        wbr = jnp.stack([wb[0], wb_b, wb[2]]).astype(BF16)
        xs = _merge(xs, h, ya, yb, yc, mod, wg, b_gate[l].reshape(N_BRANCH, 1, d),
                    wbr, w_out[l].astype(BF16), mod_idx, n_mix)

        xs = _ffn(xs, mod, norm_g[l], w_gu, w_dn, l, 1, mod_idx, n_tiles=n_mix, mi=2, emit_h=False)

    return xs[:b * t].reshape(b, t, d)
```
